```python
import math, functools
import jax, jax.numpy as jnp
from jax import lax
import numpy as np

D_MODEL = 1024
BATCH = 16
SEQ = 4096
DEPTH = 2

GRID_W = 64
CTX_LEN = 256
Q_BLOCK = 128
ROPE_BASE = 10000.0
EPS = 1e-6
N_MOD = 6

DIFF_HEADS = 4
DIFF_QK_DIM = 64
DIFF_V_DIM = 2 * DIFF_QK_DIM
DIFF_SCALE = DIFF_QK_DIM ** -0.5
FOURIER_GROUPS = 4
FOURIER_GROUP_DIM = 128
GQA_Q_HEADS = 8
GQA_KV_HEADS = 2
GQA_GROUP = GQA_Q_HEADS // GQA_KV_HEADS
GQA_HEAD_DIM = DIFF_QK_DIM
GQA_SCALE = GQA_HEAD_DIM ** -0.5
MLA_HEADS = 8
MLA_Q_RANK = 256
MLA_KV_RANK = 128
MLA_NOPE_DIM = 64
MLA_ROPE_DIM = 32
MLA_V_DIM = 64
MLA_SCALE = (MLA_NOPE_DIM + MLA_ROPE_DIM) ** -0.5
N_BRANCHES = 4
BRANCH_WIDTH = 512
IN_SECTIONS = (
    DIFF_HEADS * 2 * DIFF_QK_DIM,
    DIFF_HEADS * 2 * DIFF_QK_DIM,
    DIFF_HEADS * DIFF_V_DIM,
    FOURIER_GROUPS * FOURIER_GROUP_DIM,
    GQA_Q_HEADS * GQA_HEAD_DIM,
    GQA_KV_HEADS * GQA_HEAD_DIM,
    GQA_KV_HEADS * GQA_HEAD_DIM,
    MLA_Q_RANK,
    MLA_KV_RANK,
    MLA_ROPE_DIM,
)
IN_COLS = 3232
FFN_DIM = 2816
N_EXPERTS = 8
TOP_K = 2
EXPERT_DIM = 3584

kernel_name = "hybrid_gated_mixers_diffusion_block"


def rms_norm(x, g):
    xf = x.astype(jnp.float32)
    y = xf * lax.rsqrt(jnp.mean(jnp.square(xf), axis=-1, keepdims=True) + EPS)
    return (y * g.astype(jnp.float32)).astype(x.dtype)


def axial_rope_tables(n_tokens, rot_dim):
    rows = n_tokens // GRID_W
    row = jnp.repeat(jnp.arange(rows, dtype=jnp.float32), GRID_W)
    col = jnp.tile(jnp.arange(GRID_W, dtype=jnp.float32), rows)
    axis_dim = rot_dim // 2
    inv_freq = 1.0 / (ROPE_BASE ** (jnp.arange(0, axis_dim, 2, dtype=jnp.float32) / axis_dim))
    ang_row = row[:, None] * inv_freq
    ang_col = col[:, None] * inv_freq
    ang = jnp.concatenate([ang_row, ang_row, ang_col, ang_col], axis=-1)
    return jnp.cos(ang), jnp.sin(ang)


def _rotate_half(y):
    y1, y2 = jnp.split(y, 2, axis=-1)
    return jnp.concatenate([-y2, y1], axis=-1)


def apply_axial_rope(x, cos, sin):
    shape = (cos.shape[0],) + (1,) * (x.ndim - 3) + (cos.shape[1],)
    cos = cos.reshape(shape)
    sin = sin.reshape(shape)
    x_row, x_col = jnp.split(x, 2, axis=-1)
    rot = jnp.concatenate([_rotate_half(x_row), _rotate_half(x_col)], axis=-1)
    return (x.astype(jnp.float32) * cos + rot.astype(jnp.float32) * sin).astype(x.dtype)


def _in_splits():
    return tuple(int(s) for s in np.cumsum(IN_SECTIONS)[:-1])


def _to_blocks(x):
    b, n = x.shape[:2]
    x = x.reshape((b, n // Q_BLOCK, Q_BLOCK) + x.shape[2:])
    return jnp.swapaxes(x, 0, 1)


def _from_blocks(y):
    y = jnp.swapaxes(y, 0, 1)
    return y.reshape((y.shape[0], y.shape[1] * y.shape[2]) + y.shape[3:])


def project_stream(h, lp, rope):
    b, n, _ = h.shape
    z = h @ lp["w_in"]
    dq, dk, dv, f_in, gq, gk, gv, mcq, mckv, mkr = jnp.split(z, _in_splits(), axis=-1)
    dq = dq.reshape(b, n, DIFF_HEADS, 2, DIFF_QK_DIM)
    dk = dk.reshape(b, n, DIFF_HEADS, 2, DIFF_QK_DIM)
    dv = dv.reshape(b, n, DIFF_HEADS, DIFF_V_DIM)
    gq = rms_norm(gq.reshape(b, n, GQA_KV_HEADS, GQA_GROUP, GQA_HEAD_DIM), lp["gqa_q_norm"])
    gk = rms_norm(gk.reshape(b, n, GQA_KV_HEADS, GQA_HEAD_DIM), lp["gqa_k_norm"])
    gv = gv.reshape(b, n, GQA_KV_HEADS, GQA_HEAD_DIM)
    mq = (rms_norm(mcq, lp["mla_q_norm"]) @ lp["mla_w_q_up"]).reshape(b, n, MLA_HEADS, MLA_NOPE_DIM + MLA_ROPE_DIM)
    mqn, mqr = mq[..., :MLA_NOPE_DIM], mq[..., MLA_NOPE_DIM:]
    mkv = (rms_norm(mckv, lp["mla_kv_norm"]) @ lp["mla_w_kv_up"]).reshape(b, n, MLA_HEADS, MLA_NOPE_DIM + MLA_V_DIM)
    mkn, mv = mkv[..., :MLA_NOPE_DIM], mkv[..., MLA_NOPE_DIM:]
    if rope is not None:
        cos_h, sin_h, cos_r, sin_r = rope
        dq = apply_axial_rope(dq, cos_h, sin_h)
        dk = apply_axial_rope(dk, cos_h, sin_h)
        gq = apply_axial_rope(gq, cos_h, sin_h)
        gk = apply_axial_rope(gk, cos_h, sin_h)
        mqr = apply_axial_rope(mqr, cos_r, sin_r)
        mkr = apply_axial_rope(mkr, cos_r, sin_r)
    q = (dq, gq, mqn, mqr)
    kv = (dk, dv, gk, gv, mkn, mkr, mv)
    return q, kv, f_in


def attention_branches(q, kv, lam):
    dq, gq, mqn, mqr = q
    dk, dv, gk, gv, mkn, mkr, mv = kv
    f32 = jnp.float32
    s = jnp.einsum("bqhmd,bkhmd->bhmqk", dq, dk, preferred_element_type=f32) * DIFF_SCALE
    p = jax.nn.softmax(s, axis=-1)
    a = p[:, :, 0] - lam * p[:, :, 1]
    y_diff = jnp.einsum("bhqk,bkhd->bqhd", a.astype(dv.dtype), dv)
    s = jnp.einsum("bqhgd,bkhd->bhgqk", gq, gk, preferred_element_type=f32) * GQA_SCALE
    p = jax.nn.softmax(s, axis=-1)
    y_gqa = jnp.einsum("bhgqk,bkhd->bqhgd", p.astype(gv.dtype), gv)
    s = (jnp.einsum("bqhd,bkhd->bhqk", mqn, mkn, preferred_element_type=f32)
         + jnp.einsum("bqhr,bkr->bhqk", mqr, mkr, preferred_element_type=f32)) * MLA_SCALE
    p = jax.nn.softmax(s, axis=-1)
    y_mla = jnp.einsum("bhqk,bkhd->bqhd", p.astype(mv.dtype), mv)
    return y_diff, y_gqa, y_mla


def fourier_mix(f_in):
    b, n, _ = f_in.shape
    u = f_in.reshape(b, n, FOURIER_GROUPS, FOURIER_GROUP_DIM).astype(jnp.float32)
    y = jnp.fft.fft2(u, axes=(1, 3), norm="ortho").real
    return y.reshape(b, n, FOURIER_GROUPS * FOURIER_GROUP_DIM).astype(f_in.dtype)


def collect_branches(y_diff, y_gqa, y_mla, f_in, lp, lam_init):
    b, n = y_diff.shape[:2]
    y_diff = rms_norm(y_diff, lp["diff_subnorm"]) * (1.0 - lam_init)
    return (y_diff.reshape(b, n, BRANCH_WIDTH), fourier_mix(f_in),
            y_gqa.reshape(b, n, BRANCH_WIDTH), y_mla.reshape(b, n, BRANCH_WIDTH))


def merge_branches(h, branches, lp):
    out = None
    for i, y in enumerate(branches):
        gate = jax.nn.sigmoid(h @ lp["w_gate"][i] + lp["b_gate"][i])
        term = gate * (y @ lp["w_branch"][i])
        out = term if out is None else out + term
    return out @ lp["w_out"]


def token_mixing(h_lat, h_ctx, lp, layer, rope, with_ctx):
    lam_init = 0.8 - 0.6 * math.exp(-0.3 * layer)
    lq1, lk1, lq2, lk2 = lp["diff_lambda"].astype(jnp.float32)
    lam = jnp.exp(jnp.sum(lq1 * lk1)) - jnp.exp(jnp.sum(lq2 * lk2)) + lam_init
    q_lat, kv_lat, f_lat = project_stream(h_lat, lp, rope)
    q_ctx, kv_ctx, f_ctx = project_stream(h_ctx, lp, None)
    kv_all = tuple(jnp.concatenate([kc, kl], axis=1) for kc, kl in zip(kv_ctx, kv_lat))
    att = lax.map(lambda qb: attention_branches(qb, kv_all, lam), tuple(_to_blocks(t) for t in q_lat))
    y_diff, y_gqa, y_mla = (_from_blocks(t) for t in att)
    m_lat = merge_branches(h_lat, collect_branches(y_diff, y_gqa, y_mla, f_lat, lp, lam_init), lp)
    if not with_ctx:
        return m_lat, None
    c_diff, c_gqa, c_mla = attention_branches(q_ctx, kv_ctx, lam)
    m_ctx = merge_branches(h_ctx, collect_branches(c_diff, c_gqa, c_mla, f_ctx, lp, lam_init), lp)
    return m_lat, m_ctx


def swiglu(h, w_gate_up, w_down):
    a, u = jnp.split(h @ w_gate_up, 2, axis=-1)
    return (jax.nn.silu(a) * u) @ w_down


def moe_ffn(h, router, w_gate_up, w_down):
    logits = jnp.einsum("bnd,de->bne", h, router, preferred_element_type=jnp.float32)
    top_val, top_idx = lax.top_k(logits, TOP_K)
    gates = jax.nn.softmax(top_val, axis=-1)
    combine = jnp.sum(jax.nn.one_hot(top_idx, N_EXPERTS, dtype=jnp.float32) * gates[..., None], axis=-2)
    out = None
    for e in range(N_EXPERTS):
        term = combine[..., e:e + 1].astype(h.dtype) * swiglu(h, w_gate_up[e], w_down[e])
        out = term if out is None else out + term
    return out


def setup_inputs(seed: int = 0) -> dict:
    key = jax.random.key(seed)
    k = jax.random.split(key, 28)
    n_dense = (DEPTH + 1) // 2
    n_moe = DEPTH // 2
    f32 = jnp.float32

    def nrm(i, shape, scale):
        return jax.random.normal(k[i], shape, f32) * scale

    def gain(i, shape):
        return 1.0 + nrm(i, shape, 0.02)

    d = D_MODEL
    return {
        "x": nrm(0, (BATCH, SEQ, d), 1.0),
        "c": nrm(1, (BATCH, d), 1.0),
        "ctx": nrm(2, (BATCH, CTX_LEN, d), 1.0),
        "c_ctx": nrm(3, (d,), 1.0),
        "w_mod": nrm(4, (DEPTH, d, N_MOD * d), 0.5 * d ** -0.5),
        "b_mod": nrm(5, (DEPTH, N_MOD * d), 0.01),
        "mix_pre_norm": gain(6, (DEPTH, d)),
        "mix_post_norm": gain(7, (DEPTH, d)),
        "ffn_pre_norm": gain(8, (DEPTH, d)),
        "ffn_post_norm": gain(9, (DEPTH, d)),
        "w_in": nrm(10, (DEPTH, d, IN_COLS), d ** -0.5),
        "diff_lambda": nrm(11, (DEPTH, 4, DIFF_QK_DIM), 0.1),
        "diff_subnorm": gain(12, (DEPTH, DIFF_V_DIM)),
        "gqa_q_norm": gain(13, (DEPTH, GQA_HEAD_DIM)),
        "gqa_k_norm": gain(14, (DEPTH, GQA_HEAD_DIM)),
        "mla_q_norm": gain(15, (DEPTH, MLA_Q_RANK)),
        "mla_kv_norm": gain(16, (DEPTH, MLA_KV_RANK)),
        "mla_w_q_up": nrm(17, (DEPTH, MLA_Q_RANK, MLA_HEADS * (MLA_NOPE_DIM + MLA_ROPE_DIM)), MLA_Q_RANK ** -0.5),
        "mla_w_kv_up": nrm(18, (DEPTH, MLA_KV_RANK, MLA_HEADS * (MLA_NOPE_DIM + MLA_V_DIM)), MLA_KV_RANK ** -0.5),
        "w_branch": nrm(19, (DEPTH, N_BRANCHES, BRANCH_WIDTH, d), BRANCH_WIDTH ** -0.5),
        "w_gate": nrm(20, (DEPTH, N_BRANCHES, d, d), d ** -0.5),
        "b_gate": nrm(21, (DEPTH, N_BRANCHES, d), 0.01),
        "w_out": nrm(22, (DEPTH, d, d), d ** -0.5),
        "dense_w_gate_up": nrm(23, (n_dense, d, 2 * FFN_DIM), d ** -0.5),
        "dense_w_down": nrm(24, (n_dense, FFN_DIM, d), FFN_DIM ** -0.5),
        "moe_router": nrm(25, (n_moe, d, N_EXPERTS), d ** -0.5),
        "moe_w_gate_up": nrm(26, (n_moe, N_EXPERTS, d, 2 * EXPERT_DIM), d ** -0.5),
        "moe_w_down": nrm(27, (n_moe, N_EXPERTS, EXPERT_DIM, d), EXPERT_DIM ** -0.5),
    }


def reference(x, c, ctx, c_ctx, w_mod, b_mod, mix_pre_norm, mix_post_norm, ffn_pre_norm, ffn_post_norm,
              w_in, diff_lambda, diff_subnorm, gqa_q_norm, gqa_k_norm, mla_q_norm, mla_kv_norm,
              mla_w_q_up, mla_w_kv_up, w_branch, w_gate, b_gate, w_out,
              dense_w_gate_up, dense_w_down, moe_router, moe_w_gate_up, moe_w_down):
    n_lat = x.shape[1]
    cos_h, sin_h = axial_rope_tables(n_lat, GQA_HEAD_DIM)
    cos_r, sin_r = axial_rope_tables(n_lat, MLA_ROPE_DIM)
    rope = (cos_h, sin_h, cos_r, sin_r)
    for layer in range(DEPTH):
        last = layer == DEPTH - 1
        lp = {
            "w_in": w_in[layer], "diff_lambda": diff_lambda[layer], "diff_subnorm": diff_subnorm[layer],
            "gqa_q_norm": gqa_q_norm[layer], "gqa_k_norm": gqa_k_norm[layer],
            "mla_q_norm": mla_q_norm[layer], "mla_kv_norm": mla_kv_norm[layer],
            "mla_w_q_up": mla_w_q_up[layer], "mla_w_kv_up": mla_w_kv_up[layer],
            "w_branch": w_branch[layer], "w_gate": w_gate[layer], "b_gate": b_gate[layer], "w_out": w_out[layer],
        }
        sh1, sc1, g1, sh2, sc2, g2 = [m[:, None, :] for m in jnp.split(jax.nn.silu(c) @ w_mod[layer] + b_mod[layer], N_MOD, axis=-1)]
        csh1, csc1, cg1, csh2, csc2, cg2 = jnp.split(jax.nn.silu(c_ctx) @ w_mod[layer] + b_mod[layer], N_MOD, axis=-1)
        if layer % 2 == 0:
            ffn = functools.partial(swiglu, w_gate_up=dense_w_gate_up[layer // 2], w_down=dense_w_down[layer // 2])
        else:
            ffn = functools.partial(moe_ffn, router=moe_router[layer // 2], w_gate_up=moe_w_gate_up[layer // 2],
                                    w_down=moe_w_down[layer // 2])
        h_lat = rms_norm(x, mix_pre_norm[layer]) * (1 + sc1) + sh1
        h_ctx = rms_norm(ctx, mix_pre_norm[layer]) * (1 + csc1) + csh1
        m_lat, m_ctx = token_mixing(h_lat, h_ctx, lp, layer, rope, not last)
        x = x + g1 * rms_norm(m_lat, mix_post_norm[layer])
        x = x + g2 * rms_norm(ffn(rms_norm(x, ffn_pre_norm[layer]) * (1 + sc2) + sh2), ffn_post_norm[layer])
        if not last:
            ctx = ctx + cg1 * rms_norm(m_ctx, mix_post_norm[layer])
            ctx = ctx + cg2 * rms_norm(ffn(rms_norm(ctx, ffn_pre_norm[layer]) * (1 + csc2) + csh2), ffn_post_norm[layer])
    return x
```

```python
import functools
import math

import jax
import jax.numpy as jnp
from jax import lax
from jax.experimental import pallas as pl
from jax.experimental.pallas import tpu as pltpu

F32 = jnp.float32
BF16 = jnp.bfloat16

D_MODEL = 1024
GRID_W = 64
ROPE_BASE = 10000.0
EPS = 1e-6
N_MOD = 6
HEAD_DIM = 64
DIFF_HEADS = 4
GQA_Q_HEADS = 8
GQA_KV_HEADS = 2
MLA_HEADS = 8
MLA_Q_RANK = 256
MLA_KV_RANK = 128
MLA_NOPE = 64
MLA_ROPE = 32
MLA_V = 64
BRANCH_W = 512
FOURIER_GROUP = 128
FFN_DIM = 2816
N_EXPERTS = 8
EXPERT_DIM = 3584
QK_SCALE = HEAD_DIM ** -0.5
MLA_SCALE = (MLA_NOPE + MLA_ROPE) ** -0.5
IN_COLS_PADDED = 3328

LANES = 128
VMEM_LIMIT = 56 * 1024 * 1024

TM = 256


def _cparams(sem):
    return pltpu.CompilerParams(dimension_semantics=sem, vmem_limit_bytes=VMEM_LIMIT)


def _full_spec(shape):
    n = len(shape)
    return pl.BlockSpec(shape, lambda *_: (0,) * n)


def _split_bf16(x):
    hi = x.astype(BF16)
    lo = (x - hi.astype(F32)).astype(BF16)
    return hi, lo


def _dot(a, b):
    return jnp.dot(a, b, preferred_element_type=F32)


def _dot3(a, w):
    ah, al = _split_bf16(a)
    wh, wl = _split_bf16(w)
    return _dot(ah, wh) + _dot(ah, wl) + _dot(al, wh)


def _rms(x, gain):
    ms = jnp.mean(x * x, axis=-1, keepdims=True)
    return x * lax.rsqrt(ms + EPS) * gain


def _mod_kernel(c_ref, w_ref, b_ref, o_ref):
    c = c_ref[...]
    a = c * jax.nn.sigmoid(c)
    o_ref[...] = _dot3(a, w_ref[...]) + b_ref[...]


def _modulation(c_rows, w_mod_l, b_mod_l):
    rows = c_rows.shape[0]
    n = w_mod_l.shape[1]
    tn = 512
    return pl.pallas_call(
        _mod_kernel,
        grid=(n // tn,),
        in_specs=[pl.BlockSpec((rows, D_MODEL), lambda j: (0, 0)),
                  pl.BlockSpec((D_MODEL, tn), lambda j: (0, j)),
                  pl.BlockSpec((1, tn), lambda j: (0, j))],
        out_specs=pl.BlockSpec((rows, tn), lambda j: (0, j)),
        out_shape=jax.ShapeDtypeStruct((rows, n), F32),
        compiler_params=_cparams(("arbitrary",)),
        name="modulation",
    )(c_rows, w_mod_l, b_mod_l.reshape(1, n))


def _rope(x, cos, sin_signed, half):
    lane = lax.broadcasted_iota(jnp.int32, x.shape, 1)
    first = (lane % (2 * half)) < half
    up = pltpu.roll(x, LANES - half, 1)
    down = pltpu.roll(x, half, 1)
    return x * cos + jnp.where(first, up, down) * sin_signed


def _proj_kernel(x_ref, mod_ref, gpre_ref, win_ref, gqn_ref, gkn_ref, mqn_ref, mkvn_ref,
                 wqup_ref, wkup_ref, wvup_ref, seg_ref, cc_ref, sc_ref,
                 cosh_ref, sinh_ref, cosm_ref, sinm_ref,
                 h_ref, dq_ref, dk_ref, dv_ref, gq_ref, gk_ref, gv_ref, mq_ref, mk_ref, mv_ref, pq_ref):
    x = x_ref[...]
    mod = mod_ref[...]
    h = _rms(x, gpre_ref[...]) * (1.0 + mod[1:2]) + mod[0:1]
    hb = h.astype(BF16)
    h_ref[...] = hb
    z = _dot(hb, win_ref[...])
    cosh, sinh = cosh_ref[...], sinh_ref[...]
    cosm, sinm = cosm_ref[...], sinm_ref[...]
    seg = seg_ref[...]

    def head_rms(v, gain, w):
        sq = v * v
        hi, lo = _split_bf16(sq)
        ms = _dot(hi, seg[:w, :w]) + _dot(lo, seg[:w, :w])
        return v * lax.rsqrt(ms + EPS) * gain

    for j in range(4):
        c0 = j * LANES
        dq_ref[:, c0:c0 + LANES] = (_rope(z[:, c0:c0 + LANES], cosh, sinh, 16) * QK_SCALE).astype(BF16)
        dk_ref[:, c0:c0 + LANES] = _rope(z[:, 512 + c0:512 + c0 + LANES], cosh, sinh, 16).astype(BF16)
    dv_ref[...] = z[:, 1024:1536].astype(BF16)
    for g in range(4):
        c0 = 1536 + g * LANES
        fb = z[:, c0:c0 + LANES].astype(BF16)
        pq_ref[0, :, g * LANES:(g + 1) * LANES] = _dot(fb, cc_ref[...]).astype(BF16)
        pq_ref[1, :, g * LANES:(g + 1) * LANES] = _dot(fb, sc_ref[...]).astype(BF16)
    gq = head_rms(z[:, 2048:2560], gqn_ref[...], 512)
    for j in range(4):
        c0 = j * LANES
        gq_ref[:, c0:c0 + LANES] = (_rope(gq[:, c0:c0 + LANES], cosh, sinh, 16) * QK_SCALE).astype(BF16)
    gk = head_rms(z[:, 2560:2688], gkn_ref[...], LANES)
    gk_ref[...] = _rope(gk, cosh, sinh, 16).astype(BF16)
    gv_ref[...] = z[:, 2688:2816].astype(BF16)
    cq = _rms(z[:, 2816:3072], mqn_ref[...]).astype(BF16)
    mq = _dot(cq, wqup_ref[...])
    ckv = _rms(z[:, 3072:3200], mkvn_ref[...]).astype(BF16)
    mk = _dot(ckv, wkup_ref[...])
    mv_ref[...] = _dot(ckv, wvup_ref[...]).astype(BF16)
    kr = _rope(z[:, 3200:3328], cosm, sinm, 8)
    for hh in range(MLA_HEADS):
        c0 = hh * LANES
        mq_ref[:, c0:c0 + LANES] = (_rope(mq[:, c0:c0 + LANES], cosm, sinm, 8) * MLA_SCALE).astype(BF16)
        mk_ref[:, c0:c0 + LANES] = (mk[:, c0:c0 + LANES] + kr).astype(BF16)


def _project(x_all, mods, gpre, wts, consts, n_lat_tiles):
    B, T, _ = x_all.shape
    nt = T // TM
    tok = lambda w: pl.BlockSpec((None, TM, w), lambda b, t: (b, t, 0))
    tab = pl.BlockSpec((TM, LANES), lambda b, t: (t, 0))
    in_specs = [
        tok(D_MODEL),
        pl.BlockSpec((None, None, N_MOD, D_MODEL), lambda b, t: (b, jnp.where(t >= n_lat_tiles, 1, 0), 0, 0)),
        _full_spec((1, D_MODEL)),
        _full_spec((D_MODEL, IN_COLS_PADDED)),
        _full_spec((1, 512)), _full_spec((1, LANES)), _full_spec((1, MLA_Q_RANK)), _full_spec((1, MLA_KV_RANK)),
        _full_spec((MLA_Q_RANK, MLA_HEADS * LANES)),
        _full_spec((MLA_KV_RANK, MLA_HEADS * LANES)),
        _full_spec((MLA_KV_RANK, MLA_HEADS * MLA_V)),
        _full_spec((512, 512)), _full_spec((LANES, LANES)), _full_spec((LANES, LANES)),
        tab, tab, tab, tab,
    ]
    widths = [D_MODEL, 512, 512, 512, 512, LANES, LANES, MLA_HEADS * LANES, MLA_HEADS * LANES, 512]
    out_specs = [tok(w) for w in widths]
    out_shape = [jax.ShapeDtypeStruct((B, T, w), BF16) for w in widths]
    out_specs.append(pl.BlockSpec((2, TM, 512), lambda b, t: (0, t, b)))
    out_shape.append(jax.ShapeDtypeStruct((2, T, B * 512), BF16))
    return pl.pallas_call(
        _proj_kernel,
        grid=(B, nt),
        in_specs=in_specs,
        out_specs=out_specs,
        out_shape=out_shape,
        compiler_params=_cparams(("parallel", "arbitrary")),
        name="project",
    )(x_all, mods, gpre, wts["w_in"], wts["gq_norm"], wts["gk_norm"], wts["mq_norm"], wts["mkv_norm"],
      wts["w_q_up"], wts["w_k_up"], wts["w_v_up"], consts["seg"], consts["cc"], consts["sc"],
      consts["cos_h"], consts["sin_h"], consts["cos_m"], consts["sin_m"])


def _attn_kernel(mode, lam_init, q_ref, k_ref, v_ref, *rest):
    if mode == "diff":
        lam_ref, sub_ref, _, o_ref = rest
    else:
        _, o_ref = rest
    q = q_ref[...]
    v = v_ref[...]
    lane = lax.broadcasted_iota(jnp.int32, (1, LANES), 1)
    low = lane < HEAD_DIM
    if mode == "mla":
        qa, qb = q[:, :LANES], q[:, LANES:]
        ka, kb = k_ref[:, :LANES], k_ref[:, LANES:]
    else:
        zero = jnp.zeros_like(q)
        qa, qb = jnp.where(low, q, zero), jnp.where(low, zero, q)
        ka = kb = k_ref[...]

    def softmax_pv(qx, kx):
        s = lax.dot_general(qx, kx, (((1,), (1,)), ((), ())), preferred_element_type=F32)
        m = jnp.max(s, axis=-1, keepdims=True)
        p = jnp.exp(s - m)
        l = jnp.sum(p, axis=-1, keepdims=True)
        return _dot(p.astype(BF16), v) / l

    ya = softmax_pv(qa, ka)
    yb = softmax_pv(qb, kb)
    if mode == "diff":
        dl = lam_ref[...]
        lam = (jnp.exp(jnp.sum(dl[0:1] * dl[1:2], axis=-1, keepdims=True))
               - jnp.exp(jnp.sum(dl[2:3] * dl[3:4], axis=-1, keepdims=True)) + lam_init)
        y = _rms(ya - lam * yb, sub_ref[...]) * (1.0 - lam_init)
    else:
        y = jnp.where(low, ya, yb)
    o_ref[...] = y.astype(BF16)


def _attention(mode, q, k, v, y_prev, *, tq, nq, q_blk0, tk, k_blk, lam_init=0.0, extra=()):
    B, T, _ = q.shape
    wq = 2 * LANES if mode == "mla" else LANES
    wk = wq
    jk = (lambda j: 0) if mode == "gqa" else (lambda j: j)
    in_specs = [
        pl.BlockSpec((None, tq, wq), lambda b, j, i: (b, i + q_blk0, j)),
        pl.BlockSpec((None, tk, wk), lambda b, j, i: (b, k_blk, jk(j))),
        pl.BlockSpec((None, tk, LANES), lambda b, j, i: (b, k_blk, jk(j))),
    ]
    args = [q, k, v]
    if mode == "diff":
        in_specs += [_full_spec((4, HEAD_DIM)), _full_spec((1, LANES))]
        args += list(extra)
    in_specs.append(pl.BlockSpec(memory_space=pl.ANY))
    args.append(y_prev)
    return pl.pallas_call(
        functools.partial(_attn_kernel, mode, lam_init),
        grid=(B, 4, nq),
        in_specs=in_specs,
        out_specs=pl.BlockSpec((None, tq, LANES), lambda b, j, i: (b, i + q_blk0, j)),
        out_shape=jax.ShapeDtypeStruct((B, T, BRANCH_W), BF16),
        input_output_aliases={len(args) - 1: 0},
        compiler_params=_cparams(("parallel", "arbitrary", "arbitrary")),
        name="attention_" + mode,
    )(*args)


def _mm_kernel(a_ref, b_ref, o_ref, acc_ref):
    k = pl.program_id(2)

    @pl.when(k == 0)
    def _():
        acc_ref[...] = jnp.zeros_like(acc_ref)

    acc_ref[...] += _dot(a_ref[...], b_ref[...])

    @pl.when(k == pl.num_programs(2) - 1)
    def _():
        o_ref[...] = acc_ref[...].astype(o_ref.dtype)


def _seq_dft(w, pq, *, rows, row_blk0, tm, tn, tk):
    m, kk = w.shape
    n = pq.shape[2]
    kh = rows // tk
    return pl.pallas_call(
        _mm_kernel,
        grid=(m // tm, n // tn, kk // tk),
        in_specs=[pl.BlockSpec((tm, tk), lambda i, j, k: (i, k)),
                  pl.BlockSpec((None, tk, tn), lambda i, j, k: (k // kh, row_blk0 + k % kh, j))],
        out_specs=pl.BlockSpec((tm, tn), lambda i, j, k: (i, j)),
        out_shape=jax.ShapeDtypeStruct((m, n), BF16),
        scratch_shapes=[pltpu.VMEM((tm, tn), F32)],
        compiler_params=_cparams(("parallel", "parallel", "arbitrary")),
        name="sequence_dft",
    )(w, pq)


def _merge_kernel(x_ref, h_ref, y0_ref, y1_ref, y2_ref, y3_ref, mod_ref, wg_ref, bg_ref, wb_ref, wo_ref,
                  gpost_ref, o_ref):
    h = h_ref[...]
    ys = (y0_ref, y1_ref, y2_ref, y3_ref)
    acc = None
    for i in range(4):
        gate = jax.nn.sigmoid(_dot(h, wg_ref[i]) + bg_ref[i:i + 1])
        term = gate * _dot(ys[i][...], wb_ref[i])
        acc = term if acc is None else acc + term
    m = _dot(acc.astype(BF16), wo_ref[...])
    mod = mod_ref[...]
    o_ref[...] = x_ref[...] + mod[2:3] * _rms(m, gpost_ref[...])


def _merge(x_all, h, y_diff, y_f, y_gqa, y_mla, mods, wts, gpost, n_lat_tiles, n_tiles):
    B, T, _ = x_all.shape
    tok = lambda w: pl.BlockSpec((None, TM, w), lambda b, t: (b, t, 0))
    in_specs = [
        tok(D_MODEL), tok(D_MODEL), tok(BRANCH_W),
        pl.BlockSpec((TM, BRANCH_W), lambda b, t: (t, b)),
        tok(BRANCH_W), tok(BRANCH_W),
        pl.BlockSpec((None, None, N_MOD, D_MODEL), lambda b, t: (b, jnp.where(t >= n_lat_tiles, 1, 0), 0, 0)),
        _full_spec((4, D_MODEL, D_MODEL)), _full_spec((4, D_MODEL)), _full_spec((4, BRANCH_W, D_MODEL)),
        _full_spec((D_MODEL, D_MODEL)), _full_spec((1, D_MODEL)),
    ]
    return pl.pallas_call(
        _merge_kernel,
        grid=(B, n_tiles),
        in_specs=in_specs,
        out_specs=tok(D_MODEL),
        out_shape=jax.ShapeDtypeStruct((B, n_tiles * TM, D_MODEL), F32),
        compiler_params=_cparams(("parallel", "arbitrary")),
        name="merge",
    )(x_all, h, y_diff, y_f, y_gqa, y_mla, mods, wts["w_gate"], wts["b_gate"], wts["w_branch"], wts["w_out"],
      gpost)


def _dense_ffn_kernel(nf, tf, x_ref, mod_ref, gpre_ref, gpost_ref, wgu_ref, wd_ref, o_ref):
    x = x_ref[...]
    mod = mod_ref[...]
    hb = (_rms(x, gpre_ref[...]) * (1.0 + mod[4:5]) + mod[3:4]).astype(BF16)
    acc = None
    for f in range(nf):
        a = _dot(hb, wgu_ref[:, f * tf:(f + 1) * tf])
        u = _dot(hb, wgu_ref[:, FFN_DIM + f * tf:FFN_DIM + (f + 1) * tf])
        act = (a * jax.nn.sigmoid(a) * u).astype(BF16)
        part = _dot(act, wd_ref[f * tf:(f + 1) * tf, :])
        acc = part if acc is None else acc + part
    o_ref[...] = x + mod[5:6] * _rms(acc, gpost_ref[...])


def _dense_ffn(x_all, mods, gpre, gpost, wgu, wd, n_lat_tiles, n_tiles):
    B = x_all.shape[0]
    nf, tf = 2, FFN_DIM // 2
    tok = pl.BlockSpec((None, TM, D_MODEL), lambda b, t: (b, t, 0))
    return pl.pallas_call(
        functools.partial(_dense_ffn_kernel, nf, tf),
        grid=(B, n_tiles),
        in_specs=[tok,
                  pl.BlockSpec((None, None, N_MOD, D_MODEL),
                               lambda b, t: (b, jnp.where(t >= n_lat_tiles, 1, 0), 0, 0)),
                  _full_spec((1, D_MODEL)), _full_spec((1, D_MODEL)),
                  _full_spec((D_MODEL, 2 * FFN_DIM)), _full_spec((FFN_DIM, D_MODEL))],
        out_specs=tok,
        out_shape=jax.ShapeDtypeStruct((B, n_tiles * TM, D_MODEL), F32),
        compiler_params=_cparams(("parallel", "arbitrary")),
        name="dense_ffn",
    )(x_all, mods, gpre, gpost, wgu, wd)


def _moe_pre_kernel(x_ref, mod_ref, gpre_ref, wr_ref, hb_ref, comb_ref):
    x = x_ref[...]
    mod = mod_ref[...]
    h = _rms(x, gpre_ref[...]) * (1.0 + mod[4:5]) + mod[3:4]
    hb_ref[...] = h.astype(BF16)
    logits = _dot3(h, wr_ref[...])
    lane = lax.broadcasted_iota(jnp.int32, logits.shape, 1)
    neg = jnp.float32(-jnp.inf)
    l1 = jnp.where(lane < N_EXPERTS, logits, neg)
    m1 = jnp.max(l1, axis=-1, keepdims=True)
    i1 = jnp.min(jnp.where(l1 == m1, lane, LANES), axis=-1, keepdims=True)
    l2 = jnp.where(lane == i1, neg, l1)
    m2 = jnp.max(l2, axis=-1, keepdims=True)
    i2 = jnp.min(jnp.where(l2 == m2, lane, LANES), axis=-1, keepdims=True)
    e2 = jnp.exp(m2 - m1)
    g1 = 1.0 / (1.0 + e2)
    g2 = e2 / (1.0 + e2)
    comb_ref[...] = jnp.where(lane == i1, g1, 0.0) + jnp.where(lane == i2, g2, 0.0)


def _moe_pre(x_all, mods, gpre, w_router, n_lat_tiles, n_tiles):
    B = x_all.shape[0]
    tok = lambda w: pl.BlockSpec((None, TM, w), lambda b, t: (b, t, 0))
    return pl.pallas_call(
        _moe_pre_kernel,
        grid=(B, n_tiles),
        in_specs=[tok(D_MODEL),
                  pl.BlockSpec((None, None, N_MOD, D_MODEL),
                               lambda b, t: (b, jnp.where(t >= n_lat_tiles, 1, 0), 0, 0)),
                  _full_spec((1, D_MODEL)), _full_spec((D_MODEL, LANES))],
        out_specs=[tok(D_MODEL), tok(LANES)],
        out_shape=[jax.ShapeDtypeStruct((B, n_tiles * TM, D_MODEL), BF16),
                   jax.ShapeDtypeStruct((B, n_tiles * TM, LANES), F32)],
        compiler_params=_cparams(("parallel", "arbitrary")),
        name="moe_router",
    )(x_all, mods, gpre, w_router)


def _moe_ffn_kernel(h_ref, comb_ref, wa_ref, wu_ref, wd_ref, o_ref, acc_ref):
    e = pl.program_id(1)
    f = pl.program_id(2)

    @pl.when(jnp.logical_and(e == 0, f == 0))
    def _():
        acc_ref[...] = jnp.zeros_like(acc_ref)

    h = h_ref[...]
    comb = comb_ref[...]
    lane = lax.broadcasted_iota(jnp.int32, comb.shape, 1)
    ce = jnp.sum(jnp.where(lane == e, comb, 0.0), axis=-1, keepdims=True)
    a = _dot(h, wa_ref[...])
    u = _dot(h, wu_ref[...])
    act = (ce * (a * jax.nn.sigmoid(a) * u)).astype(BF16)
    acc_ref[...] += _dot(act, wd_ref[...])

    @pl.when(jnp.logical_and(e == pl.num_programs(1) - 1, f == pl.num_programs(2) - 1))
    def _():
        o_ref[...] = acc_ref[...]


def _moe_ffn(hb, comb, wgu, wd):
    n = hb.shape[0]
    tm = 1024 if n % 1024 == 0 else TM
    nf = 2
    tf = EXPERT_DIM // nf
    return pl.pallas_call(
        _moe_ffn_kernel,
        grid=(n // tm, N_EXPERTS, nf),
        in_specs=[pl.BlockSpec((tm, D_MODEL), lambda i, e, f: (i, 0)),
                  pl.BlockSpec((tm, LANES), lambda i, e, f: (i, 0)),
                  pl.BlockSpec((None, D_MODEL, tf), lambda i, e, f: (e, 0, f)),
                  pl.BlockSpec((None, D_MODEL, tf), lambda i, e, f: (e, 0, nf + f)),
                  pl.BlockSpec((None, tf, D_MODEL), lambda i, e, f: (e, f, 0))],
        out_specs=pl.BlockSpec((tm, D_MODEL), lambda i, e, f: (i, 0)),
        out_shape=jax.ShapeDtypeStruct((n, D_MODEL), F32),
        scratch_shapes=[pltpu.VMEM((tm, D_MODEL), F32)],
        compiler_params=_cparams(("parallel", "arbitrary", "arbitrary")),
        name="moe_experts",
    )(hb, comb, wgu, wgu, wd)


def _post_kernel(x_ref, y_ref, mod_ref, gpost_ref, o_ref):
    mod = mod_ref[...]
    o_ref[...] = x_ref[...] + mod[5:6] * _rms(y_ref[...], gpost_ref[...])


def _ffn_post(x_all, y, mods, gpost, n_lat_tiles, n_tiles):
    B = x_all.shape[0]
    tok = pl.BlockSpec((None, TM, D_MODEL), lambda b, t: (b, t, 0))
    return pl.pallas_call(
        _post_kernel,
        grid=(B, n_tiles),
        in_specs=[tok, tok,
                  pl.BlockSpec((None, None, N_MOD, D_MODEL),
                               lambda b, t: (b, jnp.where(t >= n_lat_tiles, 1, 0), 0, 0)),
                  _full_spec((1, D_MODEL))],
        out_specs=tok,
        out_shape=jax.ShapeDtypeStruct((B, n_tiles * TM, D_MODEL), F32),
        compiler_params=_cparams(("parallel", "arbitrary")),
        name="ffn_post",
    )(x_all, y, mods, gpost)


def _rope_angles(n_lat, rot_dim):
    rows = n_lat // GRID_W
    row = jnp.repeat(jnp.arange(rows, dtype=F32), GRID_W)
    col = jnp.tile(jnp.arange(GRID_W, dtype=F32), rows)
    axis_dim = rot_dim // 2
    inv_freq = 1.0 / (ROPE_BASE ** (jnp.arange(0, axis_dim, 2, dtype=F32) / axis_dim))
    ar = row[:, None] * inv_freq
    ac = col[:, None] * inv_freq
    return jnp.concatenate([ar, ar, ac, ac], axis=-1)


def _constants(n_lat, n_ctx):
    lane = jnp.arange(LANES)
    ang = _rope_angles(n_lat, HEAD_DIM)
    cos_h = jnp.tile(jnp.cos(ang), (1, 2))
    sin_h = jnp.tile(jnp.sin(ang), (1, 2)) * jnp.where((lane % 32) < 16, -1.0, 1.0)
    cos_h = jnp.concatenate([cos_h, jnp.ones((n_ctx, LANES), F32)], axis=0)
    sin_h = jnp.concatenate([sin_h, jnp.zeros((n_ctx, LANES), F32)], axis=0)
    angm = _rope_angles(n_lat, MLA_ROPE)
    pad_l = jnp.zeros((n_lat, MLA_NOPE), F32)
    pad_r = jnp.zeros((n_lat, LANES - MLA_NOPE - MLA_ROPE), F32)
    cos_m = jnp.concatenate([pad_l + 1.0, jnp.cos(angm), pad_r + 1.0], axis=1)
    sin_m = jnp.concatenate([pad_l, jnp.sin(angm), pad_r], axis=1) * jnp.where((lane % 16) < 8, -1.0, 1.0)
    cos_m = jnp.concatenate([cos_m, jnp.ones((n_ctx, LANES), F32)], axis=0)
    sin_m = jnp.concatenate([sin_m, jnp.zeros((n_ctx, LANES), F32)], axis=0)
    idx = jnp.arange(512)
    seg = jnp.where((idx[:, None] // HEAD_DIM) == (idx[None, :] // HEAD_DIM), 1.0 / HEAD_DIM, 0.0).astype(BF16)

    def dft(n):
        i = jnp.arange(n, dtype=jnp.int32)
        a = ((i[:, None] * i[None, :]) % n).astype(F32) * (2.0 * math.pi / n)
        s = n ** -0.5
        return jnp.cos(a) * s, jnp.sin(a) * s

    cc, sc = dft(FOURIER_GROUP)
    cl, sl = dft(n_lat)
    cx, sx = dft(n_ctx)
    return {
        "cos_h": cos_h, "sin_h": sin_h, "cos_m": cos_m, "sin_m": sin_m, "seg": seg,
        "cc": cc.astype(BF16), "sc": sc.astype(BF16),
        "w_lat": jnp.concatenate([cl, -sl], axis=1).astype(BF16),
        "w_ctx": jnp.concatenate([cx, -sx], axis=1).astype(BF16),
    }


_GQA_HEAD_ORDER = (0, 4, 1, 5, 2, 6, 3, 7)


def _layer_weights(l, w_in, gqa_q_norm, gqa_k_norm, mla_q_norm, mla_kv_norm, mla_w_q_up, mla_w_kv_up,
                   w_branch, w_gate, b_gate, w_out):
    wi = w_in[l]
    gq = wi[:, 2048:2560].reshape(D_MODEL, GQA_Q_HEADS, HEAD_DIM)[:, jnp.array(_GQA_HEAD_ORDER)]
    zeros = lambda n: jnp.zeros((D_MODEL, n), F32)
    w_in_p = jnp.concatenate([wi[:, :2048], gq.reshape(D_MODEL, 512), wi[:, 2560:3200],
                              zeros(MLA_NOPE), wi[:, 3200:3232], zeros(LANES - MLA_NOPE - MLA_ROPE)], axis=1)
    qu = mla_w_q_up[l].reshape(MLA_Q_RANK, MLA_HEADS, MLA_NOPE + MLA_ROPE)
    qu = jnp.pad(qu, ((0, 0), (0, 0), (0, LANES - MLA_NOPE - MLA_ROPE))).reshape(MLA_Q_RANK, MLA_HEADS * LANES)
    kvu = mla_w_kv_up[l].reshape(MLA_KV_RANK, MLA_HEADS, MLA_NOPE + MLA_V)
    ku = jnp.pad(kvu[:, :, :MLA_NOPE], ((0, 0), (0, 0), (0, LANES - MLA_NOPE))).reshape(MLA_KV_RANK, -1)
    vu = kvu[:, :, MLA_NOPE:].reshape(MLA_KV_RANK, MLA_HEADS * MLA_V)
    wb = w_branch[l]
    wb_gqa = wb[2].reshape(GQA_Q_HEADS, HEAD_DIM, D_MODEL)[jnp.array(_GQA_HEAD_ORDER)].reshape(BRANCH_W, D_MODEL)
    wb = jnp.stack([wb[0], wb[1], wb_gqa, wb[3]])
    return {
        "w_in": w_in_p.astype(BF16),
        "gq_norm": jnp.tile(gqa_q_norm[l], GQA_Q_HEADS).reshape(1, 512),
        "gk_norm": jnp.tile(gqa_k_norm[l], GQA_KV_HEADS).reshape(1, LANES),
        "mq_norm": mla_q_norm[l].reshape(1, MLA_Q_RANK),
        "mkv_norm": mla_kv_norm[l].reshape(1, MLA_KV_RANK),
        "w_q_up": qu.astype(BF16), "w_k_up": ku.astype(BF16), "w_v_up": vu.astype(BF16),
        "w_branch": wb.astype(BF16), "w_gate": w_gate[l].astype(BF16), "b_gate": b_gate[l],
        "w_out": w_out[l].astype(BF16),
    }


def kernel(x, c, ctx, c_ctx, w_mod, b_mod, mix_pre_norm, mix_post_norm, ffn_pre_norm, ffn_post_norm,
           w_in, diff_lambda, diff_subnorm, gqa_q_norm, gqa_k_norm, mla_q_norm, mla_kv_norm,
           mla_w_q_up, mla_w_kv_up, w_branch, w_gate, b_gate, w_out,
           dense_w_gate_up, dense_w_down, moe_router, moe_w_gate_up, moe_w_down):
    B, n_lat, d = x.shape
    n_ctx = ctx.shape[1]
    depth = w_mod.shape[0]
    assert d == D_MODEL and n_lat % TM == 0 and n_ctx % TM == 0 and n_lat % n_ctx == 0
    assert n_lat % GRID_W == 0
    T = n_lat + n_ctx
    n_lat_tiles, n_all_tiles = n_lat // TM, T // TM
    consts = _constants(n_lat, n_ctx)
    x_all = jnp.concatenate([x, ctx], axis=1)
    c_rows = jnp.pad(jnp.concatenate([c, c_ctx[None]], axis=0), ((0, (-(B + 1)) % 8), (0, 0)))
    row = lambda v: v.reshape(1, -1)

    for layer in range(depth):
        last = layer == depth - 1
        n_tiles = n_lat_tiles if last else n_all_tiles
        lam_init = 0.8 - 0.6 * math.exp(-0.3 * layer)
        wts = _layer_weights(layer, w_in, gqa_q_norm, gqa_k_norm, mla_q_norm, mla_kv_norm, mla_w_q_up,
                             mla_w_kv_up, w_branch, w_gate, b_gate, w_out)
        mod_rows = _modulation(c_rows, w_mod[layer], b_mod[layer])
        mods = jnp.stack([mod_rows[:B].reshape(B, N_MOD, D_MODEL),
                          jnp.broadcast_to(mod_rows[B].reshape(1, N_MOD, D_MODEL), (B, N_MOD, D_MODEL))], axis=1)

        (h, dq, dk, dv, gq, gk, gv, mq, mk, mv, pq) = _project(
            x_all, mods, row(mix_pre_norm[layer]), wts, consts, n_lat_tiles)

        diff_extra = (diff_lambda[layer], diff_subnorm[layer].reshape(1, LANES))
        ys = []
        for mode, q, k, v in (("diff", dq, dk, dv), ("gqa", gq, gk, gv), ("mla", mq, mk, mv)):
            y = jnp.zeros((B, T, BRANCH_W), BF16)
            kw = dict(lam_init=lam_init, extra=diff_extra)
            y = _attention(mode, q, k, v, y, tq=TM, nq=n_lat_tiles, q_blk0=0, tk=T, k_blk=0, **kw)
            if not last:
                y = _attention(mode, q, k, v, y, tq=TM, nq=n_ctx // TM, q_blk0=n_lat_tiles,
                               tk=n_ctx, k_blk=n_lat // n_ctx, **kw)
            ys.append(y)
        y_diff, y_gqa, y_mla = ys

        y_f = _seq_dft(consts["w_lat"], pq, rows=n_lat, row_blk0=0,
                       tm=min(1024, n_lat), tn=min(2048, B * 512), tk=min(512, n_lat))
        if not last:
            y_fc = _seq_dft(consts["w_ctx"], pq, rows=n_ctx, row_blk0=n_lat // n_ctx,
                            tm=n_ctx, tn=min(2048, B * 512), tk=n_ctx)
            y_f = jnp.concatenate([y_f, y_fc], axis=0)

        x_all = _merge(x_all, h, y_diff, y_f, y_gqa, y_mla, mods, wts, row(mix_post_norm[layer]),
                       n_lat_tiles, n_tiles)

        if layer % 2 == 0:
            x_all = _dense_ffn(x_all, mods, row(ffn_pre_norm[layer]), row(ffn_post_norm[layer]),
                               dense_w_gate_up[layer // 2].astype(BF16), dense_w_down[layer // 2].astype(BF16),
                               n_lat_tiles, n_tiles)
        else:
            wr = jnp.pad(moe_router[layer // 2], ((0, 0), (0, LANES - N_EXPERTS)))
            hb, comb = _moe_pre(x_all, mods, row(ffn_pre_norm[layer]), wr, n_lat_tiles, n_tiles)
            n_tok = B * n_tiles * TM
            y = _moe_ffn(hb.reshape(n_tok, D_MODEL), comb.reshape(n_tok, LANES),
                         moe_w_gate_up[layer // 2].astype(BF16), moe_w_down[layer // 2].astype(BF16))
            x_all = _ffn_post(x_all, y.reshape(B, n_tiles * TM, D_MODEL), mods, row(ffn_post_norm[layer]),
                              n_lat_tiles, n_tiles)
    return x_all[:, :n_lat]
```

```python
import functools
import math

import jax
import jax.numpy as jnp
from jax import lax
from jax.experimental import pallas as pl
from jax.experimental.pallas import tpu as pltpu

F32 = jnp.float32
BF16 = jnp.bfloat16

D_MODEL = 1024
GRID_W = 64
ROPE_BASE = 10000.0
EPS = 1e-6
N_MOD = 6
HEAD_DIM = 64
DIFF_HEADS = 4
GQA_Q_HEADS = 8
GQA_KV_HEADS = 2
MLA_HEADS = 8
MLA_Q_RANK = 256
MLA_KV_RANK = 128
MLA_NOPE = 64
MLA_ROPE = 32
MLA_V = 64
BRANCH_W = 512
FOURIER_GROUP = 128
FFN_DIM = 2816
N_EXPERTS = 8
EXPERT_DIM = 3584
LOG2E = math.log2(math.e)
QK_SCALE = HEAD_DIM ** -0.5 * LOG2E
MLA_SCALE = (MLA_NOPE + MLA_ROPE) ** -0.5 * LOG2E
IN_COLS_PADDED = 3328

LANES = 128
VMEM_LIMIT = 56 * 1024 * 1024

TM = 256
TQ = 512


def _cparams(sem):
    return pltpu.CompilerParams(dimension_semantics=sem, vmem_limit_bytes=VMEM_LIMIT)


def _full_spec(shape):
    n = len(shape)
    return pl.BlockSpec(shape, lambda *_: (0,) * n)


def _split_bf16(x):
    hi = x.astype(BF16)
    lo = (x - hi.astype(F32)).astype(BF16)
    return hi, lo


def _dot(a, b):
    return jnp.dot(a, b, preferred_element_type=F32)


def _dot3(a, w):
    ah, al = _split_bf16(a)
    wh, wl = _split_bf16(w)
    return _dot(ah, wh) + _dot(ah, wl) + _dot(al, wh)


def _rms(x, gain):
    ms = jnp.mean(x * x, axis=-1, keepdims=True)
    return x * lax.rsqrt(ms + EPS) * gain


def _mod_kernel(c_ref, w_ref, b_ref, o_ref):
    c = c_ref[...]
    a = c * jax.nn.sigmoid(c)
    o_ref[...] = _dot3(a, w_ref[...]) + b_ref[...]


def _modulation(c_rows, w_mod_l, b_mod_l):
    rows = c_rows.shape[0]
    n = w_mod_l.shape[1]
    tn = 512
    return pl.pallas_call(
        _mod_kernel,
        grid=(n // tn,),
        in_specs=[pl.BlockSpec((rows, D_MODEL), lambda j: (0, 0)),
                  pl.BlockSpec((D_MODEL, tn), lambda j: (0, j)),
                  pl.BlockSpec((1, tn), lambda j: (0, j))],
        out_specs=pl.BlockSpec((rows, tn), lambda j: (0, j)),
        out_shape=jax.ShapeDtypeStruct((rows, n), F32),
        compiler_params=_cparams(("arbitrary",)),
        name="modulation",
    )(c_rows, w_mod_l, b_mod_l.reshape(1, n))


def _rope(x, cos, sin_signed, half):
    lane = lax.broadcasted_iota(jnp.int32, x.shape, 1)
    first = (lane % (2 * half)) < half
    up = pltpu.roll(x, LANES - half, 1)
    down = pltpu.roll(x, half, 1)
    return x * cos + jnp.where(first, up, down) * sin_signed


def _proj_kernel(x_ref, mod_ref, gpre_ref, win_ref, gqn_ref, gkn_ref, mqn_ref, mkvn_ref,
                 wqup_ref, wkup_ref, wvup_ref, seg_ref, cc_ref, sc_ref,
                 cosh_ref, sinh_ref, cosm_ref, sinm_ref,
                 h_ref, dq_ref, dk_ref, dv_ref, gq_ref, gk_ref, gv_ref, mq_ref, mk_ref, mv_ref, pq_ref):
    x = x_ref[...]
    mod = mod_ref[...]
    h = _rms(x, gpre_ref[...]) * (1.0 + mod[1:2]) + mod[0:1]
    hb = h.astype(BF16)
    h_ref[...] = hb
    z = _dot(hb, win_ref[...])
    cosh, sinh = cosh_ref[...], sinh_ref[...]
    cosm, sinm = cosm_ref[...], sinm_ref[...]
    seg = seg_ref[...]

    def head_rms(v, gain, w):
        sq = v * v
        hi, lo = _split_bf16(sq)
        ms = _dot(hi, seg[:w, :w]) + _dot(lo, seg[:w, :w])
        return v * lax.rsqrt(ms + EPS) * gain

    for j in range(4):
        c0 = j * LANES
        dq_ref[:, c0:c0 + LANES] = (_rope(z[:, c0:c0 + LANES], cosh, sinh, 16) * QK_SCALE).astype(BF16)
        dk_ref[:, c0:c0 + LANES] = _rope(z[:, 512 + c0:512 + c0 + LANES], cosh, sinh, 16).astype(BF16)
    dv_ref[...] = z[:, 1024:1536].astype(BF16)
    for g in range(4):
        c0 = 1536 + g * LANES
        fb = z[:, c0:c0 + LANES].astype(BF16)
        pq_ref[0, :, g * LANES:(g + 1) * LANES] = _dot(fb, cc_ref[...]).astype(BF16)
        pq_ref[1, :, g * LANES:(g + 1) * LANES] = _dot(fb, sc_ref[...]).astype(BF16)
    gq = head_rms(z[:, 2048:2560], gqn_ref[...], 512)
    for j in range(4):
        c0 = j * LANES
        gq_ref[:, c0:c0 + LANES] = (_rope(gq[:, c0:c0 + LANES], cosh, sinh, 16) * QK_SCALE).astype(BF16)
    gk = head_rms(z[:, 2560:2688], gkn_ref[...], LANES)
    gk_ref[...] = _rope(gk, cosh, sinh, 16).astype(BF16)
    gv_ref[...] = z[:, 2688:2816].astype(BF16)
    cq = _rms(z[:, 2816:3072], mqn_ref[...]).astype(BF16)
    mq = _dot(cq, wqup_ref[...])
    ckv = _rms(z[:, 3072:3200], mkvn_ref[...]).astype(BF16)
    mk = _dot(ckv, wkup_ref[...])
    mv_ref[...] = _dot(ckv, wvup_ref[...]).astype(BF16)
    kr = _rope(z[:, 3200:3328], cosm, sinm, 8)
    for hh in range(MLA_HEADS):
        c0 = hh * LANES
        mq_ref[:, c0:c0 + LANES] = (_rope(mq[:, c0:c0 + LANES], cosm, sinm, 8) * MLA_SCALE).astype(BF16)
        mk_ref[:, c0:c0 + LANES] = (mk[:, c0:c0 + LANES] + kr).astype(BF16)


def _project(x_all, mods, gpre, wts, consts, n_lat_tiles):
    B, T, _ = x_all.shape
    nt = T // TM
    tok = lambda w: pl.BlockSpec((None, TM, w), lambda b, t: (b, t, 0))
    tab = pl.BlockSpec((TM, LANES), lambda b, t: (t, 0))
    in_specs = [
        tok(D_MODEL),
        pl.BlockSpec((None, None, N_MOD, D_MODEL), lambda b, t: (b, jnp.where(t >= n_lat_tiles, 1, 0), 0, 0)),
        _full_spec((1, D_MODEL)),
        _full_spec((D_MODEL, IN_COLS_PADDED)),
        _full_spec((1, 512)), _full_spec((1, LANES)), _full_spec((1, MLA_Q_RANK)), _full_spec((1, MLA_KV_RANK)),
        _full_spec((MLA_Q_RANK, MLA_HEADS * LANES)),
        _full_spec((MLA_KV_RANK, MLA_HEADS * LANES)),
        _full_spec((MLA_KV_RANK, MLA_HEADS * MLA_V)),
        _full_spec((512, 512)), _full_spec((LANES, LANES)), _full_spec((LANES, LANES)),
        tab, tab, tab, tab,
    ]
    widths = [D_MODEL, 512, 512, 512, 512, LANES, LANES, MLA_HEADS * LANES, MLA_HEADS * LANES, 512]
    out_specs = [tok(w) for w in widths]
    out_shape = [jax.ShapeDtypeStruct((B, T, w), BF16) for w in widths]
    out_specs.append(pl.BlockSpec((2, TM, 512), lambda b, t: (0, t, b)))
    out_shape.append(jax.ShapeDtypeStruct((2, T, B * 512), BF16))
    return pl.pallas_call(
        _proj_kernel,
        grid=(B, nt),
        in_specs=in_specs,
        out_specs=out_specs,
        out_shape=out_shape,
        compiler_params=_cparams(("arbitrary", "arbitrary")),
        name="project",
    )(x_all, mods, gpre, wts["w_in"], wts["gq_norm"], wts["gk_norm"], wts["mq_norm"], wts["mkv_norm"],
      wts["w_q_up"], wts["w_k_up"], wts["w_v_up"], consts["seg"], consts["cc"], consts["sc"],
      consts["cos_h"], consts["sin_h"], consts["cos_m"], consts["sin_m"])


def _attn_kernel(mode, lam_init, q_ref, k_ref, v_ref, *rest):
    if mode == "diff":
        lam_ref, sub_ref, _, o_ref, vx_ref = rest
    else:
        _, o_ref, vx_ref = rest

    @pl.when(pl.program_id(2) == 0)
    def _():
        vx_ref[:, :LANES] = v_ref[...]
        vx_ref[:, LANES:] = jnp.ones((vx_ref.shape[0], LANES), BF16)

    q = q_ref[...]
    lane = lax.broadcasted_iota(jnp.int32, (1, LANES), 1)
    low = lane < HEAD_DIM
    if mode == "mla":
        qa, qb = q[:, :LANES], q[:, LANES:]
        ka, kb = k_ref[:, :LANES], k_ref[:, LANES:]
    else:
        zero = jnp.zeros_like(q)
        qa, qb = jnp.where(low, q, zero), jnp.where(low, zero, q)
        ka = kb = k_ref[...]

    nt = (((1,), (1,)), ((), ()))
    sa = lax.dot_general(qa, ka, nt, preferred_element_type=F32)
    sb = lax.dot_general(qb, kb, nt, preferred_element_type=F32)
    pa = jnp.exp2(sa - jnp.max(sa, axis=-1, keepdims=True)).astype(BF16)
    pb = jnp.exp2(sb - jnp.max(sb, axis=-1, keepdims=True)).astype(BF16)
    vx = vx_ref[...]
    ea = _dot(pa, vx)
    eb = _dot(pb, vx)
    ya = ea[:, :LANES] / ea[:, LANES:]
    yb = eb[:, :LANES] / eb[:, LANES:]
    if mode == "diff":
        dl = lam_ref[...]
        lam = (jnp.exp(jnp.sum(dl[0:1] * dl[1:2], axis=-1, keepdims=True))
               - jnp.exp(jnp.sum(dl[2:3] * dl[3:4], axis=-1, keepdims=True)) + lam_init)
        y = _rms(ya - lam * yb, sub_ref[...]) * (1.0 - lam_init)
    else:
        y = jnp.where(low, ya, yb)
    o_ref[...] = y.astype(BF16)


def _attention(mode, q, k, v, y_prev, *, tq, nq, q_blk0, tk, k_blk, lam_init=0.0, extra=()):
    B, T, _ = q.shape
    wq = 2 * LANES if mode == "mla" else LANES
    wk = wq
    jk = (lambda j: 0) if mode == "gqa" else (lambda j: j)
    in_specs = [
        pl.BlockSpec((None, tq, wq), lambda b, j, i: (b, i + q_blk0, j)),
        pl.BlockSpec((None, tk, wk), lambda b, j, i: (b, k_blk, jk(j))),
        pl.BlockSpec((None, tk, LANES), lambda b, j, i: (b, k_blk, jk(j))),
    ]
    args = [q, k, v]
    if mode == "diff":
        in_specs += [_full_spec((4, HEAD_DIM)), _full_spec((1, LANES))]
        args += list(extra)
    in_specs.append(pl.BlockSpec(memory_space=pl.ANY))
    args.append(y_prev)
    return pl.pallas_call(
        functools.partial(_attn_kernel, mode, lam_init),
        grid=(B, 4, nq),
        in_specs=in_specs,
        out_specs=pl.BlockSpec((None, tq, LANES), lambda b, j, i: (b, i + q_blk0, j)),
        out_shape=jax.ShapeDtypeStruct((B, T, BRANCH_W), BF16),
        input_output_aliases={len(args) - 1: 0},
        scratch_shapes=[pltpu.VMEM((tk, 2 * LANES), BF16)],
        compiler_params=_cparams(("arbitrary", "arbitrary", "arbitrary")),
        name="attention_" + mode,
    )(*args)


def _mm_kernel(a_ref, b_ref, o_ref, acc_ref):
    k = pl.program_id(2)

    @pl.when(k == 0)
    def _():
        acc_ref[...] = jnp.zeros_like(acc_ref)

    acc_ref[...] += _dot(a_ref[...], b_ref[...])

    @pl.when(k == pl.num_programs(2) - 1)
    def _():
        o_ref[...] = acc_ref[...].astype(o_ref.dtype)


def _seq_dft(w, pq, *, rows, row_blk0, tm, tn, tk):
    m, kk = w.shape
    n = pq.shape[2]
    kh = rows // tk
    return pl.pallas_call(
        _mm_kernel,
        grid=(m // tm, n // tn, kk // tk),
        in_specs=[pl.BlockSpec((tm, tk), lambda i, j, k: (i, k)),
                  pl.BlockSpec((None, tk, tn), lambda i, j, k: (k // kh, row_blk0 + k % kh, j))],
        out_specs=pl.BlockSpec((tm, tn), lambda i, j, k: (i, j)),
        out_shape=jax.ShapeDtypeStruct((m, n), BF16),
        scratch_shapes=[pltpu.VMEM((tm, tn), F32)],
        compiler_params=_cparams(("arbitrary", "arbitrary", "arbitrary")),
        name="sequence_dft",
    )(w, pq)


def _merge_kernel(x_ref, h_ref, y0_ref, y1_ref, y2_ref, y3_ref, mod_ref, wg_ref, bg_ref, wb_ref, wo_ref,
                  gpost_ref, o_ref):
    h = h_ref[...]
    ys = (y0_ref, y1_ref, y2_ref, y3_ref)
    acc = None
    for i in range(4):
        gate = jax.nn.sigmoid(_dot(h, wg_ref[i]) + bg_ref[i:i + 1])
        term = gate * _dot(ys[i][...], wb_ref[i])
        acc = term if acc is None else acc + term
    m = _dot(acc.astype(BF16), wo_ref[...])
    mod = mod_ref[...]
    o_ref[...] = x_ref[...] + mod[2:3] * _rms(m, gpost_ref[...])


def _merge(x_all, h, y_diff, y_f, y_gqa, y_mla, mods, wts, gpost, n_lat_tiles, n_tiles):
    B, T, _ = x_all.shape
    tok = lambda w: pl.BlockSpec((None, TM, w), lambda b, t: (b, t, 0))
    in_specs = [
        tok(D_MODEL), tok(D_MODEL), tok(BRANCH_W),
        pl.BlockSpec((TM, BRANCH_W), lambda b, t: (t, b)),
        tok(BRANCH_W), tok(BRANCH_W),
        pl.BlockSpec((None, None, N_MOD, D_MODEL), lambda b, t: (b, jnp.where(t >= n_lat_tiles, 1, 0), 0, 0)),
        _full_spec((4, D_MODEL, D_MODEL)), _full_spec((4, D_MODEL)), _full_spec((4, BRANCH_W, D_MODEL)),
        _full_spec((D_MODEL, D_MODEL)), _full_spec((1, D_MODEL)),
    ]
    return pl.pallas_call(
        _merge_kernel,
        grid=(B, n_tiles),
        in_specs=in_specs,
        out_specs=tok(D_MODEL),
        out_shape=jax.ShapeDtypeStruct((B, n_tiles * TM, D_MODEL), F32),
        compiler_params=_cparams(("arbitrary", "arbitrary")),
        name="merge",
    )(x_all, h, y_diff, y_f, y_gqa, y_mla, mods, wts["w_gate"], wts["b_gate"], wts["w_branch"], wts["w_out"],
      gpost)


def _dense_ffn_kernel(nf, tf, x_ref, mod_ref, gpre_ref, gpost_ref, wgu_ref, wd_ref, o_ref):
    x = x_ref[...]
    mod = mod_ref[...]
    hb = (_rms(x, gpre_ref[...]) * (1.0 + mod[4:5]) + mod[3:4]).astype(BF16)
    acc = None
    for f in range(nf):
        a = _dot(hb, wgu_ref[:, f * tf:(f + 1) * tf])
        u = _dot(hb, wgu_ref[:, FFN_DIM + f * tf:FFN_DIM + (f + 1) * tf])
        act = (a * jax.nn.sigmoid(a) * u).astype(BF16)
        part = _dot(act, wd_ref[f * tf:(f + 1) * tf, :])
        acc = part if acc is None else acc + part
    o_ref[...] = x + mod[5:6] * _rms(acc, gpost_ref[...])


def _dense_ffn(x_all, mods, gpre, gpost, wgu, wd, n_lat_tiles, n_tiles):
    B = x_all.shape[0]
    nf, tf = 2, FFN_DIM // 2
    tok = pl.BlockSpec((None, TM, D_MODEL), lambda b, t: (b, t, 0))
    return pl.pallas_call(
        functools.partial(_dense_ffn_kernel, nf, tf),
        grid=(B, n_tiles),
        in_specs=[tok,
                  pl.BlockSpec((None, None, N_MOD, D_MODEL),
                               lambda b, t: (b, jnp.where(t >= n_lat_tiles, 1, 0), 0, 0)),
                  _full_spec((1, D_MODEL)), _full_spec((1, D_MODEL)),
                  _full_spec((D_MODEL, 2 * FFN_DIM)), _full_spec((FFN_DIM, D_MODEL))],
        out_specs=tok,
        out_shape=jax.ShapeDtypeStruct((B, n_tiles * TM, D_MODEL), F32),
        compiler_params=_cparams(("arbitrary", "arbitrary")),
        name="dense_ffn",
    )(x_all, mods, gpre, gpost, wgu, wd)


TE = 512
R_E1, R_E2, R_G1, R_G2, R_RANK1, R_RANK2 = range(6)


def _moe_pre_kernel(x_ref, mod_ref, gpre_ref, wr_ref, tri_ref, h_ref, route_ref, cnt_ref, carry_ref):
    first = jnp.logical_and(pl.program_id(0) == 0, pl.program_id(1) == 0)

    @pl.when(first)
    def _():
        carry_ref[...] = jnp.zeros_like(carry_ref)

    x = x_ref[...]
    mod = mod_ref[...]
    h = _rms(x, gpre_ref[...]) * (1.0 + mod[4:5]) + mod[3:4]
    h_ref[...] = h
    logits = _dot3(h, wr_ref[...])
    lane = lax.broadcasted_iota(jnp.int32, logits.shape, 1)
    neg = jnp.float32(-jnp.inf)
    l1 = jnp.where(lane < N_EXPERTS, logits, neg)
    m1 = jnp.max(l1, axis=-1, keepdims=True)
    i1 = jnp.min(jnp.where(l1 == m1, lane, LANES), axis=-1, keepdims=True)
    l2 = jnp.where(lane == i1, neg, l1)
    m2 = jnp.max(l2, axis=-1, keepdims=True)
    i2 = jnp.min(jnp.where(l2 == m2, lane, LANES), axis=-1, keepdims=True)
    e2 = jnp.exp(m2 - m1)
    g1 = 1.0 / (1.0 + e2)
    g2 = e2 / (1.0 + e2)
    sel1, sel2 = lane == i1, lane == i2
    chosen = jnp.where(jnp.logical_or(sel1, sel2), 1.0, 0.0)
    before = _dot(tri_ref[...], chosen.astype(BF16)) + carry_ref[...]
    r1 = jnp.sum(jnp.where(sel1, before, 0.0), axis=-1, keepdims=True)
    r2 = jnp.sum(jnp.where(sel2, before, 0.0), axis=-1, keepdims=True)
    carry_ref[...] += jnp.sum(chosen, axis=0, keepdims=True)
    cnt_ref[...] = carry_ref[...]
    rec = jnp.zeros_like(logits)
    for col, val in ((R_E1, i1.astype(F32)), (R_E2, i2.astype(F32)), (R_G1, g1), (R_G2, g2),
                     (R_RANK1, r1), (R_RANK2, r2)):
        rec = jnp.where(lane == col, val, rec)
    route_ref[...] = rec


def _moe_pre(x_all, mods, gpre, w_router, tri, n_lat_tiles, n_tiles):
    B = x_all.shape[0]
    tok = lambda w: pl.BlockSpec((None, TM, w), lambda b, t: (b, t, 0))
    return pl.pallas_call(
        _moe_pre_kernel,
        grid=(B, n_tiles),
        in_specs=[tok(D_MODEL),
                  pl.BlockSpec((None, None, N_MOD, D_MODEL),
                               lambda b, t: (b, jnp.where(t >= n_lat_tiles, 1, 0), 0, 0)),
                  _full_spec((1, D_MODEL)), _full_spec((D_MODEL, LANES)), _full_spec((TM, TM))],
        out_specs=[tok(D_MODEL), tok(LANES), _full_spec((1, LANES))],
        out_shape=[jax.ShapeDtypeStruct((B, n_tiles * TM, D_MODEL), F32),
                   jax.ShapeDtypeStruct((B, n_tiles * TM, LANES), F32),
                   jax.ShapeDtypeStruct((1, LANES), F32)],
        scratch_shapes=[pltpu.VMEM((1, LANES), F32)],
        compiler_params=_cparams(("arbitrary", "arbitrary")),
        name="moe_router",
    )(x_all, mods, gpre, w_router, tri)


def _row_copies(n_rows, copy_of):
    def start(r, carry):
        copy_of(r, 0).start()
        copy_of(r, 1).start()
        return carry

    def wait(r, carry):
        copy_of(r, 0).wait()
        copy_of(r, 1).wait()
        return carry

    lax.fori_loop(0, n_rows, start, 0)
    lax.fori_loop(0, n_rows, wait, 0)


def _dispatch_kernel(pos_ref, h_ref, _, xs_ref, sem):
    def copy_of(r, slot):
        p = pos_ref[0, 2 * r + slot]
        return pltpu.make_async_copy(h_ref.at[pl.ds(r, 1)], xs_ref.at[pl.ds(p, 1)], sem)

    _row_copies(h_ref.shape[0], copy_of)


def _dispatch(h, pos, n_rows):
    n = h.shape[0]
    xs0 = jnp.zeros((n_rows, D_MODEL), F32)
    return pl.pallas_call(
        _dispatch_kernel,
        grid=(n // TM,),
        in_specs=[pl.BlockSpec((None, 1, 2 * TM), lambda i: (i, 0, 0), memory_space=pltpu.SMEM),
                  pl.BlockSpec((TM, D_MODEL), lambda i: (i, 0)),
                  pl.BlockSpec(memory_space=pl.ANY)],
        out_specs=pl.BlockSpec(memory_space=pl.ANY),
        out_shape=jax.ShapeDtypeStruct((n_rows, D_MODEL), F32),
        input_output_aliases={2: 0},
        scratch_shapes=[pltpu.SemaphoreType.DMA],
        compiler_params=_cparams(("arbitrary",)),
        name="moe_dispatch",
    )(pos.reshape(n // TM, 1, 2 * TM), h, xs0)


def _experts_kernel(te_ref, nv_ref, xs_ref, wa_ref, wu_ref, wd_ref, ys_ref, acc_ref):
    del te_ref
    f = pl.program_id(1)

    @pl.when(pl.program_id(0) < nv_ref[0])
    def _():
        x = xs_ref[...].astype(BF16)
        a = _dot(x, wa_ref[...])
        u = _dot(x, wu_ref[...])
        part = _dot((a * jax.nn.sigmoid(a) * u).astype(BF16), wd_ref[...])

        @pl.when(f == 0)
        def _():
            acc_ref[...] = part

        @pl.when(f > 0)
        def _():
            acc_ref[...] += part

        @pl.when(f == pl.num_programs(1) - 1)
        def _():
            ys_ref[...] = acc_ref[...]


def _experts(xs, tile_expert, n_valid, wgu, wd):
    n_rows = xs.shape[0]
    nf = 2
    tf = EXPERT_DIM // nf
    row_blk = lambda i, f, te, nv: (jnp.minimum(i, nv[0] - 1), 0)
    grid_spec = pltpu.PrefetchScalarGridSpec(
        num_scalar_prefetch=2,
        grid=(n_rows // TE, nf),
        in_specs=[pl.BlockSpec((TE, D_MODEL), row_blk),
                  pl.BlockSpec((None, D_MODEL, tf), lambda i, f, te, nv: (te[i], 0, f)),
                  pl.BlockSpec((None, D_MODEL, tf), lambda i, f, te, nv: (te[i], 0, nf + f)),
                  pl.BlockSpec((None, tf, D_MODEL), lambda i, f, te, nv: (te[i], f, 0))],
        out_specs=pl.BlockSpec((TE, D_MODEL), row_blk),
        scratch_shapes=[pltpu.VMEM((TE, D_MODEL), F32)],
    )
    return pl.pallas_call(
        _experts_kernel,
        grid_spec=grid_spec,
        out_shape=jax.ShapeDtypeStruct((n_rows, D_MODEL), F32),
        compiler_params=_cparams(("arbitrary", "arbitrary")),
        name="moe_experts",
    )(tile_expert, n_valid, xs, wgu, wgu, wd)


def _combine_kernel(pos_ref, x_ref, route_ref, mod_ref, gpost_ref, ys_ref, o_ref, buf_ref, sem):
    def copy_of(r, slot):
        p = pos_ref[0, 2 * r + slot]
        return pltpu.make_async_copy(ys_ref.at[pl.ds(p, 1)], buf_ref.at[slot, pl.ds(r, 1)], sem)

    _row_copies(x_ref.shape[0], copy_of)
    route = route_ref[...]
    y = route[:, R_G1:R_G1 + 1] * buf_ref[0] + route[:, R_G2:R_G2 + 1] * buf_ref[1]
    mod = mod_ref[...]
    o_ref[...] = x_ref[...] + mod[5:6] * _rms(y, gpost_ref[...])


def _combine(x_all, route, pos, ys, mods, gpost, n_lat_tiles, n_tiles):
    B = x_all.shape[0]
    tok = lambda w: pl.BlockSpec((None, TM, w), lambda b, t: (b, t, 0))
    return pl.pallas_call(
        _combine_kernel,
        grid=(B, n_tiles),
        in_specs=[pl.BlockSpec((None, None, 1, 2 * TM), lambda b, t: (b, t, 0, 0), memory_space=pltpu.SMEM),
                  tok(D_MODEL), tok(LANES),
                  pl.BlockSpec((None, None, N_MOD, D_MODEL),
                               lambda b, t: (b, jnp.where(t >= n_lat_tiles, 1, 0), 0, 0)),
                  _full_spec((1, D_MODEL)),
                  pl.BlockSpec(memory_space=pl.ANY)],
        out_specs=tok(D_MODEL),
        out_shape=jax.ShapeDtypeStruct((B, n_tiles * TM, D_MODEL), F32),
        scratch_shapes=[pltpu.VMEM((2, TM, D_MODEL), F32), pltpu.SemaphoreType.DMA],
        compiler_params=_cparams(("arbitrary", "arbitrary")),
        name="moe_combine",
    )(pos.reshape(B, n_tiles, 1, 2 * TM), x_all, route, mods, gpost, ys)


def _routed_ffn(x_all, mods, gpre, gpost, w_router, wgu, wd, n_lat_tiles, n_tiles):
    B = x_all.shape[0]
    n_tok = B * n_tiles * TM
    idx = jnp.arange(TM)
    tri = jnp.where(idx[:, None] > idx[None, :], 1.0, 0.0).astype(BF16)
    h, route, counts = _moe_pre(x_all, mods, gpre, w_router, tri, n_lat_tiles, n_tiles)
    rec = route.reshape(n_tok, LANES)
    expert = rec[:, R_E1:R_E2 + 1].astype(jnp.int32)
    rank = rec[:, R_RANK1:R_RANK2 + 1].astype(jnp.int32)
    tiles_per_expert = (counts[0, :N_EXPERTS].astype(jnp.int32) + TE - 1) // TE
    tile_end = jnp.cumsum(tiles_per_expert)
    n_valid = tile_end[-1:]
    pos = (tile_end - tiles_per_expert)[expert] * TE + rank
    n_tiles_max = (2 * n_tok) // TE + N_EXPERTS
    tile_id = jnp.minimum(jnp.arange(n_tiles_max, dtype=jnp.int32), n_valid - 1)
    tile_expert = jnp.minimum(jnp.searchsorted(tile_end, tile_id, side="right"), N_EXPERTS - 1).astype(jnp.int32)
    xs = _dispatch(h.reshape(n_tok, D_MODEL), pos, n_tiles_max * TE)
    ys = _experts(xs, tile_expert, n_valid.astype(jnp.int32), wgu, wd)
    return _combine(x_all, route, pos, ys, mods, gpost, n_lat_tiles, n_tiles)


def _rope_angles(n_lat, rot_dim):
    rows = n_lat // GRID_W
    row = jnp.repeat(jnp.arange(rows, dtype=F32), GRID_W)
    col = jnp.tile(jnp.arange(GRID_W, dtype=F32), rows)
    axis_dim = rot_dim // 2
    inv_freq = 1.0 / (ROPE_BASE ** (jnp.arange(0, axis_dim, 2, dtype=F32) / axis_dim))
    ar = row[:, None] * inv_freq
    ac = col[:, None] * inv_freq
    return jnp.concatenate([ar, ar, ac, ac], axis=-1)


def _constants(n_lat, n_ctx):
    lane = jnp.arange(LANES)
    ang = _rope_angles(n_lat, HEAD_DIM)
    cos_h = jnp.tile(jnp.cos(ang), (1, 2))
    sin_h = jnp.tile(jnp.sin(ang), (1, 2)) * jnp.where((lane % 32) < 16, -1.0, 1.0)
    cos_h = jnp.concatenate([cos_h, jnp.ones((n_ctx, LANES), F32)], axis=0)
    sin_h = jnp.concatenate([sin_h, jnp.zeros((n_ctx, LANES), F32)], axis=0)
    angm = _rope_angles(n_lat, MLA_ROPE)
    pad_l = jnp.zeros((n_lat, MLA_NOPE), F32)
    pad_r = jnp.zeros((n_lat, LANES - MLA_NOPE - MLA_ROPE), F32)
    cos_m = jnp.concatenate([pad_l + 1.0, jnp.cos(angm), pad_r + 1.0], axis=1)
    sin_m = jnp.concatenate([pad_l, jnp.sin(angm), pad_r], axis=1) * jnp.where((lane % 16) < 8, -1.0, 1.0)
    cos_m = jnp.concatenate([cos_m, jnp.ones((n_ctx, LANES), F32)], axis=0)
    sin_m = jnp.concatenate([sin_m, jnp.zeros((n_ctx, LANES), F32)], axis=0)
    idx = jnp.arange(512)
    seg = jnp.where((idx[:, None] // HEAD_DIM) == (idx[None, :] // HEAD_DIM), 1.0 / HEAD_DIM, 0.0).astype(BF16)

    def dft(n):
        i = jnp.arange(n, dtype=jnp.int32)
        a = ((i[:, None] * i[None, :]) % n).astype(F32) * (2.0 * math.pi / n)
        s = n ** -0.5
        return jnp.cos(a) * s, jnp.sin(a) * s

    cc, sc = dft(FOURIER_GROUP)
    cl, sl = dft(n_lat)
    cx, sx = dft(n_ctx)
    return {
        "cos_h": cos_h, "sin_h": sin_h, "cos_m": cos_m, "sin_m": sin_m, "seg": seg,
        "cc": cc.astype(BF16), "sc": sc.astype(BF16),
        "w_lat": jnp.concatenate([cl, -sl], axis=1).astype(BF16),
        "w_ctx": jnp.concatenate([cx, -sx], axis=1).astype(BF16),
    }


_GQA_HEAD_ORDER = (0, 4, 1, 5, 2, 6, 3, 7)


def _layer_weights(l, w_in, gqa_q_norm, gqa_k_norm, mla_q_norm, mla_kv_norm, mla_w_q_up, mla_w_kv_up,
                   w_branch, w_gate, b_gate, w_out):
    wi = w_in[l]
    gq = wi[:, 2048:2560].reshape(D_MODEL, GQA_Q_HEADS, HEAD_DIM)[:, jnp.array(_GQA_HEAD_ORDER)]
    zeros = lambda n: jnp.zeros((D_MODEL, n), F32)
    w_in_p = jnp.concatenate([wi[:, :2048], gq.reshape(D_MODEL, 512), wi[:, 2560:3200],
                              zeros(MLA_NOPE), wi[:, 3200:3232], zeros(LANES - MLA_NOPE - MLA_ROPE)], axis=1)
    qu = mla_w_q_up[l].reshape(MLA_Q_RANK, MLA_HEADS, MLA_NOPE + MLA_ROPE)
    qu = jnp.pad(qu, ((0, 0), (0, 0), (0, LANES - MLA_NOPE - MLA_ROPE))).reshape(MLA_Q_RANK, MLA_HEADS * LANES)
    kvu = mla_w_kv_up[l].reshape(MLA_KV_RANK, MLA_HEADS, MLA_NOPE + MLA_V)
    ku = jnp.pad(kvu[:, :, :MLA_NOPE], ((0, 0), (0, 0), (0, LANES - MLA_NOPE))).reshape(MLA_KV_RANK, -1)
    vu = kvu[:, :, MLA_NOPE:].reshape(MLA_KV_RANK, MLA_HEADS * MLA_V)
    wb = w_branch[l]
    wb_gqa = wb[2].reshape(GQA_Q_HEADS, HEAD_DIM, D_MODEL)[jnp.array(_GQA_HEAD_ORDER)].reshape(BRANCH_W, D_MODEL)
    wb = jnp.stack([wb[0], wb[1], wb_gqa, wb[3]])
    return {
        "w_in": w_in_p.astype(BF16),
        "gq_norm": jnp.tile(gqa_q_norm[l], GQA_Q_HEADS).reshape(1, 512),
        "gk_norm": jnp.tile(gqa_k_norm[l], GQA_KV_HEADS).reshape(1, LANES),
        "mq_norm": mla_q_norm[l].reshape(1, MLA_Q_RANK),
        "mkv_norm": mla_kv_norm[l].reshape(1, MLA_KV_RANK),
        "w_q_up": qu.astype(BF16), "w_k_up": ku.astype(BF16), "w_v_up": vu.astype(BF16),
        "w_branch": wb.astype(BF16), "w_gate": w_gate[l].astype(BF16), "b_gate": b_gate[l],
        "w_out": w_out[l].astype(BF16),
    }


def kernel(x, c, ctx, c_ctx, w_mod, b_mod, mix_pre_norm, mix_post_norm, ffn_pre_norm, ffn_post_norm,
           w_in, diff_lambda, diff_subnorm, gqa_q_norm, gqa_k_norm, mla_q_norm, mla_kv_norm,
           mla_w_q_up, mla_w_kv_up, w_branch, w_gate, b_gate, w_out,
           dense_w_gate_up, dense_w_down, moe_router, moe_w_gate_up, moe_w_down):
    B, n_lat, d = x.shape
    n_ctx = ctx.shape[1]
    depth = w_mod.shape[0]
    assert d == D_MODEL and n_lat % TM == 0 and n_ctx % TM == 0 and n_lat % n_ctx == 0
    assert n_lat % GRID_W == 0
    T = n_lat + n_ctx
    n_lat_tiles, n_all_tiles = n_lat // TM, T // TM
    consts = _constants(n_lat, n_ctx)
    x_all = jnp.concatenate([x, ctx], axis=1)
    c_rows = jnp.pad(jnp.concatenate([c, c_ctx[None]], axis=0), ((0, (-(B + 1)) % 8), (0, 0)))
    row = lambda v: v.reshape(1, -1)

    for layer in range(depth):
        last = layer == depth - 1
        n_tiles = n_lat_tiles if last else n_all_tiles
        lam_init = 0.8 - 0.6 * math.exp(-0.3 * layer)
        wts = _layer_weights(layer, w_in, gqa_q_norm, gqa_k_norm, mla_q_norm, mla_kv_norm, mla_w_q_up,
                             mla_w_kv_up, w_branch, w_gate, b_gate, w_out)
        mod_rows = _modulation(c_rows, w_mod[layer], b_mod[layer])
        mods = jnp.stack([mod_rows[:B].reshape(B, N_MOD, D_MODEL),
                          jnp.broadcast_to(mod_rows[B].reshape(1, N_MOD, D_MODEL), (B, N_MOD, D_MODEL))], axis=1)

        (h, dq, dk, dv, gq, gk, gv, mq, mk, mv, pq) = _project(
            x_all, mods, row(mix_pre_norm[layer]), wts, consts, n_lat_tiles)

        diff_extra = (diff_lambda[layer], diff_subnorm[layer].reshape(1, LANES))
        ys = []
        for mode, q, k, v in (("diff", dq, dk, dv), ("gqa", gq, gk, gv), ("mla", mq, mk, mv)):
            y = jnp.zeros((B, T, BRANCH_W), BF16)
            kw = dict(lam_init=lam_init, extra=diff_extra)
            y = _attention(mode, q, k, v, y, tq=TQ, nq=n_lat // TQ, q_blk0=0, tk=T, k_blk=0, **kw)
            if not last:
                y = _attention(mode, q, k, v, y, tq=TM, nq=n_ctx // TM, q_blk0=n_lat_tiles,
                               tk=n_ctx, k_blk=n_lat // n_ctx, **kw)
            ys.append(y)
        y_diff, y_gqa, y_mla = ys

        y_f = _seq_dft(consts["w_lat"], pq, rows=n_lat, row_blk0=0,
                       tm=min(1024, n_lat), tn=min(2048, B * 512), tk=min(512, n_lat))
        if not last:
            y_fc = _seq_dft(consts["w_ctx"], pq, rows=n_ctx, row_blk0=n_lat // n_ctx,
                            tm=n_ctx, tn=min(2048, B * 512), tk=n_ctx)
            y_f = jnp.concatenate([y_f, y_fc], axis=0)

        x_all = _merge(x_all, h, y_diff, y_f, y_gqa, y_mla, mods, wts, row(mix_post_norm[layer]),
                       n_lat_tiles, n_tiles)

        if layer % 2 == 0:
            x_all = _dense_ffn(x_all, mods, row(ffn_pre_norm[layer]), row(ffn_post_norm[layer]),
                               dense_w_gate_up[layer // 2].astype(BF16), dense_w_down[layer // 2].astype(BF16),
                               n_lat_tiles, n_tiles)
        else:
            wr = jnp.pad(moe_router[layer // 2], ((0, 0), (0, LANES - N_EXPERTS)))
            x_all = _routed_ffn(x_all, mods, row(ffn_pre_norm[layer]), row(ffn_post_norm[layer]), wr,
                                moe_w_gate_up[layer // 2].astype(BF16), moe_w_down[layer // 2].astype(BF16),
                                n_lat_tiles, n_tiles)
    return x_all[:, :n_lat]
```

```python
import functools
import math

import jax
import jax.numpy as jnp
from jax import lax
from jax.experimental import pallas as pl
from jax.experimental.pallas import tpu as pltpu

F32 = jnp.float32
BF16 = jnp.bfloat16

D_MODEL = 1024
GRID_W = 64
ROPE_BASE = 10000.0
EPS = 1e-6
N_MOD = 6
HEAD_DIM = 64
DIFF_HEADS = 4
GQA_Q_HEADS = 8
GQA_KV_HEADS = 2
MLA_HEADS = 8
MLA_Q_RANK = 256
MLA_KV_RANK = 128
MLA_NOPE = 64
MLA_ROPE = 32
MLA_V = 64
BRANCH_W = 512
FOURIER_GROUP = 128
FFN_DIM = 2816
N_EXPERTS = 8
EXPERT_DIM = 3584
LOG2E = math.log2(math.e)
QK_SCALE = HEAD_DIM ** -0.5 * LOG2E
MLA_SCALE = (MLA_NOPE + MLA_ROPE) ** -0.5 * LOG2E
IN_COLS_PADDED = 3328

LANES = 128
VMEM_LIMIT = 56 * 1024 * 1024

TM = 256
TQ = 256
KEY_CHUNK = 256
ONES_ROWS = 16


def _cparams(sem):
    return pltpu.CompilerParams(dimension_semantics=sem, vmem_limit_bytes=VMEM_LIMIT)


def _full_spec(shape):
    n = len(shape)
    return pl.BlockSpec(shape, lambda *_: (0,) * n)


def _split_bf16(x):
    hi = x.astype(BF16)
    lo = (x - hi.astype(F32)).astype(BF16)
    return hi, lo


def _dot(a, b):
    return jnp.dot(a, b, preferred_element_type=F32)


def _dot3(a, w):
    ah, al = _split_bf16(a)
    wh, wl = _split_bf16(w)
    return _dot(ah, wh) + _dot(ah, wl) + _dot(al, wh)


def _rms(x, gain):
    ms = jnp.mean(x * x, axis=-1, keepdims=True)
    return x * lax.rsqrt(ms + EPS) * gain


def _mod_kernel(c_ref, w_ref, b_ref, o_ref):
    c = c_ref[...]
    a = c * jax.nn.sigmoid(c)
    o_ref[...] = _dot3(a, w_ref[...]) + b_ref[...]


def _modulation(c_rows, w_mod_l, b_mod_l):
    rows = c_rows.shape[0]
    n = w_mod_l.shape[1]
    tn = 512
    return pl.pallas_call(
        _mod_kernel,
        grid=(n // tn,),
        in_specs=[pl.BlockSpec((rows, D_MODEL), lambda j: (0, 0)),
                  pl.BlockSpec((D_MODEL, tn), lambda j: (0, j)),
                  pl.BlockSpec((1, tn), lambda j: (0, j))],
        out_specs=pl.BlockSpec((rows, tn), lambda j: (0, j)),
        out_shape=jax.ShapeDtypeStruct((rows, n), F32),
        compiler_params=_cparams(("arbitrary",)),
        name="modulation",
    )(c_rows, w_mod_l, b_mod_l.reshape(1, n))


def _rope(x, cos, sin_signed, half):
    lane = lax.broadcasted_iota(jnp.int32, x.shape, 1)
    first = (lane % (2 * half)) < half
    up = pltpu.roll(x, LANES - half, 1)
    down = pltpu.roll(x, half, 1)
    return x * cos + jnp.where(first, up, down) * sin_signed


def _proj_kernel(n_lat_tiles, x_ref, mod_ref, gpre_ref, win_ref, gqn_ref, gkn_ref, mqn_ref, mkvn_ref,
                 wqup_ref, wkup_ref, wvup_ref, seg_ref, cc_ref, sc_ref,
                 cosh_ref, sinh_ref, cosm_ref, sinm_ref,
                 h_ref, dq_ref, dk_ref, dv_ref, gq_ref, gk_ref, gv_ref, mq_ref, mk_ref, mv_ref, pq_ref, pqc_ref):
    x = x_ref[...]
    mod = mod_ref[...]
    h = _rms(x, gpre_ref[...]) * (1.0 + mod[1:2]) + mod[0:1]
    hb = h.astype(BF16)
    h_ref[...] = hb
    z = _dot(hb, win_ref[...])
    cosh, sinh = cosh_ref[...], sinh_ref[...]
    cosm, sinm = cosm_ref[...], sinm_ref[...]
    seg = seg_ref[...]

    def head_rms(v, gain, w):
        sq = v * v
        hi, lo = _split_bf16(sq)
        ms = _dot(hi, seg[:w, :w]) + _dot(lo, seg[:w, :w])
        return v * lax.rsqrt(ms + EPS) * gain

    for j in range(4):
        c0 = j * LANES
        dq_ref[:, c0:c0 + LANES] = (_rope(z[:, c0:c0 + LANES], cosh, sinh, 16) * QK_SCALE).astype(BF16)
        dk_ref[:, c0:c0 + LANES] = _rope(z[:, 512 + c0:512 + c0 + LANES], cosh, sinh, 16).astype(BF16)
    dv_ref[...] = z[:, 1024:1536].astype(BF16)
    fparts = []
    for g in range(4):
        c0 = 1536 + g * LANES
        fb = z[:, c0:c0 + LANES].astype(BF16)
        fparts.append((_dot(fb, cc_ref[...]).astype(BF16), _dot(fb, sc_ref[...]).astype(BF16)))
    is_lat = pl.program_id(1) < n_lat_tiles

    def put(ref):
        for g, (pc, ps) in enumerate(fparts):
            ref[0, :, g * LANES:(g + 1) * LANES] = pc
            ref[1, :, g * LANES:(g + 1) * LANES] = ps

    pl.when(is_lat)(lambda: put(pq_ref))
    pl.when(jnp.logical_not(is_lat))(lambda: put(pqc_ref))
    gq = head_rms(z[:, 2048:2560], gqn_ref[...], 512)
    for j in range(4):
        c0 = j * LANES
        gq_ref[:, c0:c0 + LANES] = (_rope(gq[:, c0:c0 + LANES], cosh, sinh, 16) * QK_SCALE).astype(BF16)
    gk = head_rms(z[:, 2560:2688], gkn_ref[...], LANES)
    gk_ref[...] = _rope(gk, cosh, sinh, 16).astype(BF16)
    gv_ref[...] = z[:, 2688:2816].astype(BF16)
    cq = _rms(z[:, 2816:3072], mqn_ref[...]).astype(BF16)
    mq = _dot(cq, wqup_ref[...])
    ckv = _rms(z[:, 3072:3200], mkvn_ref[...]).astype(BF16)
    mk = _dot(ckv, wkup_ref[...])
    mv_ref[...] = _dot(ckv, wvup_ref[...]).astype(BF16)
    kr = _rope(z[:, 3200:3328], cosm, sinm, 8)
    for hh in range(MLA_HEADS):
        c0 = hh * LANES
        mq_ref[:, c0:c0 + LANES] = (_rope(mq[:, c0:c0 + LANES], cosm, sinm, 8) * MLA_SCALE).astype(BF16)
        mk_ref[:, c0:c0 + LANES] = (mk[:, c0:c0 + LANES] + kr).astype(BF16)


def _project(x_all, mods, gpre, wts, consts, n_lat_tiles):
    B, T, _ = x_all.shape
    nt = T // TM
    tok = lambda w: pl.BlockSpec((None, TM, w), lambda b, t: (b, t, 0))
    tab = pl.BlockSpec((TM, LANES), lambda b, t: (t, 0))
    in_specs = [
        tok(D_MODEL),
        pl.BlockSpec((None, None, N_MOD, D_MODEL), lambda b, t: (b, jnp.where(t >= n_lat_tiles, 1, 0), 0, 0)),
        _full_spec((1, D_MODEL)),
        _full_spec((D_MODEL, IN_COLS_PADDED)),
        _full_spec((1, 512)), _full_spec((1, LANES)), _full_spec((1, MLA_Q_RANK)), _full_spec((1, MLA_KV_RANK)),
        _full_spec((MLA_Q_RANK, MLA_HEADS * LANES)),
        _full_spec((MLA_KV_RANK, MLA_HEADS * LANES)),
        _full_spec((MLA_KV_RANK, MLA_HEADS * MLA_V)),
        _full_spec((512, 512)), _full_spec((LANES, LANES)), _full_spec((LANES, LANES)),
        tab, tab, tab, tab,
    ]
    widths = [D_MODEL, 512, 512, 512, 512, LANES, LANES, MLA_HEADS * LANES, MLA_HEADS * LANES, 512]
    out_specs = [tok(w) for w in widths]
    out_shape = [jax.ShapeDtypeStruct((B, T, w), BF16) for w in widths]
    out_specs.append(pl.BlockSpec((2, None, TM, 512), lambda b, t: (0, b, jnp.minimum(t, n_lat_tiles - 1), 0)))
    out_shape.append(jax.ShapeDtypeStruct((2, B, n_lat_tiles * TM, 512), BF16))
    out_specs.append(pl.BlockSpec((2, None, TM, 512), lambda b, t: (0, b, jnp.maximum(t - n_lat_tiles, 0), 0)))
    out_shape.append(jax.ShapeDtypeStruct((2, B, T - n_lat_tiles * TM, 512), BF16))
    return pl.pallas_call(
        functools.partial(_proj_kernel, n_lat_tiles),
        grid=(B, nt),
        in_specs=in_specs,
        out_specs=out_specs,
        out_shape=out_shape,
        compiler_params=_cparams(("arbitrary", "arbitrary")),
        name="project",
    )(x_all, mods, gpre, wts["w_in"], wts["gq_norm"], wts["gk_norm"], wts["mq_norm"], wts["mkv_norm"],
      wts["w_q_up"], wts["w_k_up"], wts["w_v_up"], consts["seg"], consts["cc"], consts["sc"],
      consts["cos_h"], consts["sin_h"], consts["cos_m"], consts["sin_m"])


N_MAPS = 8


def _map_plan(mode, m):
    if mode == "diff":
        return m // 2, m % 2, m // 2, m // 2
    if mode == "gqa":
        return m // 2, m % 2, 0, m % 2
    return m, None, m, m


def _attn_kernel(mode, lam_init, q_ref, k_ref, v_ref, *rest):
    if mode == "diff":
        lam_ref, sub_ref, _, o_ref, vt_ref, s_ref, p_ref, m_ref, acc_ref, y_ref = rest
    else:
        _, o_ref, vt_ref, s_ref, p_ref, m_ref, acc_ref, y_ref = rest
    n_chunks = vt_ref.shape[1]
    dv = vt_ref.shape[2] - ONES_ROWS
    n_groups = vt_ref.shape[0]
    nt = (((1,), (1,)), ((), ()))

    def rows(c):
        if isinstance(c, int):
            return slice(c * KEY_CHUNK, (c + 1) * KEY_CHUNK)
        return pl.ds(pl.multiple_of(c * KEY_CHUNK, KEY_CHUNK), KEY_CHUNK)

    @pl.when(pl.program_id(1) == 0)
    def _():
        def build(c, carry):
            vt = v_ref[rows(c), :].astype(F32).T.astype(BF16)
            for g in range(n_groups):
                vt_ref[g, c, :dv] = vt[g * dv:(g + 1) * dv]
                vt_ref[g, c, dv:] = jnp.ones((ONES_ROWS, KEY_CHUNK), BF16)
            return carry
        lax.fori_loop(0, n_chunks, build, 0)

    lane = lax.broadcasted_iota(jnp.int32, (1, LANES), 1)

    def masked_q(m):
        qblk, half, _, _ = _map_plan(mode, m)
        q = q_ref[:, qblk * LANES:(qblk + 1) * LANES]
        if half is None:
            return q
        keep = (lane >= HEAD_DIM) if half == 1 else (lane < HEAD_DIM)
        return jnp.where(keep, q, jnp.zeros_like(q))

    for t in range(N_MAPS + 2):
        m_qk, m_exp, m_pv = t, t - 1, t - 2
        do_qk, do_exp, do_pv = m_qk < N_MAPS, 0 <= m_exp < N_MAPS, 0 <= m_pv
        if do_qk:
            qx = masked_q(m_qk)
            kblk = _map_plan(mode, m_qk)[2]
        if do_exp:
            mx_exp = m_ref[m_exp % 2]
        if do_pv:
            acc_ref[...] = jnp.zeros_like(acc_ref)

        def body(c, mx):
            if do_qk:
                s = lax.dot_general(k_ref[rows(c), kblk * LANES:(kblk + 1) * LANES], qx, nt,
                                    preferred_element_type=F32)
                s_ref[m_qk % 2, rows(c), :] = s
                mx = jnp.maximum(mx, jnp.max(s, axis=0, keepdims=True))
            if do_exp:
                p_ref[m_exp % 2, rows(c), :] = jnp.exp2(s_ref[m_exp % 2, rows(c), :] - mx_exp).astype(BF16)
            if do_pv:
                acc_ref[...] += _dot(vt_ref[_map_plan(mode, m_pv)[3], c], p_ref[m_pv % 2, rows(c), :])
            return mx

        mx = jnp.full((1, q_ref.shape[0]), -jnp.inf, F32)
        for c in range(n_chunks):
            mx = body(c, mx)
        if do_qk:
            m_ref[m_qk % 2] = mx
        if do_pv:
            y_ref[m_pv] = acc_ref[:dv] / acc_ref[dv:dv + 1]

    if mode == "diff":
        dl = lam_ref[...]
        lam = (jnp.exp(jnp.sum(dl[0:1] * dl[1:2], axis=-1, keepdims=True))
               - jnp.exp(jnp.sum(dl[2:3] * dl[3:4], axis=-1, keepdims=True)) + lam_init)
    for j in range(N_MAPS // 2):
        if mode == "diff":
            y = _rms((y_ref[2 * j] - lam * y_ref[2 * j + 1]).T, sub_ref[...]) * (1.0 - lam_init)
        else:
            y = jnp.concatenate([y_ref[2 * j], y_ref[2 * j + 1]], axis=0).T
        o_ref[:, j * LANES:(j + 1) * LANES] = y.astype(BF16)


def _attention(mode, q, k, v, y_prev, *, tq, nq, q_blk0, tk, k_blk, lam_init=0.0, extra=()):
    B, T, _ = q.shape
    dv = LANES if mode == "diff" else HEAD_DIM
    n_groups = v.shape[2] // dv
    in_specs = [
        pl.BlockSpec((None, tq, q.shape[2]), lambda b, i: (b, i + q_blk0, 0)),
        pl.BlockSpec((None, tk, k.shape[2]), lambda b, i: (b, k_blk, 0)),
        pl.BlockSpec((None, tk, v.shape[2]), lambda b, i: (b, k_blk, 0)),
    ]
    args = [q, k, v]
    if mode == "diff":
        in_specs += [_full_spec((4, HEAD_DIM)), _full_spec((1, LANES))]
        args += list(extra)
    in_specs.append(pl.BlockSpec(memory_space=pl.ANY))
    args.append(y_prev)
    return pl.pallas_call(
        functools.partial(_attn_kernel, mode, lam_init),
        grid=(B, nq),
        in_specs=in_specs,
        out_specs=pl.BlockSpec((None, tq, BRANCH_W), lambda b, i: (b, i + q_blk0, 0)),
        out_shape=jax.ShapeDtypeStruct((B, T, BRANCH_W), BF16),
        input_output_aliases={len(args) - 1: 0},
        scratch_shapes=[pltpu.VMEM((n_groups, tk // KEY_CHUNK, dv + ONES_ROWS, KEY_CHUNK), BF16),
                        pltpu.VMEM((2, tk, tq), F32), pltpu.VMEM((2, tk, tq), BF16),
                        pltpu.VMEM((2, 1, tq), F32), pltpu.VMEM((dv + ONES_ROWS, tq), F32),
                        pltpu.VMEM((N_MAPS, dv, tq), F32)],
        compiler_params=_cparams(("arbitrary", "arbitrary")),
        name="attention_" + mode,
    )(*args)


def _dft_rows_kernel(pq_ref, m1_ref, a_ref):
    zin = jnp.concatenate([pq_ref[0], pq_ref[1]], axis=0)
    a_ref[...] = _dot(m1_ref[...], zin).astype(BF16)


def _dft_rows(pq_lat, m1):
    _, B, n_lat, w = pq_lat.shape
    r = n_lat // GRID_W
    lanes = GRID_W * w
    tn = 4096
    return pl.pallas_call(
        _dft_rows_kernel,
        grid=(B, lanes // tn),
        in_specs=[pl.BlockSpec((2, None, r, tn), lambda b, j: (0, b, 0, j)), _full_spec((2 * r, 2 * r))],
        out_specs=pl.BlockSpec((None, 2 * r, tn), lambda b, j: (b, 0, j)),
        out_shape=jax.ShapeDtypeStruct((B, 2 * r, lanes), BF16),
        compiler_params=_cparams(("arbitrary", "arbitrary")),
        name="dft_rows",
    )(pq_lat.reshape(2, B, r, lanes), m1)


def _dft_cols_kernel(a_ref, ct_ref, st_ref, m3_ref, o_ref):
    kb = a_ref.shape[1]
    w = a_ref.shape[3]
    for j in range(kb):
        ar = a_ref[0, j].astype(F32)
        ai = a_ref[1, j].astype(F32)
        ct = jnp.concatenate([ct_ref[j]] * (w // LANES), axis=1)
        st = jnp.concatenate([st_ref[j]] * (w // LANES), axis=1)
        b = jnp.concatenate([ar * ct + ai * st, ai * ct - ar * st], axis=0).astype(BF16)
        o_ref[:, j * w:(j + 1) * w] = _dot(m3_ref[...], b).astype(BF16)


def _dft_cols(a, ct, st, m3, w):
    B, r2, lanes = a.shape
    r = r2 // 2
    kb = min(8, r)
    return pl.pallas_call(
        _dft_cols_kernel,
        grid=(B, r // kb),
        in_specs=[pl.BlockSpec((None, 2, kb, GRID_W, w), lambda b, k: (b, 0, k, 0, 0)),
                  pl.BlockSpec((kb, GRID_W, LANES), lambda b, k: (k, 0, 0)),
                  pl.BlockSpec((kb, GRID_W, LANES), lambda b, k: (k, 0, 0)),
                  _full_spec((GRID_W, 2 * GRID_W))],
        out_specs=pl.BlockSpec((None, GRID_W, kb * w), lambda b, k: (b, 0, k)),
        out_shape=jax.ShapeDtypeStruct((B, GRID_W, r * w), BF16),
        compiler_params=_cparams(("arbitrary", "arbitrary")),
        name="dft_cols",
    )(a.reshape(B, 2, r, GRID_W, w), ct, st, m3)


def _dft_ctx_kernel(pq_ref, w_ref, o_ref):
    zin = jnp.concatenate([pq_ref[0], pq_ref[1]], axis=0)
    o_ref[...] = _dot(w_ref[...], zin).astype(BF16)


def _dft_ctx(pq_ctx, w_ctx):
    _, B, n_ctx, w = pq_ctx.shape
    return pl.pallas_call(
        _dft_ctx_kernel,
        grid=(B,),
        in_specs=[pl.BlockSpec((2, None, n_ctx, w), lambda b: (0, b, 0, 0)), _full_spec((n_ctx, 2 * n_ctx))],
        out_specs=pl.BlockSpec((None, n_ctx, w), lambda b: (b, 0, 0)),
        out_shape=jax.ShapeDtypeStruct((B, n_ctx, w), BF16),
        compiler_params=_cparams(("arbitrary",)),
        name="dft_ctx",
    )(pq_ctx, w_ctx)


def _merge_kernel(x_ref, h_ref, y0_ref, y1_ref, y2_ref, y3_ref, mod_ref, wg_ref, bg_ref, wb_ref, wo_ref,
                  gpost_ref, o_ref):
    h = h_ref[...]
    ys = (y0_ref, y1_ref, y2_ref, y3_ref)
    acc = None
    for i in range(4):
        gate = jax.nn.sigmoid(_dot(h, wg_ref[i]) + bg_ref[i:i + 1])
        term = gate * _dot(ys[i][...], wb_ref[i])
        acc = term if acc is None else acc + term
    m = _dot(acc.astype(BF16), wo_ref[...])
    mod = mod_ref[...]
    o_ref[...] = x_ref[...] + mod[2:3] * _rms(m, gpost_ref[...])


def _merge(x_all, h, y_diff, y_f, y_gqa, y_mla, mods, wts, gpost, n_lat_tiles, n_tiles):
    B, T, _ = x_all.shape
    tok = lambda w: pl.BlockSpec((None, TM, w), lambda b, t: (b, t, 0))
    in_specs = [
        tok(D_MODEL), tok(D_MODEL), tok(BRANCH_W), tok(BRANCH_W), tok(BRANCH_W), tok(BRANCH_W),
        pl.BlockSpec((None, None, N_MOD, D_MODEL), lambda b, t: (b, jnp.where(t >= n_lat_tiles, 1, 0), 0, 0)),
        _full_spec((4, D_MODEL, D_MODEL)), _full_spec((4, D_MODEL)), _full_spec((4, BRANCH_W, D_MODEL)),
        _full_spec((D_MODEL, D_MODEL)), _full_spec((1, D_MODEL)),
    ]
    return pl.pallas_call(
        _merge_kernel,
        grid=(B, n_tiles),
        in_specs=in_specs,
        out_specs=tok(D_MODEL),
        out_shape=jax.ShapeDtypeStruct((B, n_tiles * TM, D_MODEL), F32),
        compiler_params=_cparams(("arbitrary", "arbitrary")),
        name="merge",
    )(x_all, h, y_diff, y_f, y_gqa, y_mla, mods, wts["w_gate"], wts["b_gate"], wts["w_branch"], wts["w_out"],
      gpost)


def _dense_ffn_kernel(nf, tf, x_ref, mod_ref, gpre_ref, gpost_ref, wgu_ref, wd_ref, o_ref):
    x = x_ref[...]
    mod = mod_ref[...]
    hb = (_rms(x, gpre_ref[...]) * (1.0 + mod[4:5]) + mod[3:4]).astype(BF16)
    acc = None
    for f in range(nf):
        a = _dot(hb, wgu_ref[:, f * tf:(f + 1) * tf])
        u = _dot(hb, wgu_ref[:, FFN_DIM + f * tf:FFN_DIM + (f + 1) * tf])
        act = (a * jax.nn.sigmoid(a) * u).astype(BF16)
        part = _dot(act, wd_ref[f * tf:(f + 1) * tf, :])
        acc = part if acc is None else acc + part
    o_ref[...] = x + mod[5:6] * _rms(acc, gpost_ref[...])


def _dense_ffn(x_all, mods, gpre, gpost, wgu, wd, n_lat_tiles, n_tiles):
    B = x_all.shape[0]
    nf, tf = 2, FFN_DIM // 2
    tok = pl.BlockSpec((None, TM, D_MODEL), lambda b, t: (b, t, 0))
    return pl.pallas_call(
        functools.partial(_dense_ffn_kernel, nf, tf),
        grid=(B, n_tiles),
        in_specs=[tok,
                  pl.BlockSpec((None, None, N_MOD, D_MODEL),
                               lambda b, t: (b, jnp.where(t >= n_lat_tiles, 1, 0), 0, 0)),
                  _full_spec((1, D_MODEL)), _full_spec((1, D_MODEL)),
                  _full_spec((D_MODEL, 2 * FFN_DIM)), _full_spec((FFN_DIM, D_MODEL))],
        out_specs=tok,
        out_shape=jax.ShapeDtypeStruct((B, n_tiles * TM, D_MODEL), F32),
        compiler_params=_cparams(("arbitrary", "arbitrary")),
        name="dense_ffn",
    )(x_all, mods, gpre, gpost, wgu, wd)


TE = 512
R_E1, R_E2, R_G1, R_G2, R_RANK1, R_RANK2 = range(6)


def _moe_pre_kernel(x_ref, mod_ref, gpre_ref, wr_ref, tri_ref, h_ref, route_ref, cnt_ref, carry_ref):
    first = jnp.logical_and(pl.program_id(0) == 0, pl.program_id(1) == 0)

    @pl.when(first)
    def _():
        carry_ref[...] = jnp.zeros_like(carry_ref)

    x = x_ref[...]
    mod = mod_ref[...]
    h = _rms(x, gpre_ref[...]) * (1.0 + mod[4:5]) + mod[3:4]
    h_ref[...] = h
    logits = _dot3(h, wr_ref[...])
    lane = lax.broadcasted_iota(jnp.int32, logits.shape, 1)
    neg = jnp.float32(-jnp.inf)
    l1 = jnp.where(lane < N_EXPERTS, logits, neg)
    m1 = jnp.max(l1, axis=-1, keepdims=True)
    i1 = jnp.min(jnp.where(l1 == m1, lane, LANES), axis=-1, keepdims=True)
    l2 = jnp.where(lane == i1, neg, l1)
    m2 = jnp.max(l2, axis=-1, keepdims=True)
    i2 = jnp.min(jnp.where(l2 == m2, lane, LANES), axis=-1, keepdims=True)
    e2 = jnp.exp(m2 - m1)
    g1 = 1.0 / (1.0 + e2)
    g2 = e2 / (1.0 + e2)
    sel1, sel2 = lane == i1, lane == i2
    chosen = jnp.where(jnp.logical_or(sel1, sel2), 1.0, 0.0)
    before = _dot(tri_ref[...], chosen.astype(BF16)) + carry_ref[...]
    r1 = jnp.sum(jnp.where(sel1, before, 0.0), axis=-1, keepdims=True)
    r2 = jnp.sum(jnp.where(sel2, before, 0.0), axis=-1, keepdims=True)
    carry_ref[...] += jnp.sum(chosen, axis=0, keepdims=True)
    cnt_ref[...] = carry_ref[...]
    rec = jnp.zeros_like(logits)
    for col, val in ((R_E1, i1.astype(F32)), (R_E2, i2.astype(F32)), (R_G1, g1), (R_G2, g2),
                     (R_RANK1, r1), (R_RANK2, r2)):
        rec = jnp.where(lane == col, val, rec)
    route_ref[...] = rec


def _moe_pre(x_all, mods, gpre, w_router, tri, n_lat_tiles, n_tiles):
    B = x_all.shape[0]
    tok = lambda w: pl.BlockSpec((None, TM, w), lambda b, t: (b, t, 0))
    return pl.pallas_call(
        _moe_pre_kernel,
        grid=(B, n_tiles),
        in_specs=[tok(D_MODEL),
                  pl.BlockSpec((None, None, N_MOD, D_MODEL),
                               lambda b, t: (b, jnp.where(t >= n_lat_tiles, 1, 0), 0, 0)),
                  _full_spec((1, D_MODEL)), _full_spec((D_MODEL, LANES)), _full_spec((TM, TM))],
        out_specs=[tok(D_MODEL), tok(LANES), _full_spec((1, LANES))],
        out_shape=[jax.ShapeDtypeStruct((B, n_tiles * TM, D_MODEL), F32),
                   jax.ShapeDtypeStruct((B, n_tiles * TM, LANES), F32),
                   jax.ShapeDtypeStruct((1, LANES), F32)],
        scratch_shapes=[pltpu.VMEM((1, LANES), F32)],
        compiler_params=_cparams(("arbitrary", "arbitrary")),
        name="moe_router",
    )(x_all, mods, gpre, w_router, tri)


def _row_copies(n_rows, copy_of, slot_wait):
    def start(r, carry):
        copy_of(r, 0).start()
        copy_of(r, 1).start()
        return carry

    lax.fori_loop(0, n_rows, start, 0, unroll=8)
    slot_wait(0).wait()
    slot_wait(1).wait()


def _dispatch_kernel(pos_ref, h_ref, _, xs_ref, sem):
    def copy_of(r, slot):
        p = pos_ref[0, 2 * r + slot]
        return pltpu.make_async_copy(h_ref.at[pl.ds(r, 1)], xs_ref.at[pl.ds(p, 1)], sem)

    n = h_ref.shape[0]
    _row_copies(n, copy_of, lambda slot: pltpu.make_async_copy(h_ref, xs_ref.at[pl.ds(0, n)], sem))


def _dispatch(h, pos, n_rows):
    n = h.shape[0]
    xs0 = jnp.zeros((n_rows, D_MODEL), F32)
    return pl.pallas_call(
        _dispatch_kernel,
        grid=(n // TM,),
        in_specs=[pl.BlockSpec((None, 1, 2 * TM), lambda i: (i, 0, 0), memory_space=pltpu.SMEM),
                  pl.BlockSpec((TM, D_MODEL), lambda i: (i, 0)),
                  pl.BlockSpec(memory_space=pl.ANY)],
        out_specs=pl.BlockSpec(memory_space=pl.ANY),
        out_shape=jax.ShapeDtypeStruct((n_rows, D_MODEL), F32),
        input_output_aliases={2: 0},
        scratch_shapes=[pltpu.SemaphoreType.DMA],
        compiler_params=_cparams(("arbitrary",)),
        name="moe_dispatch",
    )(pos.reshape(n // TM, 1, 2 * TM), h, xs0)


def _experts_kernel(te_ref, nv_ref, xs_ref, wa_ref, wu_ref, wd_ref, ys_ref, acc_ref):
    del te_ref
    f = pl.program_id(1)

    @pl.when(pl.program_id(0) < nv_ref[0])
    def _():
        x = xs_ref[...].astype(BF16)
        a = _dot(x, wa_ref[...])
        u = _dot(x, wu_ref[...])
        part = _dot((a * jax.nn.sigmoid(a) * u).astype(BF16), wd_ref[...])

        @pl.when(f == 0)
        def _():
            acc_ref[...] = part

        @pl.when(f > 0)
        def _():
            acc_ref[...] += part

        @pl.when(f == pl.num_programs(1) - 1)
        def _():
            ys_ref[...] = acc_ref[...]


def _experts(xs, tile_expert, n_valid, wgu, wd):
    n_rows = xs.shape[0]
    nf = 2
    tf = EXPERT_DIM // nf
    row_blk = lambda i, f, te, nv: (jnp.minimum(i, nv[0] - 1), 0)
    grid_spec = pltpu.PrefetchScalarGridSpec(
        num_scalar_prefetch=2,
        grid=(n_rows // TE, nf),
        in_specs=[pl.BlockSpec((TE, D_MODEL), row_blk),
                  pl.BlockSpec((None, D_MODEL, tf), lambda i, f, te, nv: (te[i], 0, f)),
                  pl.BlockSpec((None, D_MODEL, tf), lambda i, f, te, nv: (te[i], 0, nf + f)),
                  pl.BlockSpec((None, tf, D_MODEL), lambda i, f, te, nv: (te[i], f, 0))],
        out_specs=pl.BlockSpec((TE, D_MODEL), row_blk),
        scratch_shapes=[pltpu.VMEM((TE, D_MODEL), F32)],
    )
    return pl.pallas_call(
        _experts_kernel,
        grid_spec=grid_spec,
        out_shape=jax.ShapeDtypeStruct((n_rows, D_MODEL), F32),
        compiler_params=_cparams(("arbitrary", "arbitrary")),
        name="moe_experts",
    )(tile_expert, n_valid, xs, wgu, wgu, wd)


def _combine_kernel(pos_ref, x_ref, route_ref, mod_ref, gpost_ref, ys_ref, o_ref, buf_ref, sem):
    def copy_of(r, slot):
        p = pos_ref[0, 2 * r + slot]
        return pltpu.make_async_copy(ys_ref.at[pl.ds(p, 1)], buf_ref.at[slot, pl.ds(r, 1)], sem)

    n = x_ref.shape[0]
    _row_copies(n, copy_of, lambda slot: pltpu.make_async_copy(ys_ref.at[pl.ds(0, n)], buf_ref.at[slot], sem))
    route = route_ref[...]
    y = route[:, R_G1:R_G1 + 1] * buf_ref[0] + route[:, R_G2:R_G2 + 1] * buf_ref[1]
    mod = mod_ref[...]
    o_ref[...] = x_ref[...] + mod[5:6] * _rms(y, gpost_ref[...])


def _combine(x_all, route, pos, ys, mods, gpost, n_lat_tiles, n_tiles):
    B = x_all.shape[0]
    tok = lambda w: pl.BlockSpec((None, TM, w), lambda b, t: (b, t, 0))
    return pl.pallas_call(
        _combine_kernel,
        grid=(B, n_tiles),
        in_specs=[pl.BlockSpec((None, None, 1, 2 * TM), lambda b, t: (b, t, 0, 0), memory_space=pltpu.SMEM),
                  tok(D_MODEL), tok(LANES),
                  pl.BlockSpec((None, None, N_MOD, D_MODEL),
                               lambda b, t: (b, jnp.where(t >= n_lat_tiles, 1, 0), 0, 0)),
                  _full_spec((1, D_MODEL)),
                  pl.BlockSpec(memory_space=pl.ANY)],
        out_specs=tok(D_MODEL),
        out_shape=jax.ShapeDtypeStruct((B, n_tiles * TM, D_MODEL), F32),
        scratch_shapes=[pltpu.VMEM((2, TM, D_MODEL), F32), pltpu.SemaphoreType.DMA],
        compiler_params=_cparams(("arbitrary", "arbitrary")),
        name="moe_combine",
    )(pos.reshape(B, n_tiles, 1, 2 * TM), x_all, route, mods, gpost, ys)


def _routed_ffn(x_all, mods, gpre, gpost, w_router, wgu, wd, n_lat_tiles, n_tiles):
    B = x_all.shape[0]
    n_tok = B * n_tiles * TM
    idx = jnp.arange(TM)
    tri = jnp.where(idx[:, None] > idx[None, :], 1.0, 0.0).astype(BF16)
    h, route, counts = _moe_pre(x_all, mods, gpre, w_router, tri, n_lat_tiles, n_tiles)
    rec = route.reshape(n_tok, LANES)
    expert = rec[:, R_E1:R_E2 + 1].astype(jnp.int32)
    rank = rec[:, R_RANK1:R_RANK2 + 1].astype(jnp.int32)
    tiles_per_expert = (counts[0, :N_EXPERTS].astype(jnp.int32) + TE - 1) // TE
    tile_end = jnp.cumsum(tiles_per_expert)
    n_valid = tile_end[-1:]
    pos = (tile_end - tiles_per_expert)[expert] * TE + rank
    n_tiles_max = (2 * n_tok) // TE + N_EXPERTS
    tile_id = jnp.minimum(jnp.arange(n_tiles_max, dtype=jnp.int32), n_valid - 1)
    tile_expert = jnp.minimum(jnp.searchsorted(tile_end, tile_id, side="right"), N_EXPERTS - 1).astype(jnp.int32)
    xs = _dispatch(h.reshape(n_tok, D_MODEL), pos, n_tiles_max * TE)
    ys = _experts(xs, tile_expert, n_valid.astype(jnp.int32), wgu, wd)
    return _combine(x_all, route, pos, ys, mods, gpost, n_lat_tiles, n_tiles)


def _rope_angles(n_lat, rot_dim):
    rows = n_lat // GRID_W
    row = jnp.repeat(jnp.arange(rows, dtype=F32), GRID_W)
    col = jnp.tile(jnp.arange(GRID_W, dtype=F32), rows)
    axis_dim = rot_dim // 2
    inv_freq = 1.0 / (ROPE_BASE ** (jnp.arange(0, axis_dim, 2, dtype=F32) / axis_dim))
    ar = row[:, None] * inv_freq
    ac = col[:, None] * inv_freq
    return jnp.concatenate([ar, ar, ac, ac], axis=-1)


def _constants(n_lat, n_ctx):
    lane = jnp.arange(LANES)
    ang = _rope_angles(n_lat, HEAD_DIM)
    cos_h = jnp.tile(jnp.cos(ang), (1, 2))
    sin_h = jnp.tile(jnp.sin(ang), (1, 2)) * jnp.where((lane % 32) < 16, -1.0, 1.0)
    cos_h = jnp.concatenate([cos_h, jnp.ones((n_ctx, LANES), F32)], axis=0)
    sin_h = jnp.concatenate([sin_h, jnp.zeros((n_ctx, LANES), F32)], axis=0)
    angm = _rope_angles(n_lat, MLA_ROPE)
    pad_l = jnp.zeros((n_lat, MLA_NOPE), F32)
    pad_r = jnp.zeros((n_lat, LANES - MLA_NOPE - MLA_ROPE), F32)
    cos_m = jnp.concatenate([pad_l + 1.0, jnp.cos(angm), pad_r + 1.0], axis=1)
    sin_m = jnp.concatenate([pad_l, jnp.sin(angm), pad_r], axis=1) * jnp.where((lane % 16) < 8, -1.0, 1.0)
    cos_m = jnp.concatenate([cos_m, jnp.ones((n_ctx, LANES), F32)], axis=0)
    sin_m = jnp.concatenate([sin_m, jnp.zeros((n_ctx, LANES), F32)], axis=0)
    idx = jnp.arange(512)
    seg = jnp.where((idx[:, None] // HEAD_DIM) == (idx[None, :] // HEAD_DIM), 1.0 / HEAD_DIM, 0.0).astype(BF16)

    def dft(n):
        i = jnp.arange(n, dtype=jnp.int32)
        a = ((i[:, None] * i[None, :]) % n).astype(F32) * (2.0 * math.pi / n)
        s = n ** -0.5
        return jnp.cos(a) * s, jnp.sin(a) * s

    cc, sc = dft(FOURIER_GROUP)
    cx, sx = dft(n_ctx)
    r = n_lat // GRID_W
    cr, sr = (t * r ** 0.5 for t in dft(r))
    c64, s64 = (t * GRID_W ** 0.5 * n_lat ** -0.5 for t in dft(GRID_W))
    k1 = jnp.arange(r, dtype=jnp.int32)[:, None]
    n2 = jnp.arange(GRID_W, dtype=jnp.int32)[None, :]
    tw = ((k1 * n2) % n_lat).astype(F32) * (2.0 * math.pi / n_lat)
    wide = lambda t: jnp.broadcast_to(t[:, :, None], (r, GRID_W, LANES))
    return {
        "cos_h": cos_h, "sin_h": sin_h, "cos_m": cos_m, "sin_m": sin_m, "seg": seg,
        "cc": cc.astype(BF16), "sc": sc.astype(BF16),
        "m1": jnp.concatenate([jnp.concatenate([cr, -sr], axis=1),
                               jnp.concatenate([-sr, -cr], axis=1)], axis=0).astype(BF16),
        "tw_cos": wide(jnp.cos(tw)), "tw_sin": wide(jnp.sin(tw)),
        "m3": jnp.concatenate([c64, s64], axis=1).astype(BF16),
        "w_ctx": jnp.concatenate([cx, -sx], axis=1).astype(BF16),
    }


_GQA_HEAD_ORDER = (0, 4, 1, 5, 2, 6, 3, 7)


def _layer_weights(l, w_in, gqa_q_norm, gqa_k_norm, mla_q_norm, mla_kv_norm, mla_w_q_up, mla_w_kv_up,
                   w_branch, w_gate, b_gate, w_out):
    wi = w_in[l]
    gq = wi[:, 2048:2560].reshape(D_MODEL, GQA_Q_HEADS, HEAD_DIM)[:, jnp.array(_GQA_HEAD_ORDER)]
    zeros = lambda n: jnp.zeros((D_MODEL, n), F32)
    w_in_p = jnp.concatenate([wi[:, :2048], gq.reshape(D_MODEL, 512), wi[:, 2560:3200],
                              zeros(MLA_NOPE), wi[:, 3200:3232], zeros(LANES - MLA_NOPE - MLA_ROPE)], axis=1)
    qu = mla_w_q_up[l].reshape(MLA_Q_RANK, MLA_HEADS, MLA_NOPE + MLA_ROPE)
    qu = jnp.pad(qu, ((0, 0), (0, 0), (0, LANES - MLA_NOPE - MLA_ROPE))).reshape(MLA_Q_RANK, MLA_HEADS * LANES)
    kvu = mla_w_kv_up[l].reshape(MLA_KV_RANK, MLA_HEADS, MLA_NOPE + MLA_V)
    ku = jnp.pad(kvu[:, :, :MLA_NOPE], ((0, 0), (0, 0), (0, LANES - MLA_NOPE))).reshape(MLA_KV_RANK, -1)
    vu = kvu[:, :, MLA_NOPE:].reshape(MLA_KV_RANK, MLA_HEADS * MLA_V)
    wb = w_branch[l]
    wb_gqa = wb[2].reshape(GQA_Q_HEADS, HEAD_DIM, D_MODEL)[jnp.array(_GQA_HEAD_ORDER)].reshape(BRANCH_W, D_MODEL)
    wb = jnp.stack([wb[0], wb[1], wb_gqa, wb[3]])
    return {
        "w_in": w_in_p.astype(BF16),
        "gq_norm": jnp.tile(gqa_q_norm[l], GQA_Q_HEADS).reshape(1, 512),
        "gk_norm": jnp.tile(gqa_k_norm[l], GQA_KV_HEADS).reshape(1, LANES),
        "mq_norm": mla_q_norm[l].reshape(1, MLA_Q_RANK),
        "mkv_norm": mla_kv_norm[l].reshape(1, MLA_KV_RANK),
        "w_q_up": qu.astype(BF16), "w_k_up": ku.astype(BF16), "w_v_up": vu.astype(BF16),
        "w_branch": wb.astype(BF16), "w_gate": w_gate[l].astype(BF16), "b_gate": b_gate[l],
        "w_out": w_out[l].astype(BF16),
    }


def kernel(x, c, ctx, c_ctx, w_mod, b_mod, mix_pre_norm, mix_post_norm, ffn_pre_norm, ffn_post_norm,
           w_in, diff_lambda, diff_subnorm, gqa_q_norm, gqa_k_norm, mla_q_norm, mla_kv_norm,
           mla_w_q_up, mla_w_kv_up, w_branch, w_gate, b_gate, w_out,
           dense_w_gate_up, dense_w_down, moe_router, moe_w_gate_up, moe_w_down):
    B, n_lat, d = x.shape
    n_ctx = ctx.shape[1]
    depth = w_mod.shape[0]
    assert d == D_MODEL and n_lat % TM == 0 and n_ctx % TM == 0 and n_lat % n_ctx == 0
    assert n_lat % GRID_W == 0
    T = n_lat + n_ctx
    n_lat_tiles, n_all_tiles = n_lat // TM, T // TM
    consts = _constants(n_lat, n_ctx)
    x_all = jnp.concatenate([x, ctx], axis=1)
    c_rows = jnp.pad(jnp.concatenate([c, c_ctx[None]], axis=0), ((0, (-(B + 1)) % 8), (0, 0)))
    row = lambda v: v.reshape(1, -1)

    for layer in range(depth):
        last = layer == depth - 1
        n_tiles = n_lat_tiles if last else n_all_tiles
        lam_init = 0.8 - 0.6 * math.exp(-0.3 * layer)
        wts = _layer_weights(layer, w_in, gqa_q_norm, gqa_k_norm, mla_q_norm, mla_kv_norm, mla_w_q_up,
                             mla_w_kv_up, w_branch, w_gate, b_gate, w_out)
        mod_rows = _modulation(c_rows, w_mod[layer], b_mod[layer])
        mods = jnp.stack([mod_rows[:B].reshape(B, N_MOD, D_MODEL),
                          jnp.broadcast_to(mod_rows[B].reshape(1, N_MOD, D_MODEL), (B, N_MOD, D_MODEL))], axis=1)

        (h, dq, dk, dv, gq, gk, gv, mq, mk, mv, pq_lat, pq_ctx) = _project(
            x_all, mods, row(mix_pre_norm[layer]), wts, consts, n_lat_tiles)

        diff_extra = (diff_lambda[layer], diff_subnorm[layer].reshape(1, LANES))
        ys = []
        for mode, q, k, v in (("diff", dq, dk, dv), ("gqa", gq, gk, gv), ("mla", mq, mk, mv)):
            y = jnp.zeros((B, T, BRANCH_W), BF16)
            kw = dict(lam_init=lam_init, extra=diff_extra)
            y = _attention(mode, q, k, v, y, tq=TQ, nq=n_lat // TQ, q_blk0=0, tk=T, k_blk=0, **kw)
            if not last:
                y = _attention(mode, q, k, v, y, tq=TM, nq=n_ctx // TM, q_blk0=n_lat_tiles,
                               tk=n_ctx, k_blk=n_lat // n_ctx, **kw)
            ys.append(y)
        y_diff, y_gqa, y_mla = ys

        a_rows = _dft_rows(pq_lat, consts["m1"])
        y_f = _dft_cols(a_rows, consts["tw_cos"], consts["tw_sin"], consts["m3"], BRANCH_W).reshape(B, n_lat, BRANCH_W)
        if not last:
            y_f = jnp.concatenate([y_f, _dft_ctx(pq_ctx, consts["w_ctx"])], axis=1)

        x_all = _merge(x_all, h, y_diff, y_f, y_gqa, y_mla, mods, wts, row(mix_post_norm[layer]),
                       n_lat_tiles, n_tiles)

        if layer % 2 == 0:
            x_all = _dense_ffn(x_all, mods, row(ffn_pre_norm[layer]), row(ffn_post_norm[layer]),
                               dense_w_gate_up[layer // 2].astype(BF16), dense_w_down[layer // 2].astype(BF16),
                               n_lat_tiles, n_tiles)
        else:
            wr = jnp.pad(moe_router[layer // 2], ((0, 0), (0, LANES - N_EXPERTS)))
            x_all = _routed_ffn(x_all, mods, row(ffn_pre_norm[layer]), row(ffn_post_norm[layer]), wr,
                                moe_w_gate_up[layer // 2].astype(BF16), moe_w_down[layer // 2].astype(BF16),
                                n_lat_tiles, n_tiles)
    return x_all[:, :n_lat]
```

```python
import functools
import math

import jax
import jax.numpy as jnp
from jax import lax
from jax.experimental import pallas as pl
from jax.experimental.pallas import tpu as pltpu

F32 = jnp.float32
BF16 = jnp.bfloat16

D_MODEL = 1024
GRID_W = 64
ROPE_BASE = 10000.0
EPS = 1e-6
N_MOD = 6
HEAD_DIM = 64
DIFF_HEADS = 4
GQA_Q_HEADS = 8
GQA_KV_HEADS = 2
MLA_HEADS = 8
MLA_Q_RANK = 256
MLA_KV_RANK = 128
MLA_NOPE = 64
MLA_ROPE = 32
MLA_V = 64
BRANCH_W = 512
FOURIER_GROUP = 128
FFN_DIM = 2816
N_EXPERTS = 8
EXPERT_DIM = 3584
LOG2E = math.log2(math.e)
QK_SCALE = HEAD_DIM ** -0.5 * LOG2E
MLA_SCALE = (MLA_NOPE + MLA_ROPE) ** -0.5 * LOG2E
IN_COLS_PADDED = 3328

LANES = 128
VMEM_LIMIT = 56 * 1024 * 1024

TM = 256
TQ = 512


def _cparams(sem):
    return pltpu.CompilerParams(dimension_semantics=sem, vmem_limit_bytes=VMEM_LIMIT)


def _full_spec(shape):
    n = len(shape)
    return pl.BlockSpec(shape, lambda *_: (0,) * n)


def _split_bf16(x):
    hi = x.astype(BF16)
    lo = (x - hi.astype(F32)).astype(BF16)
    return hi, lo


def _dot(a, b):
    return jnp.dot(a, b, preferred_element_type=F32)


def _dot3(a, w):
    ah, al = _split_bf16(a)
    wh, wl = _split_bf16(w)
    return _dot(ah, wh) + _dot(ah, wl) + _dot(al, wh)


def _rms(x, gain):
    ms = jnp.mean(x * x, axis=-1, keepdims=True)
    return x * lax.rsqrt(ms + EPS) * gain


def _mod_kernel(c_ref, w_ref, b_ref, o_ref):
    c = c_ref[...]
    a = c * jax.nn.sigmoid(c)
    o_ref[...] = _dot3(a, w_ref[...]) + b_ref[...]


def _modulation(c_rows, w_mod_l, b_mod_l):
    rows = c_rows.shape[0]
    n = w_mod_l.shape[1]
    tn = 512
    return pl.pallas_call(
        _mod_kernel,
        grid=(n // tn,),
        in_specs=[pl.BlockSpec((rows, D_MODEL), lambda j: (0, 0)),
                  pl.BlockSpec((D_MODEL, tn), lambda j: (0, j)),
                  pl.BlockSpec((1, tn), lambda j: (0, j))],
        out_specs=pl.BlockSpec((rows, tn), lambda j: (0, j)),
        out_shape=jax.ShapeDtypeStruct((rows, n), F32),
        compiler_params=_cparams(("arbitrary",)),
        name="modulation",
    )(c_rows, w_mod_l, b_mod_l.reshape(1, n))


def _rope(x, cos, sin_signed, half):
    lane = lax.broadcasted_iota(jnp.int32, x.shape, 1)
    first = (lane % (2 * half)) < half
    up = pltpu.roll(x, LANES - half, 1)
    down = pltpu.roll(x, half, 1)
    return x * cos + jnp.where(first, up, down) * sin_signed


def _proj_kernel(n_lat_tiles, x_ref, mod_ref, gpre_ref, win_ref, gqn_ref, gkn_ref, mqn_ref, mkvn_ref,
                 wqup_ref, wkup_ref, wvup_ref, seg_ref, cc_ref, sc_ref,
                 cosh_ref, sinh_ref, cosm_ref, sinm_ref,
                 h_ref, dq_ref, dk_ref, dv_ref, gq_ref, gk_ref, gv_ref, mq_ref, mk_ref, mv_ref, pq_ref, pqc_ref):
    x = x_ref[...]
    mod = mod_ref[...]
    h = _rms(x, gpre_ref[...]) * (1.0 + mod[1:2]) + mod[0:1]
    hb = h.astype(BF16)
    h_ref[...] = hb
    z = _dot(hb, win_ref[...])
    cosh, sinh = cosh_ref[...], sinh_ref[...]
    cosm, sinm = cosm_ref[...], sinm_ref[...]
    seg = seg_ref[...]

    def head_rms(v, gain, w):
        sq = v * v
        hi, lo = _split_bf16(sq)
        ms = _dot(hi, seg[:w, :w]) + _dot(lo, seg[:w, :w])
        return v * lax.rsqrt(ms + EPS) * gain

    for j in range(4):
        c0 = j * LANES
        dq_ref[:, c0:c0 + LANES] = (_rope(z[:, c0:c0 + LANES], cosh, sinh, 16) * QK_SCALE).astype(BF16)
        dk_ref[:, c0:c0 + LANES] = _rope(z[:, 512 + c0:512 + c0 + LANES], cosh, sinh, 16).astype(BF16)
    dv_ref[...] = z[:, 1024:1536].astype(BF16)
    fparts = []
    for g in range(4):
        c0 = 1536 + g * LANES
        fb = z[:, c0:c0 + LANES].astype(BF16)
        fparts.append((_dot(fb, cc_ref[...]).astype(BF16), _dot(fb, sc_ref[...]).astype(BF16)))
    is_lat = pl.program_id(1) < n_lat_tiles

    def put(ref):
        for g, (pc, ps) in enumerate(fparts):
            ref[0, :, g * LANES:(g + 1) * LANES] = pc
            ref[1, :, g * LANES:(g + 1) * LANES] = ps

    pl.when(is_lat)(lambda: put(pq_ref))
    pl.when(jnp.logical_not(is_lat))(lambda: put(pqc_ref))
    gq = head_rms(z[:, 2048:2560], gqn_ref[...], 512)
    for j in range(4):
        c0 = j * LANES
        gq_ref[:, c0:c0 + LANES] = (_rope(gq[:, c0:c0 + LANES], cosh, sinh, 16) * QK_SCALE).astype(BF16)
    gk = head_rms(z[:, 2560:2688], gkn_ref[...], LANES)
    gk_ref[...] = _rope(gk, cosh, sinh, 16).astype(BF16)
    gv_ref[...] = z[:, 2688:2816].astype(BF16)
    cq = _rms(z[:, 2816:3072], mqn_ref[...]).astype(BF16)
    mq = _dot(cq, wqup_ref[...])
    ckv = _rms(z[:, 3072:3200], mkvn_ref[...]).astype(BF16)
    mk = _dot(ckv, wkup_ref[...])
    mv_ref[...] = _dot(ckv, wvup_ref[...]).astype(BF16)
    kr = _rope(z[:, 3200:3328], cosm, sinm, 8)
    for hh in range(MLA_HEADS):
        c0 = hh * LANES
        mq_ref[:, c0:c0 + LANES] = (_rope(mq[:, c0:c0 + LANES], cosm, sinm, 8) * MLA_SCALE).astype(BF16)
        mk_ref[:, c0:c0 + LANES] = (mk[:, c0:c0 + LANES] + kr).astype(BF16)


def _project(x_all, mods, gpre, wts, consts, n_lat_tiles):
    B, T, _ = x_all.shape
    nt = T // TM
    tok = lambda w: pl.BlockSpec((None, TM, w), lambda b, t: (b, t, 0))
    tab = pl.BlockSpec((TM, LANES), lambda b, t: (t, 0))
    in_specs = [
        tok(D_MODEL),
        pl.BlockSpec((None, None, N_MOD, D_MODEL), lambda b, t: (b, jnp.where(t >= n_lat_tiles, 1, 0), 0, 0)),
        _full_spec((1, D_MODEL)),
        _full_spec((D_MODEL, IN_COLS_PADDED)),
        _full_spec((1, 512)), _full_spec((1, LANES)), _full_spec((1, MLA_Q_RANK)), _full_spec((1, MLA_KV_RANK)),
        _full_spec((MLA_Q_RANK, MLA_HEADS * LANES)),
        _full_spec((MLA_KV_RANK, MLA_HEADS * LANES)),
        _full_spec((MLA_KV_RANK, MLA_HEADS * MLA_V)),
        _full_spec((512, 512)), _full_spec((LANES, LANES)), _full_spec((LANES, LANES)),
        tab, tab, tab, tab,
    ]
    widths = [D_MODEL, 512, 512, 512, 512, LANES, LANES, MLA_HEADS * LANES, MLA_HEADS * LANES, 512]
    out_specs = [tok(w) for w in widths]
    out_shape = [jax.ShapeDtypeStruct((B, T, w), BF16) for w in widths]
    out_specs.append(pl.BlockSpec((2, None, TM, 512), lambda b, t: (0, b, jnp.minimum(t, n_lat_tiles - 1), 0)))
    out_shape.append(jax.ShapeDtypeStruct((2, B, n_lat_tiles * TM, 512), BF16))
    out_specs.append(pl.BlockSpec((2, None, TM, 512), lambda b, t: (0, b, jnp.maximum(t - n_lat_tiles, 0), 0)))
    out_shape.append(jax.ShapeDtypeStruct((2, B, T - n_lat_tiles * TM, 512), BF16))
    return pl.pallas_call(
        functools.partial(_proj_kernel, n_lat_tiles),
        grid=(B, nt),
        in_specs=in_specs,
        out_specs=out_specs,
        out_shape=out_shape,
        compiler_params=_cparams(("arbitrary", "arbitrary")),
        name="project",
    )(x_all, mods, gpre, wts["w_in"], wts["gq_norm"], wts["gk_norm"], wts["mq_norm"], wts["mkv_norm"],
      wts["w_q_up"], wts["w_k_up"], wts["w_v_up"], consts["seg"], consts["cc"], consts["sc"],
      consts["cos_h"], consts["sin_h"], consts["cos_m"], consts["sin_m"])


N_MAPS = 8
MAPS_PER_STEP = 4


def _map_plan(mode, m):
    if mode == "diff":
        return m // 2, m % 2, m // 2, m // 2
    if mode == "gqa":
        return m // 2, m % 2, 0, 0
    return m, None, m, m // 2


def _attn_kernel(mode, lam_init, q_ref, k_ref, v_ref, *rest):
    if mode == "diff":
        lam_ref, sub_ref, _, o_ref, vx_ref = rest
    else:
        _, o_ref, vx_ref = rest
    nt = (((1,), (1,)), ((), ()))
    lane = lax.broadcasted_iota(jnp.int32, (1, LANES), 1)
    low = lane < HEAD_DIM

    @pl.when(pl.program_id(2) == 0)
    def _():
        for g in range(vx_ref.shape[0]):
            vx_ref[g, :, :LANES] = v_ref[:, g * LANES:(g + 1) * LANES]
            vx_ref[g, :, LANES:] = jnp.ones((vx_ref.shape[1], LANES), BF16)

    def scores(m):
        qblk, half, kblk, _ = _map_plan(mode, m)
        q = q_ref[:, qblk * LANES:(qblk + 1) * LANES]
        if half is not None:
            q = jnp.where(low if half == 0 else jnp.logical_not(low), q, jnp.zeros_like(q))
        return lax.dot_general(q, k_ref[:, kblk * LANES:(kblk + 1) * LANES], nt, preferred_element_type=F32)

    def numerator(s):
        return jnp.exp2(s - jnp.max(s, axis=-1, keepdims=True)).astype(BF16)

    def values(m, p):
        e = _dot(p, vx_ref[_map_plan(mode, m)[3]])
        return e[:, :LANES] / e[:, LANES:]

    s, pl_, y = {}, {}, {}
    for t in range(MAPS_PER_STEP + 2):
        if t < MAPS_PER_STEP:
            s[t] = scores(t)
        if 0 <= t - 1 < MAPS_PER_STEP:
            pl_[t - 1] = numerator(s.pop(t - 1))
        if 0 <= t - 2:
            y[t - 2] = values(t - 2, pl_.pop(t - 2))

    if mode == "diff":
        dl = lam_ref[...]
        lam = (jnp.exp(jnp.sum(dl[0:1] * dl[1:2], axis=-1, keepdims=True))
               - jnp.exp(jnp.sum(dl[2:3] * dl[3:4], axis=-1, keepdims=True)) + lam_init)
    for j in range(MAPS_PER_STEP // 2):
        if mode == "diff":
            out = _rms(y[2 * j] - lam * y[2 * j + 1], sub_ref[...]) * (1.0 - lam_init)
        else:
            out = jnp.where(low, y[2 * j], y[2 * j + 1])
        o_ref[:, j * LANES:(j + 1) * LANES] = out.astype(BF16)


def _attention(mode, q, k, v, y_prev, *, tq, nq, q_blk0, tk, k_blk, lam_init=0.0, extra=()):
    B, T, _ = q.shape
    n_steps = N_MAPS // MAPS_PER_STEP
    wq, wk, wv = q.shape[2] // n_steps, max(k.shape[2] // n_steps, LANES), max(v.shape[2] // n_steps, LANES)
    kv_blk = (lambda g: 0) if mode == "gqa" else (lambda g: g)
    in_specs = [
        pl.BlockSpec((None, tq, wq), lambda b, g, i: (b, i + q_blk0, g)),
        pl.BlockSpec((None, tk, wk), lambda b, g, i: (b, k_blk, kv_blk(g))),
        pl.BlockSpec((None, tk, wv), lambda b, g, i: (b, k_blk, kv_blk(g))),
    ]
    args = [q, k, v]
    if mode == "diff":
        in_specs += [_full_spec((4, HEAD_DIM)), _full_spec((1, LANES))]
        args += list(extra)
    in_specs.append(pl.BlockSpec(memory_space=pl.ANY))
    args.append(y_prev)
    wo = BRANCH_W // n_steps
    return pl.pallas_call(
        functools.partial(_attn_kernel, mode, lam_init),
        grid=(B, n_steps, nq),
        in_specs=in_specs,
        out_specs=pl.BlockSpec((None, tq, wo), lambda b, g, i: (b, i + q_blk0, g)),
        out_shape=jax.ShapeDtypeStruct((B, T, BRANCH_W), BF16),
        input_output_aliases={len(args) - 1: 0},
        scratch_shapes=[pltpu.VMEM((wv // LANES, tk, 2 * LANES), BF16)],
        compiler_params=_cparams(("arbitrary", "arbitrary", "arbitrary")),
        name="attention_" + mode,
    )(*args)


DFT_COLS_PER_STEP = 8


def _dft_rows_kernel(pq_ref, m1_ref, a_ref):
    r = pq_ref.shape[1]
    for n2 in range(pq_ref.shape[2]):
        zin = jnp.concatenate([pq_ref[0, :, n2, :], pq_ref[1, :, n2, :]], axis=0)
        a = _dot(m1_ref[...], zin).astype(BF16)
        a_ref[0, :, n2, :] = a[:r]
        a_ref[1, :, n2, :] = a[r:]


def _dft_rows(pq_lat, m1):
    _, B, n_lat, w = pq_lat.shape
    r = n_lat // GRID_W
    nb = DFT_COLS_PER_STEP
    return pl.pallas_call(
        _dft_rows_kernel,
        grid=(B, GRID_W // nb),
        in_specs=[pl.BlockSpec((2, None, r, nb, w), lambda b, j: (0, b, 0, j, 0)), _full_spec((2 * r, 2 * r))],
        out_specs=pl.BlockSpec((None, 2, r, nb, w), lambda b, j: (b, 0, 0, j, 0)),
        out_shape=jax.ShapeDtypeStruct((B, 2, r, GRID_W, w), BF16),
        compiler_params=_cparams(("arbitrary", "arbitrary")),
        name="dft_rows",
    )(pq_lat.reshape(2, B, r, GRID_W, w), m1)


def _dft_cols_kernel(a_ref, ct_ref, st_ref, m3_ref, o_ref):
    kb = a_ref.shape[1]
    w = a_ref.shape[3]
    for j in range(kb):
        ar = a_ref[0, j].astype(F32)
        ai = a_ref[1, j].astype(F32)
        ct = jnp.concatenate([ct_ref[j]] * (w // LANES), axis=1)
        st = jnp.concatenate([st_ref[j]] * (w // LANES), axis=1)
        b = jnp.concatenate([ar * ct + ai * st, ai * ct - ar * st], axis=0).astype(BF16)
        o_ref[:, j, :] = _dot(m3_ref[...], b).astype(BF16)


def _dft_cols(a, ct, st, m3):
    B, _, r, _, w = a.shape
    kb = DFT_COLS_PER_STEP
    return pl.pallas_call(
        _dft_cols_kernel,
        grid=(B, r // kb),
        in_specs=[pl.BlockSpec((None, 2, kb, GRID_W, w), lambda b, k: (b, 0, k, 0, 0)),
                  pl.BlockSpec((kb, GRID_W, LANES), lambda b, k: (k, 0, 0)),
                  pl.BlockSpec((kb, GRID_W, LANES), lambda b, k: (k, 0, 0)),
                  _full_spec((GRID_W, 2 * GRID_W))],
        out_specs=pl.BlockSpec((None, GRID_W, kb, w), lambda b, k: (b, 0, k, 0)),
        out_shape=jax.ShapeDtypeStruct((B, GRID_W, r, w), BF16),
        compiler_params=_cparams(("arbitrary", "arbitrary")),
        name="dft_cols",
    )(a, ct, st, m3)


def _dft_ctx_kernel(pq_ref, w_ref, o_ref):
    zin = jnp.concatenate([pq_ref[0], pq_ref[1]], axis=0)
    o_ref[...] = _dot(w_ref[...], zin).astype(BF16)


def _dft_ctx(pq_ctx, w_ctx):
    _, B, n_ctx, w = pq_ctx.shape
    return pl.pallas_call(
        _dft_ctx_kernel,
        grid=(B,),
        in_specs=[pl.BlockSpec((2, None, n_ctx, w), lambda b: (0, b, 0, 0)), _full_spec((n_ctx, 2 * n_ctx))],
        out_specs=pl.BlockSpec((None, n_ctx, w), lambda b: (b, 0, 0)),
        out_shape=jax.ShapeDtypeStruct((B, n_ctx, w), BF16),
        compiler_params=_cparams(("arbitrary",)),
        name="dft_ctx",
    )(pq_ctx, w_ctx)


def _merge_kernel(x_ref, h_ref, y0_ref, y1_ref, y2_ref, y3_ref, mod_ref, wg_ref, bg_ref, wb_ref, wo_ref,
                  gpost_ref, o_ref):
    h = h_ref[...]
    ys = (y0_ref, y1_ref, y2_ref, y3_ref)
    acc = None
    for i in range(4):
        gate = jax.nn.sigmoid(_dot(h, wg_ref[i]) + bg_ref[i:i + 1])
        term = gate * _dot(ys[i][...], wb_ref[i])
        acc = term if acc is None else acc + term
    m = _dot(acc.astype(BF16), wo_ref[...])
    mod = mod_ref[...]
    o_ref[...] = x_ref[...] + mod[2:3] * _rms(m, gpost_ref[...])


def _merge(x_all, h, y_diff, y_f, y_gqa, y_mla, mods, wts, gpost, n_lat_tiles, n_tiles):
    B, T, _ = x_all.shape
    tok = lambda w: pl.BlockSpec((None, TM, w), lambda b, t: (b, t, 0))
    in_specs = [
        tok(D_MODEL), tok(D_MODEL), tok(BRANCH_W), tok(BRANCH_W), tok(BRANCH_W), tok(BRANCH_W),
        pl.BlockSpec((None, None, N_MOD, D_MODEL), lambda b, t: (b, jnp.where(t >= n_lat_tiles, 1, 0), 0, 0)),
        _full_spec((4, D_MODEL, D_MODEL)), _full_spec((4, D_MODEL)), _full_spec((4, BRANCH_W, D_MODEL)),
        _full_spec((D_MODEL, D_MODEL)), _full_spec((1, D_MODEL)),
    ]
    return pl.pallas_call(
        _merge_kernel,
        grid=(B, n_tiles),
        in_specs=in_specs,
        out_specs=tok(D_MODEL),
        out_shape=jax.ShapeDtypeStruct((B, n_tiles * TM, D_MODEL), F32),
        compiler_params=_cparams(("arbitrary", "arbitrary")),
        name="merge",
    )(x_all, h, y_diff, y_f, y_gqa, y_mla, mods, wts["w_gate"], wts["b_gate"], wts["w_branch"], wts["w_out"],
      gpost)


def _dense_ffn_kernel(nf, tf, x_ref, mod_ref, gpre_ref, gpost_ref, wgu_ref, wd_ref, o_ref):
    x = x_ref[...]
    mod = mod_ref[...]
    hb = (_rms(x, gpre_ref[...]) * (1.0 + mod[4:5]) + mod[3:4]).astype(BF16)
    acc = None
    for f in range(nf):
        a = _dot(hb, wgu_ref[:, f * tf:(f + 1) * tf])
        u = _dot(hb, wgu_ref[:, FFN_DIM + f * tf:FFN_DIM + (f + 1) * tf])
        act = (a * jax.nn.sigmoid(a) * u).astype(BF16)
        part = _dot(act, wd_ref[f * tf:(f + 1) * tf, :])
        acc = part if acc is None else acc + part
    o_ref[...] = x + mod[5:6] * _rms(acc, gpost_ref[...])


def _dense_ffn(x_all, mods, gpre, gpost, wgu, wd, n_lat_tiles, n_tiles):
    B = x_all.shape[0]
    nf, tf = 2, FFN_DIM // 2
    tok = pl.BlockSpec((None, TM, D_MODEL), lambda b, t: (b, t, 0))
    return pl.pallas_call(
        functools.partial(_dense_ffn_kernel, nf, tf),
        grid=(B, n_tiles),
        in_specs=[tok,
                  pl.BlockSpec((None, None, N_MOD, D_MODEL),
                               lambda b, t: (b, jnp.where(t >= n_lat_tiles, 1, 0), 0, 0)),
                  _full_spec((1, D_MODEL)), _full_spec((1, D_MODEL)),
                  _full_spec((D_MODEL, 2 * FFN_DIM)), _full_spec((FFN_DIM, D_MODEL))],
        out_specs=tok,
        out_shape=jax.ShapeDtypeStruct((B, n_tiles * TM, D_MODEL), F32),
        compiler_params=_cparams(("arbitrary", "arbitrary")),
        name="dense_ffn",
    )(x_all, mods, gpre, gpost, wgu, wd)


TE = 512
R_E1, R_E2, R_G1, R_G2, R_RANK1, R_RANK2 = range(6)


def _moe_pre_kernel(x_ref, mod_ref, gpre_ref, wr_ref, tri_ref, h_ref, route_ref, cnt_ref, carry_ref):
    first = jnp.logical_and(pl.program_id(0) == 0, pl.program_id(1) == 0)

    @pl.when(first)
    def _():
        carry_ref[...] = jnp.zeros_like(carry_ref)

    x = x_ref[...]
    mod = mod_ref[...]
    h = _rms(x, gpre_ref[...]) * (1.0 + mod[4:5]) + mod[3:4]
    h_ref[...] = h
    logits = _dot3(h, wr_ref[...])
    lane = lax.broadcasted_iota(jnp.int32, logits.shape, 1)
    neg = jnp.float32(-jnp.inf)
    l1 = jnp.where(lane < N_EXPERTS, logits, neg)
    m1 = jnp.max(l1, axis=-1, keepdims=True)
    i1 = jnp.min(jnp.where(l1 == m1, lane, LANES), axis=-1, keepdims=True)
    l2 = jnp.where(lane == i1, neg, l1)
    m2 = jnp.max(l2, axis=-1, keepdims=True)
    i2 = jnp.min(jnp.where(l2 == m2, lane, LANES), axis=-1, keepdims=True)
    e2 = jnp.exp(m2 - m1)
    g1 = 1.0 / (1.0 + e2)
    g2 = e2 / (1.0 + e2)
    sel1, sel2 = lane == i1, lane == i2
    chosen = jnp.where(jnp.logical_or(sel1, sel2), 1.0, 0.0)
    before = _dot(tri_ref[...], chosen.astype(BF16)) + carry_ref[...]
    r1 = jnp.sum(jnp.where(sel1, before, 0.0), axis=-1, keepdims=True)
    r2 = jnp.sum(jnp.where(sel2, before, 0.0), axis=-1, keepdims=True)
    carry_ref[...] += jnp.sum(chosen, axis=0, keepdims=True)
    cnt_ref[...] = carry_ref[...]
    rec = jnp.zeros_like(logits)
    for col, val in ((R_E1, i1.astype(F32)), (R_E2, i2.astype(F32)), (R_G1, g1), (R_G2, g2),
                     (R_RANK1, r1), (R_RANK2, r2)):
        rec = jnp.where(lane == col, val, rec)
    route_ref[...] = rec


def _moe_pre(x_all, mods, gpre, w_router, tri, n_lat_tiles, n_tiles):
    B = x_all.shape[0]
    tok = lambda w: pl.BlockSpec((None, TM, w), lambda b, t: (b, t, 0))
    return pl.pallas_call(
        _moe_pre_kernel,
        grid=(B, n_tiles),
        in_specs=[tok(D_MODEL),
                  pl.BlockSpec((None, None, N_MOD, D_MODEL),
                               lambda b, t: (b, jnp.where(t >= n_lat_tiles, 1, 0), 0, 0)),
                  _full_spec((1, D_MODEL)), _full_spec((D_MODEL, LANES)), _full_spec((TM, TM))],
        out_specs=[tok(D_MODEL), tok(LANES), _full_spec((1, LANES))],
        out_shape=[jax.ShapeDtypeStruct((B, n_tiles * TM, D_MODEL), F32),
                   jax.ShapeDtypeStruct((B, n_tiles * TM, LANES), F32),
                   jax.ShapeDtypeStruct((1, LANES), F32)],
        scratch_shapes=[pltpu.VMEM((1, LANES), F32)],
        compiler_params=_cparams(("arbitrary", "arbitrary")),
        name="moe_router",
    )(x_all, mods, gpre, w_router, tri)


def _row_copies(n_rows, copy_of, slot_wait):
    def start(r, carry):
        copy_of(r, 0).start()
        copy_of(r, 1).start()
        return carry

    lax.fori_loop(0, n_rows, start, 0, unroll=8)
    slot_wait(0).wait()
    slot_wait(1).wait()


def _dispatch_kernel(pos_ref, h_ref, _, xs_ref, sem):
    def copy_of(r, slot):
        p = pos_ref[0, 2 * r + slot]
        return pltpu.make_async_copy(h_ref.at[pl.ds(r, 1)], xs_ref.at[pl.ds(p, 1)], sem)

    n = h_ref.shape[0]
    _row_copies(n, copy_of, lambda slot: pltpu.make_async_copy(h_ref, xs_ref.at[pl.ds(0, n)], sem))


def _dispatch(h, pos, n_rows):
    n = h.shape[0]
    xs0 = jnp.zeros((n_rows, D_MODEL), F32)
    return pl.pallas_call(
        _dispatch_kernel,
        grid=(n // TM,),
        in_specs=[pl.BlockSpec((None, 1, 2 * TM), lambda i: (i, 0, 0), memory_space=pltpu.SMEM),
                  pl.BlockSpec((TM, D_MODEL), lambda i: (i, 0)),
                  pl.BlockSpec(memory_space=pl.ANY)],
        out_specs=pl.BlockSpec(memory_space=pl.ANY),
        out_shape=jax.ShapeDtypeStruct((n_rows, D_MODEL), F32),
        input_output_aliases={2: 0},
        scratch_shapes=[pltpu.SemaphoreType.DMA],
        compiler_params=_cparams(("arbitrary",)),
        name="moe_dispatch",
    )(pos.reshape(n // TM, 1, 2 * TM), h, xs0)


def _experts_kernel(te_ref, nv_ref, xs_ref, wa_ref, wu_ref, wd_ref, ys_ref, acc_ref):
    del te_ref
    f = pl.program_id(1)

    @pl.when(pl.program_id(0) < nv_ref[0])
    def _():
        x = xs_ref[...].astype(BF16)
        a = _dot(x, wa_ref[...])
        u = _dot(x, wu_ref[...])
        part = _dot((a * jax.nn.sigmoid(a) * u).astype(BF16), wd_ref[...])

        @pl.when(f == 0)
        def _():
            acc_ref[...] = part

        @pl.when(f > 0)
        def _():
            acc_ref[...] += part

        @pl.when(f == pl.num_programs(1) - 1)
        def _():
            ys_ref[...] = acc_ref[...]


def _experts(xs, tile_expert, n_valid, wgu, wd):
    n_rows = xs.shape[0]
    nf = 2
    tf = EXPERT_DIM // nf
    row_blk = lambda i, f, te, nv: (jnp.minimum(i, nv[0] - 1), 0)
    grid_spec = pltpu.PrefetchScalarGridSpec(
        num_scalar_prefetch=2,
        grid=(n_rows // TE, nf),
        in_specs=[pl.BlockSpec((TE, D_MODEL), row_blk),
                  pl.BlockSpec((None, D_MODEL, tf), lambda i, f, te, nv: (te[i], 0, f)),
                  pl.BlockSpec((None, D_MODEL, tf), lambda i, f, te, nv: (te[i], 0, nf + f)),
                  pl.BlockSpec((None, tf, D_MODEL), lambda i, f, te, nv: (te[i], f, 0))],
        out_specs=pl.BlockSpec((TE, D_MODEL), row_blk),
        scratch_shapes=[pltpu.VMEM((TE, D_MODEL), F32)],
    )
    return pl.pallas_call(
        _experts_kernel,
        grid_spec=grid_spec,
        out_shape=jax.ShapeDtypeStruct((n_rows, D_MODEL), F32),
        compiler_params=_cparams(("arbitrary", "arbitrary")),
        name="moe_experts",
    )(tile_expert, n_valid, xs, wgu, wgu, wd)


def _combine_kernel(pos_ref, x_ref, route_ref, mod_ref, gpost_ref, ys_ref, o_ref, buf_ref, sem):
    def copy_of(r, slot):
        p = pos_ref[0, 2 * r + slot]
        return pltpu.make_async_copy(ys_ref.at[pl.ds(p, 1)], buf_ref.at[slot, pl.ds(r, 1)], sem)

    n = x_ref.shape[0]
    _row_copies(n, copy_of, lambda slot: pltpu.make_async_copy(ys_ref.at[pl.ds(0, n)], buf_ref.at[slot], sem))
    route = route_ref[...]
    y = route[:, R_G1:R_G1 + 1] * buf_ref[0] + route[:, R_G2:R_G2 + 1] * buf_ref[1]
    mod = mod_ref[...]
    o_ref[...] = x_ref[...] + mod[5:6] * _rms(y, gpost_ref[...])


def _combine(x_all, route, pos, ys, mods, gpost, n_lat_tiles, n_tiles):
    B = x_all.shape[0]
    tok = lambda w: pl.BlockSpec((None, TM, w), lambda b, t: (b, t, 0))
    return pl.pallas_call(
        _combine_kernel,
        grid=(B, n_tiles),
        in_specs=[pl.BlockSpec((None, None, 1, 2 * TM), lambda b, t: (b, t, 0, 0), memory_space=pltpu.SMEM),
                  tok(D_MODEL), tok(LANES),
                  pl.BlockSpec((None, None, N_MOD, D_MODEL),
                               lambda b, t: (b, jnp.where(t >= n_lat_tiles, 1, 0), 0, 0)),
                  _full_spec((1, D_MODEL)),
                  pl.BlockSpec(memory_space=pl.ANY)],
        out_specs=tok(D_MODEL),
        out_shape=jax.ShapeDtypeStruct((B, n_tiles * TM, D_MODEL), F32),
        scratch_shapes=[pltpu.VMEM((2, TM, D_MODEL), F32), pltpu.SemaphoreType.DMA],
        compiler_params=_cparams(("arbitrary", "arbitrary")),
        name="moe_combine",
    )(pos.reshape(B, n_tiles, 1, 2 * TM), x_all, route, mods, gpost, ys)


def _routed_ffn(x_all, mods, gpre, gpost, w_router, wgu, wd, n_lat_tiles, n_tiles):
    B = x_all.shape[0]
    n_tok = B * n_tiles * TM
    idx = jnp.arange(TM)
    tri = jnp.where(idx[:, None] > idx[None, :], 1.0, 0.0).astype(BF16)
    h, route, counts = _moe_pre(x_all, mods, gpre, w_router, tri, n_lat_tiles, n_tiles)
    rec = route.reshape(n_tok, LANES)
    expert = rec[:, R_E1:R_E2 + 1].astype(jnp.int32)
    rank = rec[:, R_RANK1:R_RANK2 + 1].astype(jnp.int32)
    tiles_per_expert = (counts[0, :N_EXPERTS].astype(jnp.int32) + TE - 1) // TE
    tile_end = jnp.cumsum(tiles_per_expert)
    n_valid = tile_end[-1:]
    pos = (tile_end - tiles_per_expert)[expert] * TE + rank
    n_tiles_max = (2 * n_tok) // TE + N_EXPERTS
    tile_id = jnp.minimum(jnp.arange(n_tiles_max, dtype=jnp.int32), n_valid - 1)
    tile_expert = jnp.minimum(jnp.searchsorted(tile_end, tile_id, side="right"), N_EXPERTS - 1).astype(jnp.int32)
    xs = _dispatch(h.reshape(n_tok, D_MODEL), pos, n_tiles_max * TE)
    ys = _experts(xs, tile_expert, n_valid.astype(jnp.int32), wgu, wd)
    return _combine(x_all, route, pos, ys, mods, gpost, n_lat_tiles, n_tiles)


def _rope_angles(n_lat, rot_dim):
    rows = n_lat // GRID_W
    row = jnp.repeat(jnp.arange(rows, dtype=F32), GRID_W)
    col = jnp.tile(jnp.arange(GRID_W, dtype=F32), rows)
    axis_dim = rot_dim // 2
    inv_freq = 1.0 / (ROPE_BASE ** (jnp.arange(0, axis_dim, 2, dtype=F32) / axis_dim))
    ar = row[:, None] * inv_freq
    ac = col[:, None] * inv_freq
    return jnp.concatenate([ar, ar, ac, ac], axis=-1)


def _constants(n_lat, n_ctx):
    lane = jnp.arange(LANES)
    ang = _rope_angles(n_lat, HEAD_DIM)
    cos_h = jnp.tile(jnp.cos(ang), (1, 2))
    sin_h = jnp.tile(jnp.sin(ang), (1, 2)) * jnp.where((lane % 32) < 16, -1.0, 1.0)
    cos_h = jnp.concatenate([cos_h, jnp.ones((n_ctx, LANES), F32)], axis=0)
    sin_h = jnp.concatenate([sin_h, jnp.zeros((n_ctx, LANES), F32)], axis=0)
    angm = _rope_angles(n_lat, MLA_ROPE)
    pad_l = jnp.zeros((n_lat, MLA_NOPE), F32)
    pad_r = jnp.zeros((n_lat, LANES - MLA_NOPE - MLA_ROPE), F32)
    cos_m = jnp.concatenate([pad_l + 1.0, jnp.cos(angm), pad_r + 1.0], axis=1)
    sin_m = jnp.concatenate([pad_l, jnp.sin(angm), pad_r], axis=1) * jnp.where((lane % 16) < 8, -1.0, 1.0)
    cos_m = jnp.concatenate([cos_m, jnp.ones((n_ctx, LANES), F32)], axis=0)
    sin_m = jnp.concatenate([sin_m, jnp.zeros((n_ctx, LANES), F32)], axis=0)
    idx = jnp.arange(512)
    seg = jnp.where((idx[:, None] // HEAD_DIM) == (idx[None, :] // HEAD_DIM), 1.0 / HEAD_DIM, 0.0).astype(BF16)

    def dft(n):
        i = jnp.arange(n, dtype=jnp.int32)
        a = ((i[:, None] * i[None, :]) % n).astype(F32) * (2.0 * math.pi / n)
        s = n ** -0.5
        return jnp.cos(a) * s, jnp.sin(a) * s

    cc, sc = dft(FOURIER_GROUP)
    cx, sx = dft(n_ctx)
    r = n_lat // GRID_W
    cr, sr = (t * r ** 0.5 for t in dft(r))
    c64, s64 = (t * GRID_W ** 0.5 * n_lat ** -0.5 for t in dft(GRID_W))
    k1 = jnp.arange(r, dtype=jnp.int32)[:, None]
    n2 = jnp.arange(GRID_W, dtype=jnp.int32)[None, :]
    tw = ((k1 * n2) % n_lat).astype(F32) * (2.0 * math.pi / n_lat)
    wide = lambda t: jnp.broadcast_to(t[:, :, None], (r, GRID_W, LANES))
    return {
        "cos_h": cos_h, "sin_h": sin_h, "cos_m": cos_m, "sin_m": sin_m, "seg": seg,
        "cc": cc.astype(BF16), "sc": sc.astype(BF16),
        "m1": jnp.concatenate([jnp.concatenate([cr, -sr], axis=1),
                               jnp.concatenate([-sr, -cr], axis=1)], axis=0).astype(BF16),
        "tw_cos": wide(jnp.cos(tw)), "tw_sin": wide(jnp.sin(tw)),
        "m3": jnp.concatenate([c64, s64], axis=1).astype(BF16),
        "w_ctx": jnp.concatenate([cx, -sx], axis=1).astype(BF16),
    }


_GQA_HEAD_ORDER = (0, 4, 1, 5, 2, 6, 3, 7)


def _layer_weights(l, w_in, gqa_q_norm, gqa_k_norm, mla_q_norm, mla_kv_norm, mla_w_q_up, mla_w_kv_up,
                   w_branch, w_gate, b_gate, w_out):
    wi = w_in[l]
    gq = wi[:, 2048:2560].reshape(D_MODEL, GQA_Q_HEADS, HEAD_DIM)[:, jnp.array(_GQA_HEAD_ORDER)]
    zeros = lambda n: jnp.zeros((D_MODEL, n), F32)
    w_in_p = jnp.concatenate([wi[:, :2048], gq.reshape(D_MODEL, 512), wi[:, 2560:3200],
                              zeros(MLA_NOPE), wi[:, 3200:3232], zeros(LANES - MLA_NOPE - MLA_ROPE)], axis=1)
    qu = mla_w_q_up[l].reshape(MLA_Q_RANK, MLA_HEADS, MLA_NOPE + MLA_ROPE)
    qu = jnp.pad(qu, ((0, 0), (0, 0), (0, LANES - MLA_NOPE - MLA_ROPE))).reshape(MLA_Q_RANK, MLA_HEADS * LANES)
    kvu = mla_w_kv_up[l].reshape(MLA_KV_RANK, MLA_HEADS, MLA_NOPE + MLA_V)
    ku = jnp.pad(kvu[:, :, :MLA_NOPE], ((0, 0), (0, 0), (0, LANES - MLA_NOPE))).reshape(MLA_KV_RANK, -1)
    vu = kvu[:, :, MLA_NOPE:].reshape(MLA_KV_RANK, MLA_HEADS * MLA_V)
    wb = w_branch[l]
    wb_gqa = wb[2].reshape(GQA_Q_HEADS, HEAD_DIM, D_MODEL)[jnp.array(_GQA_HEAD_ORDER)].reshape(BRANCH_W, D_MODEL)
    wb = jnp.stack([wb[0], wb[1], wb_gqa, wb[3]])
    return {
        "w_in": w_in_p.astype(BF16),
        "gq_norm": jnp.tile(gqa_q_norm[l], GQA_Q_HEADS).reshape(1, 512),
        "gk_norm": jnp.tile(gqa_k_norm[l], GQA_KV_HEADS).reshape(1, LANES),
        "mq_norm": mla_q_norm[l].reshape(1, MLA_Q_RANK),
        "mkv_norm": mla_kv_norm[l].reshape(1, MLA_KV_RANK),
        "w_q_up": qu.astype(BF16), "w_k_up": ku.astype(BF16), "w_v_up": vu.astype(BF16),
        "w_branch": wb.astype(BF16), "w_gate": w_gate[l].astype(BF16), "b_gate": b_gate[l],
        "w_out": w_out[l].astype(BF16),
    }


def kernel(x, c, ctx, c_ctx, w_mod, b_mod, mix_pre_norm, mix_post_norm, ffn_pre_norm, ffn_post_norm,
           w_in, diff_lambda, diff_subnorm, gqa_q_norm, gqa_k_norm, mla_q_norm, mla_kv_norm,
           mla_w_q_up, mla_w_kv_up, w_branch, w_gate, b_gate, w_out,
           dense_w_gate_up, dense_w_down, moe_router, moe_w_gate_up, moe_w_down):
    B, n_lat, d = x.shape
    n_ctx = ctx.shape[1]
    depth = w_mod.shape[0]
    assert d == D_MODEL and n_lat % TM == 0 and n_ctx % TM == 0 and n_lat % n_ctx == 0
    assert n_lat % GRID_W == 0
    T = n_lat + n_ctx
    n_lat_tiles, n_all_tiles = n_lat // TM, T // TM
    consts = _constants(n_lat, n_ctx)
    x_all = jnp.concatenate([x, ctx], axis=1)
    c_rows = jnp.pad(jnp.concatenate([c, c_ctx[None]], axis=0), ((0, (-(B + 1)) % 8), (0, 0)))
    row = lambda v: v.reshape(1, -1)

    for layer in range(depth):
        last = layer == depth - 1
        n_tiles = n_lat_tiles if last else n_all_tiles
        lam_init = 0.8 - 0.6 * math.exp(-0.3 * layer)
        wts = _layer_weights(layer, w_in, gqa_q_norm, gqa_k_norm, mla_q_norm, mla_kv_norm, mla_w_q_up,
                             mla_w_kv_up, w_branch, w_gate, b_gate, w_out)
        mod_rows = _modulation(c_rows, w_mod[layer], b_mod[layer])
        mods = jnp.stack([mod_rows[:B].reshape(B, N_MOD, D_MODEL),
                          jnp.broadcast_to(mod_rows[B].reshape(1, N_MOD, D_MODEL), (B, N_MOD, D_MODEL))], axis=1)

        (h, dq, dk, dv, gq, gk, gv, mq, mk, mv, pq_lat, pq_ctx) = _project(
            x_all, mods, row(mix_pre_norm[layer]), wts, consts, n_lat_tiles)

        diff_extra = (diff_lambda[layer], diff_subnorm[layer].reshape(1, LANES))
        ys = []
        for mode, q, k, v in (("diff", dq, dk, dv), ("gqa", gq, gk, gv), ("mla", mq, mk, mv)):
            y = jnp.zeros((B, T, BRANCH_W), BF16)
            kw = dict(lam_init=lam_init, extra=diff_extra)
            y = _attention(mode, q, k, v, y, tq=TQ, nq=n_lat // TQ, q_blk0=0, tk=T, k_blk=0, **kw)
            if not last:
                y = _attention(mode, q, k, v, y, tq=TM, nq=n_ctx // TM, q_blk0=n_lat_tiles,
                               tk=n_ctx, k_blk=n_lat // n_ctx, **kw)
            ys.append(y)
        y_diff, y_gqa, y_mla = ys

        a_rows = _dft_rows(pq_lat, consts["m1"])
        y_f = _dft_cols(a_rows, consts["tw_cos"], consts["tw_sin"], consts["m3"]).reshape(B, n_lat, BRANCH_W)
        if not last:
            y_f = jnp.concatenate([y_f, _dft_ctx(pq_ctx, consts["w_ctx"])], axis=1)

        x_all = _merge(x_all, h, y_diff, y_f, y_gqa, y_mla, mods, wts, row(mix_post_norm[layer]),
                       n_lat_tiles, n_tiles)

        if layer % 2 == 0:
            x_all = _dense_ffn(x_all, mods, row(ffn_pre_norm[layer]), row(ffn_post_norm[layer]),
                               dense_w_gate_up[layer // 2].astype(BF16), dense_w_down[layer // 2].astype(BF16),
                               n_lat_tiles, n_tiles)
        else:
            wr = jnp.pad(moe_router[layer // 2], ((0, 0), (0, LANES - N_EXPERTS)))
            x_all = _routed_ffn(x_all, mods, row(ffn_pre_norm[layer]), row(ffn_post_norm[layer]), wr,
                                moe_w_gate_up[layer // 2].astype(BF16), moe_w_down[layer // 2].astype(BF16),
                                n_lat_tiles, n_tiles)
    return x_all[:, :n_lat]
```

```python
import functools
import math

import jax
import jax.numpy as jnp
from jax import lax
from jax.experimental import pallas as pl
from jax.experimental.pallas import tpu as pltpu

F32 = jnp.float32
BF16 = jnp.bfloat16

D_MODEL = 1024
GRID_W = 64
ROPE_BASE = 10000.0
EPS = 1e-6
N_MOD = 6
HEAD_DIM = 64
DIFF_HEADS = 4
GQA_Q_HEADS = 8
GQA_KV_HEADS = 2
MLA_HEADS = 8
MLA_Q_RANK = 256
MLA_KV_RANK = 128
MLA_NOPE = 64
MLA_ROPE = 32
MLA_V = 64
BRANCH_W = 512
FOURIER_GROUP = 128
FFN_DIM = 2816
N_EXPERTS = 8
EXPERT_DIM = 3584
LOG2E = math.log2(math.e)
QK_SCALE = HEAD_DIM ** -0.5 * LOG2E
MLA_SCALE = (MLA_NOPE + MLA_ROPE) ** -0.5 * LOG2E
IN_COLS_PADDED = 3328

LANES = 128
VMEM_LIMIT = 56 * 1024 * 1024

TM = 256
TQ = 512


def _cparams(sem):
    return pltpu.CompilerParams(dimension_semantics=sem, vmem_limit_bytes=VMEM_LIMIT)


def _full_spec(shape):
    n = len(shape)
    return pl.BlockSpec(shape, lambda *_: (0,) * n)


def _split_bf16(x):
    hi = x.astype(BF16)
    lo = (x - hi.astype(F32)).astype(BF16)
    return hi, lo


def _dot(a, b):
    return jnp.dot(a, b, preferred_element_type=F32)


def _dot3(a, w):
    ah, al = _split_bf16(a)
    wh, wl = _split_bf16(w)
    return _dot(ah, wh) + _dot(ah, wl) + _dot(al, wh)


def _rms(x, gain):
    ms = jnp.mean(x * x, axis=-1, keepdims=True)
    return x * lax.rsqrt(ms + EPS) * gain


def _mod_kernel(c_ref, w_ref, b_ref, o_ref):
    c = c_ref[...]
    a = c * jax.nn.sigmoid(c)
    o_ref[...] = _dot3(a, w_ref[...]) + b_ref[...]


def _modulation(c_rows, w_mod_l, b_mod_l):
    rows = c_rows.shape[0]
    n = w_mod_l.shape[1]
    tn = 512
    return pl.pallas_call(
        _mod_kernel,
        grid=(n // tn,),
        in_specs=[pl.BlockSpec((rows, D_MODEL), lambda j: (0, 0)),
                  pl.BlockSpec((D_MODEL, tn), lambda j: (0, j)),
                  pl.BlockSpec((1, tn), lambda j: (0, j))],
        out_specs=pl.BlockSpec((rows, tn), lambda j: (0, j)),
        out_shape=jax.ShapeDtypeStruct((rows, n), F32),
        compiler_params=_cparams(("arbitrary",)),
        name="modulation",
    )(c_rows, w_mod_l, b_mod_l.reshape(1, n))


def _rope(x, cos, sin_signed, half):
    lane = lax.broadcasted_iota(jnp.int32, x.shape, 1)
    first = (lane % (2 * half)) < half
    up = pltpu.roll(x, LANES - half, 1)
    down = pltpu.roll(x, half, 1)
    return x * cos + jnp.where(first, up, down) * sin_signed


def _stream_specs(xs, n_lat_tiles):
    if not isinstance(xs, tuple):
        return [pl.BlockSpec((None, TM, D_MODEL), lambda b, t: (b, t, 0))], [xs]
    return ([pl.BlockSpec((None, TM, D_MODEL), lambda b, t: (b, jnp.minimum(t, n_lat_tiles - 1), 0)),
             pl.BlockSpec((None, TM, D_MODEL), lambda b, t: (b, jnp.maximum(t - n_lat_tiles, 0), 0))], list(xs))


def _stream_tile(x_refs, n_lat_tiles):
    if len(x_refs) == 1:
        return x_refs[0][...]
    return jnp.where(pl.program_id(1) < n_lat_tiles, x_refs[0][...], x_refs[1][...])


def _stream_dims(xs):
    if isinstance(xs, tuple):
        return xs[0].shape[0], xs[0].shape[1] + xs[1].shape[1]
    return xs.shape[0], xs.shape[1]


def _proj_kernel(n_lat_tiles, n_x, *refs):
    x_refs, refs = refs[:n_x], refs[n_x:]
    (mod_ref, gpre_ref, win_ref, gqn_ref, gkn_ref, mqn_ref, mkvn_ref,
     wqup_ref, wkup_ref, wvup_ref, seg_ref, cc_ref, sc_ref,
     cosh_ref, sinh_ref, cosm_ref, sinm_ref,
     h_ref, dq_ref, dk_ref, dv_ref, gq_ref, gk_ref, gv_ref, mq_ref, mk_ref, mv_ref, pq_ref, pqc_ref) = refs
    x = _stream_tile(x_refs, n_lat_tiles)
    mod = mod_ref[...]
    h = _rms(x, gpre_ref[...]) * (1.0 + mod[1:2]) + mod[0:1]
    hb = h.astype(BF16)
    h_ref[...] = hb
    z = _dot(hb, win_ref[...])

    def zcols(lo, hi):
        return z[:, lo:hi]
    cosh, sinh = cosh_ref[...], sinh_ref[...]
    cosm, sinm = cosm_ref[...], sinm_ref[...]
    seg = seg_ref[...]

    def head_rms(v, gain, w):
        sq = v * v
        hi, lo = _split_bf16(sq)
        ms = _dot(hi, seg[:w, :w]) + _dot(lo, seg[:w, :w])
        return v * lax.rsqrt(ms + EPS) * gain

    for j in range(4):
        c0 = j * LANES
        dq_ref[:, c0:c0 + LANES] = (_rope(zcols(c0, c0 + LANES), cosh, sinh, 16) * QK_SCALE).astype(BF16)
        dk_ref[:, c0:c0 + LANES] = _rope(zcols(512 + c0, 512 + c0 + LANES), cosh, sinh, 16).astype(BF16)
    dv_ref[...] = zcols(1024, 1536).astype(BF16)
    fparts = []
    for g in range(4):
        c0 = 1536 + g * LANES
        fb = zcols(c0, c0 + LANES).astype(BF16)
        fparts.append((_dot(fb, cc_ref[...]).astype(BF16), _dot(fb, sc_ref[...]).astype(BF16)))
    is_lat = pl.program_id(1) < n_lat_tiles

    def put(ref):
        for g, (pc, ps) in enumerate(fparts):
            ref[0, :, g * LANES:(g + 1) * LANES] = pc
            ref[1, :, g * LANES:(g + 1) * LANES] = ps

    pl.when(is_lat)(lambda: put(pq_ref))
    pl.when(jnp.logical_not(is_lat))(lambda: put(pqc_ref))
    gq = head_rms(zcols(2048, 2560), gqn_ref[...], 512)
    for j in range(4):
        c0 = j * LANES
        gq_ref[:, c0:c0 + LANES] = (_rope(gq[:, c0:c0 + LANES], cosh, sinh, 16) * QK_SCALE).astype(BF16)
    gk = head_rms(zcols(2560, 2688), gkn_ref[...], LANES)
    gk_ref[...] = _rope(gk, cosh, sinh, 16).astype(BF16)
    gv_ref[...] = zcols(2688, 2816).astype(BF16)
    cq = _rms(zcols(2816, 3072), mqn_ref[...]).astype(BF16)
    mq = _dot(cq, wqup_ref[...])
    ckv = _rms(zcols(3072, 3200), mkvn_ref[...]).astype(BF16)
    mk = _dot(ckv, wkup_ref[...])
    mv_ref[...] = _dot(ckv, wvup_ref[...]).astype(BF16)
    kr = _rope(zcols(3200, 3328), cosm, sinm, 8)
    for hh in range(MLA_HEADS):
        c0 = hh * LANES
        mq_ref[:, c0:c0 + LANES] = (_rope(mq[:, c0:c0 + LANES], cosm, sinm, 8) * MLA_SCALE).astype(BF16)
        mk_ref[:, c0:c0 + LANES] = (mk[:, c0:c0 + LANES] + kr).astype(BF16)


def _project(x_all, mods, gpre, wts, consts, n_lat_tiles):
    B, T = _stream_dims(x_all)
    nt = T // TM
    tok = lambda w: pl.BlockSpec((None, TM, w), lambda b, t: (b, t, 0))
    tab = pl.BlockSpec((TM, LANES), lambda b, t: (t, 0))
    x_specs, x_args = _stream_specs(x_all, n_lat_tiles)
    in_specs = x_specs + [
        pl.BlockSpec((None, None, N_MOD, D_MODEL), lambda b, t: (b, jnp.where(t >= n_lat_tiles, 1, 0), 0, 0)),
        _full_spec((1, D_MODEL)),
        _full_spec((D_MODEL, IN_COLS_PADDED)),
        _full_spec((1, 512)), _full_spec((1, LANES)), _full_spec((1, MLA_Q_RANK)), _full_spec((1, MLA_KV_RANK)),
        _full_spec((MLA_Q_RANK, MLA_HEADS * LANES)),
        _full_spec((MLA_KV_RANK, MLA_HEADS * LANES)),
        _full_spec((MLA_KV_RANK, MLA_HEADS * MLA_V)),
        _full_spec((512, 512)), _full_spec((LANES, LANES)), _full_spec((LANES, LANES)),
        tab, tab, tab, tab,
    ]
    widths = [D_MODEL, 512, 512, 512, 512, LANES, LANES, MLA_HEADS * LANES, MLA_HEADS * LANES, 512]
    out_specs = [tok(w) for w in widths]
    out_shape = [jax.ShapeDtypeStruct((B, T, w), BF16) for w in widths]
    out_specs.append(pl.BlockSpec((2, None, TM, 512), lambda b, t: (0, b, jnp.minimum(t, n_lat_tiles - 1), 0)))
    out_shape.append(jax.ShapeDtypeStruct((2, B, n_lat_tiles * TM, 512), BF16))
    out_specs.append(pl.BlockSpec((2, None, TM, 512), lambda b, t: (0, b, jnp.maximum(t - n_lat_tiles, 0), 0)))
    out_shape.append(jax.ShapeDtypeStruct((2, B, T - n_lat_tiles * TM, 512), BF16))
    return pl.pallas_call(
        functools.partial(_proj_kernel, n_lat_tiles, len(x_args)),
        grid=(B, nt),
        in_specs=in_specs,
        out_specs=out_specs,
        out_shape=out_shape,
        compiler_params=_cparams(("arbitrary", "arbitrary")),
        name="project",
    )(*x_args, mods, gpre, wts["w_in"], wts["gq_norm"], wts["gk_norm"], wts["mq_norm"], wts["mkv_norm"],
      wts["w_q_up"], wts["w_k_up"], wts["w_v_up"], consts["seg"], consts["cc"], consts["sc"],
      consts["cos_h"], consts["sin_h"], consts["cos_m"], consts["sin_m"])


N_MAPS = 8
MAPS_PER_STEP = 4


def _map_plan(mode, m):
    if mode == "diff":
        return m // 2, m % 2, m // 2, m // 2
    if mode == "gqa":
        return m // 2, m % 2, 0, 0
    return m, None, m, m // 2


def _attn_kernel(mode, lam_init, q_ref, k_ref, v_ref, *rest):
    if mode == "diff":
        lam_ref, sub_ref, _, o_ref, vx_ref = rest
    else:
        _, o_ref, vx_ref = rest
    nt = (((1,), (1,)), ((), ()))
    lane = lax.broadcasted_iota(jnp.int32, (1, LANES), 1)
    low = lane < HEAD_DIM

    @pl.when(pl.program_id(2) == 0)
    def _():
        for g in range(vx_ref.shape[0]):
            vx_ref[g, :, :LANES] = v_ref[:, g * LANES:(g + 1) * LANES]
            vx_ref[g, :, LANES:] = jnp.ones((vx_ref.shape[1], LANES), BF16)

    def scores(m):
        qblk, half, kblk, _ = _map_plan(mode, m)
        q = q_ref[:, qblk * LANES:(qblk + 1) * LANES]
        if half is not None:
            q = jnp.where(low if half == 0 else jnp.logical_not(low), q, jnp.zeros_like(q))
        return lax.dot_general(q, k_ref[:, kblk * LANES:(kblk + 1) * LANES], nt, preferred_element_type=F32)

    def numerator(s):
        return jnp.exp2(s - jnp.max(s, axis=-1, keepdims=True)).astype(BF16)

    def values(m, p):
        e = _dot(p, vx_ref[_map_plan(mode, m)[3]])
        return e[:, :LANES] / e[:, LANES:]

    s, pl_, y = {}, {}, {}
    for t in range(MAPS_PER_STEP + 2):
        if t < MAPS_PER_STEP:
            s[t] = scores(t)
        if 0 <= t - 1 < MAPS_PER_STEP:
            pl_[t - 1] = numerator(s.pop(t - 1))
        if 0 <= t - 2:
            y[t - 2] = values(t - 2, pl_.pop(t - 2))

    if mode == "diff":
        dl = lam_ref[...]
        lam = (jnp.exp(jnp.sum(dl[0:1] * dl[1:2], axis=-1, keepdims=True))
               - jnp.exp(jnp.sum(dl[2:3] * dl[3:4], axis=-1, keepdims=True)) + lam_init)
    for j in range(MAPS_PER_STEP // 2):
        if mode == "diff":
            out = _rms(y[2 * j] - lam * y[2 * j + 1], sub_ref[...]) * (1.0 - lam_init)
        else:
            out = jnp.where(low, y[2 * j], y[2 * j + 1])
        o_ref[:, j * LANES:(j + 1) * LANES] = out.astype(BF16)


def _attention(mode, q, k, v, y_prev, *, tq, nq, q_blk0, tk, k_blk, lam_init=0.0, extra=()):
    B, T, _ = q.shape
    n_steps = N_MAPS // MAPS_PER_STEP
    wq, wk, wv = q.shape[2] // n_steps, max(k.shape[2] // n_steps, LANES), max(v.shape[2] // n_steps, LANES)
    kv_blk = (lambda g: 0) if mode == "gqa" else (lambda g: g)
    in_specs = [
        pl.BlockSpec((None, tq, wq), lambda b, g, i: (b, i + q_blk0, g)),
        pl.BlockSpec((None, tk, wk), lambda b, g, i: (b, k_blk, kv_blk(g))),
        pl.BlockSpec((None, tk, wv), lambda b, g, i: (b, k_blk, kv_blk(g))),
    ]
    args = [q, k, v]
    if mode == "diff":
        in_specs += [_full_spec((4, HEAD_DIM)), _full_spec((1, LANES))]
        args += list(extra)
    in_specs.append(pl.BlockSpec(memory_space=pl.ANY))
    args.append(y_prev)
    wo = BRANCH_W // n_steps
    return pl.pallas_call(
        functools.partial(_attn_kernel, mode, lam_init),
        grid=(B, n_steps, nq),
        in_specs=in_specs,
        out_specs=pl.BlockSpec((None, tq, wo), lambda b, g, i: (b, i + q_blk0, g)),
        out_shape=jax.ShapeDtypeStruct((B, T, BRANCH_W), BF16),
        input_output_aliases={len(args) - 1: 0},
        scratch_shapes=[pltpu.VMEM((wv // LANES, tk, 2 * LANES), BF16)],
        compiler_params=_cparams(("arbitrary", "arbitrary", "arbitrary")),
        name="attention_" + mode,
    )(*args)


DFT_COLS_PER_STEP = 8


def _dft_rows_kernel(pq_ref, m1_ref, a_ref):
    r = pq_ref.shape[1]
    for n2 in range(pq_ref.shape[2]):
        zin = jnp.concatenate([pq_ref[0, :, n2, :], pq_ref[1, :, n2, :]], axis=0)
        a = _dot(m1_ref[...], zin).astype(BF16)
        a_ref[0, :, n2, :] = a[:r]
        a_ref[1, :, n2, :] = a[r:]


def _dft_rows(pq_lat, m1):
    _, B, n_lat, w = pq_lat.shape
    r = n_lat // GRID_W
    nb = DFT_COLS_PER_STEP
    return pl.pallas_call(
        _dft_rows_kernel,
        grid=(B, GRID_W // nb),
        in_specs=[pl.BlockSpec((2, None, r, nb, w), lambda b, j: (0, b, 0, j, 0)), _full_spec((2 * r, 2 * r))],
        out_specs=pl.BlockSpec((None, 2, r, nb, w), lambda b, j: (b, 0, 0, j, 0)),
        out_shape=jax.ShapeDtypeStruct((B, 2, r, GRID_W, w), BF16),
        compiler_params=_cparams(("arbitrary", "arbitrary")),
        name="dft_rows",
    )(pq_lat.reshape(2, B, r, GRID_W, w), m1)


def _dft_cols_kernel(a_ref, ct_ref, st_ref, m3_ref, o_ref):
    kb = a_ref.shape[1]
    w = a_ref.shape[3]
    for j in range(kb):
        ar = a_ref[0, j].astype(F32)
        ai = a_ref[1, j].astype(F32)
        ct = jnp.concatenate([ct_ref[j]] * (w // LANES), axis=1)
        st = jnp.concatenate([st_ref[j]] * (w // LANES), axis=1)
        b = jnp.concatenate([ar * ct + ai * st, ai * ct - ar * st], axis=0).astype(BF16)
        o_ref[:, j, :] = _dot(m3_ref[...], b).astype(BF16)


def _dft_cols(a, ct, st, m3):
    B, _, r, _, w = a.shape
    kb = DFT_COLS_PER_STEP
    return pl.pallas_call(
        _dft_cols_kernel,
        grid=(B, r // kb),
        in_specs=[pl.BlockSpec((None, 2, kb, GRID_W, w), lambda b, k: (b, 0, k, 0, 0)),
                  pl.BlockSpec((kb, GRID_W, LANES), lambda b, k: (k, 0, 0)),
                  pl.BlockSpec((kb, GRID_W, LANES), lambda b, k: (k, 0, 0)),
                  _full_spec((GRID_W, 2 * GRID_W))],
        out_specs=pl.BlockSpec((None, GRID_W, kb, w), lambda b, k: (b, 0, k, 0)),
        out_shape=jax.ShapeDtypeStruct((B, GRID_W, r, w), BF16),
        compiler_params=_cparams(("arbitrary", "arbitrary")),
        name="dft_cols",
    )(a, ct, st, m3)


def _dft_ctx_kernel(pq_ref, w_ref, o_ref):
    zin = jnp.concatenate([pq_ref[0], pq_ref[1]], axis=0)
    o_ref[...] = _dot(w_ref[...], zin).astype(BF16)


def _dft_ctx(pq_ctx, w_ctx):
    _, B, n_ctx, w = pq_ctx.shape
    return pl.pallas_call(
        _dft_ctx_kernel,
        grid=(B,),
        in_specs=[pl.BlockSpec((2, None, n_ctx, w), lambda b: (0, b, 0, 0)), _full_spec((n_ctx, 2 * n_ctx))],
        out_specs=pl.BlockSpec((None, n_ctx, w), lambda b: (b, 0, 0)),
        out_shape=jax.ShapeDtypeStruct((B, n_ctx, w), BF16),
        compiler_params=_cparams(("arbitrary",)),
        name="dft_ctx",
    )(pq_ctx, w_ctx)


def _merge_kernel(n_lat_tiles, n_x, *refs):
    x_refs, refs = refs[:n_x], refs[n_x:]
    h_ref, y0_ref, y1_ref, y2_ref, y3_ref, mod_ref, wg_ref, bg_ref, wb_ref, wo_ref, gpost_ref = refs[:11]
    rest = refs[11:]
    h = h_ref[...]
    ys = (y0_ref, y1_ref, y2_ref, y3_ref)
    acc = None
    for i in range(4):
        gate = jax.nn.sigmoid(_dot(h, wg_ref[i]) + bg_ref[i:i + 1])
        term = gate * _dot(ys[i][...], wb_ref[i])
        acc = term if acc is None else acc + term
    m = _dot(acc.astype(BF16), wo_ref[...])
    mod = mod_ref[...]
    x = _stream_tile(x_refs, n_lat_tiles) + mod[2:3] * _rms(m, gpost_ref[...])
    o_ref = rest[3] if len(rest) > 1 else rest[0]
    o_ref[...] = x
    if len(rest) > 1:
        gffn_ref, wr_ref, tri_ref, _, hf_ref, route_ref, cnt_ref, carry_ref = rest
        _route_tile(x, mod, gffn_ref, wr_ref, tri_ref, hf_ref, route_ref, cnt_ref, carry_ref)


def _merge(x_all, h, y_diff, y_f, y_gqa, y_mla, mods, wts, gpost, n_lat_tiles, n_tiles, router=None):
    B, _ = _stream_dims(x_all)
    tok = lambda w: pl.BlockSpec((None, TM, w), lambda b, t: (b, t, 0))
    x_specs, x_args = _stream_specs(x_all, n_lat_tiles)
    in_specs = x_specs + [
        tok(D_MODEL), tok(BRANCH_W), tok(BRANCH_W), tok(BRANCH_W), tok(BRANCH_W),
        pl.BlockSpec((None, None, N_MOD, D_MODEL), lambda b, t: (b, jnp.where(t >= n_lat_tiles, 1, 0), 0, 0)),
        _full_spec((4, D_MODEL, D_MODEL)), _full_spec((4, D_MODEL)), _full_spec((4, BRANCH_W, D_MODEL)),
        _full_spec((D_MODEL, D_MODEL)), _full_spec((1, D_MODEL)),
    ]
    args = x_args + [h, y_diff, y_f, y_gqa, y_mla, mods, wts["w_gate"], wts["b_gate"], wts["w_branch"],
                     wts["w_out"], gpost]
    out_specs = [tok(D_MODEL)]
    out_shape = [jax.ShapeDtypeStruct((B, n_tiles * TM, D_MODEL), F32)]
    scratch = []
    if router is not None:
        in_specs += [_full_spec((1, D_MODEL)), _full_spec((D_MODEL, LANES)), _full_spec((TM, TM))]
        args += list(router)
        out_specs += [tok(D_MODEL), tok(LANES), _full_spec((1, LANES))]
        out_shape += [jax.ShapeDtypeStruct((B, n_tiles * TM, D_MODEL), F32),
                      jax.ShapeDtypeStruct((B, n_tiles * TM, LANES), F32),
                      jax.ShapeDtypeStruct((1, LANES), F32)]
        scratch = [pltpu.VMEM((1, LANES), F32)]
    out = pl.pallas_call(
        functools.partial(_merge_kernel, n_lat_tiles, len(x_args)),
        grid=(B, n_tiles),
        in_specs=in_specs,
        out_specs=out_specs,
        out_shape=out_shape,
        scratch_shapes=scratch,
        compiler_params=_cparams(("arbitrary", "arbitrary")),
        name="merge",
    )(*args)
    return out[0] if router is None else out


def _dense_ffn_kernel(nf, tf, x_ref, mod_ref, gpre_ref, gpost_ref, wgu_ref, wd_ref, o_ref):
    x = x_ref[...]
    mod = mod_ref[...]
    hb = (_rms(x, gpre_ref[...]) * (1.0 + mod[4:5]) + mod[3:4]).astype(BF16)
    acc = None
    for f in range(nf):
        a = _dot(hb, wgu_ref[:, f * tf:(f + 1) * tf])
        u = _dot(hb, wgu_ref[:, FFN_DIM + f * tf:FFN_DIM + (f + 1) * tf])
        act = (a * jax.nn.sigmoid(a) * u).astype(BF16)
        part = _dot(act, wd_ref[f * tf:(f + 1) * tf, :])
        acc = part if acc is None else acc + part
    o_ref[...] = x + mod[5:6] * _rms(acc, gpost_ref[...])


def _dense_ffn(x_all, mods, gpre, gpost, wgu, wd, n_lat_tiles, n_tiles):
    B = x_all.shape[0]
    nf, tf = 2, FFN_DIM // 2
    tok = pl.BlockSpec((None, TM, D_MODEL), lambda b, t: (b, t, 0))
    return pl.pallas_call(
        functools.partial(_dense_ffn_kernel, nf, tf),
        grid=(B, n_tiles),
        in_specs=[tok,
                  pl.BlockSpec((None, None, N_MOD, D_MODEL),
                               lambda b, t: (b, jnp.where(t >= n_lat_tiles, 1, 0), 0, 0)),
                  _full_spec((1, D_MODEL)), _full_spec((1, D_MODEL)),
                  _full_spec((D_MODEL, 2 * FFN_DIM)), _full_spec((FFN_DIM, D_MODEL))],
        out_specs=tok,
        out_shape=jax.ShapeDtypeStruct((B, n_tiles * TM, D_MODEL), F32),
        compiler_params=_cparams(("arbitrary", "arbitrary")),
        name="dense_ffn",
    )(x_all, mods, gpre, gpost, wgu, wd)


TE = 512
R_E1, R_E2, R_G1, R_G2, R_RANK1, R_RANK2 = range(6)


def _route_tile(x, mod, gpre_ref, wr_ref, tri_ref, h_ref, route_ref, cnt_ref, carry_ref):
    first = jnp.logical_and(pl.program_id(0) == 0, pl.program_id(1) == 0)

    @pl.when(first)
    def _():
        carry_ref[...] = jnp.zeros_like(carry_ref)

    h = _rms(x, gpre_ref[...]) * (1.0 + mod[4:5]) + mod[3:4]
    h_ref[...] = h
    logits = _dot3(h, wr_ref[...])
    lane = lax.broadcasted_iota(jnp.int32, logits.shape, 1)
    neg = jnp.float32(-jnp.inf)
    l1 = jnp.where(lane < N_EXPERTS, logits, neg)
    m1 = jnp.max(l1, axis=-1, keepdims=True)
    i1 = jnp.min(jnp.where(l1 == m1, lane, LANES), axis=-1, keepdims=True)
    l2 = jnp.where(lane == i1, neg, l1)
    m2 = jnp.max(l2, axis=-1, keepdims=True)
    i2 = jnp.min(jnp.where(l2 == m2, lane, LANES), axis=-1, keepdims=True)
    e2 = jnp.exp(m2 - m1)
    g1 = 1.0 / (1.0 + e2)
    g2 = e2 / (1.0 + e2)
    sel1, sel2 = lane == i1, lane == i2
    chosen = jnp.where(jnp.logical_or(sel1, sel2), 1.0, 0.0)
    before = _dot(tri_ref[...], chosen.astype(BF16)) + carry_ref[...]
    r1 = jnp.sum(jnp.where(sel1, before, 0.0), axis=-1, keepdims=True)
    r2 = jnp.sum(jnp.where(sel2, before, 0.0), axis=-1, keepdims=True)
    carry_ref[...] += jnp.sum(chosen, axis=0, keepdims=True)
    cnt_ref[...] = carry_ref[...]
    rec = jnp.zeros_like(logits)
    for col, val in ((R_E1, i1.astype(F32)), (R_E2, i2.astype(F32)), (R_G1, g1), (R_G2, g2),
                     (R_RANK1, r1), (R_RANK2, r2)):
        rec = jnp.where(lane == col, val, rec)
    route_ref[...] = rec


def _row_copies(n_rows, copy_of, slot_wait):
    def start(r, carry):
        copy_of(r, 0).start()
        copy_of(r, 1).start()
        return carry

    lax.fori_loop(0, n_rows, start, 0, unroll=8)
    slot_wait(0).wait()
    slot_wait(1).wait()


def _dispatch_kernel(pos_ref, h_ref, _, xs_ref, sem):
    def copy_of(r, slot):
        p = pos_ref[0, 2 * r + slot]
        return pltpu.make_async_copy(h_ref.at[pl.ds(r, 1)], xs_ref.at[pl.ds(p, 1)], sem)

    n = h_ref.shape[0]
    _row_copies(n, copy_of, lambda slot: pltpu.make_async_copy(h_ref, xs_ref.at[pl.ds(0, n)], sem))


def _dispatch(h, pos, n_rows):
    n = h.shape[0]
    xs0 = jnp.zeros((n_rows, D_MODEL), F32)
    return pl.pallas_call(
        _dispatch_kernel,
        grid=(n // TM,),
        in_specs=[pl.BlockSpec((None, 1, 2 * TM), lambda i: (i, 0, 0), memory_space=pltpu.SMEM),
                  pl.BlockSpec((TM, D_MODEL), lambda i: (i, 0)),
                  pl.BlockSpec(memory_space=pl.ANY)],
        out_specs=pl.BlockSpec(memory_space=pl.ANY),
        out_shape=jax.ShapeDtypeStruct((n_rows, D_MODEL), F32),
        input_output_aliases={2: 0},
        scratch_shapes=[pltpu.SemaphoreType.DMA],
        compiler_params=_cparams(("arbitrary",)),
        name="moe_dispatch",
    )(pos.reshape(n // TM, 1, 2 * TM), h, xs0)


def _experts_kernel(te_ref, nv_ref, xs_ref, wa_ref, wu_ref, wd_ref, ys_ref, acc_ref):
    del te_ref
    f = pl.program_id(1)

    @pl.when(pl.program_id(0) < nv_ref[0])
    def _():
        x = xs_ref[...].astype(BF16)
        a = _dot(x, wa_ref[...])
        u = _dot(x, wu_ref[...])
        part = _dot((a * jax.nn.sigmoid(a) * u).astype(BF16), wd_ref[...])

        @pl.when(f == 0)
        def _():
            acc_ref[...] = part

        @pl.when(f > 0)
        def _():
            acc_ref[...] += part

        @pl.when(f == pl.num_programs(1) - 1)
        def _():
            ys_ref[...] = acc_ref[...]


def _experts(xs, tile_expert, n_valid, wgu, wd):
    n_rows = xs.shape[0]
    nf = 2
    tf = EXPERT_DIM // nf
    row_blk = lambda i, f, te, nv: (jnp.minimum(i, nv[0] - 1), 0)
    grid_spec = pltpu.PrefetchScalarGridSpec(
        num_scalar_prefetch=2,
        grid=(n_rows // TE, nf),
        in_specs=[pl.BlockSpec((TE, D_MODEL), row_blk),
                  pl.BlockSpec((None, D_MODEL, tf), lambda i, f, te, nv: (te[i], 0, f)),
                  pl.BlockSpec((None, D_MODEL, tf), lambda i, f, te, nv: (te[i], 0, nf + f)),
                  pl.BlockSpec((None, tf, D_MODEL), lambda i, f, te, nv: (te[i], f, 0))],
        out_specs=pl.BlockSpec((TE, D_MODEL), row_blk),
        scratch_shapes=[pltpu.VMEM((TE, D_MODEL), F32)],
    )
    return pl.pallas_call(
        _experts_kernel,
        grid_spec=grid_spec,
        out_shape=jax.ShapeDtypeStruct((n_rows, D_MODEL), F32),
        compiler_params=_cparams(("arbitrary", "arbitrary")),
        name="moe_experts",
    )(tile_expert, n_valid, xs, wgu, wgu, wd)


def _combine_kernel(pos_ref, x_ref, route_ref, mod_ref, gpost_ref, ys_ref, o_ref, buf_ref, sem):
    def copy_of(r, slot):
        p = pos_ref[0, 2 * r + slot]
        return pltpu.make_async_copy(ys_ref.at[pl.ds(p, 1)], buf_ref.at[slot, pl.ds(r, 1)], sem)

    n = x_ref.shape[0]
    _row_copies(n, copy_of, lambda slot: pltpu.make_async_copy(ys_ref.at[pl.ds(0, n)], buf_ref.at[slot], sem))
    route = route_ref[...]
    y = route[:, R_G1:R_G1 + 1] * buf_ref[0] + route[:, R_G2:R_G2 + 1] * buf_ref[1]
    mod = mod_ref[...]
    o_ref[...] = x_ref[...] + mod[5:6] * _rms(y, gpost_ref[...])


def _combine(x_all, route, pos, ys, mods, gpost, n_lat_tiles, n_tiles):
    B = x_all.shape[0]
    tok = lambda w: pl.BlockSpec((None, TM, w), lambda b, t: (b, t, 0))
    return pl.pallas_call(
        _combine_kernel,
        grid=(B, n_tiles),
        in_specs=[pl.BlockSpec((None, None, 1, 2 * TM), lambda b, t: (b, t, 0, 0), memory_space=pltpu.SMEM),
                  tok(D_MODEL), tok(LANES),
                  pl.BlockSpec((None, None, N_MOD, D_MODEL),
                               lambda b, t: (b, jnp.where(t >= n_lat_tiles, 1, 0), 0, 0)),
                  _full_spec((1, D_MODEL)),
                  pl.BlockSpec(memory_space=pl.ANY)],
        out_specs=tok(D_MODEL),
        out_shape=jax.ShapeDtypeStruct((B, n_tiles * TM, D_MODEL), F32),
        scratch_shapes=[pltpu.VMEM((2, TM, D_MODEL), F32), pltpu.SemaphoreType.DMA],
        compiler_params=_cparams(("arbitrary", "arbitrary")),
        name="moe_combine",
    )(pos.reshape(B, n_tiles, 1, 2 * TM), x_all, route, mods, gpost, ys)


def _routed_ffn(x_all, h, route, counts, mods, gpost, wgu, wd, n_lat_tiles, n_tiles):
    B = x_all.shape[0]
    n_tok = B * n_tiles * TM
    rec = route.reshape(n_tok, LANES)
    expert = rec[:, R_E1:R_E2 + 1].astype(jnp.int32)
    rank = rec[:, R_RANK1:R_RANK2 + 1].astype(jnp.int32)
    tiles_per_expert = (counts[0, :N_EXPERTS].astype(jnp.int32) + TE - 1) // TE
    tile_end = jnp.cumsum(tiles_per_expert)
    n_valid = tile_end[-1:]
    pos = (tile_end - tiles_per_expert)[expert] * TE + rank
    n_tiles_max = (2 * n_tok) // TE + N_EXPERTS
    tile_id = jnp.minimum(jnp.arange(n_tiles_max, dtype=jnp.int32), n_valid - 1)
    tile_expert = jnp.minimum(jnp.searchsorted(tile_end, tile_id, side="right"), N_EXPERTS - 1).astype(jnp.int32)
    xs = _dispatch(h.reshape(n_tok, D_MODEL), pos, n_tiles_max * TE)
    ys = _experts(xs, tile_expert, n_valid.astype(jnp.int32), wgu, wd)
    return _combine(x_all, route, pos, ys, mods, gpost, n_lat_tiles, n_tiles)


def _rope_angles(n_lat, rot_dim):
    rows = n_lat // GRID_W
    row = jnp.repeat(jnp.arange(rows, dtype=F32), GRID_W)
    col = jnp.tile(jnp.arange(GRID_W, dtype=F32), rows)
    axis_dim = rot_dim // 2
    inv_freq = 1.0 / (ROPE_BASE ** (jnp.arange(0, axis_dim, 2, dtype=F32) / axis_dim))
    ar = row[:, None] * inv_freq
    ac = col[:, None] * inv_freq
    return jnp.concatenate([ar, ar, ac, ac], axis=-1)


def _constants(n_lat, n_ctx):
    lane = jnp.arange(LANES)
    ang = _rope_angles(n_lat, HEAD_DIM)
    cos_h = jnp.tile(jnp.cos(ang), (1, 2))
    sin_h = jnp.tile(jnp.sin(ang), (1, 2)) * jnp.where((lane % 32) < 16, -1.0, 1.0)
    cos_h = jnp.concatenate([cos_h, jnp.ones((n_ctx, LANES), F32)], axis=0)
    sin_h = jnp.concatenate([sin_h, jnp.zeros((n_ctx, LANES), F32)], axis=0)
    angm = _rope_angles(n_lat, MLA_ROPE)
    pad_l = jnp.zeros((n_lat, MLA_NOPE), F32)
    pad_r = jnp.zeros((n_lat, LANES - MLA_NOPE - MLA_ROPE), F32)
    cos_m = jnp.concatenate([pad_l + 1.0, jnp.cos(angm), pad_r + 1.0], axis=1)
    sin_m = jnp.concatenate([pad_l, jnp.sin(angm), pad_r], axis=1) * jnp.where((lane % 16) < 8, -1.0, 1.0)
    cos_m = jnp.concatenate([cos_m, jnp.ones((n_ctx, LANES), F32)], axis=0)
    sin_m = jnp.concatenate([sin_m, jnp.zeros((n_ctx, LANES), F32)], axis=0)
    idx = jnp.arange(512)
    seg = jnp.where((idx[:, None] // HEAD_DIM) == (idx[None, :] // HEAD_DIM), 1.0 / HEAD_DIM, 0.0).astype(BF16)

    def dft(n):
        i = jnp.arange(n, dtype=jnp.int32)
        a = ((i[:, None] * i[None, :]) % n).astype(F32) * (2.0 * math.pi / n)
        s = n ** -0.5
        return jnp.cos(a) * s, jnp.sin(a) * s

    cc, sc = dft(FOURIER_GROUP)
    cx, sx = dft(n_ctx)
    r = n_lat // GRID_W
    cr, sr = (t * r ** 0.5 for t in dft(r))
    c64, s64 = (t * GRID_W ** 0.5 * n_lat ** -0.5 for t in dft(GRID_W))
    k1 = jnp.arange(r, dtype=jnp.int32)[:, None]
    n2 = jnp.arange(GRID_W, dtype=jnp.int32)[None, :]
    tw = ((k1 * n2) % n_lat).astype(F32) * (2.0 * math.pi / n_lat)
    wide = lambda t: jnp.broadcast_to(t[:, :, None], (r, GRID_W, LANES))
    return {
        "cos_h": cos_h, "sin_h": sin_h, "cos_m": cos_m, "sin_m": sin_m, "seg": seg,
        "cc": cc.astype(BF16), "sc": sc.astype(BF16),
        "m1": jnp.concatenate([jnp.concatenate([cr, -sr], axis=1),
                               jnp.concatenate([-sr, -cr], axis=1)], axis=0).astype(BF16),
        "tw_cos": wide(jnp.cos(tw)), "tw_sin": wide(jnp.sin(tw)),
        "m3": jnp.concatenate([c64, s64], axis=1).astype(BF16),
        "w_ctx": jnp.concatenate([cx, -sx], axis=1).astype(BF16),
    }


_GQA_HEAD_ORDER = (0, 4, 1, 5, 2, 6, 3, 7)


def _layer_weights(l, w_in, gqa_q_norm, gqa_k_norm, mla_q_norm, mla_kv_norm, mla_w_q_up, mla_w_kv_up,
                   w_branch, w_gate, b_gate, w_out):
    wi = w_in[l]
    gq = wi[:, 2048:2560].reshape(D_MODEL, GQA_Q_HEADS, HEAD_DIM)[:, jnp.array(_GQA_HEAD_ORDER)]
    zeros = lambda n: jnp.zeros((D_MODEL, n), F32)
    w_in_p = jnp.concatenate([wi[:, :2048], gq.reshape(D_MODEL, 512), wi[:, 2560:3200],
                              zeros(MLA_NOPE), wi[:, 3200:3232], zeros(LANES - MLA_NOPE - MLA_ROPE)], axis=1)
    qu = mla_w_q_up[l].reshape(MLA_Q_RANK, MLA_HEADS, MLA_NOPE + MLA_ROPE)
    qu = jnp.pad(qu, ((0, 0), (0, 0), (0, LANES - MLA_NOPE - MLA_ROPE))).reshape(MLA_Q_RANK, MLA_HEADS * LANES)
    kvu = mla_w_kv_up[l].reshape(MLA_KV_RANK, MLA_HEADS, MLA_NOPE + MLA_V)
    ku = jnp.pad(kvu[:, :, :MLA_NOPE], ((0, 0), (0, 0), (0, LANES - MLA_NOPE))).reshape(MLA_KV_RANK, -1)
    vu = kvu[:, :, MLA_NOPE:].reshape(MLA_KV_RANK, MLA_HEADS * MLA_V)
    wb = w_branch[l]
    wb_gqa = wb[2].reshape(GQA_Q_HEADS, HEAD_DIM, D_MODEL)[jnp.array(_GQA_HEAD_ORDER)].reshape(BRANCH_W, D_MODEL)
    wb = jnp.stack([wb[0], wb[1], wb_gqa, wb[3]])
    return {
        "w_in": w_in_p.astype(BF16),
        "gq_norm": jnp.tile(gqa_q_norm[l], GQA_Q_HEADS).reshape(1, 512),
        "gk_norm": jnp.tile(gqa_k_norm[l], GQA_KV_HEADS).reshape(1, LANES),
        "mq_norm": mla_q_norm[l].reshape(1, MLA_Q_RANK),
        "mkv_norm": mla_kv_norm[l].reshape(1, MLA_KV_RANK),
        "w_q_up": qu.astype(BF16), "w_k_up": ku.astype(BF16), "w_v_up": vu.astype(BF16),
        "w_branch": wb.astype(BF16), "w_gate": w_gate[l].astype(BF16), "b_gate": b_gate[l],
        "w_out": w_out[l].astype(BF16),
    }


def kernel(x, c, ctx, c_ctx, w_mod, b_mod, mix_pre_norm, mix_post_norm, ffn_pre_norm, ffn_post_norm,
           w_in, diff_lambda, diff_subnorm, gqa_q_norm, gqa_k_norm, mla_q_norm, mla_kv_norm,
           mla_w_q_up, mla_w_kv_up, w_branch, w_gate, b_gate, w_out,
           dense_w_gate_up, dense_w_down, moe_router, moe_w_gate_up, moe_w_down):
    B, n_lat, d = x.shape
    n_ctx = ctx.shape[1]
    depth = w_mod.shape[0]
    assert d == D_MODEL and n_lat % TM == 0 and n_ctx % TM == 0 and n_lat % n_ctx == 0
    assert n_lat % GRID_W == 0
    T = n_lat + n_ctx
    n_lat_tiles, n_all_tiles = n_lat // TM, T // TM
    consts = _constants(n_lat, n_ctx)
    x_all = (x, ctx)
    c_rows = jnp.pad(jnp.concatenate([c, c_ctx[None]], axis=0), ((0, (-(B + 1)) % 8), (0, 0)))
    row = lambda v: v.reshape(1, -1)

    for layer in range(depth):
        last = layer == depth - 1
        n_tiles = n_lat_tiles if last else n_all_tiles
        lam_init = 0.8 - 0.6 * math.exp(-0.3 * layer)
        wts = _layer_weights(layer, w_in, gqa_q_norm, gqa_k_norm, mla_q_norm, mla_kv_norm, mla_w_q_up,
                             mla_w_kv_up, w_branch, w_gate, b_gate, w_out)
        mod_rows = _modulation(c_rows, w_mod[layer], b_mod[layer])
        mods = jnp.stack([mod_rows[:B].reshape(B, N_MOD, D_MODEL),
                          jnp.broadcast_to(mod_rows[B].reshape(1, N_MOD, D_MODEL), (B, N_MOD, D_MODEL))], axis=1)

        (h, dq, dk, dv, gq, gk, gv, mq, mk, mv, pq_lat, pq_ctx) = _project(
            x_all, mods, row(mix_pre_norm[layer]), wts, consts, n_lat_tiles)

        diff_extra = (diff_lambda[layer], diff_subnorm[layer].reshape(1, LANES))
        ys = []
        for mode, q, k, v in (("diff", dq, dk, dv), ("gqa", gq, gk, gv), ("mla", mq, mk, mv)):
            y = jnp.zeros((B, T, BRANCH_W), BF16)
            kw = dict(lam_init=lam_init, extra=diff_extra)
            y = _attention(mode, q, k, v, y, tq=TQ, nq=n_lat // TQ, q_blk0=0, tk=T, k_blk=0, **kw)
            if not last:
                y = _attention(mode, q, k, v, y, tq=TM, nq=n_ctx // TM, q_blk0=n_lat_tiles,
                               tk=n_ctx, k_blk=n_lat // n_ctx, **kw)
            ys.append(y)
        y_diff, y_gqa, y_mla = ys

        a_rows = _dft_rows(pq_lat, consts["m1"])
        y_f = _dft_cols(a_rows, consts["tw_cos"], consts["tw_sin"], consts["m3"]).reshape(B, n_lat, BRANCH_W)
        if not last:
            y_f = jnp.concatenate([y_f, _dft_ctx(pq_ctx, consts["w_ctx"])], axis=1)

        merge_args = (x_all, h, y_diff, y_f, y_gqa, y_mla, mods, wts, row(mix_post_norm[layer]),
                      n_lat_tiles, n_tiles)
        if layer % 2 == 0:
            x_all = _merge(*merge_args)
            x_all = _dense_ffn(x_all, mods, row(ffn_pre_norm[layer]), row(ffn_post_norm[layer]),
                               dense_w_gate_up[layer // 2].astype(BF16), dense_w_down[layer // 2].astype(BF16),
                               n_lat_tiles, n_tiles)
        else:
            wr = jnp.pad(moe_router[layer // 2], ((0, 0), (0, LANES - N_EXPERTS)))
            idx = jnp.arange(TM)
            tri = jnp.where(idx[:, None] > idx[None, :], 1.0, 0.0).astype(BF16)
            x_all, h_ffn, route, counts = _merge(*merge_args, router=(row(ffn_pre_norm[layer]), wr, tri))
            x_all = _routed_ffn(x_all, h_ffn, route, counts, mods, row(ffn_post_norm[layer]),
                                moe_w_gate_up[layer // 2].astype(BF16), moe_w_down[layer // 2].astype(BF16),
                                n_lat_tiles, n_tiles)
    return x_all[:, :n_lat]
```

```python
import functools
import math

import jax
import jax.numpy as jnp
from jax import lax
from jax.experimental import pallas as pl
from jax.experimental.pallas import tpu as pltpu

F32 = jnp.float32
BF16 = jnp.bfloat16

D_MODEL = 1024
GRID_W = 64
ROPE_BASE = 10000.0
EPS = 1e-6
N_MOD = 6
HEAD_DIM = 64
DIFF_HEADS = 4
GQA_Q_HEADS = 8
GQA_KV_HEADS = 2
MLA_HEADS = 8
MLA_Q_RANK = 256
MLA_KV_RANK = 128
MLA_NOPE = 64
MLA_ROPE = 32
MLA_V = 64
BRANCH_W = 512
FOURIER_GROUP = 128
FFN_DIM = 2816
N_EXPERTS = 8
EXPERT_DIM = 3584
LOG2E = math.log2(math.e)
QK_SCALE = HEAD_DIM ** -0.5 * LOG2E
MLA_SCALE = (MLA_NOPE + MLA_ROPE) ** -0.5 * LOG2E
IN_COLS_PADDED = 3328

LANES = 128
VMEM_LIMIT = 56 * 1024 * 1024

TM = 256
TQ = 1024


def _cparams(sem):
    return pltpu.CompilerParams(dimension_semantics=sem, vmem_limit_bytes=VMEM_LIMIT)


def _full_spec(shape):
    n = len(shape)
    return pl.BlockSpec(shape, lambda *_: (0,) * n)


def _split_bf16(x):
    hi = x.astype(BF16)
    lo = (x - hi.astype(F32)).astype(BF16)
    return hi, lo


def _dot(a, b):
    return jnp.dot(a, b, preferred_element_type=F32)


def _dot3(a, w):
    ah, al = _split_bf16(a)
    wh, wl = _split_bf16(w)
    return _dot(ah, wh) + _dot(ah, wl) + _dot(al, wh)


def _rms(x, gain):
    ms = jnp.mean(x * x, axis=-1, keepdims=True)
    return x * lax.rsqrt(ms + EPS) * gain


def _mod_kernel(c_ref, w_ref, b_ref, o_ref):
    c = c_ref[...]
    a = c * jax.nn.sigmoid(c)
    o_ref[...] = _dot3(a, w_ref[...]) + b_ref[...]


def _modulation(c_rows, w_mod_l, b_mod_l):
    rows = c_rows.shape[0]
    n = w_mod_l.shape[1]
    tn = 512
    return pl.pallas_call(
        _mod_kernel,
        grid=(n // tn,),
        in_specs=[pl.BlockSpec((rows, D_MODEL), lambda j: (0, 0)),
                  pl.BlockSpec((D_MODEL, tn), lambda j: (0, j)),
                  pl.BlockSpec((1, tn), lambda j: (0, j))],
        out_specs=pl.BlockSpec((rows, tn), lambda j: (0, j)),
        out_shape=jax.ShapeDtypeStruct((rows, n), F32),
        compiler_params=_cparams(("arbitrary",)),
        name="modulation",
    )(c_rows, w_mod_l, b_mod_l.reshape(1, n))


def _rope(x, cos, sin_signed, half):
    lane = lax.broadcasted_iota(jnp.int32, x.shape, 1)
    first = (lane % (2 * half)) < half
    up = pltpu.roll(x, LANES - half, 1)
    down = pltpu.roll(x, half, 1)
    return x * cos + jnp.where(first, up, down) * sin_signed


def _stream_specs(xs, n_lat_tiles):
    if not isinstance(xs, tuple):
        return [pl.BlockSpec((None, TM, D_MODEL), lambda b, t: (b, t, 0))], [xs]
    return ([pl.BlockSpec((None, TM, D_MODEL), lambda b, t: (b, jnp.minimum(t, n_lat_tiles - 1), 0)),
             pl.BlockSpec((None, TM, D_MODEL), lambda b, t: (b, jnp.maximum(t - n_lat_tiles, 0), 0))], list(xs))


def _stream_tile(x_refs, n_lat_tiles):
    if len(x_refs) == 1:
        return x_refs[0][...]
    return jnp.where(pl.program_id(1) < n_lat_tiles, x_refs[0][...], x_refs[1][...])


def _stream_dims(xs):
    if isinstance(xs, tuple):
        return xs[0].shape[0], xs[0].shape[1] + xs[1].shape[1]
    return xs.shape[0], xs.shape[1]


def _proj_kernel(n_lat_tiles, n_x, *refs):
    x_refs, refs = refs[:n_x], refs[n_x:]
    (mod_ref, gpre_ref, win_ref, gqn_ref, gkn_ref, mqn_ref, mkvn_ref,
     wqup_ref, wkup_ref, wvup_ref, seg_ref, cc_ref, sc_ref,
     cosh_ref, sinh_ref, cosm_ref, sinm_ref,
     h_ref, dq_ref, dk_ref, dv_ref, gq_ref, gk_ref, gv_ref, mq_ref, mk_ref, mv_ref, pq_ref, pqc_ref) = refs
    x = _stream_tile(x_refs, n_lat_tiles)
    mod = mod_ref[...]
    h = _rms(x, gpre_ref[...]) * (1.0 + mod[1:2]) + mod[0:1]
    hb = h.astype(BF16)
    h_ref[...] = hb
    z = _dot(hb, win_ref[...])

    def zcols(lo, hi):
        return z[:, lo:hi]
    cosh, sinh = cosh_ref[...], sinh_ref[...]
    cosm, sinm = cosm_ref[...], sinm_ref[...]
    seg = seg_ref[...]

    def head_rms(v, gain, w):
        sq = v * v
        hi, lo = _split_bf16(sq)
        ms = _dot(hi, seg[:w, :w]) + _dot(lo, seg[:w, :w])
        return v * lax.rsqrt(ms + EPS) * gain

    for j in range(4):
        c0 = j * LANES
        dq_ref[:, c0:c0 + LANES] = (_rope(zcols(c0, c0 + LANES), cosh, sinh, 16) * QK_SCALE).astype(BF16)
        dk_ref[:, c0:c0 + LANES] = _rope(zcols(512 + c0, 512 + c0 + LANES), cosh, sinh, 16).astype(BF16)
    dv_ref[...] = zcols(1024, 1536).astype(BF16)
    fparts = []
    for g in range(4):
        c0 = 1536 + g * LANES
        fb = zcols(c0, c0 + LANES).astype(BF16)
        fparts.append((_dot(fb, cc_ref[...]).astype(BF16), _dot(fb, sc_ref[...]).astype(BF16)))
    is_lat = pl.program_id(1) < n_lat_tiles

    def put(ref):
        for g, (pc, ps) in enumerate(fparts):
            ref[0, :, g * LANES:(g + 1) * LANES] = pc
            ref[1, :, g * LANES:(g + 1) * LANES] = ps

    pl.when(is_lat)(lambda: put(pq_ref))
    pl.when(jnp.logical_not(is_lat))(lambda: put(pqc_ref))
    gq = head_rms(zcols(2048, 2560), gqn_ref[...], 512)
    for j in range(4):
        c0 = j * LANES
        gq_ref[:, c0:c0 + LANES] = (_rope(gq[:, c0:c0 + LANES], cosh, sinh, 16) * QK_SCALE).astype(BF16)
    gk = head_rms(zcols(2560, 2688), gkn_ref[...], LANES)
    gk_ref[...] = _rope(gk, cosh, sinh, 16).astype(BF16)
    gv_ref[...] = zcols(2688, 2816).astype(BF16)
    cq = _rms(zcols(2816, 3072), mqn_ref[...]).astype(BF16)
    mq = _dot(cq, wqup_ref[...])
    ckv = _rms(zcols(3072, 3200), mkvn_ref[...]).astype(BF16)
    mk = _dot(ckv, wkup_ref[...])
    mv_ref[...] = _dot(ckv, wvup_ref[...]).astype(BF16)
    kr = _rope(zcols(3200, 3328), cosm, sinm, 8)
    for hh in range(MLA_HEADS):
        c0 = hh * LANES
        mq_ref[:, c0:c0 + LANES] = (_rope(mq[:, c0:c0 + LANES], cosm, sinm, 8) * MLA_SCALE).astype(BF16)
        mk_ref[:, c0:c0 + LANES] = (mk[:, c0:c0 + LANES] + kr).astype(BF16)


def _project(x_all, mods, gpre, wts, consts, n_lat_tiles):
    B, T = _stream_dims(x_all)
    nt = T // TM
    tok = lambda w: pl.BlockSpec((None, TM, w), lambda b, t: (b, t, 0))
    tab = pl.BlockSpec((TM, LANES), lambda b, t: (t, 0))
    x_specs, x_args = _stream_specs(x_all, n_lat_tiles)
    in_specs = x_specs + [
        pl.BlockSpec((None, None, N_MOD, D_MODEL), lambda b, t: (b, jnp.where(t >= n_lat_tiles, 1, 0), 0, 0)),
        _full_spec((1, D_MODEL)),
        _full_spec((D_MODEL, IN_COLS_PADDED)),
        _full_spec((1, 512)), _full_spec((1, LANES)), _full_spec((1, MLA_Q_RANK)), _full_spec((1, MLA_KV_RANK)),
        _full_spec((MLA_Q_RANK, MLA_HEADS * LANES)),
        _full_spec((MLA_KV_RANK, MLA_HEADS * LANES)),
        _full_spec((MLA_KV_RANK, MLA_HEADS * MLA_V)),
        _full_spec((512, 512)), _full_spec((LANES, LANES)), _full_spec((LANES, LANES)),
        tab, tab, tab, tab,
    ]
    widths = [D_MODEL, 512, 512, 512, 512, LANES, LANES, MLA_HEADS * LANES, MLA_HEADS * LANES, 512]
    out_specs = [tok(w) for w in widths]
    out_shape = [jax.ShapeDtypeStruct((B, T, w), BF16) for w in widths]
    out_specs.append(pl.BlockSpec((2, None, TM, 512), lambda b, t: (0, b, jnp.minimum(t, n_lat_tiles - 1), 0)))
    out_shape.append(jax.ShapeDtypeStruct((2, B, n_lat_tiles * TM, 512), BF16))
    out_specs.append(pl.BlockSpec((2, None, TM, 512), lambda b, t: (0, b, jnp.maximum(t - n_lat_tiles, 0), 0)))
    out_shape.append(jax.ShapeDtypeStruct((2, B, T - n_lat_tiles * TM, 512), BF16))
    return pl.pallas_call(
        functools.partial(_proj_kernel, n_lat_tiles, len(x_args)),
        grid=(B, nt),
        in_specs=in_specs,
        out_specs=out_specs,
        out_shape=out_shape,
        compiler_params=_cparams(("arbitrary", "arbitrary")),
        name="project",
    )(*x_args, mods, gpre, wts["w_in"], wts["gq_norm"], wts["gk_norm"], wts["mq_norm"], wts["mkv_norm"],
      wts["w_q_up"], wts["w_k_up"], wts["w_v_up"], consts["seg"], consts["cc"], consts["sc"],
      consts["cos_h"], consts["sin_h"], consts["cos_m"], consts["sin_m"])


N_MAPS = 8
MAPS_PER_STEP = 2


def _map_plan(mode, m):
    if mode == "diff":
        return m // 2, m % 2, m // 2, m // 2
    if mode == "gqa":
        return m // 2, m % 2, 0, 0
    return m, None, m, m // 2


def _attn_kernel(mode, lam_init, q_ref, k_ref, v_ref, *rest):
    if mode == "diff":
        lam_ref, sub_ref, _, o_ref, vx_ref = rest
    else:
        _, o_ref, vx_ref = rest
    nt = (((1,), (1,)), ((), ()))
    lane = lax.broadcasted_iota(jnp.int32, (1, LANES), 1)
    low = lane < HEAD_DIM

    @pl.when(pl.program_id(2) == 0)
    def _():
        for g in range(vx_ref.shape[0]):
            vx_ref[g, :, :LANES] = v_ref[:, g * LANES:(g + 1) * LANES]
            vx_ref[g, :, LANES:] = jnp.ones((vx_ref.shape[1], LANES), BF16)

    def scores(m):
        qblk, half, kblk, _ = _map_plan(mode, m)
        q = q_ref[:, qblk * LANES:(qblk + 1) * LANES]
        if half is not None:
            q = jnp.where(low if half == 0 else jnp.logical_not(low), q, jnp.zeros_like(q))
        return lax.dot_general(q, k_ref[:, kblk * LANES:(kblk + 1) * LANES], nt, preferred_element_type=F32)

    def numerator(s):
        return jnp.exp2(s - jnp.max(s, axis=-1, keepdims=True)).astype(BF16)

    def values(m, p):
        e = _dot(p, vx_ref[_map_plan(mode, m)[3]])
        return e[:, :LANES] / e[:, LANES:]

    s, pl_, y = {}, {}, {}
    for t in range(MAPS_PER_STEP + 2):
        if t < MAPS_PER_STEP:
            s[t] = scores(t)
        if 0 <= t - 1 < MAPS_PER_STEP:
            pl_[t - 1] = numerator(s.pop(t - 1))
        if 0 <= t - 2:
            y[t - 2] = values(t - 2, pl_.pop(t - 2))

    if mode == "diff":
        dl = lam_ref[...]
        lam = (jnp.exp(jnp.sum(dl[0:1] * dl[1:2], axis=-1, keepdims=True))
               - jnp.exp(jnp.sum(dl[2:3] * dl[3:4], axis=-1, keepdims=True)) + lam_init)
    for j in range(MAPS_PER_STEP // 2):
        if mode == "diff":
            out = _rms(y[2 * j] - lam * y[2 * j + 1], sub_ref[...]) * (1.0 - lam_init)
        else:
            out = jnp.where(low, y[2 * j], y[2 * j + 1])
        o_ref[:, j * LANES:(j + 1) * LANES] = out.astype(BF16)


def _attention(mode, q, k, v, y_prev, *, tq, nq, q_blk0, tk, k_blk, lam_init=0.0, extra=()):
    B, T, _ = q.shape
    n_steps = N_MAPS // MAPS_PER_STEP
    wq, wk, wv = q.shape[2] // n_steps, max(k.shape[2] // n_steps, LANES), max(v.shape[2] // n_steps, LANES)
    kv_blk = (lambda g: 0) if mode == "gqa" else (lambda g: g)
    in_specs = [
        pl.BlockSpec((None, tq, wq), lambda b, g, i: (b, i + q_blk0, g)),
        pl.BlockSpec((None, tk, wk), lambda b, g, i: (b, k_blk, kv_blk(g))),
        pl.BlockSpec((None, tk, wv), lambda b, g, i: (b, k_blk, kv_blk(g))),
    ]
    args = [q, k, v]
    if mode == "diff":
        in_specs += [_full_spec((4, HEAD_DIM)), _full_spec((1, LANES))]
        args += list(extra)
    in_specs.append(pl.BlockSpec(memory_space=pl.ANY))
    args.append(y_prev)
    wo = BRANCH_W // n_steps
    return pl.pallas_call(
        functools.partial(_attn_kernel, mode, lam_init),
        grid=(B, n_steps, nq),
        in_specs=in_specs,
        out_specs=pl.BlockSpec((None, tq, wo), lambda b, g, i: (b, i + q_blk0, g)),
        out_shape=jax.ShapeDtypeStruct((B, T, BRANCH_W), BF16),
        input_output_aliases={len(args) - 1: 0},
        scratch_shapes=[pltpu.VMEM((wv // LANES, tk, 2 * LANES), BF16)],
        compiler_params=_cparams(("arbitrary", "arbitrary", "arbitrary")),
        name="attention_" + mode,
    )(*args)


DFT_COLS_PER_STEP = 8


def _dft_rows_kernel(pq_ref, m1_ref, a_ref):
    r = pq_ref.shape[1]
    for n2 in range(pq_ref.shape[2]):
        zin = jnp.concatenate([pq_ref[0, :, n2, :], pq_ref[1, :, n2, :]], axis=0)
        a = _dot(m1_ref[...], zin).astype(BF16)
        a_ref[0, :, n2, :] = a[:r]
        a_ref[1, :, n2, :] = a[r:]


def _dft_rows(pq_lat, m1):
    _, B, n_lat, w = pq_lat.shape
    r = n_lat // GRID_W
    nb = DFT_COLS_PER_STEP
    return pl.pallas_call(
        _dft_rows_kernel,
        grid=(B, GRID_W // nb),
        in_specs=[pl.BlockSpec((2, None, r, nb, w), lambda b, j: (0, b, 0, j, 0)), _full_spec((2 * r, 2 * r))],
        out_specs=pl.BlockSpec((None, 2, r, nb, w), lambda b, j: (b, 0, 0, j, 0)),
        out_shape=jax.ShapeDtypeStruct((B, 2, r, GRID_W, w), BF16),
        compiler_params=_cparams(("arbitrary", "arbitrary")),
        name="dft_rows",
    )(pq_lat.reshape(2, B, r, GRID_W, w), m1)


def _dft_cols_kernel(a_ref, ct_ref, st_ref, m3_ref, o_ref):
    kb = a_ref.shape[1]
    w = a_ref.shape[3]
    for j in range(kb):
        ar = a_ref[0, j].astype(F32)
        ai = a_ref[1, j].astype(F32)
        ct = jnp.concatenate([ct_ref[j]] * (w // LANES), axis=1)
        st = jnp.concatenate([st_ref[j]] * (w // LANES), axis=1)
        b = jnp.concatenate([ar * ct + ai * st, ai * ct - ar * st], axis=0).astype(BF16)
        o_ref[:, j, :] = _dot(m3_ref[...], b).astype(BF16)


def _dft_cols(a, ct, st, m3):
    B, _, r, _, w = a.shape
    kb = DFT_COLS_PER_STEP
    return pl.pallas_call(
        _dft_cols_kernel,
        grid=(B, r // kb),
        in_specs=[pl.BlockSpec((None, 2, kb, GRID_W, w), lambda b, k: (b, 0, k, 0, 0)),
                  pl.BlockSpec((kb, GRID_W, LANES), lambda b, k: (k, 0, 0)),
                  pl.BlockSpec((kb, GRID_W, LANES), lambda b, k: (k, 0, 0)),
                  _full_spec((GRID_W, 2 * GRID_W))],
        out_specs=pl.BlockSpec((None, GRID_W, kb, w), lambda b, k: (b, 0, k, 0)),
        out_shape=jax.ShapeDtypeStruct((B, GRID_W, r, w), BF16),
        compiler_params=_cparams(("arbitrary", "arbitrary")),
        name="dft_cols",
    )(a, ct, st, m3)


def _dft_ctx_kernel(pq_ref, w_ref, o_ref):
    zin = jnp.concatenate([pq_ref[0], pq_ref[1]], axis=0)
    o_ref[...] = _dot(w_ref[...], zin).astype(BF16)


def _dft_ctx(pq_ctx, w_ctx):
    _, B, n_ctx, w = pq_ctx.shape
    return pl.pallas_call(
        _dft_ctx_kernel,
        grid=(B,),
        in_specs=[pl.BlockSpec((2, None, n_ctx, w), lambda b: (0, b, 0, 0)), _full_spec((n_ctx, 2 * n_ctx))],
        out_specs=pl.BlockSpec((None, n_ctx, w), lambda b: (b, 0, 0)),
        out_shape=jax.ShapeDtypeStruct((B, n_ctx, w), BF16),
        compiler_params=_cparams(("arbitrary",)),
        name="dft_ctx",
    )(pq_ctx, w_ctx)


def _merge_kernel(n_lat_tiles, n_x, *refs):
    x_refs, refs = refs[:n_x], refs[n_x:]
    h_ref, y0_ref, y1_ref, y2_ref, y3_ref, mod_ref, wg_ref, bg_ref, wb_ref, wo_ref, gpost_ref = refs[:11]
    rest = refs[11:]
    h = h_ref[...]
    ys = (y0_ref, y1_ref, y2_ref, y3_ref)
    acc = None
    for i in range(4):
        gate = jax.nn.sigmoid(_dot(h, wg_ref[i]) + bg_ref[i:i + 1])
        term = gate * _dot(ys[i][...], wb_ref[i])
        acc = term if acc is None else acc + term
    m = _dot(acc.astype(BF16), wo_ref[...])
    mod = mod_ref[...]
    x = _stream_tile(x_refs, n_lat_tiles) + mod[2:3] * _rms(m, gpost_ref[...])
    o_ref = rest[3] if len(rest) > 1 else rest[0]
    o_ref[...] = x
    if len(rest) > 1:
        gffn_ref, wr_ref, tri_ref, _, hf_ref, route_ref, cnt_ref, carry_ref = rest
        _route_tile(x, mod, gffn_ref, wr_ref, tri_ref, hf_ref, route_ref, cnt_ref, carry_ref)


def _merge(x_all, h, y_diff, y_f, y_gqa, y_mla, mods, wts, gpost, n_lat_tiles, n_tiles, router=None):
    B, _ = _stream_dims(x_all)
    tok = lambda w: pl.BlockSpec((None, TM, w), lambda b, t: (b, t, 0))
    x_specs, x_args = _stream_specs(x_all, n_lat_tiles)
    in_specs = x_specs + [
        tok(D_MODEL), tok(BRANCH_W), tok(BRANCH_W), tok(BRANCH_W), tok(BRANCH_W),
        pl.BlockSpec((None, None, N_MOD, D_MODEL), lambda b, t: (b, jnp.where(t >= n_lat_tiles, 1, 0), 0, 0)),
        _full_spec((4, D_MODEL, D_MODEL)), _full_spec((4, D_MODEL)), _full_spec((4, BRANCH_W, D_MODEL)),
        _full_spec((D_MODEL, D_MODEL)), _full_spec((1, D_MODEL)),
    ]
    args = x_args + [h, y_diff, y_f, y_gqa, y_mla, mods, wts["w_gate"], wts["b_gate"], wts["w_branch"],
                     wts["w_out"], gpost]
    out_specs = [tok(D_MODEL)]
    out_shape = [jax.ShapeDtypeStruct((B, n_tiles * TM, D_MODEL), F32)]
    scratch = []
    if router is not None:
        in_specs += [_full_spec((1, D_MODEL)), _full_spec((D_MODEL, LANES)), _full_spec((TM, TM))]
        args += list(router)
        out_specs += [tok(D_MODEL), tok(LANES), _full_spec((1, LANES))]
        out_shape += [jax.ShapeDtypeStruct((B, n_tiles * TM, D_MODEL), F32),
                      jax.ShapeDtypeStruct((B, n_tiles * TM, LANES), F32),
                      jax.ShapeDtypeStruct((1, LANES), F32)]
        scratch = [pltpu.VMEM((1, LANES), F32)]
    out = pl.pallas_call(
        functools.partial(_merge_kernel, n_lat_tiles, len(x_args)),
        grid=(B, n_tiles),
        in_specs=in_specs,
        out_specs=out_specs,
        out_shape=out_shape,
        scratch_shapes=scratch,
        compiler_params=_cparams(("arbitrary", "arbitrary")),
        name="merge",
    )(*args)
    return out[0] if router is None else out


def _dense_ffn_kernel(nf, tf, x_ref, mod_ref, gpre_ref, gpost_ref, wgu_ref, wd_ref, o_ref):
    x = x_ref[...]
    mod = mod_ref[...]
    hb = (_rms(x, gpre_ref[...]) * (1.0 + mod[4:5]) + mod[3:4]).astype(BF16)
    acc = None
    for f in range(nf):
        a = _dot(hb, wgu_ref[:, f * tf:(f + 1) * tf])
        u = _dot(hb, wgu_ref[:, FFN_DIM + f * tf:FFN_DIM + (f + 1) * tf])
        act = (a * jax.nn.sigmoid(a) * u).astype(BF16)
        part = _dot(act, wd_ref[f * tf:(f + 1) * tf, :])
        acc = part if acc is None else acc + part
    o_ref[...] = x + mod[5:6] * _rms(acc, gpost_ref[...])


def _dense_ffn(x_all, mods, gpre, gpost, wgu, wd, n_lat_tiles, n_tiles):
    B = x_all.shape[0]
    nf, tf = 1, FFN_DIM
    tok = pl.BlockSpec((None, TM, D_MODEL), lambda b, t: (b, t, 0))
    return pl.pallas_call(
        functools.partial(_dense_ffn_kernel, nf, tf),
        grid=(B, n_tiles),
        in_specs=[tok,
                  pl.BlockSpec((None, None, N_MOD, D_MODEL),
                               lambda b, t: (b, jnp.where(t >= n_lat_tiles, 1, 0), 0, 0)),
                  _full_spec((1, D_MODEL)), _full_spec((1, D_MODEL)),
                  _full_spec((D_MODEL, 2 * FFN_DIM)), _full_spec((FFN_DIM, D_MODEL))],
        out_specs=tok,
        out_shape=jax.ShapeDtypeStruct((B, n_tiles * TM, D_MODEL), F32),
        compiler_params=_cparams(("arbitrary", "arbitrary")),
        name="dense_ffn",
    )(x_all, mods, gpre, gpost, wgu, wd)


TE = 512
R_E1, R_E2, R_G1, R_G2, R_RANK1, R_RANK2 = range(6)


def _route_tile(x, mod, gpre_ref, wr_ref, tri_ref, h_ref, route_ref, cnt_ref, carry_ref):
    first = jnp.logical_and(pl.program_id(0) == 0, pl.program_id(1) == 0)

    @pl.when(first)
    def _():
        carry_ref[...] = jnp.zeros_like(carry_ref)

    h = _rms(x, gpre_ref[...]) * (1.0 + mod[4:5]) + mod[3:4]
    h_ref[...] = h
    logits = _dot3(h, wr_ref[...])
    lane = lax.broadcasted_iota(jnp.int32, logits.shape, 1)
    neg = jnp.float32(-jnp.inf)
    l1 = jnp.where(lane < N_EXPERTS, logits, neg)
    m1 = jnp.max(l1, axis=-1, keepdims=True)
    i1 = jnp.min(jnp.where(l1 == m1, lane, LANES), axis=-1, keepdims=True)
    l2 = jnp.where(lane == i1, neg, l1)
    m2 = jnp.max(l2, axis=-1, keepdims=True)
    i2 = jnp.min(jnp.where(l2 == m2, lane, LANES), axis=-1, keepdims=True)
    e2 = jnp.exp(m2 - m1)
    g1 = 1.0 / (1.0 + e2)
    g2 = e2 / (1.0 + e2)
    sel1, sel2 = lane == i1, lane == i2
    chosen = jnp.where(jnp.logical_or(sel1, sel2), 1.0, 0.0)
    before = _dot(tri_ref[...], chosen.astype(BF16)) + carry_ref[...]
    r1 = jnp.sum(jnp.where(sel1, before, 0.0), axis=-1, keepdims=True)
    r2 = jnp.sum(jnp.where(sel2, before, 0.0), axis=-1, keepdims=True)
    carry_ref[...] += jnp.sum(chosen, axis=0, keepdims=True)
    cnt_ref[...] = carry_ref[...]
    rec = jnp.zeros_like(logits)
    for col, val in ((R_E1, i1.astype(F32)), (R_E2, i2.astype(F32)), (R_G1, g1), (R_G2, g2),
                     (R_RANK1, r1), (R_RANK2, r2)):
        rec = jnp.where(lane == col, val, rec)
    route_ref[...] = rec


def _row_copies(n_rows, copy_of, slot_wait):
    def start(r, carry):
        copy_of(r, 0).start()
        copy_of(r, 1).start()
        return carry

    lax.fori_loop(0, n_rows, start, 0, unroll=8)
    slot_wait(0).wait()
    slot_wait(1).wait()


def _dispatch_kernel(pos_ref, h_ref, _, xs_ref, sem):
    def copy_of(r, slot):
        p = pos_ref[0, 2 * r + slot]
        return pltpu.make_async_copy(h_ref.at[pl.ds(r, 1)], xs_ref.at[pl.ds(p, 1)], sem)

    n = h_ref.shape[0]
    _row_copies(n, copy_of, lambda slot: pltpu.make_async_copy(h_ref, xs_ref.at[pl.ds(0, n)], sem))


def _dispatch(h, pos, n_rows):
    n = h.shape[0]
    xs0 = jnp.zeros((n_rows, D_MODEL), F32)
    return pl.pallas_call(
        _dispatch_kernel,
        grid=(n // TM,),
        in_specs=[pl.BlockSpec((None, 1, 2 * TM), lambda i: (i, 0, 0), memory_space=pltpu.SMEM),
                  pl.BlockSpec((TM, D_MODEL), lambda i: (i, 0)),
                  pl.BlockSpec(memory_space=pl.ANY)],
        out_specs=pl.BlockSpec(memory_space=pl.ANY),
        out_shape=jax.ShapeDtypeStruct((n_rows, D_MODEL), F32),
        input_output_aliases={2: 0},
        scratch_shapes=[pltpu.SemaphoreType.DMA],
        compiler_params=_cparams(("arbitrary",)),
        name="moe_dispatch",
    )(pos.reshape(n // TM, 1, 2 * TM), h, xs0)


def _experts_kernel(te_ref, nv_ref, xs_ref, wa_ref, wu_ref, wd_ref, ys_ref, acc_ref):
    del te_ref
    f = pl.program_id(1)

    @pl.when(pl.program_id(0) < nv_ref[0])
    def _():
        x = xs_ref[...].astype(BF16)
        a = _dot(x, wa_ref[...])
        u = _dot(x, wu_ref[...])
        part = _dot((a * jax.nn.sigmoid(a) * u).astype(BF16), wd_ref[...])

        last = pl.num_programs(1) - 1

        @pl.when(f == 0)
        def _():
            acc_ref[...] = part

        @pl.when(jnp.logical_and(f > 0, f < last))
        def _():
            acc_ref[...] += part

        @pl.when(f == last)
        def _():
            ys_ref[...] = acc_ref[...] + part


def _experts(xs, tile_expert, n_valid, wgu, wd):
    n_rows = xs.shape[0]
    nf = 2
    tf = EXPERT_DIM // nf
    row_blk = lambda i, f, te, nv: (jnp.minimum(i, nv[0] - 1), 0)
    grid_spec = pltpu.PrefetchScalarGridSpec(
        num_scalar_prefetch=2,
        grid=(n_rows // TE, nf),
        in_specs=[pl.BlockSpec((TE, D_MODEL), row_blk),
                  pl.BlockSpec((None, D_MODEL, tf), lambda i, f, te, nv: (te[i], 0, f)),
                  pl.BlockSpec((None, D_MODEL, tf), lambda i, f, te, nv: (te[i], 0, nf + f)),
                  pl.BlockSpec((None, tf, D_MODEL), lambda i, f, te, nv: (te[i], f, 0))],
        out_specs=pl.BlockSpec((TE, D_MODEL), row_blk),
        scratch_shapes=[pltpu.VMEM((TE, D_MODEL), F32)],
    )
    return pl.pallas_call(
        _experts_kernel,
        grid_spec=grid_spec,
        out_shape=jax.ShapeDtypeStruct((n_rows, D_MODEL), F32),
        compiler_params=_cparams(("arbitrary", "arbitrary")),
        name="moe_experts",
    )(tile_expert, n_valid, xs, wgu, wgu, wd)


def _combine_kernel(pos_ref, x_ref, route_ref, mod_ref, gpost_ref, ys_ref, o_ref, buf_ref, sem):
    def copy_of(r, slot):
        p = pos_ref[0, 2 * r + slot]
        return pltpu.make_async_copy(ys_ref.at[pl.ds(p, 1)], buf_ref.at[slot, pl.ds(r, 1)], sem)

    n = x_ref.shape[0]
    _row_copies(n, copy_of, lambda slot: pltpu.make_async_copy(ys_ref.at[pl.ds(0, n)], buf_ref.at[slot], sem))
    route = route_ref[...]
    y = route[:, R_G1:R_G1 + 1] * buf_ref[0] + route[:, R_G2:R_G2 + 1] * buf_ref[1]
    mod = mod_ref[...]
    o_ref[...] = x_ref[...] + mod[5:6] * _rms(y, gpost_ref[...])


def _combine(x_all, route, pos, ys, mods, gpost, n_lat_tiles, n_tiles):
    B = x_all.shape[0]
    tok = lambda w: pl.BlockSpec((None, TM, w), lambda b, t: (b, t, 0))
    return pl.pallas_call(
        _combine_kernel,
        grid=(B, n_tiles),
        in_specs=[pl.BlockSpec((None, None, 1, 2 * TM), lambda b, t: (b, t, 0, 0), memory_space=pltpu.SMEM),
                  tok(D_MODEL), tok(LANES),
                  pl.BlockSpec((None, None, N_MOD, D_MODEL),
                               lambda b, t: (b, jnp.where(t >= n_lat_tiles, 1, 0), 0, 0)),
                  _full_spec((1, D_MODEL)),
                  pl.BlockSpec(memory_space=pl.ANY)],
        out_specs=tok(D_MODEL),
        out_shape=jax.ShapeDtypeStruct((B, n_tiles * TM, D_MODEL), F32),
        scratch_shapes=[pltpu.VMEM((2, TM, D_MODEL), F32), pltpu.SemaphoreType.DMA],
        compiler_params=_cparams(("arbitrary", "arbitrary")),
        name="moe_combine",
    )(pos.reshape(B, n_tiles, 1, 2 * TM), x_all, route, mods, gpost, ys)


def _routed_ffn(x_all, h, route, counts, mods, gpost, wgu, wd, n_lat_tiles, n_tiles):
    B = x_all.shape[0]
    n_tok = B * n_tiles * TM
    rec = route.reshape(n_tok, LANES)
    expert = rec[:, R_E1:R_E2 + 1].astype(jnp.int32)
    rank = rec[:, R_RANK1:R_RANK2 + 1].astype(jnp.int32)
    tiles_per_expert = (counts[0, :N_EXPERTS].astype(jnp.int32) + TE - 1) // TE
    tile_end = jnp.cumsum(tiles_per_expert)
    n_valid = tile_end[-1:]
    pos = (tile_end - tiles_per_expert)[expert] * TE + rank
    n_tiles_max = (2 * n_tok) // TE + N_EXPERTS
    tile_id = jnp.minimum(jnp.arange(n_tiles_max, dtype=jnp.int32), n_valid - 1)
    tile_expert = jnp.minimum(jnp.searchsorted(tile_end, tile_id, side="right"), N_EXPERTS - 1).astype(jnp.int32)
    xs = _dispatch(h.reshape(n_tok, D_MODEL), pos, n_tiles_max * TE)
    ys = _experts(xs, tile_expert, n_valid.astype(jnp.int32), wgu, wd)
    return _combine(x_all, route, pos, ys, mods, gpost, n_lat_tiles, n_tiles)


def _rope_angles(n_lat, rot_dim):
    rows = n_lat // GRID_W
    row = jnp.repeat(jnp.arange(rows, dtype=F32), GRID_W)
    col = jnp.tile(jnp.arange(GRID_W, dtype=F32), rows)
    axis_dim = rot_dim // 2
    inv_freq = 1.0 / (ROPE_BASE ** (jnp.arange(0, axis_dim, 2, dtype=F32) / axis_dim))
    ar = row[:, None] * inv_freq
    ac = col[:, None] * inv_freq
    return jnp.concatenate([ar, ar, ac, ac], axis=-1)


def _constants(n_lat, n_ctx):
    lane = jnp.arange(LANES)
    ang = _rope_angles(n_lat, HEAD_DIM)
    cos_h = jnp.tile(jnp.cos(ang), (1, 2))
    sin_h = jnp.tile(jnp.sin(ang), (1, 2)) * jnp.where((lane % 32) < 16, -1.0, 1.0)
    cos_h = jnp.concatenate([cos_h, jnp.ones((n_ctx, LANES), F32)], axis=0)
    sin_h = jnp.concatenate([sin_h, jnp.zeros((n_ctx, LANES), F32)], axis=0)
    angm = _rope_angles(n_lat, MLA_ROPE)
    pad_l = jnp.zeros((n_lat, MLA_NOPE), F32)
    pad_r = jnp.zeros((n_lat, LANES - MLA_NOPE - MLA_ROPE), F32)
    cos_m = jnp.concatenate([pad_l + 1.0, jnp.cos(angm), pad_r + 1.0], axis=1)
    sin_m = jnp.concatenate([pad_l, jnp.sin(angm), pad_r], axis=1) * jnp.where((lane % 16) < 8, -1.0, 1.0)
    cos_m = jnp.concatenate([cos_m, jnp.ones((n_ctx, LANES), F32)], axis=0)
    sin_m = jnp.concatenate([sin_m, jnp.zeros((n_ctx, LANES), F32)], axis=0)
    idx = jnp.arange(512)
    seg = jnp.where((idx[:, None] // HEAD_DIM) == (idx[None, :] // HEAD_DIM), 1.0 / HEAD_DIM, 0.0).astype(BF16)

    def dft(n):
        i = jnp.arange(n, dtype=jnp.int32)
        a = ((i[:, None] * i[None, :]) % n).astype(F32) * (2.0 * math.pi / n)
        s = n ** -0.5
        return jnp.cos(a) * s, jnp.sin(a) * s

    cc, sc = dft(FOURIER_GROUP)
    cx, sx = dft(n_ctx)
    r = n_lat // GRID_W
    cr, sr = (t * r ** 0.5 for t in dft(r))
    c64, s64 = (t * GRID_W ** 0.5 * n_lat ** -0.5 for t in dft(GRID_W))
    k1 = jnp.arange(r, dtype=jnp.int32)[:, None]
    n2 = jnp.arange(GRID_W, dtype=jnp.int32)[None, :]
    tw = ((k1 * n2) % n_lat).astype(F32) * (2.0 * math.pi / n_lat)
    wide = lambda t: jnp.broadcast_to(t[:, :, None], (r, GRID_W, LANES))
    return {
        "cos_h": cos_h, "sin_h": sin_h, "cos_m": cos_m, "sin_m": sin_m, "seg": seg,
        "cc": cc.astype(BF16), "sc": sc.astype(BF16),
        "m1": jnp.concatenate([jnp.concatenate([cr, -sr], axis=1),
                               jnp.concatenate([-sr, -cr], axis=1)], axis=0).astype(BF16),
        "tw_cos": wide(jnp.cos(tw)), "tw_sin": wide(jnp.sin(tw)),
        "m3": jnp.concatenate([c64, s64], axis=1).astype(BF16),
        "w_ctx": jnp.concatenate([cx, -sx], axis=1).astype(BF16),
    }


_GQA_HEAD_ORDER = (0, 4, 1, 5, 2, 6, 3, 7)


def _layer_weights(l, w_in, gqa_q_norm, gqa_k_norm, mla_q_norm, mla_kv_norm, mla_w_q_up, mla_w_kv_up,
                   w_branch, w_gate, b_gate, w_out):
    wi = w_in[l]
    gq = wi[:, 2048:2560].reshape(D_MODEL, GQA_Q_HEADS, HEAD_DIM)[:, jnp.array(_GQA_HEAD_ORDER)]
    zeros = lambda n: jnp.zeros((D_MODEL, n), F32)
    w_in_p = jnp.concatenate([wi[:, :2048], gq.reshape(D_MODEL, 512), wi[:, 2560:3200],
                              zeros(MLA_NOPE), wi[:, 3200:3232], zeros(LANES - MLA_NOPE - MLA_ROPE)], axis=1)
    qu = mla_w_q_up[l].reshape(MLA_Q_RANK, MLA_HEADS, MLA_NOPE + MLA_ROPE)
    qu = jnp.pad(qu, ((0, 0), (0, 0), (0, LANES - MLA_NOPE - MLA_ROPE))).reshape(MLA_Q_RANK, MLA_HEADS * LANES)
    kvu = mla_w_kv_up[l].reshape(MLA_KV_RANK, MLA_HEADS, MLA_NOPE + MLA_V)
    ku = jnp.pad(kvu[:, :, :MLA_NOPE], ((0, 0), (0, 0), (0, LANES - MLA_NOPE))).reshape(MLA_KV_RANK, -1)
    vu = kvu[:, :, MLA_NOPE:].reshape(MLA_KV_RANK, MLA_HEADS * MLA_V)
    wb = w_branch[l]
    wb_gqa = wb[2].reshape(GQA_Q_HEADS, HEAD_DIM, D_MODEL)[jnp.array(_GQA_HEAD_ORDER)].reshape(BRANCH_W, D_MODEL)
    wb = jnp.stack([wb[0], wb[1], wb_gqa, wb[3]])
    return {
        "w_in": w_in_p.astype(BF16),
        "gq_norm": jnp.tile(gqa_q_norm[l], GQA_Q_HEADS).reshape(1, 512),
        "gk_norm": jnp.tile(gqa_k_norm[l], GQA_KV_HEADS).reshape(1, LANES),
        "mq_norm": mla_q_norm[l].reshape(1, MLA_Q_RANK),
        "mkv_norm": mla_kv_norm[l].reshape(1, MLA_KV_RANK),
        "w_q_up": qu.astype(BF16), "w_k_up": ku.astype(BF16), "w_v_up": vu.astype(BF16),
        "w_branch": wb.astype(BF16), "w_gate": w_gate[l].astype(BF16), "b_gate": b_gate[l],
        "w_out": w_out[l].astype(BF16),
    }


def kernel(x, c, ctx, c_ctx, w_mod, b_mod, mix_pre_norm, mix_post_norm, ffn_pre_norm, ffn_post_norm,
           w_in, diff_lambda, diff_subnorm, gqa_q_norm, gqa_k_norm, mla_q_norm, mla_kv_norm,
           mla_w_q_up, mla_w_kv_up, w_branch, w_gate, b_gate, w_out,
           dense_w_gate_up, dense_w_down, moe_router, moe_w_gate_up, moe_w_down):
    B, n_lat, d = x.shape
    n_ctx = ctx.shape[1]
    depth = w_mod.shape[0]
    assert d == D_MODEL and n_lat % TM == 0 and n_ctx % TM == 0 and n_lat % n_ctx == 0
    assert n_lat % GRID_W == 0
    T = n_lat + n_ctx
    n_lat_tiles, n_all_tiles = n_lat // TM, T // TM
    consts = _constants(n_lat, n_ctx)
    x_all = (x, ctx)
    c_rows = jnp.pad(jnp.concatenate([c, c_ctx[None]], axis=0), ((0, (-(B + 1)) % 8), (0, 0)))
    row = lambda v: v.reshape(1, -1)

    for layer in range(depth):
        last = layer == depth - 1
        n_tiles = n_lat_tiles if last else n_all_tiles
        lam_init = 0.8 - 0.6 * math.exp(-0.3 * layer)
        wts = _layer_weights(layer, w_in, gqa_q_norm, gqa_k_norm, mla_q_norm, mla_kv_norm, mla_w_q_up,
                             mla_w_kv_up, w_branch, w_gate, b_gate, w_out)
        mod_rows = _modulation(c_rows, w_mod[layer], b_mod[layer])
        mods = jnp.stack([mod_rows[:B].reshape(B, N_MOD, D_MODEL),
                          jnp.broadcast_to(mod_rows[B].reshape(1, N_MOD, D_MODEL), (B, N_MOD, D_MODEL))], axis=1)

        (h, dq, dk, dv, gq, gk, gv, mq, mk, mv, pq_lat, pq_ctx) = _project(
            x_all, mods, row(mix_pre_norm[layer]), wts, consts, n_lat_tiles)

        diff_extra = (diff_lambda[layer], diff_subnorm[layer].reshape(1, LANES))
        ys = []
        for mode, q, k, v in (("diff", dq, dk, dv), ("gqa", gq, gk, gv), ("mla", mq, mk, mv)):
            y = jnp.zeros((B, T, BRANCH_W), BF16)
            kw = dict(lam_init=lam_init, extra=diff_extra)
            tq = TQ if n_lat % TQ == 0 else TM
            y = _attention(mode, q, k, v, y, tq=tq, nq=n_lat // tq, q_blk0=0, tk=T, k_blk=0, **kw)
            if not last:
                y = _attention(mode, q, k, v, y, tq=TM, nq=n_ctx // TM, q_blk0=n_lat_tiles,
                               tk=n_ctx, k_blk=n_lat // n_ctx, **kw)
            ys.append(y)
        y_diff, y_gqa, y_mla = ys

        a_rows = _dft_rows(pq_lat, consts["m1"])
        y_f = _dft_cols(a_rows, consts["tw_cos"], consts["tw_sin"], consts["m3"]).reshape(B, n_lat, BRANCH_W)
        if not last:
            y_f = jnp.concatenate([y_f, _dft_ctx(pq_ctx, consts["w_ctx"])], axis=1)

        merge_args = (x_all, h, y_diff, y_f, y_gqa, y_mla, mods, wts, row(mix_post_norm[layer]),
                      n_lat_tiles, n_tiles)
        if layer % 2 == 0:
            x_all = _merge(*merge_args)
            x_all = _dense_ffn(x_all, mods, row(ffn_pre_norm[layer]), row(ffn_post_norm[layer]),
                               dense_w_gate_up[layer // 2].astype(BF16), dense_w_down[layer // 2].astype(BF16),
                               n_lat_tiles, n_tiles)
        else:
            wr = jnp.pad(moe_router[layer // 2], ((0, 0), (0, LANES - N_EXPERTS)))
            idx = jnp.arange(TM)
            tri = jnp.where(idx[:, None] > idx[None, :], 1.0, 0.0).astype(BF16)
            x_all, h_ffn, route, counts = _merge(*merge_args, router=(row(ffn_pre_norm[layer]), wr, tri))
            x_all = _routed_ffn(x_all, h_ffn, route, counts, mods, row(ffn_post_norm[layer]),
                                moe_w_gate_up[layer // 2].astype(BF16), moe_w_down[layer // 2].astype(BF16),
                                n_lat_tiles, n_tiles)
    return x_all[:, :n_lat]
```

```python
import functools
import math

import jax
import jax.numpy as jnp
from jax import lax
from jax.experimental import pallas as pl
from jax.experimental.pallas import tpu as pltpu

F32 = jnp.float32
BF16 = jnp.bfloat16

D_MODEL = 1024
GRID_W = 64
ROPE_BASE = 10000.0
EPS = 1e-6
N_MOD = 6
HEAD_DIM = 64
DIFF_HEADS = 4
GQA_Q_HEADS = 8
GQA_KV_HEADS = 2
MLA_HEADS = 8
MLA_Q_RANK = 256
MLA_KV_RANK = 128
MLA_NOPE = 64
MLA_ROPE = 32
MLA_V = 64
BRANCH_W = 512
FOURIER_GROUP = 128
FFN_DIM = 2816
N_EXPERTS = 8
EXPERT_DIM = 3584
LOG2E = math.log2(math.e)
QK_SCALE = HEAD_DIM ** -0.5 * LOG2E
MLA_SCALE = (MLA_NOPE + MLA_ROPE) ** -0.5 * LOG2E
IN_COLS_PADDED = 3328

LANES = 128
VMEM_LIMIT = 56 * 1024 * 1024

TM = 256
TQ = 1024


def _cparams(sem):
    return pltpu.CompilerParams(dimension_semantics=sem, vmem_limit_bytes=VMEM_LIMIT)


def _full_spec(shape):
    n = len(shape)
    return pl.BlockSpec(shape, lambda *_: (0,) * n)


def _split_bf16(x):
    hi = x.astype(BF16)
    lo = (x - hi.astype(F32)).astype(BF16)
    return hi, lo


def _dot(a, b):
    return jnp.dot(a, b, preferred_element_type=F32)


def _dot3(a, w):
    ah, al = _split_bf16(a)
    wh, wl = _split_bf16(w)
    return _dot(ah, wh) + _dot(ah, wl) + _dot(al, wh)


def _rms(x, gain):
    ms = jnp.mean(x * x, axis=-1, keepdims=True)
    return x * lax.rsqrt(ms + EPS) * gain


def _mod_kernel(c_ref, w_ref, b_ref, o_ref):
    c = c_ref[...]
    a = c * jax.nn.sigmoid(c)
    o_ref[...] = _dot3(a, w_ref[...]) + b_ref[...]


def _modulation(c_rows, w_mod_l, b_mod_l):
    rows = c_rows.shape[0]
    n = w_mod_l.shape[1]
    tn = 512
    return pl.pallas_call(
        _mod_kernel,
        grid=(n // tn,),
        in_specs=[pl.BlockSpec((rows, D_MODEL), lambda j: (0, 0)),
                  pl.BlockSpec((D_MODEL, tn), lambda j: (0, j)),
                  pl.BlockSpec((1, tn), lambda j: (0, j))],
        out_specs=pl.BlockSpec((rows, tn), lambda j: (0, j)),
        out_shape=jax.ShapeDtypeStruct((rows, n), F32),
        compiler_params=_cparams(("arbitrary",)),
        name="modulation",
    )(c_rows, w_mod_l, b_mod_l.reshape(1, n))


def _rope(x, cos, sin_signed, half):
    lane = lax.broadcasted_iota(jnp.int32, x.shape, 1)
    first = (lane % (2 * half)) < half
    up = pltpu.roll(x, LANES - half, 1)
    down = pltpu.roll(x, half, 1)
    return x * cos + jnp.where(first, up, down) * sin_signed


def _stream_specs(xs, n_lat_tiles):
    if not isinstance(xs, tuple):
        return [pl.BlockSpec((None, TM, D_MODEL), lambda b, t: (b, t, 0))], [xs]
    return ([pl.BlockSpec((None, TM, D_MODEL), lambda b, t: (b, jnp.minimum(t, n_lat_tiles - 1), 0)),
             pl.BlockSpec((None, TM, D_MODEL), lambda b, t: (b, jnp.maximum(t - n_lat_tiles, 0), 0))], list(xs))


def _stream_tile(x_refs, n_lat_tiles):
    if len(x_refs) == 1:
        return x_refs[0][...]
    return jnp.where(pl.program_id(1) < n_lat_tiles, x_refs[0][...], x_refs[1][...])


def _stream_dims(xs):
    if isinstance(xs, tuple):
        return xs[0].shape[0], xs[0].shape[1] + xs[1].shape[1]
    return xs.shape[0], xs.shape[1]


def _proj_kernel(n_lat_tiles, n_x, *refs):
    x_refs, refs = refs[:n_x], refs[n_x:]
    (mod_ref, gpre_ref, win_ref, gqn_ref, gkn_ref, mqn_ref, mkvn_ref,
     wqup_ref, wkup_ref, wvup_ref, seg_ref, cc_ref, sc_ref,
     cosh_ref, sinh_ref, cosm_ref, sinm_ref,
     h_ref, dq_ref, dk_ref, dv_ref, gq_ref, gk_ref, gv_ref, mq_ref, mk_ref, mv_ref, pq_ref, pqc_ref) = refs
    x = _stream_tile(x_refs, n_lat_tiles)
    mod = mod_ref[...]
    h = _rms(x, gpre_ref[...]) * (1.0 + mod[1:2]) + mod[0:1]
    hb = h.astype(BF16)
    h_ref[...] = hb
    z = _dot(hb, win_ref[...])

    def zcols(lo, hi):
        return z[:, lo:hi]
    cosh, sinh = cosh_ref[...], sinh_ref[...]
    cosm, sinm = cosm_ref[...], sinm_ref[...]
    seg = seg_ref[...]

    def head_rms(v, gain, w):
        sq = v * v
        hi, lo = _split_bf16(sq)
        ms = _dot(hi, seg[:w, :w]) + _dot(lo, seg[:w, :w])
        return v * lax.rsqrt(ms + EPS) * gain

    for j in range(4):
        c0 = j * LANES
        dq_ref[:, c0:c0 + LANES] = (_rope(zcols(c0, c0 + LANES), cosh, sinh, 16) * QK_SCALE).astype(BF16)
        dk_ref[:, c0:c0 + LANES] = _rope(zcols(512 + c0, 512 + c0 + LANES), cosh, sinh, 16).astype(BF16)
    dv_ref[...] = zcols(1024, 1536).astype(BF16)
    fparts = []
    for g in range(4):
        c0 = 1536 + g * LANES
        fb = zcols(c0, c0 + LANES).astype(BF16)
        fparts.append((_dot(fb, cc_ref[...]).astype(BF16), _dot(fb, sc_ref[...]).astype(BF16)))
    is_lat = pl.program_id(1) < n_lat_tiles

    def put(ref):
        for g, (pc, ps) in enumerate(fparts):
            ref[0, :, g * LANES:(g + 1) * LANES] = pc
            ref[1, :, g * LANES:(g + 1) * LANES] = ps

    pl.when(is_lat)(lambda: put(pq_ref))
    pl.when(jnp.logical_not(is_lat))(lambda: put(pqc_ref))
    gq = head_rms(zcols(2048, 2560), gqn_ref[...], 512)
    for j in range(4):
        c0 = j * LANES
        gq_ref[:, c0:c0 + LANES] = (_rope(gq[:, c0:c0 + LANES], cosh, sinh, 16) * QK_SCALE).astype(BF16)
    gk = head_rms(zcols(2560, 2688), gkn_ref[...], LANES)
    gk_ref[...] = _rope(gk, cosh, sinh, 16).astype(BF16)
    gv_ref[...] = zcols(2688, 2816).astype(BF16)
    cq = _rms(zcols(2816, 3072), mqn_ref[...]).astype(BF16)
    mq = _dot(cq, wqup_ref[...])
    ckv = _rms(zcols(3072, 3200), mkvn_ref[...]).astype(BF16)
    mk = _dot(ckv, wkup_ref[...])
    mv_ref[...] = _dot(ckv, wvup_ref[...]).astype(BF16)
    kr = _rope(zcols(3200, 3328), cosm, sinm, 8)
    for hh in range(MLA_HEADS):
        c0 = hh * LANES
        mq_ref[:, c0:c0 + LANES] = (_rope(mq[:, c0:c0 + LANES], cosm, sinm, 8) * MLA_SCALE).astype(BF16)
        mk_ref[:, c0:c0 + LANES] = (mk[:, c0:c0 + LANES] + kr).astype(BF16)


def _project(x_all, mods, gpre, wts, consts, n_lat_tiles):
    B, T = _stream_dims(x_all)
    nt = T // TM
    tok = lambda w: pl.BlockSpec((None, TM, w), lambda b, t: (b, t, 0))
    tab = pl.BlockSpec((TM, LANES), lambda b, t: (t, 0))
    x_specs, x_args = _stream_specs(x_all, n_lat_tiles)
    in_specs = x_specs + [
        pl.BlockSpec((None, None, N_MOD, D_MODEL), lambda b, t: (b, jnp.where(t >= n_lat_tiles, 1, 0), 0, 0)),
        _full_spec((1, D_MODEL)),
        _full_spec((D_MODEL, IN_COLS_PADDED)),
        _full_spec((1, 512)), _full_spec((1, LANES)), _full_spec((1, MLA_Q_RANK)), _full_spec((1, MLA_KV_RANK)),
        _full_spec((MLA_Q_RANK, MLA_HEADS * LANES)),
        _full_spec((MLA_KV_RANK, MLA_HEADS * LANES)),
        _full_spec((MLA_KV_RANK, MLA_HEADS * MLA_V)),
        _full_spec((512, 512)), _full_spec((LANES, LANES)), _full_spec((LANES, LANES)),
        tab, tab, tab, tab,
    ]
    widths = [D_MODEL, 512, 512, 512, 512, LANES, LANES, MLA_HEADS * LANES, MLA_HEADS * LANES, 512]
    out_specs = [tok(w) for w in widths]
    out_shape = [jax.ShapeDtypeStruct((B, T, w), BF16) for w in widths]
    out_specs.append(pl.BlockSpec((2, None, TM, 512), lambda b, t: (0, b, jnp.minimum(t, n_lat_tiles - 1), 0)))
    out_shape.append(jax.ShapeDtypeStruct((2, B, n_lat_tiles * TM, 512), BF16))
    out_specs.append(pl.BlockSpec((2, None, TM, 512), lambda b, t: (0, b, jnp.maximum(t - n_lat_tiles, 0), 0)))
    out_shape.append(jax.ShapeDtypeStruct((2, B, T - n_lat_tiles * TM, 512), BF16))
    return pl.pallas_call(
        functools.partial(_proj_kernel, n_lat_tiles, len(x_args)),
        grid=(B, nt),
        in_specs=in_specs,
        out_specs=out_specs,
        out_shape=out_shape,
        compiler_params=_cparams(("arbitrary", "arbitrary")),
        name="project",
    )(*x_args, mods, gpre, wts["w_in"], wts["gq_norm"], wts["gk_norm"], wts["mq_norm"], wts["mkv_norm"],
      wts["w_q_up"], wts["w_k_up"], wts["w_v_up"], consts["seg"], consts["cc"], consts["sc"],
      consts["cos_h"], consts["sin_h"], consts["cos_m"], consts["sin_m"])


N_MAPS = 8
MAPS_PER_STEP = 2


def _map_plan(mode, m):
    if mode == "diff":
        return m // 2, m % 2, m // 2, m // 2
    if mode == "gqa":
        return m // 2, m % 2, 0, 0
    return m, None, m, m // 2


def _attn_kernel(mode, lam_init, q_ref, k_ref, v_ref, *rest):
    if mode == "diff":
        lam_ref, sub_ref, _, o_ref, vx_ref = rest
    else:
        _, o_ref, vx_ref = rest
    nt = (((1,), (1,)), ((), ()))
    lane = lax.broadcasted_iota(jnp.int32, (1, LANES), 1)
    low = lane < HEAD_DIM

    @pl.when(pl.program_id(2) == 0)
    def _():
        for g in range(vx_ref.shape[0]):
            vx_ref[g, :, :LANES] = v_ref[:, g * LANES:(g + 1) * LANES]
            vx_ref[g, :, LANES:] = jnp.ones((vx_ref.shape[1], LANES), BF16)

    def scores(m):
        qblk, half, kblk, _ = _map_plan(mode, m)
        q = q_ref[:, qblk * LANES:(qblk + 1) * LANES]
        if half is not None:
            q = jnp.where(low if half == 0 else jnp.logical_not(low), q, jnp.zeros_like(q))
        return lax.dot_general(q, k_ref[:, kblk * LANES:(kblk + 1) * LANES], nt, preferred_element_type=F32)

    def numerator(s):
        return jnp.exp2(s - jnp.max(s, axis=-1, keepdims=True)).astype(BF16)

    def values(m, p):
        e = _dot(p, vx_ref[_map_plan(mode, m)[3]])
        return e[:, :LANES] / e[:, LANES:]

    s, pl_, y = {}, {}, {}
    for t in range(MAPS_PER_STEP + 2):
        if t < MAPS_PER_STEP:
            s[t] = scores(t)
        if 0 <= t - 1 < MAPS_PER_STEP:
            pl_[t - 1] = numerator(s.pop(t - 1))
        if 0 <= t - 2:
            y[t - 2] = values(t - 2, pl_.pop(t - 2))

    if mode == "diff":
        dl = lam_ref[...]
        lam = (jnp.exp(jnp.sum(dl[0:1] * dl[1:2], axis=-1, keepdims=True))
               - jnp.exp(jnp.sum(dl[2:3] * dl[3:4], axis=-1, keepdims=True)) + lam_init)
    for j in range(MAPS_PER_STEP // 2):
        if mode == "diff":
            out = _rms(y[2 * j] - lam * y[2 * j + 1], sub_ref[...]) * (1.0 - lam_init)
        else:
            out = jnp.where(low, y[2 * j], y[2 * j + 1])
        o_ref[:, j * LANES:(j + 1) * LANES] = out.astype(BF16)


def _attention(mode, q, k, v, y_prev, *, tq, nq, q_blk0, tk, k_blk, lam_init=0.0, extra=()):
    B, T, _ = q.shape
    n_steps = N_MAPS // MAPS_PER_STEP
    wq, wk, wv = q.shape[2] // n_steps, max(k.shape[2] // n_steps, LANES), max(v.shape[2] // n_steps, LANES)
    kv_blk = (lambda g: 0) if mode == "gqa" else (lambda g: g)
    in_specs = [
        pl.BlockSpec((None, tq, wq), lambda b, g, i: (b, i + q_blk0, g)),
        pl.BlockSpec((None, tk, wk), lambda b, g, i: (b, k_blk, kv_blk(g))),
        pl.BlockSpec((None, tk, wv), lambda b, g, i: (b, k_blk, kv_blk(g))),
    ]
    args = [q, k, v]
    if mode == "diff":
        in_specs += [_full_spec((4, HEAD_DIM)), _full_spec((1, LANES))]
        args += list(extra)
    in_specs.append(pl.BlockSpec(memory_space=pl.ANY))
    args.append(y_prev)
    wo = BRANCH_W // n_steps
    return pl.pallas_call(
        functools.partial(_attn_kernel, mode, lam_init),
        grid=(B, n_steps, nq),
        in_specs=in_specs,
        out_specs=pl.BlockSpec((None, tq, wo), lambda b, g, i: (b, i + q_blk0, g)),
        out_shape=jax.ShapeDtypeStruct((B, T, BRANCH_W), BF16),
        input_output_aliases={len(args) - 1: 0},
        scratch_shapes=[pltpu.VMEM((wv // LANES, tk, 2 * LANES), BF16)],
        compiler_params=_cparams(("arbitrary", "arbitrary", "arbitrary")),
        name="attention_" + mode,
    )(*args)


DFT_COLS_PER_STEP = 8


def _dft_rows_kernel(pq_ref, m1_ref, a_ref):
    r = pq_ref.shape[1]
    for n2 in range(pq_ref.shape[2]):
        zin = jnp.concatenate([pq_ref[0, :, n2, :], pq_ref[1, :, n2, :]], axis=0)
        a = _dot(m1_ref[...], zin).astype(BF16)
        a_ref[0, :, n2, :] = a[:r]
        a_ref[1, :, n2, :] = a[r:]


def _dft_rows(pq_lat, m1):
    _, B, n_lat, w = pq_lat.shape
    r = n_lat // GRID_W
    nb = DFT_COLS_PER_STEP
    return pl.pallas_call(
        _dft_rows_kernel,
        grid=(B, GRID_W // nb),
        in_specs=[pl.BlockSpec((2, None, r, nb, w), lambda b, j: (0, b, 0, j, 0)), _full_spec((2 * r, 2 * r))],
        out_specs=pl.BlockSpec((None, 2, r, nb, w), lambda b, j: (b, 0, 0, j, 0)),
        out_shape=jax.ShapeDtypeStruct((B, 2, r, GRID_W, w), BF16),
        compiler_params=_cparams(("arbitrary", "arbitrary")),
        name="dft_rows",
    )(pq_lat.reshape(2, B, r, GRID_W, w), m1)


def _dft_cols_kernel(a_ref, ct_ref, st_ref, m3_ref, o_ref):
    kb = a_ref.shape[1]
    w = a_ref.shape[3]
    for j in range(kb):
        ar = a_ref[0, j].astype(F32)
        ai = a_ref[1, j].astype(F32)
        ct = jnp.concatenate([ct_ref[j]] * (w // LANES), axis=1)
        st = jnp.concatenate([st_ref[j]] * (w // LANES), axis=1)
        b = jnp.concatenate([ar * ct + ai * st, ai * ct - ar * st], axis=0).astype(BF16)
        o_ref[:, j, :] = _dot(m3_ref[...], b).astype(BF16)


def _dft_cols(a, ct, st, m3):
    B, _, r, _, w = a.shape
    kb = DFT_COLS_PER_STEP
    return pl.pallas_call(
        _dft_cols_kernel,
        grid=(B, r // kb),
        in_specs=[pl.BlockSpec((None, 2, kb, GRID_W, w), lambda b, k: (b, 0, k, 0, 0)),
                  pl.BlockSpec((kb, GRID_W, LANES), lambda b, k: (k, 0, 0)),
                  pl.BlockSpec((kb, GRID_W, LANES), lambda b, k: (k, 0, 0)),
                  _full_spec((GRID_W, 2 * GRID_W))],
        out_specs=pl.BlockSpec((None, GRID_W, kb, w), lambda b, k: (b, 0, k, 0)),
        out_shape=jax.ShapeDtypeStruct((B, GRID_W, r, w), BF16),
        compiler_params=_cparams(("arbitrary", "arbitrary")),
        name="dft_cols",
    )(a, ct, st, m3)


def _dft_ctx_kernel(pq_ref, w_ref, o_ref):
    zin = jnp.concatenate([pq_ref[0], pq_ref[1]], axis=0)
    o_ref[...] = _dot(w_ref[...], zin).astype(BF16)


def _dft_ctx(pq_ctx, w_ctx):
    _, B, n_ctx, w = pq_ctx.shape
    return pl.pallas_call(
        _dft_ctx_kernel,
        grid=(B,),
        in_specs=[pl.BlockSpec((2, None, n_ctx, w), lambda b: (0, b, 0, 0)), _full_spec((n_ctx, 2 * n_ctx))],
        out_specs=pl.BlockSpec((None, n_ctx, w), lambda b: (b, 0, 0)),
        out_shape=jax.ShapeDtypeStruct((B, n_ctx, w), BF16),
        compiler_params=_cparams(("arbitrary",)),
        name="dft_ctx",
    )(pq_ctx, w_ctx)


def _merge_kernel(n_lat_tiles, n_x, *refs):
    x_refs, refs = refs[:n_x], refs[n_x:]
    h_ref, y0_ref, y1_ref, y2_ref, y3_ref, mod_ref, wg_ref, bg_ref, wb_ref, wo_ref, gpost_ref = refs[:11]
    rest = refs[11:]
    h = h_ref[...]
    ys = (y0_ref, y1_ref, y2_ref, y3_ref)
    acc = None
    for i in range(4):
        gate = jax.nn.sigmoid(_dot(h, wg_ref[i]) + bg_ref[i:i + 1])
        term = gate * _dot(ys[i][...], wb_ref[i])
        acc = term if acc is None else acc + term
    m = _dot(acc.astype(BF16), wo_ref[...])
    mod = mod_ref[...]
    x = _stream_tile(x_refs, n_lat_tiles) + mod[2:3] * _rms(m, gpost_ref[...])
    o_ref = rest[3] if len(rest) > 1 else rest[0]
    o_ref[...] = x
    if len(rest) > 1:
        gffn_ref, wr_ref, tri_ref, _, hf_ref, route_ref, cnt_ref, carry_ref = rest
        _route_tile(x, mod, gffn_ref, wr_ref, tri_ref, hf_ref, route_ref, cnt_ref, carry_ref)


def _merge(x_all, h, y_diff, y_f, y_gqa, y_mla, mods, wts, gpost, n_lat_tiles, n_tiles, router=None):
    B, _ = _stream_dims(x_all)
    tok = lambda w: pl.BlockSpec((None, TM, w), lambda b, t: (b, t, 0))
    x_specs, x_args = _stream_specs(x_all, n_lat_tiles)
    in_specs = x_specs + [
        tok(D_MODEL), tok(BRANCH_W), tok(BRANCH_W), tok(BRANCH_W), tok(BRANCH_W),
        pl.BlockSpec((None, None, N_MOD, D_MODEL), lambda b, t: (b, jnp.where(t >= n_lat_tiles, 1, 0), 0, 0)),
        _full_spec((4, D_MODEL, D_MODEL)), _full_spec((4, D_MODEL)), _full_spec((4, BRANCH_W, D_MODEL)),
        _full_spec((D_MODEL, D_MODEL)), _full_spec((1, D_MODEL)),
    ]
    args = x_args + [h, y_diff, y_f, y_gqa, y_mla, mods, wts["w_gate"], wts["b_gate"], wts["w_branch"],
                     wts["w_out"], gpost]
    out_specs = [tok(D_MODEL)]
    out_shape = [jax.ShapeDtypeStruct((B, n_tiles * TM, D_MODEL), F32)]
    scratch = []
    if router is not None:
        in_specs += [_full_spec((1, D_MODEL)), _full_spec((D_MODEL, LANES)), _full_spec((TM, TM))]
        args += list(router)
        out_specs += [tok(D_MODEL), tok(LANES), _full_spec((1, LANES))]
        out_shape += [jax.ShapeDtypeStruct((B, n_tiles * TM, D_MODEL), F32),
                      jax.ShapeDtypeStruct((B, n_tiles * TM, LANES), F32),
                      jax.ShapeDtypeStruct((1, LANES), F32)]
        scratch = [pltpu.VMEM((1, LANES), F32)]
    out = pl.pallas_call(
        functools.partial(_merge_kernel, n_lat_tiles, len(x_args)),
        grid=(B, n_tiles),
        in_specs=in_specs,
        out_specs=out_specs,
        out_shape=out_shape,
        scratch_shapes=scratch,
        compiler_params=_cparams(("arbitrary", "arbitrary")),
        name="merge",
    )(*args)
    return out[0] if router is None else out


def _dense_ffn_kernel(nf, tf, x_ref, mod_ref, gpre_ref, gpost_ref, wgu_ref, wd_ref, o_ref):
    x = x_ref[...]
    mod = mod_ref[...]
    hb = (_rms(x, gpre_ref[...]) * (1.0 + mod[4:5]) + mod[3:4]).astype(BF16)
    acc = None
    for f in range(nf):
        a = _dot(hb, wgu_ref[:, f * tf:(f + 1) * tf])
        u = _dot(hb, wgu_ref[:, FFN_DIM + f * tf:FFN_DIM + (f + 1) * tf])
        act = (a * jax.nn.sigmoid(a) * u).astype(BF16)
        part = _dot(act, wd_ref[f * tf:(f + 1) * tf, :])
        acc = part if acc is None else acc + part
    o_ref[...] = x + mod[5:6] * _rms(acc, gpost_ref[...])


def _dense_ffn(x_all, mods, gpre, gpost, wgu, wd, n_lat_tiles, n_tiles):
    B = x_all.shape[0]
    nf, tf = 1, FFN_DIM
    tok = pl.BlockSpec((None, TM, D_MODEL), lambda b, t: (b, t, 0))
    return pl.pallas_call(
        functools.partial(_dense_ffn_kernel, nf, tf),
        grid=(B, n_tiles),
        in_specs=[tok,
                  pl.BlockSpec((None, None, N_MOD, D_MODEL),
                               lambda b, t: (b, jnp.where(t >= n_lat_tiles, 1, 0), 0, 0)),
                  _full_spec((1, D_MODEL)), _full_spec((1, D_MODEL)),
                  _full_spec((D_MODEL, 2 * FFN_DIM)), _full_spec((FFN_DIM, D_MODEL))],
        out_specs=tok,
        out_shape=jax.ShapeDtypeStruct((B, n_tiles * TM, D_MODEL), F32),
        compiler_params=_cparams(("arbitrary", "arbitrary")),
        name="dense_ffn",
    )(x_all, mods, gpre, gpost, wgu, wd)


TE = 512
R_E1, R_E2, R_G1, R_G2, R_RANK1, R_RANK2 = range(6)


def _route_tile(x, mod, gpre_ref, wr_ref, tri_ref, h_ref, route_ref, cnt_ref, carry_ref):
    first = jnp.logical_and(pl.program_id(0) == 0, pl.program_id(1) == 0)

    @pl.when(first)
    def _():
        carry_ref[...] = jnp.zeros_like(carry_ref)

    h = _rms(x, gpre_ref[...]) * (1.0 + mod[4:5]) + mod[3:4]
    h_ref[...] = h
    logits = _dot3(h, wr_ref[...])
    lane = lax.broadcasted_iota(jnp.int32, logits.shape, 1)
    neg = jnp.float32(-jnp.inf)
    l1 = jnp.where(lane < N_EXPERTS, logits, neg)
    m1 = jnp.max(l1, axis=-1, keepdims=True)
    i1 = jnp.min(jnp.where(l1 == m1, lane, LANES), axis=-1, keepdims=True)
    l2 = jnp.where(lane == i1, neg, l1)
    m2 = jnp.max(l2, axis=-1, keepdims=True)
    i2 = jnp.min(jnp.where(l2 == m2, lane, LANES), axis=-1, keepdims=True)
    e2 = jnp.exp(m2 - m1)
    g1 = 1.0 / (1.0 + e2)
    g2 = e2 / (1.0 + e2)
    sel1, sel2 = lane == i1, lane == i2
    chosen = jnp.where(jnp.logical_or(sel1, sel2), 1.0, 0.0)
    before = _dot(tri_ref[...], chosen.astype(BF16)) + carry_ref[...]
    r1 = jnp.sum(jnp.where(sel1, before, 0.0), axis=-1, keepdims=True)
    r2 = jnp.sum(jnp.where(sel2, before, 0.0), axis=-1, keepdims=True)
    carry_ref[...] += jnp.sum(chosen, axis=0, keepdims=True)
    cnt_ref[...] = carry_ref[...]
    rec = jnp.zeros_like(logits)
    for col, val in ((R_E1, i1.astype(F32)), (R_E2, i2.astype(F32)), (R_G1, g1), (R_G2, g2),
                     (R_RANK1, r1), (R_RANK2, r2)):
        rec = jnp.where(lane == col, val, rec)
    route_ref[...] = rec


def _row_copies(n_rows, copy_of, slot_wait):
    def start(r, carry):
        copy_of(r, 0).start()
        copy_of(r, 1).start()
        return carry

    lax.fori_loop(0, n_rows, start, 0, unroll=8)
    slot_wait(0).wait()
    slot_wait(1).wait()


def _dispatch_kernel(pos_ref, h_ref, _, xs_ref, sem):
    def copy_of(r, slot):
        p = pos_ref[0, 2 * r + slot]
        return pltpu.make_async_copy(h_ref.at[pl.ds(r, 1)], xs_ref.at[pl.ds(p, 1)], sem)

    n = h_ref.shape[0]
    _row_copies(n, copy_of, lambda slot: pltpu.make_async_copy(h_ref, xs_ref.at[pl.ds(0, n)], sem))


def _dispatch(h, pos, n_rows):
    n = h.shape[0]
    xs0 = jnp.zeros((n_rows, D_MODEL), F32)
    return pl.pallas_call(
        _dispatch_kernel,
        grid=(n // TM,),
        in_specs=[pl.BlockSpec((None, 1, 2 * TM), lambda i: (i, 0, 0), memory_space=pltpu.SMEM),
                  pl.BlockSpec((TM, D_MODEL), lambda i: (i, 0)),
                  pl.BlockSpec(memory_space=pl.ANY)],
        out_specs=pl.BlockSpec(memory_space=pl.ANY),
        out_shape=jax.ShapeDtypeStruct((n_rows, D_MODEL), F32),
        input_output_aliases={2: 0},
        scratch_shapes=[pltpu.SemaphoreType.DMA],
        compiler_params=_cparams(("arbitrary",)),
        name="moe_dispatch",
    )(pos.reshape(n // TM, 1, 2 * TM), h, xs0)


def _experts_kernel(te_ref, nv_ref, xs_ref, wa_ref, wu_ref, wd_ref, ys_ref):
    del te_ref

    @pl.when(pl.program_id(0) < nv_ref[0])
    def _():
        x = xs_ref[...].astype(BF16)
        a = _dot(x, wa_ref[...])
        u = _dot(x, wu_ref[...])
        ys_ref[...] = _dot((a * jax.nn.sigmoid(a) * u).astype(BF16), wd_ref[...])


def _experts(xs, tile_expert, n_valid, wgu, wd):
    n_rows = xs.shape[0]
    row_blk = lambda i, te, nv: (jnp.minimum(i, nv[0] - 1), 0)
    resident = pl.Buffered(1)
    grid_spec = pltpu.PrefetchScalarGridSpec(
        num_scalar_prefetch=2,
        grid=(n_rows // TE,),
        in_specs=[pl.BlockSpec((TE, D_MODEL), row_blk),
                  pl.BlockSpec((None, D_MODEL, EXPERT_DIM), lambda i, te, nv: (te[i], 0, 0), pipeline_mode=resident),
                  pl.BlockSpec((None, D_MODEL, EXPERT_DIM), lambda i, te, nv: (te[i], 0, 1), pipeline_mode=resident),
                  pl.BlockSpec((None, EXPERT_DIM, D_MODEL), lambda i, te, nv: (te[i], 0, 0), pipeline_mode=resident)],
        out_specs=pl.BlockSpec((TE, D_MODEL), row_blk),
    )
    return pl.pallas_call(
        _experts_kernel,
        grid_spec=grid_spec,
        out_shape=jax.ShapeDtypeStruct((n_rows, D_MODEL), F32),
        compiler_params=_cparams(("arbitrary",)),
        name="moe_experts",
    )(tile_expert, n_valid, xs, wgu, wgu, wd)


def _combine_kernel(pos_ref, x_ref, route_ref, mod_ref, gpost_ref, ys_ref, o_ref, buf_ref, sem):
    def copy_of(r, slot):
        p = pos_ref[0, 2 * r + slot]
        return pltpu.make_async_copy(ys_ref.at[pl.ds(p, 1)], buf_ref.at[slot, pl.ds(r, 1)], sem)

    n = x_ref.shape[0]
    _row_copies(n, copy_of, lambda slot: pltpu.make_async_copy(ys_ref.at[pl.ds(0, n)], buf_ref.at[slot], sem))
    route = route_ref[...]
    y = route[:, R_G1:R_G1 + 1] * buf_ref[0] + route[:, R_G2:R_G2 + 1] * buf_ref[1]
    mod = mod_ref[...]
    o_ref[...] = x_ref[...] + mod[5:6] * _rms(y, gpost_ref[...])


def _combine(x_all, route, pos, ys, mods, gpost, n_lat_tiles, n_tiles):
    B = x_all.shape[0]
    tok = lambda w: pl.BlockSpec((None, TM, w), lambda b, t: (b, t, 0))
    return pl.pallas_call(
        _combine_kernel,
        grid=(B, n_tiles),
        in_specs=[pl.BlockSpec((None, None, 1, 2 * TM), lambda b, t: (b, t, 0, 0), memory_space=pltpu.SMEM),
                  tok(D_MODEL), tok(LANES),
                  pl.BlockSpec((None, None, N_MOD, D_MODEL),
                               lambda b, t: (b, jnp.where(t >= n_lat_tiles, 1, 0), 0, 0)),
                  _full_spec((1, D_MODEL)),
                  pl.BlockSpec(memory_space=pl.ANY)],
        out_specs=tok(D_MODEL),
        out_shape=jax.ShapeDtypeStruct((B, n_tiles * TM, D_MODEL), F32),
        scratch_shapes=[pltpu.VMEM((2, TM, D_MODEL), F32), pltpu.SemaphoreType.DMA],
        compiler_params=_cparams(("arbitrary", "arbitrary")),
        name="moe_combine",
    )(pos.reshape(B, n_tiles, 1, 2 * TM), x_all, route, mods, gpost, ys)


def _routed_ffn(x_all, h, route, counts, mods, gpost, wgu, wd, n_lat_tiles, n_tiles):
    B = x_all.shape[0]
    n_tok = B * n_tiles * TM
    rec = route.reshape(n_tok, LANES)
    expert = rec[:, R_E1:R_E2 + 1].astype(jnp.int32)
    rank = rec[:, R_RANK1:R_RANK2 + 1].astype(jnp.int32)
    tiles_per_expert = (counts[0, :N_EXPERTS].astype(jnp.int32) + TE - 1) // TE
    tile_end = jnp.cumsum(tiles_per_expert)
    n_valid = tile_end[-1:]
    pos = (tile_end - tiles_per_expert)[expert] * TE + rank
    n_tiles_max = (2 * n_tok) // TE + N_EXPERTS
    tile_id = jnp.minimum(jnp.arange(n_tiles_max, dtype=jnp.int32), n_valid - 1)
    tile_expert = jnp.minimum(jnp.searchsorted(tile_end, tile_id, side="right"), N_EXPERTS - 1).astype(jnp.int32)
    xs = _dispatch(h.reshape(n_tok, D_MODEL), pos, n_tiles_max * TE)
    ys = _experts(xs, tile_expert, n_valid.astype(jnp.int32), wgu, wd)
    return _combine(x_all, route, pos, ys, mods, gpost, n_lat_tiles, n_tiles)


def _rope_angles(n_lat, rot_dim):
    rows = n_lat // GRID_W
    row = jnp.repeat(jnp.arange(rows, dtype=F32), GRID_W)
    col = jnp.tile(jnp.arange(GRID_W, dtype=F32), rows)
    axis_dim = rot_dim // 2
    inv_freq = 1.0 / (ROPE_BASE ** (jnp.arange(0, axis_dim, 2, dtype=F32) / axis_dim))
    ar = row[:, None] * inv_freq
    ac = col[:, None] * inv_freq
    return jnp.concatenate([ar, ar, ac, ac], axis=-1)


def _constants(n_lat, n_ctx):
    lane = jnp.arange(LANES)
    ang = _rope_angles(n_lat, HEAD_DIM)
    cos_h = jnp.tile(jnp.cos(ang), (1, 2))
    sin_h = jnp.tile(jnp.sin(ang), (1, 2)) * jnp.where((lane % 32) < 16, -1.0, 1.0)
    cos_h = jnp.concatenate([cos_h, jnp.ones((n_ctx, LANES), F32)], axis=0)
    sin_h = jnp.concatenate([sin_h, jnp.zeros((n_ctx, LANES), F32)], axis=0)
    angm = _rope_angles(n_lat, MLA_ROPE)
    pad_l = jnp.zeros((n_lat, MLA_NOPE), F32)
    pad_r = jnp.zeros((n_lat, LANES - MLA_NOPE - MLA_ROPE), F32)
    cos_m = jnp.concatenate([pad_l + 1.0, jnp.cos(angm), pad_r + 1.0], axis=1)
    sin_m = jnp.concatenate([pad_l, jnp.sin(angm), pad_r], axis=1) * jnp.where((lane % 16) < 8, -1.0, 1.0)
    cos_m = jnp.concatenate([cos_m, jnp.ones((n_ctx, LANES), F32)], axis=0)
    sin_m = jnp.concatenate([sin_m, jnp.zeros((n_ctx, LANES), F32)], axis=0)
    idx = jnp.arange(512)
    seg = jnp.where((idx[:, None] // HEAD_DIM) == (idx[None, :] // HEAD_DIM), 1.0 / HEAD_DIM, 0.0).astype(BF16)

    def dft(n):
        i = jnp.arange(n, dtype=jnp.int32)
        a = ((i[:, None] * i[None, :]) % n).astype(F32) * (2.0 * math.pi / n)
        s = n ** -0.5
        return jnp.cos(a) * s, jnp.sin(a) * s

    cc, sc = dft(FOURIER_GROUP)
    cx, sx = dft(n_ctx)
    r = n_lat // GRID_W
    cr, sr = (t * r ** 0.5 for t in dft(r))
    c64, s64 = (t * GRID_W ** 0.5 * n_lat ** -0.5 for t in dft(GRID_W))
    k1 = jnp.arange(r, dtype=jnp.int32)[:, None]
    n2 = jnp.arange(GRID_W, dtype=jnp.int32)[None, :]
    tw = ((k1 * n2) % n_lat).astype(F32) * (2.0 * math.pi / n_lat)
    wide = lambda t: jnp.broadcast_to(t[:, :, None], (r, GRID_W, LANES))
    return {
        "cos_h": cos_h, "sin_h": sin_h, "cos_m": cos_m, "sin_m": sin_m, "seg": seg,
        "cc": cc.astype(BF16), "sc": sc.astype(BF16),
        "m1": jnp.concatenate([jnp.concatenate([cr, -sr], axis=1),
                               jnp.concatenate([-sr, -cr], axis=1)], axis=0).astype(BF16),
        "tw_cos": wide(jnp.cos(tw)), "tw_sin": wide(jnp.sin(tw)),
        "m3": jnp.concatenate([c64, s64], axis=1).astype(BF16),
        "w_ctx": jnp.concatenate([cx, -sx], axis=1).astype(BF16),
    }


_GQA_HEAD_ORDER = (0, 4, 1, 5, 2, 6, 3, 7)


def _layer_weights(l, w_in, gqa_q_norm, gqa_k_norm, mla_q_norm, mla_kv_norm, mla_w_q_up, mla_w_kv_up,
                   w_branch, w_gate, b_gate, w_out):
    wi = w_in[l]
    gq = wi[:, 2048:2560].reshape(D_MODEL, GQA_Q_HEADS, HEAD_DIM)[:, jnp.array(_GQA_HEAD_ORDER)]
    zeros = lambda n: jnp.zeros((D_MODEL, n), F32)
    w_in_p = jnp.concatenate([wi[:, :2048], gq.reshape(D_MODEL, 512), wi[:, 2560:3200],
                              zeros(MLA_NOPE), wi[:, 3200:3232], zeros(LANES - MLA_NOPE - MLA_ROPE)], axis=1)
    qu = mla_w_q_up[l].reshape(MLA_Q_RANK, MLA_HEADS, MLA_NOPE + MLA_ROPE)
    qu = jnp.pad(qu, ((0, 0), (0, 0), (0, LANES - MLA_NOPE - MLA_ROPE))).reshape(MLA_Q_RANK, MLA_HEADS * LANES)
    kvu = mla_w_kv_up[l].reshape(MLA_KV_RANK, MLA_HEADS, MLA_NOPE + MLA_V)
    ku = jnp.pad(kvu[:, :, :MLA_NOPE], ((0, 0), (0, 0), (0, LANES - MLA_NOPE))).reshape(MLA_KV_RANK, -1)
    vu = kvu[:, :, MLA_NOPE:].reshape(MLA_KV_RANK, MLA_HEADS * MLA_V)
    wb = w_branch[l]
    wb_gqa = wb[2].reshape(GQA_Q_HEADS, HEAD_DIM, D_MODEL)[jnp.array(_GQA_HEAD_ORDER)].reshape(BRANCH_W, D_MODEL)
    wb = jnp.stack([wb[0], wb[1], wb_gqa, wb[3]])
    return {
        "w_in": w_in_p.astype(BF16),
        "gq_norm": jnp.tile(gqa_q_norm[l], GQA_Q_HEADS).reshape(1, 512),
        "gk_norm": jnp.tile(gqa_k_norm[l], GQA_KV_HEADS).reshape(1, LANES),
        "mq_norm": mla_q_norm[l].reshape(1, MLA_Q_RANK),
        "mkv_norm": mla_kv_norm[l].reshape(1, MLA_KV_RANK),
        "w_q_up": qu.astype(BF16), "w_k_up": ku.astype(BF16), "w_v_up": vu.astype(BF16),
        "w_branch": wb.astype(BF16), "w_gate": w_gate[l].astype(BF16), "b_gate": b_gate[l],
        "w_out": w_out[l].astype(BF16),
    }


def kernel(x, c, ctx, c_ctx, w_mod, b_mod, mix_pre_norm, mix_post_norm, ffn_pre_norm, ffn_post_norm,
           w_in, diff_lambda, diff_subnorm, gqa_q_norm, gqa_k_norm, mla_q_norm, mla_kv_norm,
           mla_w_q_up, mla_w_kv_up, w_branch, w_gate, b_gate, w_out,
           dense_w_gate_up, dense_w_down, moe_router, moe_w_gate_up, moe_w_down):
    B, n_lat, d = x.shape
    n_ctx = ctx.shape[1]
    depth = w_mod.shape[0]
    assert d == D_MODEL and n_lat % TM == 0 and n_ctx % TM == 0 and n_lat % n_ctx == 0
    assert n_lat % GRID_W == 0
    T = n_lat + n_ctx
    n_lat_tiles, n_all_tiles = n_lat // TM, T // TM
    consts = _constants(n_lat, n_ctx)
    x_all = (x, ctx)
    c_rows = jnp.pad(jnp.concatenate([c, c_ctx[None]], axis=0), ((0, (-(B + 1)) % 8), (0, 0)))
    row = lambda v: v.reshape(1, -1)

    for layer in range(depth):
        last = layer == depth - 1
        n_tiles = n_lat_tiles if last else n_all_tiles
        lam_init = 0.8 - 0.6 * math.exp(-0.3 * layer)
        wts = _layer_weights(layer, w_in, gqa_q_norm, gqa_k_norm, mla_q_norm, mla_kv_norm, mla_w_q_up,
                             mla_w_kv_up, w_branch, w_gate, b_gate, w_out)
        mod_rows = _modulation(c_rows, w_mod[layer], b_mod[layer])
        mods = jnp.stack([mod_rows[:B].reshape(B, N_MOD, D_MODEL),
                          jnp.broadcast_to(mod_rows[B].reshape(1, N_MOD, D_MODEL), (B, N_MOD, D_MODEL))], axis=1)

        (h, dq, dk, dv, gq, gk, gv, mq, mk, mv, pq_lat, pq_ctx) = _project(
            x_all, mods, row(mix_pre_norm[layer]), wts, consts, n_lat_tiles)

        diff_extra = (diff_lambda[layer], diff_subnorm[layer].reshape(1, LANES))
        ys = []
        for mode, q, k, v in (("diff", dq, dk, dv), ("gqa", gq, gk, gv), ("mla", mq, mk, mv)):
            y = jnp.zeros((B, T, BRANCH_W), BF16)
            kw = dict(lam_init=lam_init, extra=diff_extra)
            tq = TQ if n_lat % TQ == 0 else TM
            y = _attention(mode, q, k, v, y, tq=tq, nq=n_lat // tq, q_blk0=0, tk=T, k_blk=0, **kw)
            if not last:
                y = _attention(mode, q, k, v, y, tq=TM, nq=n_ctx // TM, q_blk0=n_lat_tiles,
                               tk=n_ctx, k_blk=n_lat // n_ctx, **kw)
            ys.append(y)
        y_diff, y_gqa, y_mla = ys

        a_rows = _dft_rows(pq_lat, consts["m1"])
        y_f = _dft_cols(a_rows, consts["tw_cos"], consts["tw_sin"], consts["m3"]).reshape(B, n_lat, BRANCH_W)
        if not last:
            y_f = jnp.concatenate([y_f, _dft_ctx(pq_ctx, consts["w_ctx"])], axis=1)

        merge_args = (x_all, h, y_diff, y_f, y_gqa, y_mla, mods, wts, row(mix_post_norm[layer]),
                      n_lat_tiles, n_tiles)
        if layer % 2 == 0:
            x_all = _merge(*merge_args)
            x_all = _dense_ffn(x_all, mods, row(ffn_pre_norm[layer]), row(ffn_post_norm[layer]),
                               dense_w_gate_up[layer // 2].astype(BF16), dense_w_down[layer // 2].astype(BF16),
                               n_lat_tiles, n_tiles)
        else:
            wr = jnp.pad(moe_router[layer // 2], ((0, 0), (0, LANES - N_EXPERTS)))
            idx = jnp.arange(TM)
            tri = jnp.where(idx[:, None] > idx[None, :], 1.0, 0.0).astype(BF16)
            x_all, h_ffn, route, counts = _merge(*merge_args, router=(row(ffn_pre_norm[layer]), wr, tri))
            x_all = _routed_ffn(x_all, h_ffn, route, counts, mods, row(ffn_post_norm[layer]),
                                moe_w_gate_up[layer // 2].astype(BF16), moe_w_down[layer // 2].astype(BF16),
                                n_lat_tiles, n_tiles)
    return x_all[:, :n_lat]
```

```python
import functools
import math

import jax
import jax.numpy as jnp
from jax import lax
from jax.experimental import pallas as pl
from jax.experimental.pallas import tpu as pltpu

F32 = jnp.float32
BF16 = jnp.bfloat16

D_MODEL = 1024
GRID_W = 64
ROPE_BASE = 10000.0
EPS = 1e-6
N_MOD = 6
HEAD_DIM = 64
DIFF_HEADS = 4
GQA_Q_HEADS = 8
GQA_KV_HEADS = 2
MLA_HEADS = 8
MLA_Q_RANK = 256
MLA_KV_RANK = 128
MLA_NOPE = 64
MLA_ROPE = 32
MLA_V = 64
BRANCH_W = 512
FOURIER_GROUP = 128
FFN_DIM = 2816
N_EXPERTS = 8
EXPERT_DIM = 3584
LOG2E = math.log2(math.e)
QK_SCALE = HEAD_DIM ** -0.5 * LOG2E
MLA_SCALE = (MLA_NOPE + MLA_ROPE) ** -0.5 * LOG2E
IN_COLS_PADDED = 3328

LANES = 128
VMEM_LIMIT = 56 * 1024 * 1024

TM = 256
TQ = 1024


def _cparams(sem):
    return pltpu.CompilerParams(dimension_semantics=sem, vmem_limit_bytes=VMEM_LIMIT)


def _full_spec(shape):
    n = len(shape)
    return pl.BlockSpec(shape, lambda *_: (0,) * n)


def _split_bf16(x):
    hi = x.astype(BF16)
    lo = (x - hi.astype(F32)).astype(BF16)
    return hi, lo


def _dot(a, b):
    return jnp.dot(a, b, preferred_element_type=F32)


def _dot3(a, w):
    ah, al = _split_bf16(a)
    wh, wl = _split_bf16(w)
    return _dot(ah, wh) + _dot(ah, wl) + _dot(al, wh)


def _rms(x, gain):
    ms = jnp.mean(x * x, axis=-1, keepdims=True)
    return x * lax.rsqrt(ms + EPS) * gain


def _mod_kernel(c_ref, w_ref, b_ref, o_ref):
    c = c_ref[...]
    a = c * jax.nn.sigmoid(c)
    o_ref[...] = _dot3(a, w_ref[...]) + b_ref[...]


def _modulation(c_rows, w_mod_l, b_mod_l):
    rows = c_rows.shape[0]
    n = w_mod_l.shape[1]
    tn = 512
    return pl.pallas_call(
        _mod_kernel,
        grid=(n // tn,),
        in_specs=[pl.BlockSpec((rows, D_MODEL), lambda j: (0, 0)),
                  pl.BlockSpec((D_MODEL, tn), lambda j: (0, j)),
                  pl.BlockSpec((1, tn), lambda j: (0, j))],
        out_specs=pl.BlockSpec((rows, tn), lambda j: (0, j)),
        out_shape=jax.ShapeDtypeStruct((rows, n), F32),
        compiler_params=_cparams(("arbitrary",)),
        name="modulation",
    )(c_rows, w_mod_l, b_mod_l.reshape(1, n))


def _rope(x, cos, sin_signed, half):
    lane = lax.broadcasted_iota(jnp.int32, x.shape, 1)
    first = (lane % (2 * half)) < half
    up = pltpu.roll(x, LANES - half, 1)
    down = pltpu.roll(x, half, 1)
    return x * cos + jnp.where(first, up, down) * sin_signed


def _stream_specs(xs, n_lat_tiles):
    if not isinstance(xs, tuple):
        return [pl.BlockSpec((None, TM, D_MODEL), lambda b, t: (b, t, 0))], [xs]
    return ([pl.BlockSpec((None, TM, D_MODEL), lambda b, t: (b, jnp.minimum(t, n_lat_tiles - 1), 0)),
             pl.BlockSpec((None, TM, D_MODEL), lambda b, t: (b, jnp.maximum(t - n_lat_tiles, 0), 0))], list(xs))


def _stream_tile(x_refs, n_lat_tiles):
    if len(x_refs) == 1:
        return x_refs[0][...]
    return jnp.where(pl.program_id(1) < n_lat_tiles, x_refs[0][...], x_refs[1][...])


def _stream_dims(xs):
    if isinstance(xs, tuple):
        return xs[0].shape[0], xs[0].shape[1] + xs[1].shape[1]
    return xs.shape[0], xs.shape[1]


def _proj_kernel(n_lat_tiles, n_x, *refs):
    x_refs, refs = refs[:n_x], refs[n_x:]
    (mod_ref, gpre_ref, win_ref, gqn_ref, gkn_ref, mqn_ref, mkvn_ref,
     wqup_ref, wkup_ref, wvup_ref, seg_ref, cc_ref, sc_ref,
     cosh_ref, sinh_ref, cosm_ref, sinm_ref,
     h_ref, dq_ref, dk_ref, dv_ref, gq_ref, gk_ref, gv_ref, mq_ref, mk_ref, mv_ref, pq_ref, pqc_ref) = refs
    x = _stream_tile(x_refs, n_lat_tiles)
    mod = mod_ref[...]
    h = _rms(x, gpre_ref[...]) * (1.0 + mod[1:2]) + mod[0:1]
    hb = h.astype(BF16)
    h_ref[...] = hb
    z = _dot(hb, win_ref[...])

    def zcols(lo, hi):
        return z[:, lo:hi]
    cosh, sinh = cosh_ref[...], sinh_ref[...]
    cosm, sinm = cosm_ref[...], sinm_ref[...]
    seg = seg_ref[...]

    def head_rms(v, gain, w):
        sq = v * v
        hi, lo = _split_bf16(sq)
        ms = _dot(hi, seg[:w, :w]) + _dot(lo, seg[:w, :w])
        return v * lax.rsqrt(ms + EPS) * gain

    for j in range(4):
        c0 = j * LANES
        dq_ref[:, c0:c0 + LANES] = (_rope(zcols(c0, c0 + LANES), cosh, sinh, 16) * QK_SCALE).astype(BF16)
        dk_ref[:, c0:c0 + LANES] = _rope(zcols(512 + c0, 512 + c0 + LANES), cosh, sinh, 16).astype(BF16)
    dv_ref[...] = zcols(1024, 1536).astype(BF16)
    fparts = []
    for g in range(4):
        c0 = 1536 + g * LANES
        fb = zcols(c0, c0 + LANES).astype(BF16)
        fparts.append((_dot(fb, cc_ref[...]).astype(BF16), _dot(fb, sc_ref[...]).astype(BF16)))
    is_lat = pl.program_id(1) < n_lat_tiles

    def put(ref):
        for g, (pc, ps) in enumerate(fparts):
            ref[0, :, g * LANES:(g + 1) * LANES] = pc
            ref[1, :, g * LANES:(g + 1) * LANES] = ps

    pl.when(is_lat)(lambda: put(pq_ref))
    pl.when(jnp.logical_not(is_lat))(lambda: put(pqc_ref))
    gq = head_rms(zcols(2048, 2560), gqn_ref[...], 512)
    for j in range(4):
        c0 = j * LANES
        gq_ref[:, c0:c0 + LANES] = (_rope(gq[:, c0:c0 + LANES], cosh, sinh, 16) * QK_SCALE).astype(BF16)
    gk = head_rms(zcols(2560, 2688), gkn_ref[...], LANES)
    gk_ref[...] = _rope(gk, cosh, sinh, 16).astype(BF16)
    gv_ref[...] = zcols(2688, 2816).astype(BF16)
    cq = _rms(zcols(2816, 3072), mqn_ref[...]).astype(BF16)
    mq = _dot(cq, wqup_ref[...])
    ckv = _rms(zcols(3072, 3200), mkvn_ref[...]).astype(BF16)
    mk = _dot(ckv, wkup_ref[...])
    mv_ref[...] = _dot(ckv, wvup_ref[...]).astype(BF16)
    kr = _rope(zcols(3200, 3328), cosm, sinm, 8)
    for hh in range(MLA_HEADS):
        c0 = hh * LANES
        mq_ref[:, c0:c0 + LANES] = (_rope(mq[:, c0:c0 + LANES], cosm, sinm, 8) * MLA_SCALE).astype(BF16)
        mk_ref[:, c0:c0 + LANES] = (mk[:, c0:c0 + LANES] + kr).astype(BF16)


def _project(x_all, mods, gpre, wts, consts, n_lat_tiles):
    B, T = _stream_dims(x_all)
    nt = T // TM
    tok = lambda w: pl.BlockSpec((None, TM, w), lambda b, t: (b, t, 0))
    tab = pl.BlockSpec((TM, LANES), lambda b, t: (t, 0))
    x_specs, x_args = _stream_specs(x_all, n_lat_tiles)
    in_specs = x_specs + [
        pl.BlockSpec((None, None, N_MOD, D_MODEL), lambda b, t: (b, jnp.where(t >= n_lat_tiles, 1, 0), 0, 0)),
        _full_spec((1, D_MODEL)),
        _full_spec((D_MODEL, IN_COLS_PADDED)),
        _full_spec((1, 512)), _full_spec((1, LANES)), _full_spec((1, MLA_Q_RANK)), _full_spec((1, MLA_KV_RANK)),
        _full_spec((MLA_Q_RANK, MLA_HEADS * LANES)),
        _full_spec((MLA_KV_RANK, MLA_HEADS * LANES)),
        _full_spec((MLA_KV_RANK, MLA_HEADS * MLA_V)),
        _full_spec((512, 512)), _full_spec((LANES, LANES)), _full_spec((LANES, LANES)),
        tab, tab, tab, tab,
    ]
    widths = [D_MODEL, 512, 512, 512, 512, LANES, LANES, MLA_HEADS * LANES, MLA_HEADS * LANES, 512]
    out_specs = [tok(w) for w in widths]
    out_shape = [jax.ShapeDtypeStruct((B, T, w), BF16) for w in widths]
    out_specs.append(pl.BlockSpec((2, None, TM, 512), lambda b, t: (0, b, jnp.minimum(t, n_lat_tiles - 1), 0)))
    out_shape.append(jax.ShapeDtypeStruct((2, B, n_lat_tiles * TM, 512), BF16))
    out_specs.append(pl.BlockSpec((2, None, TM, 512), lambda b, t: (0, b, jnp.maximum(t - n_lat_tiles, 0), 0)))
    out_shape.append(jax.ShapeDtypeStruct((2, B, T - n_lat_tiles * TM, 512), BF16))
    return pl.pallas_call(
        functools.partial(_proj_kernel, n_lat_tiles, len(x_args)),
        grid=(B, nt),
        in_specs=in_specs,
        out_specs=out_specs,
        out_shape=out_shape,
        compiler_params=_cparams(("arbitrary", "arbitrary")),
        name="project",
    )(*x_args, mods, gpre, wts["w_in"], wts["gq_norm"], wts["gk_norm"], wts["mq_norm"], wts["mkv_norm"],
      wts["w_q_up"], wts["w_k_up"], wts["w_v_up"], consts["seg"], consts["cc"], consts["sc"],
      consts["cos_h"], consts["sin_h"], consts["cos_m"], consts["sin_m"])


N_MAPS = 8
MAPS_PER_STEP = 2


def _map_plan(mode, m):
    if mode == "diff":
        return m // 2, m % 2, m // 2, m // 2
    if mode == "gqa":
        return m // 2, m % 2, 0, 0
    return m, None, m, m // 2


def _attn_kernel(mode, lam_init, q_ref, k_ref, v_ref, *rest):
    o_ref, vx_ref = rest[-2:]
    if mode == "diff":
        lam_ref, sub_ref = rest[:2]
    nt = (((1,), (1,)), ((), ()))
    lane = lax.broadcasted_iota(jnp.int32, (1, LANES), 1)
    low = lane < HEAD_DIM

    @pl.when(pl.program_id(2) == 0)
    def _():
        for g in range(vx_ref.shape[0]):
            vx_ref[g, :, :LANES] = v_ref[:, g * LANES:(g + 1) * LANES]
            vx_ref[g, :, LANES:] = jnp.ones((vx_ref.shape[1], LANES), BF16)

    def scores(m):
        qblk, half, kblk, _ = _map_plan(mode, m)
        q = q_ref[:, qblk * LANES:(qblk + 1) * LANES]
        if half is not None:
            q = jnp.where(low if half == 0 else jnp.logical_not(low), q, jnp.zeros_like(q))
        return lax.dot_general(q, k_ref[:, kblk * LANES:(kblk + 1) * LANES], nt, preferred_element_type=F32)

    def numerator(s):
        return jnp.exp2(s - jnp.max(s, axis=-1, keepdims=True)).astype(BF16)

    def values(m, p):
        e = _dot(p, vx_ref[_map_plan(mode, m)[3]])
        return e[:, :LANES] / e[:, LANES:]

    s, pl_, y = {}, {}, {}
    for t in range(MAPS_PER_STEP + 2):
        if t < MAPS_PER_STEP:
            s[t] = scores(t)
        if 0 <= t - 1 < MAPS_PER_STEP:
            pl_[t - 1] = numerator(s.pop(t - 1))
        if 0 <= t - 2:
            y[t - 2] = values(t - 2, pl_.pop(t - 2))

    if mode == "diff":
        dl = lam_ref[...]
        lam = (jnp.exp(jnp.sum(dl[0:1] * dl[1:2], axis=-1, keepdims=True))
               - jnp.exp(jnp.sum(dl[2:3] * dl[3:4], axis=-1, keepdims=True)) + lam_init)
    for j in range(MAPS_PER_STEP // 2):
        if mode == "diff":
            out = _rms(y[2 * j] - lam * y[2 * j + 1], sub_ref[...]) * (1.0 - lam_init)
        else:
            out = jnp.where(low, y[2 * j], y[2 * j + 1])
        o_ref[:, j * LANES:(j + 1) * LANES] = out.astype(BF16)


def _attention(mode, q, k, v, y_prev, *, tq, nq, q_blk0, tk, k_blk, lam_init=0.0, extra=()):
    B, T, _ = q.shape
    n_steps = N_MAPS // MAPS_PER_STEP
    wq, wk, wv = q.shape[2] // n_steps, max(k.shape[2] // n_steps, LANES), max(v.shape[2] // n_steps, LANES)
    kv_blk = (lambda g: 0) if mode == "gqa" else (lambda g: g)
    in_specs = [
        pl.BlockSpec((None, tq, wq), lambda b, g, i: (b, i + q_blk0, g)),
        pl.BlockSpec((None, tk, wk), lambda b, g, i: (b, k_blk, kv_blk(g))),
        pl.BlockSpec((None, tk, wv), lambda b, g, i: (b, k_blk, kv_blk(g))),
    ]
    args = [q, k, v]
    if mode == "diff":
        in_specs += [_full_spec((4, HEAD_DIM)), _full_spec((1, LANES))]
        args += list(extra)
    aliases = {}
    if y_prev is not None:
        in_specs.append(pl.BlockSpec(memory_space=pl.ANY))
        args.append(y_prev)
        aliases = {len(args) - 1: 0}
    wo = BRANCH_W // n_steps
    return pl.pallas_call(
        functools.partial(_attn_kernel, mode, lam_init),
        grid=(B, n_steps, nq),
        in_specs=in_specs,
        out_specs=pl.BlockSpec((None, tq, wo), lambda b, g, i: (b, i + q_blk0, g)),
        out_shape=jax.ShapeDtypeStruct((B, T, BRANCH_W), BF16),
        input_output_aliases=aliases,
        scratch_shapes=[pltpu.VMEM((wv // LANES, tk, 2 * LANES), BF16)],
        compiler_params=_cparams(("arbitrary", "arbitrary", "arbitrary")),
        name="attention_" + mode,
    )(*args)


DFT_COLS_PER_STEP = 8


def _dft_rows_kernel(pq_ref, m1_ref, a_ref):
    r = pq_ref.shape[1]
    for n2 in range(pq_ref.shape[2]):
        zin = jnp.concatenate([pq_ref[0, :, n2, :], pq_ref[1, :, n2, :]], axis=0)
        a = _dot(m1_ref[...], zin).astype(BF16)
        a_ref[0, :, n2, :] = a[:r]
        a_ref[1, :, n2, :] = a[r:]


def _dft_rows(pq_lat, m1):
    _, B, n_lat, w = pq_lat.shape
    r = n_lat // GRID_W
    nb = DFT_COLS_PER_STEP
    return pl.pallas_call(
        _dft_rows_kernel,
        grid=(B, GRID_W // nb),
        in_specs=[pl.BlockSpec((2, None, r, nb, w), lambda b, j: (0, b, 0, j, 0)), _full_spec((2 * r, 2 * r))],
        out_specs=pl.BlockSpec((None, 2, r, nb, w), lambda b, j: (b, 0, 0, j, 0)),
        out_shape=jax.ShapeDtypeStruct((B, 2, r, GRID_W, w), BF16),
        compiler_params=_cparams(("arbitrary", "arbitrary")),
        name="dft_rows",
    )(pq_lat.reshape(2, B, r, GRID_W, w), m1)


def _dft_cols_kernel(a_ref, ct_ref, st_ref, m3_ref, o_ref):
    kb = a_ref.shape[1]
    w = a_ref.shape[3]
    for j in range(kb):
        ar = a_ref[0, j].astype(F32)
        ai = a_ref[1, j].astype(F32)
        ct = jnp.concatenate([ct_ref[j]] * (w // LANES), axis=1)
        st = jnp.concatenate([st_ref[j]] * (w // LANES), axis=1)
        b = jnp.concatenate([ar * ct + ai * st, ai * ct - ar * st], axis=0).astype(BF16)
        o_ref[:, j, :] = _dot(m3_ref[...], b).astype(BF16)


def _dft_cols(a, ct, st, m3):
    B, _, r, _, w = a.shape
    kb = DFT_COLS_PER_STEP
    return pl.pallas_call(
        _dft_cols_kernel,
        grid=(B, r // kb),
        in_specs=[pl.BlockSpec((None, 2, kb, GRID_W, w), lambda b, k: (b, 0, k, 0, 0)),
                  pl.BlockSpec((kb, GRID_W, LANES), lambda b, k: (k, 0, 0)),
                  pl.BlockSpec((kb, GRID_W, LANES), lambda b, k: (k, 0, 0)),
                  _full_spec((GRID_W, 2 * GRID_W))],
        out_specs=pl.BlockSpec((None, GRID_W, kb, w), lambda b, k: (b, 0, k, 0)),
        out_shape=jax.ShapeDtypeStruct((B, GRID_W, r, w), BF16),
        compiler_params=_cparams(("arbitrary", "arbitrary")),
        name="dft_cols",
    )(a, ct, st, m3)


def _dft_ctx_kernel(pq_ref, w_ref, o_ref):
    zin = jnp.concatenate([pq_ref[0], pq_ref[1]], axis=0)
    o_ref[...] = _dot(w_ref[...], zin).astype(BF16)


def _dft_ctx(pq_ctx, w_ctx):
    _, B, n_ctx, w = pq_ctx.shape
    return pl.pallas_call(
        _dft_ctx_kernel,
        grid=(B,),
        in_specs=[pl.BlockSpec((2, None, n_ctx, w), lambda b: (0, b, 0, 0)), _full_spec((n_ctx, 2 * n_ctx))],
        out_specs=pl.BlockSpec((None, n_ctx, w), lambda b: (b, 0, 0)),
        out_shape=jax.ShapeDtypeStruct((B, n_ctx, w), BF16),
        compiler_params=_cparams(("arbitrary",)),
        name="dft_ctx",
    )(pq_ctx, w_ctx)


def _merge_kernel(n_lat_tiles, n_x, *refs):
    x_refs, refs = refs[:n_x], refs[n_x:]
    h_ref, y0_ref, y1_ref, y2_ref, y3_ref, mod_ref, wg_ref, bg_ref, wb_ref, wo_ref, gpost_ref = refs[:11]
    rest = refs[11:]
    h = h_ref[...]
    ys = (y0_ref, y1_ref, y2_ref, y3_ref)
    acc = None
    for i in range(4):
        gate = jax.nn.sigmoid(_dot(h, wg_ref[i]) + bg_ref[i:i + 1])
        term = gate * _dot(ys[i][...], wb_ref[i])
        acc = term if acc is None else acc + term
    m = _dot(acc.astype(BF16), wo_ref[...])
    mod = mod_ref[...]
    x = _stream_tile(x_refs, n_lat_tiles) + mod[2:3] * _rms(m, gpost_ref[...])
    o_ref = rest[3] if len(rest) > 1 else rest[0]
    o_ref[...] = x
    if len(rest) > 1:
        gffn_ref, wr_ref, tri_ref, _, hf_ref, route_ref, cnt_ref, carry_ref = rest
        _route_tile(x, mod, gffn_ref, wr_ref, tri_ref, hf_ref, route_ref, cnt_ref, carry_ref)


def _merge(x_all, h, y_diff, y_f, y_gqa, y_mla, mods, wts, gpost, n_lat_tiles, n_tiles, router=None):
    B, _ = _stream_dims(x_all)
    tok = lambda w: pl.BlockSpec((None, TM, w), lambda b, t: (b, t, 0))
    x_specs, x_args = _stream_specs(x_all, n_lat_tiles)
    in_specs = x_specs + [
        tok(D_MODEL), tok(BRANCH_W), tok(BRANCH_W), tok(BRANCH_W), tok(BRANCH_W),
        pl.BlockSpec((None, None, N_MOD, D_MODEL), lambda b, t: (b, jnp.where(t >= n_lat_tiles, 1, 0), 0, 0)),
        _full_spec((4, D_MODEL, D_MODEL)), _full_spec((4, D_MODEL)), _full_spec((4, BRANCH_W, D_MODEL)),
        _full_spec((D_MODEL, D_MODEL)), _full_spec((1, D_MODEL)),
    ]
    args = x_args + [h, y_diff, y_f, y_gqa, y_mla, mods, wts["w_gate"], wts["b_gate"], wts["w_branch"],
                     wts["w_out"], gpost]
    out_specs = [tok(D_MODEL)]
    out_shape = [jax.ShapeDtypeStruct((B, n_tiles * TM, D_MODEL), F32)]
    scratch = []
    if router is not None:
        in_specs += [_full_spec((1, D_MODEL)), _full_spec((D_MODEL, LANES)), _full_spec((TM, TM))]
        args += list(router)
        out_specs += [tok(D_MODEL), tok(LANES), _full_spec((1, LANES))]
        out_shape += [jax.ShapeDtypeStruct((B, n_tiles * TM, D_MODEL), F32),
                      jax.ShapeDtypeStruct((B, n_tiles * TM, LANES), F32),
                      jax.ShapeDtypeStruct((1, LANES), F32)]
        scratch = [pltpu.VMEM((1, LANES), F32)]
    out = pl.pallas_call(
        functools.partial(_merge_kernel, n_lat_tiles, len(x_args)),
        grid=(B, n_tiles),
        in_specs=in_specs,
        out_specs=out_specs,
        out_shape=out_shape,
        scratch_shapes=scratch,
        compiler_params=_cparams(("arbitrary", "arbitrary")),
        name="merge",
    )(*args)
    return out[0] if router is None else out


def _dense_ffn_kernel(nf, tf, x_ref, mod_ref, gpre_ref, gpost_ref, wgu_ref, wd_ref, o_ref):
    x = x_ref[...]
    mod = mod_ref[...]
    hb = (_rms(x, gpre_ref[...]) * (1.0 + mod[4:5]) + mod[3:4]).astype(BF16)
    acc = None
    for f in range(nf):
        a = _dot(hb, wgu_ref[:, f * tf:(f + 1) * tf])
        u = _dot(hb, wgu_ref[:, FFN_DIM + f * tf:FFN_DIM + (f + 1) * tf])
        act = (a * jax.nn.sigmoid(a) * u).astype(BF16)
        part = _dot(act, wd_ref[f * tf:(f + 1) * tf, :])
        acc = part if acc is None else acc + part
    o_ref[...] = x + mod[5:6] * _rms(acc, gpost_ref[...])


def _dense_ffn(x_all, mods, gpre, gpost, wgu, wd, n_lat_tiles, n_tiles):
    B = x_all.shape[0]
    nf, tf = 1, FFN_DIM
    tok = pl.BlockSpec((None, TM, D_MODEL), lambda b, t: (b, t, 0))
    return pl.pallas_call(
        functools.partial(_dense_ffn_kernel, nf, tf),
        grid=(B, n_tiles),
        in_specs=[tok,
                  pl.BlockSpec((None, None, N_MOD, D_MODEL),
                               lambda b, t: (b, jnp.where(t >= n_lat_tiles, 1, 0), 0, 0)),
                  _full_spec((1, D_MODEL)), _full_spec((1, D_MODEL)),
                  _full_spec((D_MODEL, 2 * FFN_DIM)), _full_spec((FFN_DIM, D_MODEL))],
        out_specs=tok,
        out_shape=jax.ShapeDtypeStruct((B, n_tiles * TM, D_MODEL), F32),
        compiler_params=_cparams(("arbitrary", "arbitrary")),
        name="dense_ffn",
    )(x_all, mods, gpre, gpost, wgu, wd)


TE = 512
R_E1, R_E2, R_G1, R_G2, R_RANK1, R_RANK2 = range(6)


def _route_tile(x, mod, gpre_ref, wr_ref, tri_ref, h_ref, route_ref, cnt_ref, carry_ref):
    first = jnp.logical_and(pl.program_id(0) == 0, pl.program_id(1) == 0)

    @pl.when(first)
    def _():
        carry_ref[...] = jnp.zeros_like(carry_ref)

    h = _rms(x, gpre_ref[...]) * (1.0 + mod[4:5]) + mod[3:4]
    h_ref[...] = h
    logits = _dot3(h, wr_ref[...])
    lane = lax.broadcasted_iota(jnp.int32, logits.shape, 1)
    neg = jnp.float32(-jnp.inf)
    l1 = jnp.where(lane < N_EXPERTS, logits, neg)
    m1 = jnp.max(l1, axis=-1, keepdims=True)
    i1 = jnp.min(jnp.where(l1 == m1, lane, LANES), axis=-1, keepdims=True)
    l2 = jnp.where(lane == i1, neg, l1)
    m2 = jnp.max(l2, axis=-1, keepdims=True)
    i2 = jnp.min(jnp.where(l2 == m2, lane, LANES), axis=-1, keepdims=True)
    e2 = jnp.exp(m2 - m1)
    g1 = 1.0 / (1.0 + e2)
    g2 = e2 / (1.0 + e2)
    sel1, sel2 = lane == i1, lane == i2
    chosen = jnp.where(jnp.logical_or(sel1, sel2), 1.0, 0.0)
    before = _dot(tri_ref[...], chosen.astype(BF16)) + carry_ref[...]
    r1 = jnp.sum(jnp.where(sel1, before, 0.0), axis=-1, keepdims=True)
    r2 = jnp.sum(jnp.where(sel2, before, 0.0), axis=-1, keepdims=True)
    carry_ref[...] += jnp.sum(chosen, axis=0, keepdims=True)
    cnt_ref[...] = carry_ref[...]
    rec = jnp.zeros_like(logits)
    for col, val in ((R_E1, i1.astype(F32)), (R_E2, i2.astype(F32)), (R_G1, g1), (R_G2, g2),
                     (R_RANK1, r1), (R_RANK2, r2)):
        rec = jnp.where(lane == col, val, rec)
    route_ref[...] = rec


def _row_copies(n_rows, copy_of, slot_wait):
    def start(r, carry):
        copy_of(r, 0).start()
        copy_of(r, 1).start()
        return carry

    lax.fori_loop(0, n_rows, start, 0, unroll=8)
    slot_wait(0).wait()
    slot_wait(1).wait()


def _dispatch_kernel(last_ref, pos_ref, h_ref, xs_ref, zero_ref, sem):
    @pl.when(pl.program_id(0) == 0)
    def _():
        zero_ref[...] = jnp.zeros_like(zero_ref)
        for e in range(N_EXPERTS):
            @pl.when(last_ref[e] >= 0)
            def _():
                fill = pltpu.make_async_copy(zero_ref, xs_ref.at[pl.ds(pl.multiple_of(last_ref[e], TE), TE)], sem)
                fill.start()
                fill.wait()

    def copy_of(r, slot):
        p = pos_ref[0, 2 * r + slot]
        return pltpu.make_async_copy(h_ref.at[pl.ds(r, 1)], xs_ref.at[pl.ds(p, 1)], sem)

    n = h_ref.shape[0]
    _row_copies(n, copy_of, lambda slot: pltpu.make_async_copy(h_ref, xs_ref.at[pl.ds(0, n)], sem))


def _dispatch(h, pos, last_tile_row, n_rows):
    n = h.shape[0]
    grid_spec = pltpu.PrefetchScalarGridSpec(
        num_scalar_prefetch=1,
        grid=(n // TM,),
        in_specs=[pl.BlockSpec((None, 1, 2 * TM), lambda i, last: (i, 0, 0), memory_space=pltpu.SMEM),
                  pl.BlockSpec((TM, D_MODEL), lambda i, last: (i, 0))],
        out_specs=pl.BlockSpec(memory_space=pl.ANY),
        scratch_shapes=[pltpu.VMEM((TE, D_MODEL), F32), pltpu.SemaphoreType.DMA],
    )
    return pl.pallas_call(
        _dispatch_kernel,
        grid_spec=grid_spec,
        out_shape=jax.ShapeDtypeStruct((n_rows, D_MODEL), F32),
        compiler_params=_cparams(("arbitrary",)),
        name="moe_dispatch",
    )(last_tile_row, pos.reshape(n // TM, 1, 2 * TM), h)


def _experts_kernel(te_ref, nv_ref, xs_ref, wa_ref, wu_ref, wd_ref, ys_ref):
    del te_ref

    @pl.when(pl.program_id(0) < nv_ref[0])
    def _():
        x = xs_ref[...].astype(BF16)
        a = _dot(x, wa_ref[...])
        u = _dot(x, wu_ref[...])
        ys_ref[...] = _dot((a * jax.nn.sigmoid(a) * u).astype(BF16), wd_ref[...])


def _experts(xs, tile_expert, n_valid, wgu, wd):
    n_rows = xs.shape[0]
    row_blk = lambda i, te, nv: (jnp.minimum(i, nv[0] - 1), 0)
    resident = pl.Buffered(1)
    grid_spec = pltpu.PrefetchScalarGridSpec(
        num_scalar_prefetch=2,
        grid=(n_rows // TE,),
        in_specs=[pl.BlockSpec((TE, D_MODEL), row_blk),
                  pl.BlockSpec((None, D_MODEL, EXPERT_DIM), lambda i, te, nv: (te[i], 0, 0), pipeline_mode=resident),
                  pl.BlockSpec((None, D_MODEL, EXPERT_DIM), lambda i, te, nv: (te[i], 0, 1), pipeline_mode=resident),
                  pl.BlockSpec((None, EXPERT_DIM, D_MODEL), lambda i, te, nv: (te[i], 0, 0), pipeline_mode=resident)],
        out_specs=pl.BlockSpec((TE, D_MODEL), row_blk),
    )
    return pl.pallas_call(
        _experts_kernel,
        grid_spec=grid_spec,
        out_shape=jax.ShapeDtypeStruct((n_rows, D_MODEL), F32),
        compiler_params=_cparams(("arbitrary",)),
        name="moe_experts",
    )(tile_expert, n_valid, xs, wgu, wgu, wd)


def _combine_kernel(pos_ref, x_ref, route_ref, mod_ref, gpost_ref, ys_ref, o_ref, buf_ref, sem):
    def copy_of(r, slot):
        p = pos_ref[0, 2 * r + slot]
        return pltpu.make_async_copy(ys_ref.at[pl.ds(p, 1)], buf_ref.at[slot, pl.ds(r, 1)], sem)

    n = x_ref.shape[0]
    _row_copies(n, copy_of, lambda slot: pltpu.make_async_copy(ys_ref.at[pl.ds(0, n)], buf_ref.at[slot], sem))
    route = route_ref[...]
    y = route[:, R_G1:R_G1 + 1] * buf_ref[0] + route[:, R_G2:R_G2 + 1] * buf_ref[1]
    mod = mod_ref[...]
    o_ref[...] = x_ref[...] + mod[5:6] * _rms(y, gpost_ref[...])


def _combine(x_all, route, pos, ys, mods, gpost, n_lat_tiles, n_tiles):
    B = x_all.shape[0]
    tok = lambda w: pl.BlockSpec((None, TM, w), lambda b, t: (b, t, 0))
    return pl.pallas_call(
        _combine_kernel,
        grid=(B, n_tiles),
        in_specs=[pl.BlockSpec((None, None, 1, 2 * TM), lambda b, t: (b, t, 0, 0), memory_space=pltpu.SMEM),
                  tok(D_MODEL), tok(LANES),
                  pl.BlockSpec((None, None, N_MOD, D_MODEL),
                               lambda b, t: (b, jnp.where(t >= n_lat_tiles, 1, 0), 0, 0)),
                  _full_spec((1, D_MODEL)),
                  pl.BlockSpec(memory_space=pl.ANY)],
        out_specs=tok(D_MODEL),
        out_shape=jax.ShapeDtypeStruct((B, n_tiles * TM, D_MODEL), F32),
        scratch_shapes=[pltpu.VMEM((2, TM, D_MODEL), F32), pltpu.SemaphoreType.DMA],
        compiler_params=_cparams(("arbitrary", "arbitrary")),
        name="moe_combine",
    )(pos.reshape(B, n_tiles, 1, 2 * TM), x_all, route, mods, gpost, ys)


def _routed_ffn(x_all, h, route, counts, mods, gpost, wgu, wd, n_lat_tiles, n_tiles):
    B = x_all.shape[0]
    n_tok = B * n_tiles * TM
    rec = route.reshape(n_tok, LANES)
    expert = rec[:, R_E1:R_E2 + 1].astype(jnp.int32)
    rank = rec[:, R_RANK1:R_RANK2 + 1].astype(jnp.int32)
    tiles_per_expert = (counts[0, :N_EXPERTS].astype(jnp.int32) + TE - 1) // TE
    tile_end = jnp.cumsum(tiles_per_expert)
    n_valid = tile_end[-1:]
    pos = (tile_end - tiles_per_expert)[expert] * TE + rank
    n_tiles_max = (2 * n_tok) // TE + N_EXPERTS
    tile_id = jnp.minimum(jnp.arange(n_tiles_max, dtype=jnp.int32), n_valid - 1)
    tile_expert = jnp.minimum(jnp.searchsorted(tile_end, tile_id, side="right"), N_EXPERTS - 1).astype(jnp.int32)
    last_tile_row = jnp.where(tiles_per_expert > 0, (tile_end - 1) * TE, -1).astype(jnp.int32)
    xs = _dispatch(h.reshape(n_tok, D_MODEL), pos, last_tile_row, n_tiles_max * TE)
    ys = _experts(xs, tile_expert, n_valid.astype(jnp.int32), wgu, wd)
    return _combine(x_all, route, pos, ys, mods, gpost, n_lat_tiles, n_tiles)


def _rope_angles(n_lat, rot_dim):
    rows = n_lat // GRID_W
    row = jnp.repeat(jnp.arange(rows, dtype=F32), GRID_W)
    col = jnp.tile(jnp.arange(GRID_W, dtype=F32), rows)
    axis_dim = rot_dim // 2
    inv_freq = 1.0 / (ROPE_BASE ** (jnp.arange(0, axis_dim, 2, dtype=F32) / axis_dim))
    ar = row[:, None] * inv_freq
    ac = col[:, None] * inv_freq
    return jnp.concatenate([ar, ar, ac, ac], axis=-1)


def _constants(n_lat, n_ctx):
    lane = jnp.arange(LANES)
    ang = _rope_angles(n_lat, HEAD_DIM)
    cos_h = jnp.tile(jnp.cos(ang), (1, 2))
    sin_h = jnp.tile(jnp.sin(ang), (1, 2)) * jnp.where((lane % 32) < 16, -1.0, 1.0)
    cos_h = jnp.concatenate([cos_h, jnp.ones((n_ctx, LANES), F32)], axis=0)
    sin_h = jnp.concatenate([sin_h, jnp.zeros((n_ctx, LANES), F32)], axis=0)
    angm = _rope_angles(n_lat, MLA_ROPE)
    pad_l = jnp.zeros((n_lat, MLA_NOPE), F32)
    pad_r = jnp.zeros((n_lat, LANES - MLA_NOPE - MLA_ROPE), F32)
    cos_m = jnp.concatenate([pad_l + 1.0, jnp.cos(angm), pad_r + 1.0], axis=1)
    sin_m = jnp.concatenate([pad_l, jnp.sin(angm), pad_r], axis=1) * jnp.where((lane % 16) < 8, -1.0, 1.0)
    cos_m = jnp.concatenate([cos_m, jnp.ones((n_ctx, LANES), F32)], axis=0)
    sin_m = jnp.concatenate([sin_m, jnp.zeros((n_ctx, LANES), F32)], axis=0)
    idx = jnp.arange(512)
    seg = jnp.where((idx[:, None] // HEAD_DIM) == (idx[None, :] // HEAD_DIM), 1.0 / HEAD_DIM, 0.0).astype(BF16)

    def dft(n):
        i = jnp.arange(n, dtype=jnp.int32)
        a = ((i[:, None] * i[None, :]) % n).astype(F32) * (2.0 * math.pi / n)
        s = n ** -0.5
        return jnp.cos(a) * s, jnp.sin(a) * s

    cc, sc = dft(FOURIER_GROUP)
    cx, sx = dft(n_ctx)
    r = n_lat // GRID_W
    cr, sr = (t * r ** 0.5 for t in dft(r))
    c64, s64 = (t * GRID_W ** 0.5 * n_lat ** -0.5 for t in dft(GRID_W))
    k1 = jnp.arange(r, dtype=jnp.int32)[:, None]
    n2 = jnp.arange(GRID_W, dtype=jnp.int32)[None, :]
    tw = ((k1 * n2) % n_lat).astype(F32) * (2.0 * math.pi / n_lat)
    wide = lambda t: jnp.broadcast_to(t[:, :, None], (r, GRID_W, LANES))
    return {
        "cos_h": cos_h, "sin_h": sin_h, "cos_m": cos_m, "sin_m": sin_m, "seg": seg,
        "cc": cc.astype(BF16), "sc": sc.astype(BF16),
        "m1": jnp.concatenate([jnp.concatenate([cr, -sr], axis=1),
                               jnp.concatenate([-sr, -cr], axis=1)], axis=0).astype(BF16),
        "tw_cos": wide(jnp.cos(tw)), "tw_sin": wide(jnp.sin(tw)),
        "m3": jnp.concatenate([c64, s64], axis=1).astype(BF16),
        "w_ctx": jnp.concatenate([cx, -sx], axis=1).astype(BF16),
    }


_GQA_HEAD_ORDER = (0, 4, 1, 5, 2, 6, 3, 7)


def _layer_weights(l, w_in, gqa_q_norm, gqa_k_norm, mla_q_norm, mla_kv_norm, mla_w_q_up, mla_w_kv_up,
                   w_branch, w_gate, b_gate, w_out):
    wi = w_in[l]
    gq = wi[:, 2048:2560].reshape(D_MODEL, GQA_Q_HEADS, HEAD_DIM)[:, jnp.array(_GQA_HEAD_ORDER)]
    zeros = lambda n: jnp.zeros((D_MODEL, n), F32)
    w_in_p = jnp.concatenate([wi[:, :2048], gq.reshape(D_MODEL, 512), wi[:, 2560:3200],
                              zeros(MLA_NOPE), wi[:, 3200:3232], zeros(LANES - MLA_NOPE - MLA_ROPE)], axis=1)
    qu = mla_w_q_up[l].reshape(MLA_Q_RANK, MLA_HEADS, MLA_NOPE + MLA_ROPE)
    qu = jnp.pad(qu, ((0, 0), (0, 0), (0, LANES - MLA_NOPE - MLA_ROPE))).reshape(MLA_Q_RANK, MLA_HEADS * LANES)
    kvu = mla_w_kv_up[l].reshape(MLA_KV_RANK, MLA_HEADS, MLA_NOPE + MLA_V)
    ku = jnp.pad(kvu[:, :, :MLA_NOPE], ((0, 0), (0, 0), (0, LANES - MLA_NOPE))).reshape(MLA_KV_RANK, -1)
    vu = kvu[:, :, MLA_NOPE:].reshape(MLA_KV_RANK, MLA_HEADS * MLA_V)
    wb = w_branch[l]
    wb_gqa = wb[2].reshape(GQA_Q_HEADS, HEAD_DIM, D_MODEL)[jnp.array(_GQA_HEAD_ORDER)].reshape(BRANCH_W, D_MODEL)
    wb = jnp.stack([wb[0], wb[1], wb_gqa, wb[3]])
    return {
        "w_in": w_in_p.astype(BF16),
        "gq_norm": jnp.tile(gqa_q_norm[l], GQA_Q_HEADS).reshape(1, 512),
        "gk_norm": jnp.tile(gqa_k_norm[l], GQA_KV_HEADS).reshape(1, LANES),
        "mq_norm": mla_q_norm[l].reshape(1, MLA_Q_RANK),
        "mkv_norm": mla_kv_norm[l].reshape(1, MLA_KV_RANK),
        "w_q_up": qu.astype(BF16), "w_k_up": ku.astype(BF16), "w_v_up": vu.astype(BF16),
        "w_branch": wb.astype(BF16), "w_gate": w_gate[l].astype(BF16), "b_gate": b_gate[l],
        "w_out": w_out[l].astype(BF16),
    }


def kernel(x, c, ctx, c_ctx, w_mod, b_mod, mix_pre_norm, mix_post_norm, ffn_pre_norm, ffn_post_norm,
           w_in, diff_lambda, diff_subnorm, gqa_q_norm, gqa_k_norm, mla_q_norm, mla_kv_norm,
           mla_w_q_up, mla_w_kv_up, w_branch, w_gate, b_gate, w_out,
           dense_w_gate_up, dense_w_down, moe_router, moe_w_gate_up, moe_w_down):
    B, n_lat, d = x.shape
    n_ctx = ctx.shape[1]
    depth = w_mod.shape[0]
    assert d == D_MODEL and n_lat % TM == 0 and n_ctx % TM == 0 and n_lat % n_ctx == 0
    assert n_lat % GRID_W == 0
    T = n_lat + n_ctx
    n_lat_tiles, n_all_tiles = n_lat // TM, T // TM
    consts = _constants(n_lat, n_ctx)
    x_all = (x, ctx)
    c_rows = jnp.pad(jnp.concatenate([c, c_ctx[None]], axis=0), ((0, (-(B + 1)) % 8), (0, 0)))
    row = lambda v: v.reshape(1, -1)

    for layer in range(depth):
        last = layer == depth - 1
        n_tiles = n_lat_tiles if last else n_all_tiles
        lam_init = 0.8 - 0.6 * math.exp(-0.3 * layer)
        wts = _layer_weights(layer, w_in, gqa_q_norm, gqa_k_norm, mla_q_norm, mla_kv_norm, mla_w_q_up,
                             mla_w_kv_up, w_branch, w_gate, b_gate, w_out)
        mod_rows = _modulation(c_rows, w_mod[layer], b_mod[layer])
        mods = jnp.stack([mod_rows[:B].reshape(B, N_MOD, D_MODEL),
                          jnp.broadcast_to(mod_rows[B].reshape(1, N_MOD, D_MODEL), (B, N_MOD, D_MODEL))], axis=1)

        (h, dq, dk, dv, gq, gk, gv, mq, mk, mv, pq_lat, pq_ctx) = _project(
            x_all, mods, row(mix_pre_norm[layer]), wts, consts, n_lat_tiles)

        diff_extra = (diff_lambda[layer], diff_subnorm[layer].reshape(1, LANES))
        ys = []
        for mode, q, k, v in (("diff", dq, dk, dv), ("gqa", gq, gk, gv), ("mla", mq, mk, mv)):
            kw = dict(lam_init=lam_init, extra=diff_extra)
            tq = TQ if n_lat % TQ == 0 else TM
            y = _attention(mode, q, k, v, None, tq=tq, nq=n_lat // tq, q_blk0=0, tk=T, k_blk=0, **kw)
            if not last:
                y = _attention(mode, q, k, v, y, tq=TM, nq=n_ctx // TM, q_blk0=n_lat_tiles,
                               tk=n_ctx, k_blk=n_lat // n_ctx, **kw)
            ys.append(y)
        y_diff, y_gqa, y_mla = ys

        a_rows = _dft_rows(pq_lat, consts["m1"])
        y_f = _dft_cols(a_rows, consts["tw_cos"], consts["tw_sin"], consts["m3"]).reshape(B, n_lat, BRANCH_W)
        if not last:
            y_f = jnp.concatenate([y_f, _dft_ctx(pq_ctx, consts["w_ctx"])], axis=1)

        merge_args = (x_all, h, y_diff, y_f, y_gqa, y_mla, mods, wts, row(mix_post_norm[layer]),
                      n_lat_tiles, n_tiles)
        if layer % 2 == 0:
            x_all = _merge(*merge_args)
            x_all = _dense_ffn(x_all, mods, row(ffn_pre_norm[layer]), row(ffn_post_norm[layer]),
                               dense_w_gate_up[layer // 2].astype(BF16), dense_w_down[layer // 2].astype(BF16),
                               n_lat_tiles, n_tiles)
        else:
            wr = jnp.pad(moe_router[layer // 2], ((0, 0), (0, LANES - N_EXPERTS)))
            idx = jnp.arange(TM)
            tri = jnp.where(idx[:, None] > idx[None, :], 1.0, 0.0).astype(BF16)
            x_all, h_ffn, route, counts = _merge(*merge_args, router=(row(ffn_pre_norm[layer]), wr, tri))
            x_all = _routed_ffn(x_all, h_ffn, route, counts, mods, row(ffn_post_norm[layer]),
                                moe_w_gate_up[layer // 2].astype(BF16), moe_w_down[layer // 2].astype(BF16),
                                n_lat_tiles, n_tiles)
    return x_all[:, :n_lat]
```

```python
import functools
import math

import jax
import jax.numpy as jnp
from jax import lax
from jax.experimental import pallas as pl
from jax.experimental.pallas import tpu as pltpu

F32 = jnp.float32
BF16 = jnp.bfloat16

D_MODEL = 1024
GRID_W = 64
ROPE_BASE = 10000.0
EPS = 1e-6
N_MOD = 6
HEAD_DIM = 64
DIFF_HEADS = 4
GQA_Q_HEADS = 8
GQA_KV_HEADS = 2
MLA_HEADS = 8
MLA_Q_RANK = 256
MLA_KV_RANK = 128
MLA_NOPE = 64
MLA_ROPE = 32
MLA_V = 64
BRANCH_W = 512
FOURIER_GROUP = 128
FFN_DIM = 2816
N_EXPERTS = 8
EXPERT_DIM = 3584
LOG2E = math.log2(math.e)
QK_SCALE = HEAD_DIM ** -0.5 * LOG2E
MLA_SCALE = (MLA_NOPE + MLA_ROPE) ** -0.5 * LOG2E
IN_COLS_PADDED = 3328

LANES = 128
VMEM_LIMIT = 56 * 1024 * 1024

TM = 256
TQ = 1024


def _cparams(sem):
    return pltpu.CompilerParams(dimension_semantics=sem, vmem_limit_bytes=VMEM_LIMIT)


def _full_spec(shape):
    n = len(shape)
    return pl.BlockSpec(shape, lambda *_: (0,) * n)


def _split_bf16(x):
    hi = x.astype(BF16)
    lo = (x - hi.astype(F32)).astype(BF16)
    return hi, lo


def _dot(a, b):
    return jnp.dot(a, b, preferred_element_type=F32)


def _dot3(a, w):
    ah, al = _split_bf16(a)
    wh, wl = _split_bf16(w)
    return _dot(ah, wh) + _dot(ah, wl) + _dot(al, wh)


def _rms(x, gain):
    ms = jnp.mean(x * x, axis=-1, keepdims=True)
    return x * lax.rsqrt(ms + EPS) * gain


def _mod_kernel(c_ref, w_ref, b_ref, o_ref):
    c = c_ref[...]
    a = c * jax.nn.sigmoid(c)
    o_ref[...] = _dot3(a, w_ref[...]) + b_ref[...]


def _modulation(c_rows, w_mod_l, b_mod_l):
    rows = c_rows.shape[0]
    n = w_mod_l.shape[1]
    tn = 512
    return pl.pallas_call(
        _mod_kernel,
        grid=(n // tn,),
        in_specs=[pl.BlockSpec((rows, D_MODEL), lambda j: (0, 0)),
                  pl.BlockSpec((D_MODEL, tn), lambda j: (0, j)),
                  pl.BlockSpec((1, tn), lambda j: (0, j))],
        out_specs=pl.BlockSpec((rows, tn), lambda j: (0, j)),
        out_shape=jax.ShapeDtypeStruct((rows, n), F32),
        compiler_params=_cparams(("arbitrary",)),
        name="modulation",
    )(c_rows, w_mod_l, b_mod_l.reshape(1, n))


def _rope(x, cos, sin_signed, half):
    lane = lax.broadcasted_iota(jnp.int32, x.shape, 1)
    first = (lane % (2 * half)) < half
    up = pltpu.roll(x, LANES - half, 1)
    down = pltpu.roll(x, half, 1)
    return x * cos + jnp.where(first, up, down) * sin_signed


def _stream_specs(xs, n_lat_tiles):
    if not isinstance(xs, tuple):
        return [pl.BlockSpec((None, TM, D_MODEL), lambda b, t: (b, t, 0))], [xs]
    return ([pl.BlockSpec((None, TM, D_MODEL), lambda b, t: (b, jnp.minimum(t, n_lat_tiles - 1), 0)),
             pl.BlockSpec((None, TM, D_MODEL), lambda b, t: (b, jnp.maximum(t - n_lat_tiles, 0), 0))], list(xs))


def _stream_tile(x_refs, n_lat_tiles):
    if len(x_refs) == 1:
        return x_refs[0][...]
    return jnp.where(pl.program_id(1) < n_lat_tiles, x_refs[0][...], x_refs[1][...])


def _stream_dims(xs):
    if isinstance(xs, tuple):
        return xs[0].shape[0], xs[0].shape[1] + xs[1].shape[1]
    return xs.shape[0], xs.shape[1]


def _proj_kernel(n_lat_tiles, n_x, *refs):
    x_refs, refs = refs[:n_x], refs[n_x:]
    (mod_ref, gpre_ref, win_ref, gqn_ref, gkn_ref, mqn_ref, mkvn_ref,
     wqup_ref, wkup_ref, wvup_ref, seg_ref, cc_ref, sc_ref,
     cosh_ref, sinh_ref, cosm_ref, sinm_ref,
     h_ref, dq_ref, dk_ref, dv_ref, gq_ref, gk_ref, gv_ref, mq_ref, mk_ref, mv_ref, pq_ref, pqc_ref) = refs
    x = _stream_tile(x_refs, n_lat_tiles)
    mod = mod_ref[...]
    h = _rms(x, gpre_ref[...]) * (1.0 + mod[1:2]) + mod[0:1]
    hb = h.astype(BF16)
    h_ref[...] = hb
    z = _dot(hb, win_ref[...])

    def zcols(lo, hi):
        return z[:, lo:hi]
    cosh, sinh = cosh_ref[...], sinh_ref[...]
    cosm, sinm = cosm_ref[...], sinm_ref[...]
    seg = seg_ref[...]

    def head_rms(v, gain, w):
        sq = v * v
        hi, lo = _split_bf16(sq)
        ms = _dot(hi, seg[:w, :w]) + _dot(lo, seg[:w, :w])
        return v * lax.rsqrt(ms + EPS) * gain

    for j in range(4):
        c0 = j * LANES
        dq_ref[:, c0:c0 + LANES] = (_rope(zcols(c0, c0 + LANES), cosh, sinh, 16) * QK_SCALE).astype(BF16)
        dk_ref[:, c0:c0 + LANES] = _rope(zcols(512 + c0, 512 + c0 + LANES), cosh, sinh, 16).astype(BF16)
    dv_ref[...] = zcols(1024, 1536).astype(BF16)
    fparts = []
    for g in range(4):
        c0 = 1536 + g * LANES
        fb = zcols(c0, c0 + LANES).astype(BF16)
        fparts.append((_dot(fb, cc_ref[...]).astype(BF16), _dot(fb, sc_ref[...]).astype(BF16)))
    is_lat = pl.program_id(1) < n_lat_tiles

    def put(ref):
        for g, (pc, ps) in enumerate(fparts):
            ref[0, :, g * LANES:(g + 1) * LANES] = pc
            ref[1, :, g * LANES:(g + 1) * LANES] = ps

    pl.when(is_lat)(lambda: put(pq_ref))
    pl.when(jnp.logical_not(is_lat))(lambda: put(pqc_ref))
    gq = head_rms(zcols(2048, 2560), gqn_ref[...], 512)
    for j in range(4):
        c0 = j * LANES
        gq_ref[:, c0:c0 + LANES] = (_rope(gq[:, c0:c0 + LANES], cosh, sinh, 16) * QK_SCALE).astype(BF16)
    gk = head_rms(zcols(2560, 2688), gkn_ref[...], LANES)
    gk_ref[...] = _rope(gk, cosh, sinh, 16).astype(BF16)
    gv_ref[...] = zcols(2688, 2816).astype(BF16)
    cq = _rms(zcols(2816, 3072), mqn_ref[...]).astype(BF16)
    mq = _dot(cq, wqup_ref[...])
    ckv = _rms(zcols(3072, 3200), mkvn_ref[...]).astype(BF16)
    mk = _dot(ckv, wkup_ref[...])
    mv_ref[...] = _dot(ckv, wvup_ref[...]).astype(BF16)
    kr = _rope(zcols(3200, 3328), cosm, sinm, 8)
    for hh in range(MLA_HEADS):
        c0 = hh * LANES
        mq_ref[:, c0:c0 + LANES] = (_rope(mq[:, c0:c0 + LANES], cosm, sinm, 8) * MLA_SCALE).astype(BF16)
        mk_ref[:, c0:c0 + LANES] = (mk[:, c0:c0 + LANES] + kr).astype(BF16)


def _project(x_all, mods, gpre, wts, consts, n_lat_tiles):
    B, T = _stream_dims(x_all)
    nt = T // TM
    tok = lambda w: pl.BlockSpec((None, TM, w), lambda b, t: (b, t, 0))
    tab = pl.BlockSpec((TM, LANES), lambda b, t: (t, 0))
    x_specs, x_args = _stream_specs(x_all, n_lat_tiles)
    in_specs = x_specs + [
        pl.BlockSpec((None, None, N_MOD, D_MODEL), lambda b, t: (b, jnp.where(t >= n_lat_tiles, 1, 0), 0, 0)),
        _full_spec((1, D_MODEL)),
        _full_spec((D_MODEL, IN_COLS_PADDED)),
        _full_spec((1, 512)), _full_spec((1, LANES)), _full_spec((1, MLA_Q_RANK)), _full_spec((1, MLA_KV_RANK)),
        _full_spec((MLA_Q_RANK, MLA_HEADS * LANES)),
        _full_spec((MLA_KV_RANK, MLA_HEADS * LANES)),
        _full_spec((MLA_KV_RANK, MLA_HEADS * MLA_V)),
        _full_spec((512, 512)), _full_spec((LANES, LANES)), _full_spec((LANES, LANES)),
        tab, tab, tab, tab,
    ]
    widths = [D_MODEL, 512, 512, 512, 512, LANES, LANES, MLA_HEADS * LANES, MLA_HEADS * LANES, 512]
    out_specs = [tok(w) for w in widths]
    out_shape = [jax.ShapeDtypeStruct((B, T, w), BF16) for w in widths]
    out_specs.append(pl.BlockSpec((2, None, TM, 512), lambda b, t: (0, b, jnp.minimum(t, n_lat_tiles - 1), 0)))
    out_shape.append(jax.ShapeDtypeStruct((2, B, n_lat_tiles * TM, 512), BF16))
    out_specs.append(pl.BlockSpec((2, None, TM, 512), lambda b, t: (0, b, jnp.maximum(t - n_lat_tiles, 0), 0)))
    out_shape.append(jax.ShapeDtypeStruct((2, B, T - n_lat_tiles * TM, 512), BF16))
    return pl.pallas_call(
        functools.partial(_proj_kernel, n_lat_tiles, len(x_args)),
        grid=(B, nt),
        in_specs=in_specs,
        out_specs=out_specs,
        out_shape=out_shape,
        compiler_params=_cparams(("arbitrary", "arbitrary")),
        name="project",
    )(*x_args, mods, gpre, wts["w_in"], wts["gq_norm"], wts["gk_norm"], wts["mq_norm"], wts["mkv_norm"],
      wts["w_q_up"], wts["w_k_up"], wts["w_v_up"], consts["seg"], consts["cc"], consts["sc"],
      consts["cos_h"], consts["sin_h"], consts["cos_m"], consts["sin_m"])


N_MAPS = 8
MAPS_PER_STEP = 2


def _map_plan(mode, m):
    if mode == "diff":
        return m // 2, m % 2, m // 2, m // 2
    if mode == "gqa":
        return m // 2, m % 2, 0, 0
    return m, None, m, m // 2


def _attn_kernel(mode, lam_init, q_ref, k_ref, v_ref, *rest):
    o_ref, vx_ref = rest[-2:]
    if mode == "diff":
        lam_ref, sub_ref = rest[:2]
    nt = (((1,), (1,)), ((), ()))
    lane = lax.broadcasted_iota(jnp.int32, (1, LANES), 1)
    low = lane < HEAD_DIM

    @pl.when(pl.program_id(2) == 0)
    def _():
        for g in range(vx_ref.shape[0]):
            vx_ref[g, :, :LANES] = v_ref[:, g * LANES:(g + 1) * LANES]
            vx_ref[g, :, LANES:] = jnp.ones((vx_ref.shape[1], LANES), BF16)

    def scores(m):
        qblk, half, kblk, _ = _map_plan(mode, m)
        q = q_ref[:, qblk * LANES:(qblk + 1) * LANES]
        if half is not None:
            q = jnp.where(low if half == 0 else jnp.logical_not(low), q, jnp.zeros_like(q))
        return lax.dot_general(q, k_ref[:, kblk * LANES:(kblk + 1) * LANES], nt, preferred_element_type=F32)

    def numerator(s):
        return jnp.exp2(s - jnp.max(s, axis=-1, keepdims=True)).astype(BF16)

    def values(m, p):
        e = _dot(p, vx_ref[_map_plan(mode, m)[3]])
        return e[:, :LANES] / e[:, LANES:]

    s, pl_, y = {}, {}, {}
    for t in range(MAPS_PER_STEP + 2):
        if t < MAPS_PER_STEP:
            s[t] = scores(t)
        if 0 <= t - 1 < MAPS_PER_STEP:
            pl_[t - 1] = numerator(s.pop(t - 1))
        if 0 <= t - 2:
            y[t - 2] = values(t - 2, pl_.pop(t - 2))

    if mode == "diff":
        dl = lam_ref[...]
        lam = (jnp.exp(jnp.sum(dl[0:1] * dl[1:2], axis=-1, keepdims=True))
               - jnp.exp(jnp.sum(dl[2:3] * dl[3:4], axis=-1, keepdims=True)) + lam_init)
    for j in range(MAPS_PER_STEP // 2):
        if mode == "diff":
            out = _rms(y[2 * j] - lam * y[2 * j + 1], sub_ref[...]) * (1.0 - lam_init)
        else:
            out = jnp.where(low, y[2 * j], y[2 * j + 1])
        o_ref[:, j * LANES:(j + 1) * LANES] = out.astype(BF16)


def _attention(mode, q, k, v, y_prev, *, tq, nq, q_blk0, tk, k_blk, lam_init=0.0, extra=()):
    B, T, _ = q.shape
    n_steps = N_MAPS // MAPS_PER_STEP
    wq, wk, wv = q.shape[2] // n_steps, max(k.shape[2] // n_steps, LANES), max(v.shape[2] // n_steps, LANES)
    kv_blk = (lambda g: 0) if mode == "gqa" else (lambda g: g)
    in_specs = [
        pl.BlockSpec((None, tq, wq), lambda b, g, i: (b, i + q_blk0, g)),
        pl.BlockSpec((None, tk, wk), lambda b, g, i: (b, k_blk, kv_blk(g))),
        pl.BlockSpec((None, tk, wv), lambda b, g, i: (b, k_blk, kv_blk(g))),
    ]
    args = [q, k, v]
    if mode == "diff":
        in_specs += [_full_spec((4, HEAD_DIM)), _full_spec((1, LANES))]
        args += list(extra)
    aliases = {}
    if y_prev is not None:
        in_specs.append(pl.BlockSpec(memory_space=pl.ANY))
        args.append(y_prev)
        aliases = {len(args) - 1: 0}
    wo = BRANCH_W // n_steps
    return pl.pallas_call(
        functools.partial(_attn_kernel, mode, lam_init),
        grid=(B, n_steps, nq),
        in_specs=in_specs,
        out_specs=pl.BlockSpec((None, tq, wo), lambda b, g, i: (b, i + q_blk0, g)),
        out_shape=jax.ShapeDtypeStruct((B, T, BRANCH_W), BF16),
        input_output_aliases=aliases,
        scratch_shapes=[pltpu.VMEM((wv // LANES, tk, 2 * LANES), BF16)],
        compiler_params=_cparams(("arbitrary", "arbitrary", "arbitrary")),
        name="attention_" + mode,
    )(*args)


DFT_COLS_PER_STEP = 8


def _dft_rows_kernel(pq_ref, m1_ref, a_ref):
    r = pq_ref.shape[1]
    for n2 in range(pq_ref.shape[2]):
        zin = jnp.concatenate([pq_ref[0, :, n2, :], pq_ref[1, :, n2, :]], axis=0)
        a = _dot(m1_ref[...], zin).astype(BF16)
        a_ref[0, :, n2, :] = a[:r]
        a_ref[1, :, n2, :] = a[r:]


def _dft_rows(pq_lat, m1):
    _, B, n_lat, w = pq_lat.shape
    r = n_lat // GRID_W
    nb = DFT_COLS_PER_STEP
    return pl.pallas_call(
        _dft_rows_kernel,
        grid=(B, GRID_W // nb),
        in_specs=[pl.BlockSpec((2, None, r, nb, w), lambda b, j: (0, b, 0, j, 0)), _full_spec((2 * r, 2 * r))],
        out_specs=pl.BlockSpec((None, 2, r, nb, w), lambda b, j: (b, 0, 0, j, 0)),
        out_shape=jax.ShapeDtypeStruct((B, 2, r, GRID_W, w), BF16),
        compiler_params=_cparams(("arbitrary", "arbitrary")),
        name="dft_rows",
    )(pq_lat.reshape(2, B, r, GRID_W, w), m1)


def _dft_cols_kernel(a_ref, ct_ref, st_ref, m3_ref, o_ref):
    kb = a_ref.shape[1]
    w = a_ref.shape[3]
    for j in range(kb):
        ar = a_ref[0, j].astype(F32)
        ai = a_ref[1, j].astype(F32)
        ct = jnp.concatenate([ct_ref[j]] * (w // LANES), axis=1)
        st = jnp.concatenate([st_ref[j]] * (w // LANES), axis=1)
        b = jnp.concatenate([ar * ct + ai * st, ai * ct - ar * st], axis=0).astype(BF16)
        o_ref[:, j, :] = _dot(m3_ref[...], b).astype(BF16)


def _dft_cols(a, ct, st, m3):
    B, _, r, _, w = a.shape
    kb = DFT_COLS_PER_STEP
    return pl.pallas_call(
        _dft_cols_kernel,
        grid=(B, r // kb),
        in_specs=[pl.BlockSpec((None, 2, kb, GRID_W, w), lambda b, k: (b, 0, k, 0, 0)),
                  pl.BlockSpec((kb, GRID_W, LANES), lambda b, k: (k, 0, 0)),
                  pl.BlockSpec((kb, GRID_W, LANES), lambda b, k: (k, 0, 0)),
                  _full_spec((GRID_W, 2 * GRID_W))],
        out_specs=pl.BlockSpec((None, GRID_W, kb, w), lambda b, k: (b, 0, k, 0)),
        out_shape=jax.ShapeDtypeStruct((B, GRID_W, r, w), BF16),
        compiler_params=_cparams(("arbitrary", "arbitrary")),
        name="dft_cols",
    )(a, ct, st, m3)


def _dft_ctx_kernel(pq_ref, w_ref, o_ref):
    zin = jnp.concatenate([pq_ref[0], pq_ref[1]], axis=0)
    o_ref[...] = _dot(w_ref[...], zin).astype(BF16)


def _dft_ctx(pq_ctx, w_ctx):
    _, B, n_ctx, w = pq_ctx.shape
    return pl.pallas_call(
        _dft_ctx_kernel,
        grid=(B,),
        in_specs=[pl.BlockSpec((2, None, n_ctx, w), lambda b: (0, b, 0, 0)), _full_spec((n_ctx, 2 * n_ctx))],
        out_specs=pl.BlockSpec((None, n_ctx, w), lambda b: (b, 0, 0)),
        out_shape=jax.ShapeDtypeStruct((B, n_ctx, w), BF16),
        compiler_params=_cparams(("arbitrary",)),
        name="dft_ctx",
    )(pq_ctx, w_ctx)


def _merge_kernel(n_lat_tiles, n_x, *refs):
    x_refs, refs = refs[:n_x], refs[n_x:]
    h_ref, y0_ref, y1_ref, y2_ref, y3_ref, mod_ref, wg_ref, bg_ref, wb_ref, wo_ref, gpost_ref = refs[:11]
    rest = refs[11:]
    h = h_ref[...]
    ys = (y0_ref, y1_ref, y2_ref, y3_ref)
    acc = None
    for i in range(4):
        gate = jax.nn.sigmoid(_dot(h, wg_ref[i]) + bg_ref[i:i + 1])
        term = gate * _dot(ys[i][...], wb_ref[i])
        acc = term if acc is None else acc + term
    m = _dot(acc.astype(BF16), wo_ref[...])
    mod = mod_ref[...]
    x = _stream_tile(x_refs, n_lat_tiles) + mod[2:3] * _rms(m, gpost_ref[...])
    o_ref = rest[3] if len(rest) > 1 else rest[0]
    o_ref[...] = x
    if len(rest) > 1:
        gffn_ref, wr_ref, tri_ref, _, hf_ref, route_ref, cnt_ref, carry_ref = rest
        _route_tile(x, mod, gffn_ref, wr_ref, tri_ref, hf_ref, route_ref, cnt_ref, carry_ref)


def _merge(x_all, h, y_diff, y_f, y_gqa, y_mla, mods, wts, gpost, n_lat_tiles, n_tiles, router=None):
    B, _ = _stream_dims(x_all)
    tok = lambda w: pl.BlockSpec((None, TM, w), lambda b, t: (b, t, 0))
    x_specs, x_args = _stream_specs(x_all, n_lat_tiles)
    in_specs = x_specs + [
        tok(D_MODEL), tok(BRANCH_W), tok(BRANCH_W), tok(BRANCH_W), tok(BRANCH_W),
        pl.BlockSpec((None, None, N_MOD, D_MODEL), lambda b, t: (b, jnp.where(t >= n_lat_tiles, 1, 0), 0, 0)),
        _full_spec((4, D_MODEL, D_MODEL)), _full_spec((4, D_MODEL)), _full_spec((4, BRANCH_W, D_MODEL)),
        _full_spec((D_MODEL, D_MODEL)), _full_spec((1, D_MODEL)),
    ]
    args = x_args + [h, y_diff, y_f, y_gqa, y_mla, mods, wts["w_gate"], wts["b_gate"], wts["w_branch"],
                     wts["w_out"], gpost]
    out_specs = [tok(D_MODEL)]
    out_shape = [jax.ShapeDtypeStruct((B, n_tiles * TM, D_MODEL), F32)]
    scratch = []
    if router is not None:
        in_specs += [_full_spec((1, D_MODEL)), _full_spec((D_MODEL, LANES)), _full_spec((TM, TM))]
        args += list(router)
        out_specs += [tok(D_MODEL), tok(LANES), _full_spec((1, LANES))]
        out_shape += [jax.ShapeDtypeStruct((B, n_tiles * TM, D_MODEL), F32),
                      jax.ShapeDtypeStruct((B, n_tiles * TM, LANES), F32),
                      jax.ShapeDtypeStruct((1, LANES), F32)]
        scratch = [pltpu.VMEM((1, LANES), F32)]
    out = pl.pallas_call(
        functools.partial(_merge_kernel, n_lat_tiles, len(x_args)),
        grid=(B, n_tiles),
        in_specs=in_specs,
        out_specs=out_specs,
        out_shape=out_shape,
        scratch_shapes=scratch,
        compiler_params=_cparams(("arbitrary", "arbitrary")),
        name="merge",
    )(*args)
    return out[0] if router is None else out


def _dense_ffn_kernel(nf, tf, x_ref, mod_ref, gpre_ref, gpost_ref, wgu_ref, wd_ref, o_ref):
    x = x_ref[...]
    mod = mod_ref[...]
    hb = (_rms(x, gpre_ref[...]) * (1.0 + mod[4:5]) + mod[3:4]).astype(BF16)
    acc = None
    for f in range(nf):
        a = _dot(hb, wgu_ref[:, f * tf:(f + 1) * tf])
        u = _dot(hb, wgu_ref[:, FFN_DIM + f * tf:FFN_DIM + (f + 1) * tf])
        act = (a * jax.nn.sigmoid(a) * u).astype(BF16)
        part = _dot(act, wd_ref[f * tf:(f + 1) * tf, :])
        acc = part if acc is None else acc + part
    o_ref[...] = x + mod[5:6] * _rms(acc, gpost_ref[...])


def _dense_ffn(x_all, mods, gpre, gpost, wgu, wd, n_lat_tiles, n_tiles):
    B = x_all.shape[0]
    nf, tf = 1, FFN_DIM
    tok = pl.BlockSpec((None, TM, D_MODEL), lambda b, t: (b, t, 0))
    return pl.pallas_call(
        functools.partial(_dense_ffn_kernel, nf, tf),
        grid=(B, n_tiles),
        in_specs=[tok,
                  pl.BlockSpec((None, None, N_MOD, D_MODEL),
                               lambda b, t: (b, jnp.where(t >= n_lat_tiles, 1, 0), 0, 0)),
                  _full_spec((1, D_MODEL)), _full_spec((1, D_MODEL)),
                  _full_spec((D_MODEL, 2 * FFN_DIM)), _full_spec((FFN_DIM, D_MODEL))],
        out_specs=tok,
        out_shape=jax.ShapeDtypeStruct((B, n_tiles * TM, D_MODEL), F32),
        compiler_params=_cparams(("arbitrary", "arbitrary")),
        name="dense_ffn",
    )(x_all, mods, gpre, gpost, wgu, wd)


TE = 512
R_E1, R_E2, R_G1, R_G2, R_RANK1, R_RANK2 = range(6)


def _route_tile(x, mod, gpre_ref, wr_ref, tri_ref, h_ref, route_ref, cnt_ref, carry_ref):
    first = jnp.logical_and(pl.program_id(0) == 0, pl.program_id(1) == 0)

    @pl.when(first)
    def _():
        carry_ref[...] = jnp.zeros_like(carry_ref)

    h = _rms(x, gpre_ref[...]) * (1.0 + mod[4:5]) + mod[3:4]
    h_ref[...] = h
    logits = _dot3(h, wr_ref[...])
    lane = lax.broadcasted_iota(jnp.int32, logits.shape, 1)
    neg = jnp.float32(-jnp.inf)
    l1 = jnp.where(lane < N_EXPERTS, logits, neg)
    m1 = jnp.max(l1, axis=-1, keepdims=True)
    i1 = jnp.min(jnp.where(l1 == m1, lane, LANES), axis=-1, keepdims=True)
    l2 = jnp.where(lane == i1, neg, l1)
    m2 = jnp.max(l2, axis=-1, keepdims=True)
    i2 = jnp.min(jnp.where(l2 == m2, lane, LANES), axis=-1, keepdims=True)
    e2 = jnp.exp(m2 - m1)
    g1 = 1.0 / (1.0 + e2)
    g2 = e2 / (1.0 + e2)
    sel1, sel2 = lane == i1, lane == i2
    chosen = jnp.where(jnp.logical_or(sel1, sel2), 1.0, 0.0)
    before = _dot(tri_ref[...], chosen.astype(BF16)) + carry_ref[...]
    r1 = jnp.sum(jnp.where(sel1, before, 0.0), axis=-1, keepdims=True)
    r2 = jnp.sum(jnp.where(sel2, before, 0.0), axis=-1, keepdims=True)
    carry_ref[...] += jnp.sum(chosen, axis=0, keepdims=True)
    cnt_ref[...] = carry_ref[...]
    rec = jnp.zeros_like(logits)
    for col, val in ((R_E1, i1.astype(F32)), (R_E2, i2.astype(F32)), (R_G1, g1), (R_G2, g2),
                     (R_RANK1, r1), (R_RANK2, r2)):
        rec = jnp.where(lane == col, val, rec)
    route_ref[...] = rec


def _dispatch_kernel(last_ref, pos_ref, h_ref, xs_ref, zero_ref, stage_ref, sems):
    i = pl.program_id(0)
    n = h_ref.shape[0]

    @pl.when(i == 0)
    def _():
        zero_ref[...] = jnp.zeros_like(zero_ref)
        for e in range(N_EXPERTS):
            @pl.when(last_ref[e] >= 0)
            def _():
                fill = pltpu.make_async_copy(zero_ref, xs_ref.at[pl.ds(pl.multiple_of(last_ref[e], TE), TE)],
                                             sems.at[0])
                fill.start()
                fill.wait()

    cur = i % 2
    stage_ref[cur] = h_ref[...]

    def start(r, carry):
        for slot in range(2):
            pltpu.make_async_copy(stage_ref.at[cur, pl.ds(r, 1)], xs_ref.at[pl.ds(pos_ref[0, 2 * r + slot], 1)],
                                  sems.at[cur]).start()
        return carry

    lax.fori_loop(0, n, start, 0, unroll=8)

    def drain(half):
        for _ in range(2):
            pltpu.make_async_copy(stage_ref.at[half], xs_ref.at[pl.ds(0, n)], sems.at[half]).wait()

    pl.when(i > 0)(lambda: drain(1 - cur))
    pl.when(i == pl.num_programs(0) - 1)(lambda: drain(cur))


def _dispatch(h, pos, last_tile_row, n_rows):
    n = h.shape[0]
    grid_spec = pltpu.PrefetchScalarGridSpec(
        num_scalar_prefetch=1,
        grid=(n // TM,),
        in_specs=[pl.BlockSpec((None, 1, 2 * TM), lambda i, last: (i, 0, 0), memory_space=pltpu.SMEM),
                  pl.BlockSpec((TM, D_MODEL), lambda i, last: (i, 0))],
        out_specs=pl.BlockSpec(memory_space=pl.ANY),
        scratch_shapes=[pltpu.VMEM((TE, D_MODEL), F32), pltpu.VMEM((2, TM, D_MODEL), F32),
                        pltpu.SemaphoreType.DMA((2,))],
    )
    return pl.pallas_call(
        _dispatch_kernel,
        grid_spec=grid_spec,
        out_shape=jax.ShapeDtypeStruct((n_rows, D_MODEL), F32),
        compiler_params=_cparams(("arbitrary",)),
        name="moe_dispatch",
    )(last_tile_row, pos.reshape(n // TM, 1, 2 * TM), h)


def _experts_kernel(te_ref, nv_ref, xs_ref, wa_ref, wu_ref, wd_ref, ys_ref):
    del te_ref

    @pl.when(pl.program_id(0) < nv_ref[0])
    def _():
        x = xs_ref[...].astype(BF16)
        a = _dot(x, wa_ref[...])
        u = _dot(x, wu_ref[...])
        ys_ref[...] = _dot((a * jax.nn.sigmoid(a) * u).astype(BF16), wd_ref[...])


def _experts(xs, tile_expert, n_valid, wgu, wd):
    n_rows = xs.shape[0]
    row_blk = lambda i, te, nv: (jnp.minimum(i, nv[0] - 1), 0)
    resident = pl.Buffered(1)
    grid_spec = pltpu.PrefetchScalarGridSpec(
        num_scalar_prefetch=2,
        grid=(n_rows // TE,),
        in_specs=[pl.BlockSpec((TE, D_MODEL), row_blk),
                  pl.BlockSpec((None, D_MODEL, EXPERT_DIM), lambda i, te, nv: (te[i], 0, 0), pipeline_mode=resident),
                  pl.BlockSpec((None, D_MODEL, EXPERT_DIM), lambda i, te, nv: (te[i], 0, 1), pipeline_mode=resident),
                  pl.BlockSpec((None, EXPERT_DIM, D_MODEL), lambda i, te, nv: (te[i], 0, 0), pipeline_mode=resident)],
        out_specs=pl.BlockSpec((TE, D_MODEL), row_blk),
    )
    return pl.pallas_call(
        _experts_kernel,
        grid_spec=grid_spec,
        out_shape=jax.ShapeDtypeStruct((n_rows, D_MODEL), F32),
        compiler_params=_cparams(("arbitrary",)),
        name="moe_experts",
    )(tile_expert, n_valid, xs, wgu, wgu, wd)


def _combine_kernel(pos_ref, nxt_ref, x_ref, route_ref, mod_ref, gpost_ref, ys_ref, o_ref, buf_ref, sems):
    i = pl.program_id(0)
    n = x_ref.shape[0]

    def gather(idx_ref, half):
        def start(r, carry):
            for slot in range(2):
                pltpu.make_async_copy(ys_ref.at[pl.ds(idx_ref[0, 2 * r + slot], 1)],
                                      buf_ref.at[half, slot, pl.ds(r, 1)], sems.at[half]).start()
            return carry
        lax.fori_loop(0, n, start, 0, unroll=8)

    cur = i % 2

    @pl.when(i == 0)
    def _():
        gather(pos_ref, 0)

    @pl.when(i + 1 < pl.num_programs(0))
    def _():
        gather(nxt_ref, 1 - cur)

    for slot in range(2):
        pltpu.make_async_copy(ys_ref.at[pl.ds(0, n)], buf_ref.at[cur, slot], sems.at[cur]).wait()
    route = route_ref[...]
    y = route[:, R_G1:R_G1 + 1] * buf_ref[cur, 0] + route[:, R_G2:R_G2 + 1] * buf_ref[cur, 1]
    mod = mod_ref[...]
    o_ref[...] = x_ref[...] + mod[5:6] * _rms(y, gpost_ref[...])


def _combine(x_all, route, pos, ys, mods, gpost, n_lat_tiles, n_tiles):
    B = x_all.shape[0]
    n_steps = B * n_tiles
    tok = lambda w: pl.BlockSpec((None, TM, w), lambda i: (i // n_tiles, i % n_tiles, 0))
    idx = lambda ahead: pl.BlockSpec((None, 1, 2 * TM), lambda i: (jnp.minimum(i + ahead, n_steps - 1), 0, 0),
                                     memory_space=pltpu.SMEM)
    pos_tiles = pos.reshape(n_steps, 1, 2 * TM)
    return pl.pallas_call(
        _combine_kernel,
        grid=(n_steps,),
        in_specs=[idx(0), idx(1), tok(D_MODEL), tok(LANES),
                  pl.BlockSpec((None, None, N_MOD, D_MODEL),
                               lambda i: (i // n_tiles, jnp.where(i % n_tiles >= n_lat_tiles, 1, 0), 0, 0)),
                  _full_spec((1, D_MODEL)),
                  pl.BlockSpec(memory_space=pl.ANY)],
        out_specs=tok(D_MODEL),
        out_shape=jax.ShapeDtypeStruct((B, n_tiles * TM, D_MODEL), F32),
        scratch_shapes=[pltpu.VMEM((2, 2, TM, D_MODEL), F32), pltpu.SemaphoreType.DMA((2,))],
        compiler_params=_cparams(("arbitrary",)),
        name="moe_combine",
    )(pos_tiles, pos_tiles, x_all, route, mods, gpost, ys)


def _routed_ffn(x_all, h, route, counts, mods, gpost, wgu, wd, n_lat_tiles, n_tiles):
    B = x_all.shape[0]
    n_tok = B * n_tiles * TM
    rec = route.reshape(n_tok, LANES)
    expert = rec[:, R_E1:R_E2 + 1].astype(jnp.int32)
    rank = rec[:, R_RANK1:R_RANK2 + 1].astype(jnp.int32)
    tiles_per_expert = (counts[0, :N_EXPERTS].astype(jnp.int32) + TE - 1) // TE
    tile_end = jnp.cumsum(tiles_per_expert)
    n_valid = tile_end[-1:]
    pos = (tile_end - tiles_per_expert)[expert] * TE + rank
    n_tiles_max = (2 * n_tok) // TE + N_EXPERTS
    tile_id = jnp.minimum(jnp.arange(n_tiles_max, dtype=jnp.int32), n_valid - 1)
    tile_expert = jnp.minimum(jnp.searchsorted(tile_end, tile_id, side="right"), N_EXPERTS - 1).astype(jnp.int32)
    last_tile_row = jnp.where(tiles_per_expert > 0, (tile_end - 1) * TE, -1).astype(jnp.int32)
    xs = _dispatch(h.reshape(n_tok, D_MODEL), pos, last_tile_row, n_tiles_max * TE)
    ys = _experts(xs, tile_expert, n_valid.astype(jnp.int32), wgu, wd)
    return _combine(x_all, route, pos, ys, mods, gpost, n_lat_tiles, n_tiles)


def _rope_angles(n_lat, rot_dim):
    rows = n_lat // GRID_W
    row = jnp.repeat(jnp.arange(rows, dtype=F32), GRID_W)
    col = jnp.tile(jnp.arange(GRID_W, dtype=F32), rows)
    axis_dim = rot_dim // 2
    inv_freq = 1.0 / (ROPE_BASE ** (jnp.arange(0, axis_dim, 2, dtype=F32) / axis_dim))
    ar = row[:, None] * inv_freq
    ac = col[:, None] * inv_freq
    return jnp.concatenate([ar, ar, ac, ac], axis=-1)


def _constants(n_lat, n_ctx):
    lane = jnp.arange(LANES)
    ang = _rope_angles(n_lat, HEAD_DIM)
    cos_h = jnp.tile(jnp.cos(ang), (1, 2))
    sin_h = jnp.tile(jnp.sin(ang), (1, 2)) * jnp.where((lane % 32) < 16, -1.0, 1.0)
    cos_h = jnp.concatenate([cos_h, jnp.ones((n_ctx, LANES), F32)], axis=0)
    sin_h = jnp.concatenate([sin_h, jnp.zeros((n_ctx, LANES), F32)], axis=0)
    angm = _rope_angles(n_lat, MLA_ROPE)
    pad_l = jnp.zeros((n_lat, MLA_NOPE), F32)
    pad_r = jnp.zeros((n_lat, LANES - MLA_NOPE - MLA_ROPE), F32)
    cos_m = jnp.concatenate([pad_l + 1.0, jnp.cos(angm), pad_r + 1.0], axis=1)
    sin_m = jnp.concatenate([pad_l, jnp.sin(angm), pad_r], axis=1) * jnp.where((lane % 16) < 8, -1.0, 1.0)
    cos_m = jnp.concatenate([cos_m, jnp.ones((n_ctx, LANES), F32)], axis=0)
    sin_m = jnp.concatenate([sin_m, jnp.zeros((n_ctx, LANES), F32)], axis=0)
    idx = jnp.arange(512)
    seg = jnp.where((idx[:, None] // HEAD_DIM) == (idx[None, :] // HEAD_DIM), 1.0 / HEAD_DIM, 0.0).astype(BF16)

    def dft(n):
        i = jnp.arange(n, dtype=jnp.int32)
        a = ((i[:, None] * i[None, :]) % n).astype(F32) * (2.0 * math.pi / n)
        s = n ** -0.5
        return jnp.cos(a) * s, jnp.sin(a) * s

    cc, sc = dft(FOURIER_GROUP)
    cx, sx = dft(n_ctx)
    r = n_lat // GRID_W
    cr, sr = (t * r ** 0.5 for t in dft(r))
    c64, s64 = (t * GRID_W ** 0.5 * n_lat ** -0.5 for t in dft(GRID_W))
    k1 = jnp.arange(r, dtype=jnp.int32)[:, None]
    n2 = jnp.arange(GRID_W, dtype=jnp.int32)[None, :]
    tw = ((k1 * n2) % n_lat).astype(F32) * (2.0 * math.pi / n_lat)
    wide = lambda t: jnp.broadcast_to(t[:, :, None], (r, GRID_W, LANES))
    return {
        "cos_h": cos_h, "sin_h": sin_h, "cos_m": cos_m, "sin_m": sin_m, "seg": seg,
        "cc": cc.astype(BF16), "sc": sc.astype(BF16),
        "m1": jnp.concatenate([jnp.concatenate([cr, -sr], axis=1),
                               jnp.concatenate([-sr, -cr], axis=1)], axis=0).astype(BF16),
        "tw_cos": wide(jnp.cos(tw)), "tw_sin": wide(jnp.sin(tw)),
        "m3": jnp.concatenate([c64, s64], axis=1).astype(BF16),
        "w_ctx": jnp.concatenate([cx, -sx], axis=1).astype(BF16),
    }


_GQA_HEAD_ORDER = (0, 4, 1, 5, 2, 6, 3, 7)


def _layer_weights(l, w_in, gqa_q_norm, gqa_k_norm, mla_q_norm, mla_kv_norm, mla_w_q_up, mla_w_kv_up,
                   w_branch, w_gate, b_gate, w_out):
    wi = w_in[l]
    gq = wi[:, 2048:2560].reshape(D_MODEL, GQA_Q_HEADS, HEAD_DIM)[:, jnp.array(_GQA_HEAD_ORDER)]
    zeros = lambda n: jnp.zeros((D_MODEL, n), F32)
    w_in_p = jnp.concatenate([wi[:, :2048], gq.reshape(D_MODEL, 512), wi[:, 2560:3200],
                              zeros(MLA_NOPE), wi[:, 3200:3232], zeros(LANES - MLA_NOPE - MLA_ROPE)], axis=1)
    qu = mla_w_q_up[l].reshape(MLA_Q_RANK, MLA_HEADS, MLA_NOPE + MLA_ROPE)
    qu = jnp.pad(qu, ((0, 0), (0, 0), (0, LANES - MLA_NOPE - MLA_ROPE))).reshape(MLA_Q_RANK, MLA_HEADS * LANES)
    kvu = mla_w_kv_up[l].reshape(MLA_KV_RANK, MLA_HEADS, MLA_NOPE + MLA_V)
    ku = jnp.pad(kvu[:, :, :MLA_NOPE], ((0, 0), (0, 0), (0, LANES - MLA_NOPE))).reshape(MLA_KV_RANK, -1)
    vu = kvu[:, :, MLA_NOPE:].reshape(MLA_KV_RANK, MLA_HEADS * MLA_V)
    wb = w_branch[l]
    wb_gqa = wb[2].reshape(GQA_Q_HEADS, HEAD_DIM, D_MODEL)[jnp.array(_GQA_HEAD_ORDER)].reshape(BRANCH_W, D_MODEL)
    wb = jnp.stack([wb[0], wb[1], wb_gqa, wb[3]])
    return {
        "w_in": w_in_p.astype(BF16),
        "gq_norm": jnp.tile(gqa_q_norm[l], GQA_Q_HEADS).reshape(1, 512),
        "gk_norm": jnp.tile(gqa_k_norm[l], GQA_KV_HEADS).reshape(1, LANES),
        "mq_norm": mla_q_norm[l].reshape(1, MLA_Q_RANK),
        "mkv_norm": mla_kv_norm[l].reshape(1, MLA_KV_RANK),
        "w_q_up": qu.astype(BF16), "w_k_up": ku.astype(BF16), "w_v_up": vu.astype(BF16),
        "w_branch": wb.astype(BF16), "w_gate": w_gate[l].astype(BF16), "b_gate": b_gate[l],
        "w_out": w_out[l].astype(BF16),
    }


def kernel(x, c, ctx, c_ctx, w_mod, b_mod, mix_pre_norm, mix_post_norm, ffn_pre_norm, ffn_post_norm,
           w_in, diff_lambda, diff_subnorm, gqa_q_norm, gqa_k_norm, mla_q_norm, mla_kv_norm,
           mla_w_q_up, mla_w_kv_up, w_branch, w_gate, b_gate, w_out,
           dense_w_gate_up, dense_w_down, moe_router, moe_w_gate_up, moe_w_down):
    B, n_lat, d = x.shape
    n_ctx = ctx.shape[1]
    depth = w_mod.shape[0]
    assert d == D_MODEL and n_lat % TM == 0 and n_ctx % TM == 0 and n_lat % n_ctx == 0
    assert n_lat % GRID_W == 0
    T = n_lat + n_ctx
    n_lat_tiles, n_all_tiles = n_lat // TM, T // TM
    consts = _constants(n_lat, n_ctx)
    x_all = (x, ctx)
    c_rows = jnp.pad(jnp.concatenate([c, c_ctx[None]], axis=0), ((0, (-(B + 1)) % 8), (0, 0)))
    row = lambda v: v.reshape(1, -1)

    for layer in range(depth):
        last = layer == depth - 1
        n_tiles = n_lat_tiles if last else n_all_tiles
        lam_init = 0.8 - 0.6 * math.exp(-0.3 * layer)
        wts = _layer_weights(layer, w_in, gqa_q_norm, gqa_k_norm, mla_q_norm, mla_kv_norm, mla_w_q_up,
                             mla_w_kv_up, w_branch, w_gate, b_gate, w_out)
        mod_rows = _modulation(c_rows, w_mod[layer], b_mod[layer])
        mods = jnp.stack([mod_rows[:B].reshape(B, N_MOD, D_MODEL),
                          jnp.broadcast_to(mod_rows[B].reshape(1, N_MOD, D_MODEL), (B, N_MOD, D_MODEL))], axis=1)

        (h, dq, dk, dv, gq, gk, gv, mq, mk, mv, pq_lat, pq_ctx) = _project(
            x_all, mods, row(mix_pre_norm[layer]), wts, consts, n_lat_tiles)

        diff_extra = (diff_lambda[layer], diff_subnorm[layer].reshape(1, LANES))
        ys = []
        for mode, q, k, v in (("diff", dq, dk, dv), ("gqa", gq, gk, gv), ("mla", mq, mk, mv)):
            kw = dict(lam_init=lam_init, extra=diff_extra)
            tq = TQ if n_lat % TQ == 0 else TM
            y = _attention(mode, q, k, v, None, tq=tq, nq=n_lat // tq, q_blk0=0, tk=T, k_blk=0, **kw)
            if not last:
                y = _attention(mode, q, k, v, y, tq=TM, nq=n_ctx // TM, q_blk0=n_lat_tiles,
                               tk=n_ctx, k_blk=n_lat // n_ctx, **kw)
            ys.append(y)
        y_diff, y_gqa, y_mla = ys

        a_rows = _dft_rows(pq_lat, consts["m1"])
        y_f = _dft_cols(a_rows, consts["tw_cos"], consts["tw_sin"], consts["m3"]).reshape(B, n_lat, BRANCH_W)
        if not last:
            y_f = jnp.concatenate([y_f, _dft_ctx(pq_ctx, consts["w_ctx"])], axis=1)

        merge_args = (x_all, h, y_diff, y_f, y_gqa, y_mla, mods, wts, row(mix_post_norm[layer]),
                      n_lat_tiles, n_tiles)
        if layer % 2 == 0:
            x_all = _merge(*merge_args)
            x_all = _dense_ffn(x_all, mods, row(ffn_pre_norm[layer]), row(ffn_post_norm[layer]),
                               dense_w_gate_up[layer // 2].astype(BF16), dense_w_down[layer // 2].astype(BF16),
                               n_lat_tiles, n_tiles)
        else:
            wr = jnp.pad(moe_router[layer // 2], ((0, 0), (0, LANES - N_EXPERTS)))
            idx = jnp.arange(TM)
            tri = jnp.where(idx[:, None] > idx[None, :], 1.0, 0.0).astype(BF16)
            x_all, h_ffn, route, counts = _merge(*merge_args, router=(row(ffn_pre_norm[layer]), wr, tri))
            x_all = _routed_ffn(x_all, h_ffn, route, counts, mods, row(ffn_post_norm[layer]),
                                moe_w_gate_up[layer // 2].astype(BF16), moe_w_down[layer // 2].astype(BF16),
                                n_lat_tiles, n_tiles)
    return x_all[:, :n_lat]
```

```python
import functools
import math

import jax
import jax.numpy as jnp
from jax import lax
from jax.experimental import pallas as pl
from jax.experimental.pallas import tpu as pltpu

F32 = jnp.float32
BF16 = jnp.bfloat16

D_MODEL = 1024
GRID_W = 64
ROPE_BASE = 10000.0
EPS = 1e-6
N_MOD = 6
HEAD_DIM = 64
DIFF_HEADS = 4
GQA_Q_HEADS = 8
GQA_KV_HEADS = 2
MLA_HEADS = 8
MLA_Q_RANK = 256
MLA_KV_RANK = 128
MLA_NOPE = 64
MLA_ROPE = 32
MLA_V = 64
BRANCH_W = 512
FOURIER_GROUP = 128
FFN_DIM = 2816
N_EXPERTS = 8
EXPERT_DIM = 3584
LOG2E = math.log2(math.e)
QK_SCALE = HEAD_DIM ** -0.5 * LOG2E
MLA_SCALE = (MLA_NOPE + MLA_ROPE) ** -0.5 * LOG2E
IN_COLS_PADDED = 3328

LANES = 128
VMEM_LIMIT = 56 * 1024 * 1024

TM = 256
TQ = 1024


def _cparams(sem):
    return pltpu.CompilerParams(dimension_semantics=sem, vmem_limit_bytes=VMEM_LIMIT)


def _full_spec(shape):
    n = len(shape)
    return pl.BlockSpec(shape, lambda *_: (0,) * n)


def _split_bf16(x):
    hi = x.astype(BF16)
    lo = (x - hi.astype(F32)).astype(BF16)
    return hi, lo


def _dot(a, b):
    return jnp.dot(a, b, preferred_element_type=F32)


def _dot3(a, w):
    ah, al = _split_bf16(a)
    wh, wl = _split_bf16(w)
    return _dot(ah, wh) + _dot(ah, wl) + _dot(al, wh)


def _rms(x, gain):
    ms = jnp.mean(x * x, axis=-1, keepdims=True)
    return x * lax.rsqrt(ms + EPS) * gain


def _mod_kernel(c_ref, w_ref, b_ref, o_ref):
    c = c_ref[...]
    a = c * jax.nn.sigmoid(c)
    o_ref[...] = _dot3(a, w_ref[...]) + b_ref[...]


def _modulation(c_rows, w_mod_l, b_mod_l):
    rows = c_rows.shape[0]
    n = w_mod_l.shape[1]
    tn = 512
    return pl.pallas_call(
        _mod_kernel,
        grid=(n // tn,),
        in_specs=[pl.BlockSpec((rows, D_MODEL), lambda j: (0, 0)),
                  pl.BlockSpec((D_MODEL, tn), lambda j: (0, j)),
                  pl.BlockSpec((1, tn), lambda j: (0, j))],
        out_specs=pl.BlockSpec((rows, tn), lambda j: (0, j)),
        out_shape=jax.ShapeDtypeStruct((rows, n), F32),
        compiler_params=_cparams(("arbitrary",)),
        name="modulation",
    )(c_rows, w_mod_l, b_mod_l.reshape(1, n))


def _rope(x, cos, sin_signed, half):
    lane = lax.broadcasted_iota(jnp.int32, x.shape, 1)
    first = (lane % (2 * half)) < half
    up = pltpu.roll(x, LANES - half, 1)
    down = pltpu.roll(x, half, 1)
    return x * cos + jnp.where(first, up, down) * sin_signed


def _stream_specs(xs, n_lat_tiles):
    if not isinstance(xs, tuple):
        return [pl.BlockSpec((None, TM, D_MODEL), lambda b, t: (b, t, 0))], [xs]
    return ([pl.BlockSpec((None, TM, D_MODEL), lambda b, t: (b, jnp.minimum(t, n_lat_tiles - 1), 0)),
             pl.BlockSpec((None, TM, D_MODEL), lambda b, t: (b, jnp.maximum(t - n_lat_tiles, 0), 0))], list(xs))


def _stream_tile(x_refs, n_lat_tiles):
    if len(x_refs) == 1:
        return x_refs[0][...]
    return jnp.where(pl.program_id(1) < n_lat_tiles, x_refs[0][...], x_refs[1][...])


def _stream_dims(xs):
    if isinstance(xs, tuple):
        return xs[0].shape[0], xs[0].shape[1] + xs[1].shape[1]
    return xs.shape[0], xs.shape[1]


def _proj_kernel(n_lat_tiles, n_x, *refs):
    x_refs, refs = refs[:n_x], refs[n_x:]
    (mod_ref, gpre_ref, win_ref, gqn_ref, gkn_ref, mqn_ref, mkvn_ref,
     wqup_ref, wkup_ref, wvup_ref, seg_ref, cc_ref, sc_ref,
     cosh_ref, sinh_ref, cosm_ref, sinm_ref,
     h_ref, dq_ref, dk_ref, dv_ref, gq_ref, gk_ref, gv_ref, mq_ref, mk_ref, mv_ref, pq_ref, pqc_ref) = refs
    x = _stream_tile(x_refs, n_lat_tiles)
    mod = mod_ref[...]
    h = _rms(x, gpre_ref[...]) * (1.0 + mod[1:2]) + mod[0:1]
    hb = h.astype(BF16)
    h_ref[...] = hb
    z = _dot(hb, win_ref[...])

    def zcols(lo, hi):
        return z[:, lo:hi]
    cosh, sinh = cosh_ref[...], sinh_ref[...]
    cosm, sinm = cosm_ref[...], sinm_ref[...]
    seg = seg_ref[...]

    def head_rms(v, gain, w):
        sq = v * v
        hi, lo = _split_bf16(sq)
        ms = _dot(hi, seg[:w, :w]) + _dot(lo, seg[:w, :w])
        return v * lax.rsqrt(ms + EPS) * gain

    for j in range(4):
        c0 = j * LANES
        dq_ref[:, c0:c0 + LANES] = (_rope(zcols(c0, c0 + LANES), cosh, sinh, 16) * QK_SCALE).astype(BF16)
        dk_ref[:, c0:c0 + LANES] = _rope(zcols(512 + c0, 512 + c0 + LANES), cosh, sinh, 16).astype(BF16)
    dv_ref[...] = zcols(1024, 1536).astype(BF16)
    fparts = []
    for g in range(4):
        c0 = 1536 + g * LANES
        fb = zcols(c0, c0 + LANES).astype(BF16)
        fparts.append((_dot(fb, cc_ref[...]).astype(BF16), _dot(fb, sc_ref[...]).astype(BF16)))
    is_lat = pl.program_id(1) < n_lat_tiles

    def put(ref):
        for g, (pc, ps) in enumerate(fparts):
            ref[0, :, g * LANES:(g + 1) * LANES] = pc
            ref[1, :, g * LANES:(g + 1) * LANES] = ps

    pl.when(is_lat)(lambda: put(pq_ref))
    pl.when(jnp.logical_not(is_lat))(lambda: put(pqc_ref))
    gq = head_rms(zcols(2048, 2560), gqn_ref[...], 512)
    for j in range(4):
        c0 = j * LANES
        gq_ref[:, c0:c0 + LANES] = (_rope(gq[:, c0:c0 + LANES], cosh, sinh, 16) * QK_SCALE).astype(BF16)
    gk = head_rms(zcols(2560, 2688), gkn_ref[...], LANES)
    gk_ref[...] = _rope(gk, cosh, sinh, 16).astype(BF16)
    gv_ref[...] = zcols(2688, 2816).astype(BF16)
    cq = _rms(zcols(2816, 3072), mqn_ref[...]).astype(BF16)
    mq = _dot(cq, wqup_ref[...])
    ckv = _rms(zcols(3072, 3200), mkvn_ref[...]).astype(BF16)
    mk = _dot(ckv, wkup_ref[...])
    mv_ref[...] = _dot(ckv, wvup_ref[...]).astype(BF16)
    kr = _rope(zcols(3200, 3328), cosm, sinm, 8)
    for hh in range(MLA_HEADS):
        c0 = hh * LANES
        mq_ref[:, c0:c0 + LANES] = (_rope(mq[:, c0:c0 + LANES], cosm, sinm, 8) * MLA_SCALE).astype(BF16)
        mk_ref[:, c0:c0 + LANES] = (mk[:, c0:c0 + LANES] + kr).astype(BF16)


def _project(x_all, mods, gpre, wts, consts, n_lat_tiles):
    B, T = _stream_dims(x_all)
    nt = T // TM
    tok = lambda w: pl.BlockSpec((None, TM, w), lambda b, t: (b, t, 0))
    tab = pl.BlockSpec((TM, LANES), lambda b, t: (t, 0))
    x_specs, x_args = _stream_specs(x_all, n_lat_tiles)
    in_specs = x_specs + [
        pl.BlockSpec((None, None, N_MOD, D_MODEL), lambda b, t: (b, jnp.where(t >= n_lat_tiles, 1, 0), 0, 0)),
        _full_spec((1, D_MODEL)),
        _full_spec((D_MODEL, IN_COLS_PADDED)),
        _full_spec((1, 512)), _full_spec((1, LANES)), _full_spec((1, MLA_Q_RANK)), _full_spec((1, MLA_KV_RANK)),
        _full_spec((MLA_Q_RANK, MLA_HEADS * LANES)),
        _full_spec((MLA_KV_RANK, MLA_HEADS * LANES)),
        _full_spec((MLA_KV_RANK, MLA_HEADS * MLA_V)),
        _full_spec((512, 512)), _full_spec((LANES, LANES)), _full_spec((LANES, LANES)),
        tab, tab, tab, tab,
    ]
    widths = [D_MODEL, 512, 512, 512, 512, LANES, LANES, MLA_HEADS * LANES, MLA_HEADS * LANES, 512]
    out_specs = [tok(w) for w in widths]
    out_shape = [jax.ShapeDtypeStruct((B, T, w), BF16) for w in widths]
    out_specs.append(pl.BlockSpec((2, None, TM, 512), lambda b, t: (0, b, jnp.minimum(t, n_lat_tiles - 1), 0)))
    out_shape.append(jax.ShapeDtypeStruct((2, B, n_lat_tiles * TM, 512), BF16))
    out_specs.append(pl.BlockSpec((2, None, TM, 512), lambda b, t: (0, b, jnp.maximum(t - n_lat_tiles, 0), 0)))
    out_shape.append(jax.ShapeDtypeStruct((2, B, T - n_lat_tiles * TM, 512), BF16))
    return pl.pallas_call(
        functools.partial(_proj_kernel, n_lat_tiles, len(x_args)),
        grid=(B, nt),
        in_specs=in_specs,
        out_specs=out_specs,
        out_shape=out_shape,
        compiler_params=_cparams(("arbitrary", "arbitrary")),
        name="project",
    )(*x_args, mods, gpre, wts["w_in"], wts["gq_norm"], wts["gk_norm"], wts["mq_norm"], wts["mkv_norm"],
      wts["w_q_up"], wts["w_k_up"], wts["w_v_up"], consts["seg"], consts["cc"], consts["sc"],
      consts["cos_h"], consts["sin_h"], consts["cos_m"], consts["sin_m"])


N_MAPS = 8
MAPS_PER_STEP = 2


def _map_plan(mode, m):
    if mode == "diff":
        return m // 2, m % 2, m // 2, m // 2
    if mode == "gqa":
        return m // 2, m % 2, 0, 0
    return m, None, m, m // 2


def _attn_kernel(mode, lam_init, q_ref, k_ref, v_ref, *rest):
    o_ref, vx_ref = rest[-2:]
    if mode == "diff":
        lam_ref, sub_ref = rest[:2]
    nt = (((1,), (1,)), ((), ()))
    lane = lax.broadcasted_iota(jnp.int32, (1, LANES), 1)
    low = lane < HEAD_DIM

    @pl.when(pl.program_id(2) == 0)
    def _():
        for g in range(vx_ref.shape[0]):
            vx_ref[g, :, :LANES] = v_ref[:, g * LANES:(g + 1) * LANES]
            vx_ref[g, :, LANES:] = jnp.ones((vx_ref.shape[1], LANES), BF16)

    def scores(m):
        qblk, half, kblk, _ = _map_plan(mode, m)
        q = q_ref[:, qblk * LANES:(qblk + 1) * LANES]
        if half is not None:
            q = jnp.where(low if half == 0 else jnp.logical_not(low), q, jnp.zeros_like(q))
        return lax.dot_general(q, k_ref[:, kblk * LANES:(kblk + 1) * LANES], nt, preferred_element_type=F32)

    def numerator(s):
        return jnp.exp2(s - jnp.max(s, axis=-1, keepdims=True)).astype(BF16)

    def values(m, p):
        e = _dot(p, vx_ref[_map_plan(mode, m)[3]])
        return e[:, :LANES] / e[:, LANES:]

    s, pl_, y = {}, {}, {}
    for t in range(MAPS_PER_STEP + 2):
        if t < MAPS_PER_STEP:
            s[t] = scores(t)
        if 0 <= t - 1 < MAPS_PER_STEP:
            pl_[t - 1] = numerator(s.pop(t - 1))
        if 0 <= t - 2:
            y[t - 2] = values(t - 2, pl_.pop(t - 2))

    if mode == "diff":
        dl = lam_ref[...]
        lam = (jnp.exp(jnp.sum(dl[0:1] * dl[1:2], axis=-1, keepdims=True))
               - jnp.exp(jnp.sum(dl[2:3] * dl[3:4], axis=-1, keepdims=True)) + lam_init)
    for j in range(MAPS_PER_STEP // 2):
        if mode == "diff":
            out = _rms(y[2 * j] - lam * y[2 * j + 1], sub_ref[...]) * (1.0 - lam_init)
        else:
            out = jnp.where(low, y[2 * j], y[2 * j + 1])
        o_ref[:, j * LANES:(j + 1) * LANES] = out.astype(BF16)


def _attention(mode, q, k, v, y_prev, *, tq, nq, q_blk0, tk, k_blk, lam_init=0.0, extra=()):
    B, T, _ = q.shape
    n_steps = N_MAPS // MAPS_PER_STEP
    wq, wk, wv = q.shape[2] // n_steps, max(k.shape[2] // n_steps, LANES), max(v.shape[2] // n_steps, LANES)
    kv_blk = (lambda g: 0) if mode == "gqa" else (lambda g: g)
    in_specs = [
        pl.BlockSpec((None, tq, wq), lambda b, g, i: (b, i + q_blk0, g)),
        pl.BlockSpec((None, tk, wk), lambda b, g, i: (b, k_blk, kv_blk(g))),
        pl.BlockSpec((None, tk, wv), lambda b, g, i: (b, k_blk, kv_blk(g))),
    ]
    args = [q, k, v]
    if mode == "diff":
        in_specs += [_full_spec((4, HEAD_DIM)), _full_spec((1, LANES))]
        args += list(extra)
    aliases = {}
    if y_prev is not None:
        in_specs.append(pl.BlockSpec(memory_space=pl.ANY))
        args.append(y_prev)
        aliases = {len(args) - 1: 0}
    wo = BRANCH_W // n_steps
    return pl.pallas_call(
        functools.partial(_attn_kernel, mode, lam_init),
        grid=(B, n_steps, nq),
        in_specs=in_specs,
        out_specs=pl.BlockSpec((None, tq, wo), lambda b, g, i: (b, i + q_blk0, g)),
        out_shape=jax.ShapeDtypeStruct((B, T, BRANCH_W), BF16),
        input_output_aliases=aliases,
        scratch_shapes=[pltpu.VMEM((wv // LANES, tk, 2 * LANES), BF16)],
        compiler_params=_cparams(("arbitrary", "arbitrary", "arbitrary")),
        name="attention_" + mode,
    )(*args)


DFT_COLS_PER_STEP = 8


DFT_ROW_COLS = 16


def _dft_rows_kernel(pq_ref, m1x_ref, a_ref):
    shape = pq_ref.shape
    z = pq_ref[...].reshape(shape[0] * shape[1] * shape[2], shape[3])
    a_ref[...] = _dot(m1x_ref[...], z).astype(BF16).reshape(shape)


def _dft_rows(pq_lat, m1x):
    _, B, n_lat, w = pq_lat.shape
    r = n_lat // GRID_W
    nb = DFT_ROW_COLS
    return pl.pallas_call(
        _dft_rows_kernel,
        grid=(B, GRID_W // nb),
        in_specs=[pl.BlockSpec((2, None, r, nb, w), lambda b, j: (0, b, 0, j, 0)),
                  _full_spec((2 * r * nb, 2 * r * nb))],
        out_specs=pl.BlockSpec((None, 2, r, nb, w), lambda b, j: (b, 0, 0, j, 0)),
        out_shape=jax.ShapeDtypeStruct((B, 2, r, GRID_W, w), BF16),
        compiler_params=_cparams(("arbitrary", "arbitrary")),
        name="dft_rows",
    )(pq_lat.reshape(2, B, r, GRID_W, w), m1x)


def _dft_cols_kernel(a_ref, ct_ref, st_ref, m3_ref, o_ref):
    kb = a_ref.shape[1]
    w = a_ref.shape[3]
    for j in range(kb):
        ar = a_ref[0, j].astype(F32)
        ai = a_ref[1, j].astype(F32)
        ct = jnp.concatenate([ct_ref[j]] * (w // LANES), axis=1)
        st = jnp.concatenate([st_ref[j]] * (w // LANES), axis=1)
        b = jnp.concatenate([ar * ct + ai * st, ai * ct - ar * st], axis=0).astype(BF16)
        o_ref[:, j, :] = _dot(m3_ref[...], b).astype(BF16)


def _dft_cols(a, ct, st, m3):
    B, _, r, _, w = a.shape
    kb = DFT_COLS_PER_STEP
    return pl.pallas_call(
        _dft_cols_kernel,
        grid=(B, r // kb),
        in_specs=[pl.BlockSpec((None, 2, kb, GRID_W, w), lambda b, k: (b, 0, k, 0, 0)),
                  pl.BlockSpec((kb, GRID_W, LANES), lambda b, k: (k, 0, 0)),
                  pl.BlockSpec((kb, GRID_W, LANES), lambda b, k: (k, 0, 0)),
                  _full_spec((GRID_W, 2 * GRID_W))],
        out_specs=pl.BlockSpec((None, GRID_W, kb, w), lambda b, k: (b, 0, k, 0)),
        out_shape=jax.ShapeDtypeStruct((B, GRID_W, r, w), BF16),
        compiler_params=_cparams(("arbitrary", "arbitrary")),
        name="dft_cols",
    )(a, ct, st, m3)


def _dft_ctx_kernel(pq_ref, w_ref, o_ref):
    zin = jnp.concatenate([pq_ref[0], pq_ref[1]], axis=0)
    o_ref[...] = _dot(w_ref[...], zin).astype(BF16)


def _dft_ctx(pq_ctx, w_ctx):
    _, B, n_ctx, w = pq_ctx.shape
    return pl.pallas_call(
        _dft_ctx_kernel,
        grid=(B,),
        in_specs=[pl.BlockSpec((2, None, n_ctx, w), lambda b: (0, b, 0, 0)), _full_spec((n_ctx, 2 * n_ctx))],
        out_specs=pl.BlockSpec((None, n_ctx, w), lambda b: (b, 0, 0)),
        out_shape=jax.ShapeDtypeStruct((B, n_ctx, w), BF16),
        compiler_params=_cparams(("arbitrary",)),
        name="dft_ctx",
    )(pq_ctx, w_ctx)


def _merge_kernel(n_lat_tiles, n_x, *refs):
    x_refs, refs = refs[:n_x], refs[n_x:]
    h_ref, y0_ref, y1_ref, y2_ref, y3_ref, mod_ref, wg_ref, bg_ref, wb_ref, wo_ref, gpost_ref = refs[:11]
    rest = refs[11:]
    h = h_ref[...]
    ys = (y0_ref, y1_ref, y2_ref, y3_ref)
    acc = None
    for i in range(4):
        gate = jax.nn.sigmoid(_dot(h, wg_ref[i]) + bg_ref[i:i + 1])
        term = gate * _dot(ys[i][...], wb_ref[i])
        acc = term if acc is None else acc + term
    m = _dot(acc.astype(BF16), wo_ref[...])
    mod = mod_ref[...]
    x = _stream_tile(x_refs, n_lat_tiles) + mod[2:3] * _rms(m, gpost_ref[...])
    o_ref = rest[3] if len(rest) > 1 else rest[0]
    o_ref[...] = x
    if len(rest) > 1:
        gffn_ref, wr_ref, tri_ref, _, hf_ref, route_ref, cnt_ref, carry_ref = rest
        _route_tile(x, mod, gffn_ref, wr_ref, tri_ref, hf_ref, route_ref, cnt_ref, carry_ref)


def _merge(x_all, h, y_diff, y_f, y_gqa, y_mla, mods, wts, gpost, n_lat_tiles, n_tiles, router=None):
    B, _ = _stream_dims(x_all)
    tok = lambda w: pl.BlockSpec((None, TM, w), lambda b, t: (b, t, 0))
    x_specs, x_args = _stream_specs(x_all, n_lat_tiles)
    in_specs = x_specs + [
        tok(D_MODEL), tok(BRANCH_W), tok(BRANCH_W), tok(BRANCH_W), tok(BRANCH_W),
        pl.BlockSpec((None, None, N_MOD, D_MODEL), lambda b, t: (b, jnp.where(t >= n_lat_tiles, 1, 0), 0, 0)),
        _full_spec((4, D_MODEL, D_MODEL)), _full_spec((4, D_MODEL)), _full_spec((4, BRANCH_W, D_MODEL)),
        _full_spec((D_MODEL, D_MODEL)), _full_spec((1, D_MODEL)),
    ]
    args = x_args + [h, y_diff, y_f, y_gqa, y_mla, mods, wts["w_gate"], wts["b_gate"], wts["w_branch"],
                     wts["w_out"], gpost]
    out_specs = [tok(D_MODEL)]
    out_shape = [jax.ShapeDtypeStruct((B, n_tiles * TM, D_MODEL), F32)]
    scratch = []
    if router is not None:
        in_specs += [_full_spec((1, D_MODEL)), _full_spec((D_MODEL, LANES)), _full_spec((TM, TM))]
        args += list(router)
        out_specs += [tok(D_MODEL), tok(LANES), _full_spec((1, LANES))]
        out_shape += [jax.ShapeDtypeStruct((B, n_tiles * TM, D_MODEL), F32),
                      jax.ShapeDtypeStruct((B, n_tiles * TM, LANES), F32),
                      jax.ShapeDtypeStruct((1, LANES), F32)]
        scratch = [pltpu.VMEM((1, LANES), F32)]
    out = pl.pallas_call(
        functools.partial(_merge_kernel, n_lat_tiles, len(x_args)),
        grid=(B, n_tiles),
        in_specs=in_specs,
        out_specs=out_specs,
        out_shape=out_shape,
        scratch_shapes=scratch,
        compiler_params=_cparams(("arbitrary", "arbitrary")),
        name="merge",
    )(*args)
    return out[0] if router is None else out


def _dense_ffn_kernel(nf, tf, x_ref, mod_ref, gpre_ref, gpost_ref, wgu_ref, wd_ref, o_ref):
    x = x_ref[...]
    mod = mod_ref[...]
    hb = (_rms(x, gpre_ref[...]) * (1.0 + mod[4:5]) + mod[3:4]).astype(BF16)
    acc = None
    for f in range(nf):
        a = _dot(hb, wgu_ref[:, f * tf:(f + 1) * tf])
        u = _dot(hb, wgu_ref[:, FFN_DIM + f * tf:FFN_DIM + (f + 1) * tf])
        act = (a * jax.nn.sigmoid(a) * u).astype(BF16)
        part = _dot(act, wd_ref[f * tf:(f + 1) * tf, :])
        acc = part if acc is None else acc + part
    o_ref[...] = x + mod[5:6] * _rms(acc, gpost_ref[...])


def _dense_ffn(x_all, mods, gpre, gpost, wgu, wd, n_lat_tiles, n_tiles):
    B = x_all.shape[0]
    nf, tf = 1, FFN_DIM
    tok = pl.BlockSpec((None, TM, D_MODEL), lambda b, t: (b, t, 0))
    return pl.pallas_call(
        functools.partial(_dense_ffn_kernel, nf, tf),
        grid=(B, n_tiles),
        in_specs=[tok,
                  pl.BlockSpec((None, None, N_MOD, D_MODEL),
                               lambda b, t: (b, jnp.where(t >= n_lat_tiles, 1, 0), 0, 0)),
                  _full_spec((1, D_MODEL)), _full_spec((1, D_MODEL)),
                  _full_spec((D_MODEL, 2 * FFN_DIM)), _full_spec((FFN_DIM, D_MODEL))],
        out_specs=tok,
        out_shape=jax.ShapeDtypeStruct((B, n_tiles * TM, D_MODEL), F32),
        compiler_params=_cparams(("arbitrary", "arbitrary")),
        name="dense_ffn",
    )(x_all, mods, gpre, gpost, wgu, wd)


TE = 512
R_E1, R_E2, R_G1, R_G2, R_RANK1, R_RANK2 = range(6)


def _route_tile(x, mod, gpre_ref, wr_ref, tri_ref, h_ref, route_ref, cnt_ref, carry_ref):
    first = jnp.logical_and(pl.program_id(0) == 0, pl.program_id(1) == 0)

    @pl.when(first)
    def _():
        carry_ref[...] = jnp.zeros_like(carry_ref)

    h = _rms(x, gpre_ref[...]) * (1.0 + mod[4:5]) + mod[3:4]
    h_ref[...] = h
    logits = _dot3(h, wr_ref[...])
    lane = lax.broadcasted_iota(jnp.int32, logits.shape, 1)
    neg = jnp.float32(-jnp.inf)
    l1 = jnp.where(lane < N_EXPERTS, logits, neg)
    m1 = jnp.max(l1, axis=-1, keepdims=True)
    i1 = jnp.min(jnp.where(l1 == m1, lane, LANES), axis=-1, keepdims=True)
    l2 = jnp.where(lane == i1, neg, l1)
    m2 = jnp.max(l2, axis=-1, keepdims=True)
    i2 = jnp.min(jnp.where(l2 == m2, lane, LANES), axis=-1, keepdims=True)
    e2 = jnp.exp(m2 - m1)
    g1 = 1.0 / (1.0 + e2)
    g2 = e2 / (1.0 + e2)
    sel1, sel2 = lane == i1, lane == i2
    chosen = jnp.where(jnp.logical_or(sel1, sel2), 1.0, 0.0)
    before = _dot(tri_ref[...], chosen.astype(BF16)) + carry_ref[...]
    r1 = jnp.sum(jnp.where(sel1, before, 0.0), axis=-1, keepdims=True)
    r2 = jnp.sum(jnp.where(sel2, before, 0.0), axis=-1, keepdims=True)
    carry_ref[...] += jnp.sum(chosen, axis=0, keepdims=True)
    cnt_ref[...] = carry_ref[...]
    rec = jnp.zeros_like(logits)
    for col, val in ((R_E1, i1.astype(F32)), (R_E2, i2.astype(F32)), (R_G1, g1), (R_G2, g2),
                     (R_RANK1, r1), (R_RANK2, r2)):
        rec = jnp.where(lane == col, val, rec)
    route_ref[...] = rec


def _dispatch_kernel(last_ref, pos_ref, h_ref, xs_ref, zero_ref, stage_ref, sems):
    i = pl.program_id(0)
    n = h_ref.shape[0]

    @pl.when(i == 0)
    def _():
        zero_ref[...] = jnp.zeros_like(zero_ref)
        for e in range(N_EXPERTS):
            @pl.when(last_ref[e] >= 0)
            def _():
                fill = pltpu.make_async_copy(zero_ref, xs_ref.at[pl.ds(pl.multiple_of(last_ref[e], TE), TE)],
                                             sems.at[0])
                fill.start()
                fill.wait()

    cur = i % 2
    stage_ref[cur] = h_ref[...]

    def start(r, carry):
        for slot in range(2):
            pltpu.make_async_copy(stage_ref.at[cur, pl.ds(r, 1)], xs_ref.at[pl.ds(pos_ref[0, 2 * r + slot], 1)],
                                  sems.at[cur]).start()
        return carry

    lax.fori_loop(0, n, start, 0, unroll=8)

    def drain(half):
        for _ in range(2):
            pltpu.make_async_copy(stage_ref.at[half], xs_ref.at[pl.ds(0, n)], sems.at[half]).wait()

    pl.when(i > 0)(lambda: drain(1 - cur))
    pl.when(i == pl.num_programs(0) - 1)(lambda: drain(cur))


def _dispatch(h, pos, last_tile_row, n_rows):
    n = h.shape[0]
    grid_spec = pltpu.PrefetchScalarGridSpec(
        num_scalar_prefetch=1,
        grid=(n // TM,),
        in_specs=[pl.BlockSpec((None, 1, 2 * TM), lambda i, last: (i, 0, 0), memory_space=pltpu.SMEM),
                  pl.BlockSpec((TM, D_MODEL), lambda i, last: (i, 0))],
        out_specs=pl.BlockSpec(memory_space=pl.ANY),
        scratch_shapes=[pltpu.VMEM((TE, D_MODEL), F32), pltpu.VMEM((2, TM, D_MODEL), F32),
                        pltpu.SemaphoreType.DMA((2,))],
    )
    return pl.pallas_call(
        _dispatch_kernel,
        grid_spec=grid_spec,
        out_shape=jax.ShapeDtypeStruct((n_rows, D_MODEL), F32),
        compiler_params=_cparams(("arbitrary",)),
        name="moe_dispatch",
    )(last_tile_row, pos.reshape(n // TM, 1, 2 * TM), h)


def _experts_kernel(te_ref, nv_ref, xs_ref, wa_ref, wu_ref, wd_ref, ys_ref):
    del te_ref

    @pl.when(pl.program_id(0) < nv_ref[0])
    def _():
        x = xs_ref[...].astype(BF16)
        a = _dot(x, wa_ref[...])
        u = _dot(x, wu_ref[...])
        ys_ref[...] = _dot((a * jax.nn.sigmoid(a) * u).astype(BF16), wd_ref[...])


def _experts(xs, tile_expert, n_valid, wgu, wd):
    n_rows = xs.shape[0]
    row_blk = lambda i, te, nv: (jnp.minimum(i, nv[0] - 1), 0)
    resident = pl.Buffered(1)
    grid_spec = pltpu.PrefetchScalarGridSpec(
        num_scalar_prefetch=2,
        grid=(n_rows // TE,),
        in_specs=[pl.BlockSpec((TE, D_MODEL), row_blk),
                  pl.BlockSpec((None, D_MODEL, EXPERT_DIM), lambda i, te, nv: (te[i], 0, 0), pipeline_mode=resident),
                  pl.BlockSpec((None, D_MODEL, EXPERT_DIM), lambda i, te, nv: (te[i], 0, 1), pipeline_mode=resident),
                  pl.BlockSpec((None, EXPERT_DIM, D_MODEL), lambda i, te, nv: (te[i], 0, 0), pipeline_mode=resident)],
        out_specs=pl.BlockSpec((TE, D_MODEL), row_blk),
    )
    return pl.pallas_call(
        _experts_kernel,
        grid_spec=grid_spec,
        out_shape=jax.ShapeDtypeStruct((n_rows, D_MODEL), F32),
        compiler_params=_cparams(("arbitrary",)),
        name="moe_experts",
    )(tile_expert, n_valid, xs, wgu, wgu, wd)


def _combine_kernel(pos_ref, nxt_ref, x_ref, route_ref, mod_ref, gpost_ref, ys_ref, o_ref, buf_ref, sems):
    i = pl.program_id(0)
    n = x_ref.shape[0]

    def gather(idx_ref, half):
        def start(r, carry):
            for slot in range(2):
                pltpu.make_async_copy(ys_ref.at[pl.ds(idx_ref[0, 2 * r + slot], 1)],
                                      buf_ref.at[half, slot, pl.ds(r, 1)], sems.at[half]).start()
            return carry
        lax.fori_loop(0, n, start, 0, unroll=8)

    cur = i % 2

    @pl.when(i == 0)
    def _():
        gather(pos_ref, 0)

    @pl.when(i + 1 < pl.num_programs(0))
    def _():
        gather(nxt_ref, 1 - cur)

    for slot in range(2):
        pltpu.make_async_copy(ys_ref.at[pl.ds(0, n)], buf_ref.at[cur, slot], sems.at[cur]).wait()
    route = route_ref[...]
    y = route[:, R_G1:R_G1 + 1] * buf_ref[cur, 0] + route[:, R_G2:R_G2 + 1] * buf_ref[cur, 1]
    mod = mod_ref[...]
    o_ref[...] = x_ref[...] + mod[5:6] * _rms(y, gpost_ref[...])


def _combine(x_all, route, pos, ys, mods, gpost, n_lat_tiles, n_tiles):
    B = x_all.shape[0]
    n_steps = B * n_tiles
    tok = lambda w: pl.BlockSpec((None, TM, w), lambda i: (i // n_tiles, i % n_tiles, 0))
    idx = lambda ahead: pl.BlockSpec((None, 1, 2 * TM), lambda i: (jnp.minimum(i + ahead, n_steps - 1), 0, 0),
                                     memory_space=pltpu.SMEM)
    pos_tiles = pos.reshape(n_steps, 1, 2 * TM)
    return pl.pallas_call(
        _combine_kernel,
        grid=(n_steps,),
        in_specs=[idx(0), idx(1), tok(D_MODEL), tok(LANES),
                  pl.BlockSpec((None, None, N_MOD, D_MODEL),
                               lambda i: (i // n_tiles, jnp.where(i % n_tiles >= n_lat_tiles, 1, 0), 0, 0)),
                  _full_spec((1, D_MODEL)),
                  pl.BlockSpec(memory_space=pl.ANY)],
        out_specs=tok(D_MODEL),
        out_shape=jax.ShapeDtypeStruct((B, n_tiles * TM, D_MODEL), F32),
        scratch_shapes=[pltpu.VMEM((2, 2, TM, D_MODEL), F32), pltpu.SemaphoreType.DMA((2,))],
        compiler_params=_cparams(("arbitrary",)),
        name="moe_combine",
    )(pos_tiles, pos_tiles, x_all, route, mods, gpost, ys)


def _routed_ffn(x_all, h, route, counts, mods, gpost, wgu, wd, n_lat_tiles, n_tiles):
    B = x_all.shape[0]
    n_tok = B * n_tiles * TM
    rec = route.reshape(n_tok, LANES)
    expert = rec[:, R_E1:R_E2 + 1].astype(jnp.int32)
    rank = rec[:, R_RANK1:R_RANK2 + 1].astype(jnp.int32)
    tiles_per_expert = (counts[0, :N_EXPERTS].astype(jnp.int32) + TE - 1) // TE
    tile_end = jnp.cumsum(tiles_per_expert)
    n_valid = tile_end[-1:]
    pos = (tile_end - tiles_per_expert)[expert] * TE + rank
    n_tiles_max = (2 * n_tok) // TE + N_EXPERTS
    tile_id = jnp.minimum(jnp.arange(n_tiles_max, dtype=jnp.int32), n_valid - 1)
    tile_expert = jnp.minimum(jnp.searchsorted(tile_end, tile_id, side="right"), N_EXPERTS - 1).astype(jnp.int32)
    last_tile_row = jnp.where(tiles_per_expert > 0, (tile_end - 1) * TE, -1).astype(jnp.int32)
    xs = _dispatch(h.reshape(n_tok, D_MODEL), pos, last_tile_row, n_tiles_max * TE)
    ys = _experts(xs, tile_expert, n_valid.astype(jnp.int32), wgu, wd)
    return _combine(x_all, route, pos, ys, mods, gpost, n_lat_tiles, n_tiles)


def _rope_angles(n_lat, rot_dim):
    rows = n_lat // GRID_W
    row = jnp.repeat(jnp.arange(rows, dtype=F32), GRID_W)
    col = jnp.tile(jnp.arange(GRID_W, dtype=F32), rows)
    axis_dim = rot_dim // 2
    inv_freq = 1.0 / (ROPE_BASE ** (jnp.arange(0, axis_dim, 2, dtype=F32) / axis_dim))
    ar = row[:, None] * inv_freq
    ac = col[:, None] * inv_freq
    return jnp.concatenate([ar, ar, ac, ac], axis=-1)


def _constants(n_lat, n_ctx):
    lane = jnp.arange(LANES)
    ang = _rope_angles(n_lat, HEAD_DIM)
    cos_h = jnp.tile(jnp.cos(ang), (1, 2))
    sin_h = jnp.tile(jnp.sin(ang), (1, 2)) * jnp.where((lane % 32) < 16, -1.0, 1.0)
    cos_h = jnp.concatenate([cos_h, jnp.ones((n_ctx, LANES), F32)], axis=0)
    sin_h = jnp.concatenate([sin_h, jnp.zeros((n_ctx, LANES), F32)], axis=0)
    angm = _rope_angles(n_lat, MLA_ROPE)
    pad_l = jnp.zeros((n_lat, MLA_NOPE), F32)
    pad_r = jnp.zeros((n_lat, LANES - MLA_NOPE - MLA_ROPE), F32)
    cos_m = jnp.concatenate([pad_l + 1.0, jnp.cos(angm), pad_r + 1.0], axis=1)
    sin_m = jnp.concatenate([pad_l, jnp.sin(angm), pad_r], axis=1) * jnp.where((lane % 16) < 8, -1.0, 1.0)
    cos_m = jnp.concatenate([cos_m, jnp.ones((n_ctx, LANES), F32)], axis=0)
    sin_m = jnp.concatenate([sin_m, jnp.zeros((n_ctx, LANES), F32)], axis=0)
    idx = jnp.arange(512)
    seg = jnp.where((idx[:, None] // HEAD_DIM) == (idx[None, :] // HEAD_DIM), 1.0 / HEAD_DIM, 0.0).astype(BF16)

    def dft(n):
        i = jnp.arange(n, dtype=jnp.int32)
        a = ((i[:, None] * i[None, :]) % n).astype(F32) * (2.0 * math.pi / n)
        s = n ** -0.5
        return jnp.cos(a) * s, jnp.sin(a) * s

    cc, sc = dft(FOURIER_GROUP)
    cx, sx = dft(n_ctx)
    r = n_lat // GRID_W
    cr, sr = (t * r ** 0.5 for t in dft(r))
    c64, s64 = (t * GRID_W ** 0.5 * n_lat ** -0.5 for t in dft(GRID_W))
    k1 = jnp.arange(r, dtype=jnp.int32)[:, None]
    n2 = jnp.arange(GRID_W, dtype=jnp.int32)[None, :]
    tw = ((k1 * n2) % n_lat).astype(F32) * (2.0 * math.pi / n_lat)
    wide = lambda t: jnp.broadcast_to(t[:, :, None], (r, GRID_W, LANES))
    return {
        "cos_h": cos_h, "sin_h": sin_h, "cos_m": cos_m, "sin_m": sin_m, "seg": seg,
        "cc": cc.astype(BF16), "sc": sc.astype(BF16),
        "m1x": jnp.kron(jnp.concatenate([jnp.concatenate([cr, -sr], axis=1),
                                         jnp.concatenate([-sr, -cr], axis=1)], axis=0),
                        jnp.eye(DFT_ROW_COLS, dtype=F32)).astype(BF16),
        "tw_cos": wide(jnp.cos(tw)), "tw_sin": wide(jnp.sin(tw)),
        "m3": jnp.concatenate([c64, s64], axis=1).astype(BF16),
        "w_ctx": jnp.concatenate([cx, -sx], axis=1).astype(BF16),
    }


_GQA_HEAD_ORDER = (0, 4, 1, 5, 2, 6, 3, 7)


def _layer_weights(l, w_in, gqa_q_norm, gqa_k_norm, mla_q_norm, mla_kv_norm, mla_w_q_up, mla_w_kv_up,
                   w_branch, w_gate, b_gate, w_out):
    wi = w_in[l]
    gq = wi[:, 2048:2560].reshape(D_MODEL, GQA_Q_HEADS, HEAD_DIM)[:, jnp.array(_GQA_HEAD_ORDER)]
    zeros = lambda n: jnp.zeros((D_MODEL, n), F32)
    w_in_p = jnp.concatenate([wi[:, :2048], gq.reshape(D_MODEL, 512), wi[:, 2560:3200],
                              zeros(MLA_NOPE), wi[:, 3200:3232], zeros(LANES - MLA_NOPE - MLA_ROPE)], axis=1)
    qu = mla_w_q_up[l].reshape(MLA_Q_RANK, MLA_HEADS, MLA_NOPE + MLA_ROPE)
    qu = jnp.pad(qu, ((0, 0), (0, 0), (0, LANES - MLA_NOPE - MLA_ROPE))).reshape(MLA_Q_RANK, MLA_HEADS * LANES)
    kvu = mla_w_kv_up[l].reshape(MLA_KV_RANK, MLA_HEADS, MLA_NOPE + MLA_V)
    ku = jnp.pad(kvu[:, :, :MLA_NOPE], ((0, 0), (0, 0), (0, LANES - MLA_NOPE))).reshape(MLA_KV_RANK, -1)
    vu = kvu[:, :, MLA_NOPE:].reshape(MLA_KV_RANK, MLA_HEADS * MLA_V)
    wb = w_branch[l]
    wb_gqa = wb[2].reshape(GQA_Q_HEADS, HEAD_DIM, D_MODEL)[jnp.array(_GQA_HEAD_ORDER)].reshape(BRANCH_W, D_MODEL)
    wb = jnp.stack([wb[0], wb[1], wb_gqa, wb[3]])
    return {
        "w_in": w_in_p.astype(BF16),
        "gq_norm": jnp.tile(gqa_q_norm[l], GQA_Q_HEADS).reshape(1, 512),
        "gk_norm": jnp.tile(gqa_k_norm[l], GQA_KV_HEADS).reshape(1, LANES),
        "mq_norm": mla_q_norm[l].reshape(1, MLA_Q_RANK),
        "mkv_norm": mla_kv_norm[l].reshape(1, MLA_KV_RANK),
        "w_q_up": qu.astype(BF16), "w_k_up": ku.astype(BF16), "w_v_up": vu.astype(BF16),
        "w_branch": wb.astype(BF16), "w_gate": w_gate[l].astype(BF16), "b_gate": b_gate[l],
        "w_out": w_out[l].astype(BF16),
    }


def kernel(x, c, ctx, c_ctx, w_mod, b_mod, mix_pre_norm, mix_post_norm, ffn_pre_norm, ffn_post_norm,
           w_in, diff_lambda, diff_subnorm, gqa_q_norm, gqa_k_norm, mla_q_norm, mla_kv_norm,
           mla_w_q_up, mla_w_kv_up, w_branch, w_gate, b_gate, w_out,
           dense_w_gate_up, dense_w_down, moe_router, moe_w_gate_up, moe_w_down):
    B, n_lat, d = x.shape
    n_ctx = ctx.shape[1]
    depth = w_mod.shape[0]
    assert d == D_MODEL and n_lat % TM == 0 and n_ctx % TM == 0 and n_lat % n_ctx == 0
    assert n_lat % GRID_W == 0
    T = n_lat + n_ctx
    n_lat_tiles, n_all_tiles = n_lat // TM, T // TM
    consts = _constants(n_lat, n_ctx)
    x_all = (x, ctx)
    c_rows = jnp.pad(jnp.concatenate([c, c_ctx[None]], axis=0), ((0, (-(B + 1)) % 8), (0, 0)))
    row = lambda v: v.reshape(1, -1)

    for layer in range(depth):
        last = layer == depth - 1
        n_tiles = n_lat_tiles if last else n_all_tiles
        lam_init = 0.8 - 0.6 * math.exp(-0.3 * layer)
        wts = _layer_weights(layer, w_in, gqa_q_norm, gqa_k_norm, mla_q_norm, mla_kv_norm, mla_w_q_up,
                             mla_w_kv_up, w_branch, w_gate, b_gate, w_out)
        mod_rows = _modulation(c_rows, w_mod[layer], b_mod[layer])
        mods = jnp.stack([mod_rows[:B].reshape(B, N_MOD, D_MODEL),
                          jnp.broadcast_to(mod_rows[B].reshape(1, N_MOD, D_MODEL), (B, N_MOD, D_MODEL))], axis=1)

        (h, dq, dk, dv, gq, gk, gv, mq, mk, mv, pq_lat, pq_ctx) = _project(
            x_all, mods, row(mix_pre_norm[layer]), wts, consts, n_lat_tiles)

        diff_extra = (diff_lambda[layer], diff_subnorm[layer].reshape(1, LANES))
        ys = []
        for mode, q, k, v in (("diff", dq, dk, dv), ("gqa", gq, gk, gv), ("mla", mq, mk, mv)):
            kw = dict(lam_init=lam_init, extra=diff_extra)
            tq = TQ if n_lat % TQ == 0 else TM
            y = _attention(mode, q, k, v, None, tq=tq, nq=n_lat // tq, q_blk0=0, tk=T, k_blk=0, **kw)
            if not last:
                y = _attention(mode, q, k, v, y, tq=TM, nq=n_ctx // TM, q_blk0=n_lat_tiles,
                               tk=n_ctx, k_blk=n_lat // n_ctx, **kw)
            ys.append(y)
        y_diff, y_gqa, y_mla = ys

        a_rows = _dft_rows(pq_lat, consts["m1x"])
        y_f = _dft_cols(a_rows, consts["tw_cos"], consts["tw_sin"], consts["m3"]).reshape(B, n_lat, BRANCH_W)
        if not last:
            y_f = jnp.concatenate([y_f, _dft_ctx(pq_ctx, consts["w_ctx"])], axis=1)

        merge_args = (x_all, h, y_diff, y_f, y_gqa, y_mla, mods, wts, row(mix_post_norm[layer]),
                      n_lat_tiles, n_tiles)
        if layer % 2 == 0:
            x_all = _merge(*merge_args)
            x_all = _dense_ffn(x_all, mods, row(ffn_pre_norm[layer]), row(ffn_post_norm[layer]),
                               dense_w_gate_up[layer // 2].astype(BF16), dense_w_down[layer // 2].astype(BF16),
                               n_lat_tiles, n_tiles)
        else:
            wr = jnp.pad(moe_router[layer // 2], ((0, 0), (0, LANES - N_EXPERTS)))
            idx = jnp.arange(TM)
            tri = jnp.where(idx[:, None] > idx[None, :], 1.0, 0.0).astype(BF16)
            x_all, h_ffn, route, counts = _merge(*merge_args, router=(row(ffn_pre_norm[layer]), wr, tri))
            x_all = _routed_ffn(x_all, h_ffn, route, counts, mods, row(ffn_post_norm[layer]),
                                moe_w_gate_up[layer // 2].astype(BF16), moe_w_down[layer // 2].astype(BF16),
                                n_lat_tiles, n_tiles)
    return x_all[:, :n_lat]
```

```python
import functools
import math

import jax
import jax.numpy as jnp
from jax import lax
from jax.experimental import pallas as pl
from jax.experimental.pallas import tpu as pltpu

F32 = jnp.float32
BF16 = jnp.bfloat16

D_MODEL = 1024
GRID_W = 64
ROPE_BASE = 10000.0
EPS = 1e-6
N_MOD = 6
HEAD_DIM = 64
DIFF_HEADS = 4
GQA_Q_HEADS = 8
GQA_KV_HEADS = 2
MLA_HEADS = 8
MLA_Q_RANK = 256
MLA_KV_RANK = 128
MLA_NOPE = 64
MLA_ROPE = 32
MLA_V = 64
BRANCH_W = 512
FOURIER_GROUP = 128
FFN_DIM = 2816
N_EXPERTS = 8
EXPERT_DIM = 3584
LOG2E = math.log2(math.e)
QK_SCALE = HEAD_DIM ** -0.5 * LOG2E
MLA_SCALE = (MLA_NOPE + MLA_ROPE) ** -0.5 * LOG2E
IN_COLS_PADDED = 3328

LANES = 128
VMEM_LIMIT = 56 * 1024 * 1024

TM = 256
TQ = 1024


def _cparams(sem):
    return pltpu.CompilerParams(dimension_semantics=sem, vmem_limit_bytes=VMEM_LIMIT)


def _full_spec(shape):
    n = len(shape)
    return pl.BlockSpec(shape, lambda *_: (0,) * n)


def _split_bf16(x):
    hi = x.astype(BF16)
    lo = (x - hi.astype(F32)).astype(BF16)
    return hi, lo


def _dot(a, b):
    return jnp.dot(a, b, preferred_element_type=F32)


def _dot3(a, w):
    ah, al = _split_bf16(a)
    wh, wl = _split_bf16(w)
    return _dot(ah, wh) + _dot(ah, wl) + _dot(al, wh)


def _rms(x, gain):
    ms = jnp.mean(x * x, axis=-1, keepdims=True)
    return x * lax.rsqrt(ms + EPS) * gain


def _mod_kernel(c_ref, w_ref, b_ref, o_ref):
    c = c_ref[...]
    a = c * jax.nn.sigmoid(c)
    o_ref[...] = _dot3(a, w_ref[...]) + b_ref[...]


def _modulation(c_rows, w_mod_l, b_mod_l):
    rows = c_rows.shape[0]
    n = w_mod_l.shape[1]
    tn = 512
    return pl.pallas_call(
        _mod_kernel,
        grid=(n // tn,),
        in_specs=[pl.BlockSpec((rows, D_MODEL), lambda j: (0, 0)),
                  pl.BlockSpec((D_MODEL, tn), lambda j: (0, j)),
                  pl.BlockSpec((1, tn), lambda j: (0, j))],
        out_specs=pl.BlockSpec((rows, tn), lambda j: (0, j)),
        out_shape=jax.ShapeDtypeStruct((rows, n), F32),
        compiler_params=_cparams(("arbitrary",)),
        name="modulation",
    )(c_rows, w_mod_l, b_mod_l.reshape(1, n))


def _rope(x, cos, sin_signed, half):
    lane = lax.broadcasted_iota(jnp.int32, x.shape, 1)
    first = (lane % (2 * half)) < half
    up = pltpu.roll(x, LANES - half, 1)
    down = pltpu.roll(x, half, 1)
    return x * cos + jnp.where(first, up, down) * sin_signed


def _stream_specs(xs, n_lat_tiles):
    if not isinstance(xs, tuple):
        return [pl.BlockSpec((None, TM, D_MODEL), lambda b, t: (b, t, 0))], [xs]
    return ([pl.BlockSpec((None, TM, D_MODEL), lambda b, t: (b, jnp.minimum(t, n_lat_tiles - 1), 0)),
             pl.BlockSpec((None, TM, D_MODEL), lambda b, t: (b, jnp.maximum(t - n_lat_tiles, 0), 0))], list(xs))


def _stream_tile(x_refs, n_lat_tiles):
    if len(x_refs) == 1:
        return x_refs[0][...]
    return jnp.where(pl.program_id(1) < n_lat_tiles, x_refs[0][...], x_refs[1][...])


def _stream_dims(xs):
    if isinstance(xs, tuple):
        return xs[0].shape[0], xs[0].shape[1] + xs[1].shape[1]
    return xs.shape[0], xs.shape[1]


def _proj_kernel(n_lat_tiles, n_x, *refs):
    x_refs, refs = refs[:n_x], refs[n_x:]
    (mod_ref, gpre_ref, win_ref, gqn_ref, gkn_ref, mqn_ref, mkvn_ref,
     wqup_ref, wkup_ref, wvup_ref, seg_ref, cc_ref, sc_ref,
     cosh_ref, sinh_ref, cosm_ref, sinm_ref,
     h_ref, dq_ref, dk_ref, dv_ref, gq_ref, gk_ref, gv_ref, mq_ref, mk_ref, mv_ref, pq_ref, pqc_ref) = refs
    x = _stream_tile(x_refs, n_lat_tiles)
    mod = mod_ref[...]
    h = _rms(x, gpre_ref[...]) * (1.0 + mod[1:2]) + mod[0:1]
    hb = h.astype(BF16)
    h_ref[...] = hb
    z = _dot(hb, win_ref[...])

    def zcols(lo, hi):
        return z[:, lo:hi]
    cosh, sinh = cosh_ref[...], sinh_ref[...]
    cosm, sinm = cosm_ref[...], sinm_ref[...]
    seg = seg_ref[...]

    def head_rms(v, gain, w):
        sq = v * v
        hi, lo = _split_bf16(sq)
        ms = _dot(hi, seg[:w, :w]) + _dot(lo, seg[:w, :w])
        return v * lax.rsqrt(ms + EPS) * gain

    for j in range(4):
        c0 = j * LANES
        dq_ref[:, c0:c0 + LANES] = (_rope(zcols(c0, c0 + LANES), cosh, sinh, 16) * QK_SCALE).astype(BF16)
        dk_ref[:, c0:c0 + LANES] = _rope(zcols(512 + c0, 512 + c0 + LANES), cosh, sinh, 16).astype(BF16)
    dv_ref[...] = zcols(1024, 1536).astype(BF16)
    fparts = []
    for g in range(4):
        c0 = 1536 + g * LANES
        fb = zcols(c0, c0 + LANES).astype(BF16)
        fparts.append((_dot(fb, cc_ref[...]).astype(BF16), _dot(fb, sc_ref[...]).astype(BF16)))
    is_lat = pl.program_id(1) < n_lat_tiles

    def put(ref):
        for g, (pc, ps) in enumerate(fparts):
            ref[0, :, g * LANES:(g + 1) * LANES] = pc
            ref[1, :, g * LANES:(g + 1) * LANES] = ps

    pl.when(is_lat)(lambda: put(pq_ref))
    pl.when(jnp.logical_not(is_lat))(lambda: put(pqc_ref))
    gq = head_rms(zcols(2048, 2560), gqn_ref[...], 512)
    for j in range(4):
        c0 = j * LANES
        gq_ref[:, c0:c0 + LANES] = (_rope(gq[:, c0:c0 + LANES], cosh, sinh, 16) * QK_SCALE).astype(BF16)
    gk = head_rms(zcols(2560, 2688), gkn_ref[...], LANES)
    gk_ref[...] = _rope(gk, cosh, sinh, 16).astype(BF16)
    gv_ref[...] = zcols(2688, 2816).astype(BF16)
    cq = _rms(zcols(2816, 3072), mqn_ref[...]).astype(BF16)
    mq = _dot(cq, wqup_ref[...])
    ckv = _rms(zcols(3072, 3200), mkvn_ref[...]).astype(BF16)
    mk = _dot(ckv, wkup_ref[...])
    mv_ref[...] = _dot(ckv, wvup_ref[...]).astype(BF16)
    kr = _rope(zcols(3200, 3328), cosm, sinm, 8)
    for hh in range(MLA_HEADS):
        c0 = hh * LANES
        mq_ref[:, c0:c0 + LANES] = (_rope(mq[:, c0:c0 + LANES], cosm, sinm, 8) * MLA_SCALE).astype(BF16)
        mk_ref[:, c0:c0 + LANES] = (mk[:, c0:c0 + LANES] + kr).astype(BF16)


def _project(x_all, mods, gpre, wts, consts, n_lat_tiles):
    B, T = _stream_dims(x_all)
    nt = T // TM
    tok = lambda w: pl.BlockSpec((None, TM, w), lambda b, t: (b, t, 0))
    tab = pl.BlockSpec((TM, LANES), lambda b, t: (t, 0))
    x_specs, x_args = _stream_specs(x_all, n_lat_tiles)
    in_specs = x_specs + [
        pl.BlockSpec((None, None, N_MOD, D_MODEL), lambda b, t: (b, jnp.where(t >= n_lat_tiles, 1, 0), 0, 0)),
        _full_spec((1, D_MODEL)),
        _full_spec((D_MODEL, IN_COLS_PADDED)),
        _full_spec((1, 512)), _full_spec((1, LANES)), _full_spec((1, MLA_Q_RANK)), _full_spec((1, MLA_KV_RANK)),
        _full_spec((MLA_Q_RANK, MLA_HEADS * LANES)),
        _full_spec((MLA_KV_RANK, MLA_HEADS * LANES)),
        _full_spec((MLA_KV_RANK, MLA_HEADS * MLA_V)),
        _full_spec((512, 512)), _full_spec((LANES, LANES)), _full_spec((LANES, LANES)),
        tab, tab, tab, tab,
    ]
    widths = [D_MODEL, 512, 512, 512, 512, LANES, LANES, MLA_HEADS * LANES, MLA_HEADS * LANES, 512]
    out_specs = [tok(w) for w in widths]
    out_shape = [jax.ShapeDtypeStruct((B, T, w), BF16) for w in widths]
    out_specs.append(pl.BlockSpec((2, None, TM, 512), lambda b, t: (0, b, jnp.minimum(t, n_lat_tiles - 1), 0)))
    out_shape.append(jax.ShapeDtypeStruct((2, B, n_lat_tiles * TM, 512), BF16))
    out_specs.append(pl.BlockSpec((2, None, TM, 512), lambda b, t: (0, b, jnp.maximum(t - n_lat_tiles, 0), 0)))
    out_shape.append(jax.ShapeDtypeStruct((2, B, T - n_lat_tiles * TM, 512), BF16))
    return pl.pallas_call(
        functools.partial(_proj_kernel, n_lat_tiles, len(x_args)),
        grid=(B, nt),
        in_specs=in_specs,
        out_specs=out_specs,
        out_shape=out_shape,
        compiler_params=_cparams(("arbitrary", "arbitrary")),
        name="project",
    )(*x_args, mods, gpre, wts["w_in"], wts["gq_norm"], wts["gk_norm"], wts["mq_norm"], wts["mkv_norm"],
      wts["w_q_up"], wts["w_k_up"], wts["w_v_up"], consts["seg"], consts["cc"], consts["sc"],
      consts["cos_h"], consts["sin_h"], consts["cos_m"], consts["sin_m"])


N_MAPS = 8
MAPS_PER_STEP = 2


def _map_plan(mode, m):
    if mode == "diff":
        return m // 2, m % 2, m // 2, m // 2
    if mode == "gqa":
        return m // 2, m % 2, 0, 0
    return m, None, m, m // 2


def _attn_kernel(mode, lam_init, q_ref, k_ref, v_ref, *rest):
    o_ref, vx_ref = rest[-2:]
    if mode == "diff":
        lam_ref, sub_ref = rest[:2]
    nt = (((1,), (1,)), ((), ()))
    lane = lax.broadcasted_iota(jnp.int32, (1, LANES), 1)
    low = lane < HEAD_DIM

    @pl.when(pl.program_id(2) == 0)
    def _():
        for g in range(vx_ref.shape[0]):
            vx_ref[g, :, :LANES] = v_ref[:, g * LANES:(g + 1) * LANES]
            vx_ref[g, :, LANES:] = jnp.ones((vx_ref.shape[1], LANES), BF16)

    def scores(m):
        qblk, half, kblk, _ = _map_plan(mode, m)
        q = q_ref[:, qblk * LANES:(qblk + 1) * LANES]
        if half is not None:
            q = jnp.where(low if half == 0 else jnp.logical_not(low), q, jnp.zeros_like(q))
        return lax.dot_general(q, k_ref[:, kblk * LANES:(kblk + 1) * LANES], nt, preferred_element_type=F32)

    def numerator(s):
        return jnp.exp2(s - jnp.max(s, axis=-1, keepdims=True)).astype(BF16)

    def values(m, p):
        e = _dot(p, vx_ref[_map_plan(mode, m)[3]])
        return e[:, :LANES] / e[:, LANES:]

    n_maps = 2 * o_ref.shape[1] // LANES
    s, pl_, y = {}, {}, {}
    for t in range(n_maps + 2):
        if t < n_maps:
            s[t] = scores(t)
        if 0 <= t - 1 < n_maps:
            pl_[t - 1] = numerator(s.pop(t - 1))
        if 0 <= t - 2:
            y[t - 2] = values(t - 2, pl_.pop(t - 2))

    if mode == "diff":
        dl = lam_ref[...]
        lam = (jnp.exp(jnp.sum(dl[0:1] * dl[1:2], axis=-1, keepdims=True))
               - jnp.exp(jnp.sum(dl[2:3] * dl[3:4], axis=-1, keepdims=True)) + lam_init)
    for j in range(n_maps // 2):
        if mode == "diff":
            out = _rms(y[2 * j] - lam * y[2 * j + 1], sub_ref[...]) * (1.0 - lam_init)
        else:
            out = jnp.where(low, y[2 * j], y[2 * j + 1])
        o_ref[:, j * LANES:(j + 1) * LANES] = out.astype(BF16)


def _attention(mode, q, k, v, y_prev, *, tq, nq, q_blk0, tk, k_blk, maps_per_step, lam_init=0.0, extra=()):
    B, T, _ = q.shape
    n_steps = N_MAPS // maps_per_step
    wq, wk, wv = q.shape[2] // n_steps, max(k.shape[2] // n_steps, LANES), max(v.shape[2] // n_steps, LANES)
    kv_blk = (lambda g: 0) if mode == "gqa" else (lambda g: g)
    in_specs = [
        pl.BlockSpec((None, tq, wq), lambda b, g, i: (b, i + q_blk0, g)),
        pl.BlockSpec((None, tk, wk), lambda b, g, i: (b, k_blk, kv_blk(g))),
        pl.BlockSpec((None, tk, wv), lambda b, g, i: (b, k_blk, kv_blk(g))),
    ]
    args = [q, k, v]
    if mode == "diff":
        in_specs += [_full_spec((4, HEAD_DIM)), _full_spec((1, LANES))]
        args += list(extra)
    aliases = {}
    if y_prev is not None:
        in_specs.append(pl.BlockSpec(memory_space=pl.ANY))
        args.append(y_prev)
        aliases = {len(args) - 1: 0}
    wo = BRANCH_W // n_steps
    return pl.pallas_call(
        functools.partial(_attn_kernel, mode, lam_init),
        grid=(B, n_steps, nq),
        in_specs=in_specs,
        out_specs=pl.BlockSpec((None, tq, wo), lambda b, g, i: (b, i + q_blk0, g)),
        out_shape=jax.ShapeDtypeStruct((B, T, BRANCH_W), BF16),
        input_output_aliases=aliases,
        scratch_shapes=[pltpu.VMEM((wv // LANES, tk, 2 * LANES), BF16)],
        compiler_params=_cparams(("arbitrary", "arbitrary", "arbitrary")),
        name="attention_" + mode,
    )(*args)


DFT_COLS_PER_STEP = 8


DFT_ROW_COLS = 16


def _dft_rows_kernel(pq_ref, m1x_ref, a_ref):
    shape = pq_ref.shape
    z = pq_ref[...].reshape(shape[0] * shape[1] * shape[2], shape[3])
    a_ref[...] = _dot(m1x_ref[...], z).astype(BF16).reshape(shape)


def _dft_rows(pq_lat, m1x):
    _, B, n_lat, w = pq_lat.shape
    r = n_lat // GRID_W
    nb = DFT_ROW_COLS
    return pl.pallas_call(
        _dft_rows_kernel,
        grid=(B, GRID_W // nb),
        in_specs=[pl.BlockSpec((2, None, r, nb, w), lambda b, j: (0, b, 0, j, 0)),
                  _full_spec((2 * r * nb, 2 * r * nb))],
        out_specs=pl.BlockSpec((None, 2, r, nb, w), lambda b, j: (b, 0, 0, j, 0)),
        out_shape=jax.ShapeDtypeStruct((B, 2, r, GRID_W, w), BF16),
        compiler_params=_cparams(("arbitrary", "arbitrary")),
        name="dft_rows",
    )(pq_lat.reshape(2, B, r, GRID_W, w), m1x)


def _dft_cols_kernel(a_ref, ct_ref, st_ref, m3_ref, o_ref):
    kb = a_ref.shape[1]
    w = a_ref.shape[3]
    for j in range(kb):
        ar = a_ref[0, j].astype(F32)
        ai = a_ref[1, j].astype(F32)
        ct = jnp.concatenate([ct_ref[j]] * (w // LANES), axis=1)
        st = jnp.concatenate([st_ref[j]] * (w // LANES), axis=1)
        b = jnp.concatenate([ar * ct + ai * st, ai * ct - ar * st], axis=0).astype(BF16)
        o_ref[:, j, :] = _dot(m3_ref[...], b).astype(BF16)


def _dft_cols(a, ct, st, m3):
    B, _, r, _, w = a.shape
    kb = DFT_COLS_PER_STEP
    return pl.pallas_call(
        _dft_cols_kernel,
        grid=(B, r // kb),
        in_specs=[pl.BlockSpec((None, 2, kb, GRID_W, w), lambda b, k: (b, 0, k, 0, 0)),
                  pl.BlockSpec((kb, GRID_W, LANES), lambda b, k: (k, 0, 0)),
                  pl.BlockSpec((kb, GRID_W, LANES), lambda b, k: (k, 0, 0)),
                  _full_spec((GRID_W, 2 * GRID_W))],
        out_specs=pl.BlockSpec((None, GRID_W, kb, w), lambda b, k: (b, 0, k, 0)),
        out_shape=jax.ShapeDtypeStruct((B, GRID_W, r, w), BF16),
        compiler_params=_cparams(("arbitrary", "arbitrary")),
        name="dft_cols",
    )(a, ct, st, m3)


def _dft_ctx_kernel(pq_ref, w_ref, o_ref):
    zin = jnp.concatenate([pq_ref[0], pq_ref[1]], axis=0)
    o_ref[...] = _dot(w_ref[...], zin).astype(BF16)


def _dft_ctx(pq_ctx, w_ctx):
    _, B, n_ctx, w = pq_ctx.shape
    return pl.pallas_call(
        _dft_ctx_kernel,
        grid=(B,),
        in_specs=[pl.BlockSpec((2, None, n_ctx, w), lambda b: (0, b, 0, 0)), _full_spec((n_ctx, 2 * n_ctx))],
        out_specs=pl.BlockSpec((None, n_ctx, w), lambda b: (b, 0, 0)),
        out_shape=jax.ShapeDtypeStruct((B, n_ctx, w), BF16),
        compiler_params=_cparams(("arbitrary",)),
        name="dft_ctx",
    )(pq_ctx, w_ctx)


def _merge_kernel(n_lat_tiles, n_x, *refs):
    x_refs, refs = refs[:n_x], refs[n_x:]
    h_ref, y0_ref, y1_ref, y2_ref, y3_ref, mod_ref, wg_ref, bg_ref, wb_ref, wo_ref, gpost_ref = refs[:11]
    rest = refs[11:]
    h = h_ref[...]
    ys = (y0_ref, y1_ref, y2_ref, y3_ref)
    acc = None
    for i in range(4):
        gate = jax.nn.sigmoid(_dot(h, wg_ref[i]) + bg_ref[i:i + 1])
        term = gate * _dot(ys[i][...], wb_ref[i])
        acc = term if acc is None else acc + term
    m = _dot(acc.astype(BF16), wo_ref[...])
    mod = mod_ref[...]
    x = _stream_tile(x_refs, n_lat_tiles) + mod[2:3] * _rms(m, gpost_ref[...])
    o_ref = rest[3] if len(rest) > 1 else rest[0]
    o_ref[...] = x
    if len(rest) > 1:
        gffn_ref, wr_ref, tri_ref, _, hf_ref, route_ref, cnt_ref, carry_ref = rest
        _route_tile(x, mod, gffn_ref, wr_ref, tri_ref, hf_ref, route_ref, cnt_ref, carry_ref)


def _merge(x_all, h, y_diff, y_f, y_gqa, y_mla, mods, wts, gpost, n_lat_tiles, n_tiles, router=None):
    B, _ = _stream_dims(x_all)
    tok = lambda w: pl.BlockSpec((None, TM, w), lambda b, t: (b, t, 0))
    x_specs, x_args = _stream_specs(x_all, n_lat_tiles)
    in_specs = x_specs + [
        tok(D_MODEL), tok(BRANCH_W), tok(BRANCH_W), tok(BRANCH_W), tok(BRANCH_W),
        pl.BlockSpec((None, None, N_MOD, D_MODEL), lambda b, t: (b, jnp.where(t >= n_lat_tiles, 1, 0), 0, 0)),
        _full_spec((4, D_MODEL, D_MODEL)), _full_spec((4, D_MODEL)), _full_spec((4, BRANCH_W, D_MODEL)),
        _full_spec((D_MODEL, D_MODEL)), _full_spec((1, D_MODEL)),
    ]
    args = x_args + [h, y_diff, y_f, y_gqa, y_mla, mods, wts["w_gate"], wts["b_gate"], wts["w_branch"],
                     wts["w_out"], gpost]
    out_specs = [tok(D_MODEL)]
    out_shape = [jax.ShapeDtypeStruct((B, n_tiles * TM, D_MODEL), F32)]
    scratch = []
    if router is not None:
        in_specs += [_full_spec((1, D_MODEL)), _full_spec((D_MODEL, LANES)), _full_spec((TM, TM))]
        args += list(router)
        out_specs += [tok(D_MODEL), tok(LANES), _full_spec((1, LANES))]
        out_shape += [jax.ShapeDtypeStruct((B, n_tiles * TM, D_MODEL), F32),
                      jax.ShapeDtypeStruct((B, n_tiles * TM, LANES), F32),
                      jax.ShapeDtypeStruct((1, LANES), F32)]
        scratch = [pltpu.VMEM((1, LANES), F32)]
    out = pl.pallas_call(
        functools.partial(_merge_kernel, n_lat_tiles, len(x_args)),
        grid=(B, n_tiles),
        in_specs=in_specs,
        out_specs=out_specs,
        out_shape=out_shape,
        scratch_shapes=scratch,
        compiler_params=_cparams(("arbitrary", "arbitrary")),
        name="merge",
    )(*args)
    return out[0] if router is None else out


def _dense_ffn_kernel(nf, tf, x_ref, mod_ref, gpre_ref, gpost_ref, wgu_ref, wd_ref, o_ref):
    x = x_ref[...]
    mod = mod_ref[...]
    hb = (_rms(x, gpre_ref[...]) * (1.0 + mod[4:5]) + mod[3:4]).astype(BF16)
    acc = None
    for f in range(nf):
        a = _dot(hb, wgu_ref[:, f * tf:(f + 1) * tf])
        u = _dot(hb, wgu_ref[:, FFN_DIM + f * tf:FFN_DIM + (f + 1) * tf])
        act = (a * jax.nn.sigmoid(a) * u).astype(BF16)
        part = _dot(act, wd_ref[f * tf:(f + 1) * tf, :])
        acc = part if acc is None else acc + part
    o_ref[...] = x + mod[5:6] * _rms(acc, gpost_ref[...])


def _dense_ffn(x_all, mods, gpre, gpost, wgu, wd, n_lat_tiles, n_tiles):
    B = x_all.shape[0]
    nf, tf = 1, FFN_DIM
    tok = pl.BlockSpec((None, TM, D_MODEL), lambda b, t: (b, t, 0))
    return pl.pallas_call(
        functools.partial(_dense_ffn_kernel, nf, tf),
        grid=(B, n_tiles),
        in_specs=[tok,
                  pl.BlockSpec((None, None, N_MOD, D_MODEL),
                               lambda b, t: (b, jnp.where(t >= n_lat_tiles, 1, 0), 0, 0)),
                  _full_spec((1, D_MODEL)), _full_spec((1, D_MODEL)),
                  _full_spec((D_MODEL, 2 * FFN_DIM)), _full_spec((FFN_DIM, D_MODEL))],
        out_specs=tok,
        out_shape=jax.ShapeDtypeStruct((B, n_tiles * TM, D_MODEL), F32),
        compiler_params=_cparams(("arbitrary", "arbitrary")),
        name="dense_ffn",
    )(x_all, mods, gpre, gpost, wgu, wd)


TE = 512
R_E1, R_E2, R_G1, R_G2, R_RANK1, R_RANK2 = range(6)


def _route_tile(x, mod, gpre_ref, wr_ref, tri_ref, h_ref, route_ref, cnt_ref, carry_ref):
    first = jnp.logical_and(pl.program_id(0) == 0, pl.program_id(1) == 0)

    @pl.when(first)
    def _():
        carry_ref[...] = jnp.zeros_like(carry_ref)

    h = _rms(x, gpre_ref[...]) * (1.0 + mod[4:5]) + mod[3:4]
    h_ref[...] = h
    logits = _dot3(h, wr_ref[...])
    lane = lax.broadcasted_iota(jnp.int32, logits.shape, 1)
    neg = jnp.float32(-jnp.inf)
    l1 = jnp.where(lane < N_EXPERTS, logits, neg)
    m1 = jnp.max(l1, axis=-1, keepdims=True)
    i1 = jnp.min(jnp.where(l1 == m1, lane, LANES), axis=-1, keepdims=True)
    l2 = jnp.where(lane == i1, neg, l1)
    m2 = jnp.max(l2, axis=-1, keepdims=True)
    i2 = jnp.min(jnp.where(l2 == m2, lane, LANES), axis=-1, keepdims=True)
    e2 = jnp.exp(m2 - m1)
    g1 = 1.0 / (1.0 + e2)
    g2 = e2 / (1.0 + e2)
    sel1, sel2 = lane == i1, lane == i2
    chosen = jnp.where(jnp.logical_or(sel1, sel2), 1.0, 0.0)
    before = _dot(tri_ref[...], chosen.astype(BF16)) + carry_ref[...]
    r1 = jnp.sum(jnp.where(sel1, before, 0.0), axis=-1, keepdims=True)
    r2 = jnp.sum(jnp.where(sel2, before, 0.0), axis=-1, keepdims=True)
    carry_ref[...] += jnp.sum(chosen, axis=0, keepdims=True)
    cnt_ref[...] = carry_ref[...]
    rec = jnp.zeros_like(logits)
    for col, val in ((R_E1, i1.astype(F32)), (R_E2, i2.astype(F32)), (R_G1, g1), (R_G2, g2),
                     (R_RANK1, r1), (R_RANK2, r2)):
        rec = jnp.where(lane == col, val, rec)
    route_ref[...] = rec


def _dispatch_kernel(last_ref, pos_ref, h_ref, xs_ref, zero_ref, stage_ref, sems):
    i = pl.program_id(0)
    n = h_ref.shape[0]

    @pl.when(i == 0)
    def _():
        zero_ref[...] = jnp.zeros_like(zero_ref)
        for e in range(N_EXPERTS):
            @pl.when(last_ref[e] >= 0)
            def _():
                fill = pltpu.make_async_copy(zero_ref, xs_ref.at[pl.ds(pl.multiple_of(last_ref[e], TE), TE)],
                                             sems.at[0])
                fill.start()
                fill.wait()

    cur = i % 2
    stage_ref[cur] = h_ref[...]

    def start(r, carry):
        for slot in range(2):
            pltpu.make_async_copy(stage_ref.at[cur, pl.ds(r, 1)], xs_ref.at[pl.ds(pos_ref[0, 2 * r + slot], 1)],
                                  sems.at[cur]).start()
        return carry

    lax.fori_loop(0, n, start, 0, unroll=8)

    def drain(half):
        for _ in range(2):
            pltpu.make_async_copy(stage_ref.at[half], xs_ref.at[pl.ds(0, n)], sems.at[half]).wait()

    pl.when(i > 0)(lambda: drain(1 - cur))
    pl.when(i == pl.num_programs(0) - 1)(lambda: drain(cur))


def _dispatch(h, pos, last_tile_row, n_rows):
    n = h.shape[0]
    grid_spec = pltpu.PrefetchScalarGridSpec(
        num_scalar_prefetch=1,
        grid=(n // TM,),
        in_specs=[pl.BlockSpec((None, 1, 2 * TM), lambda i, last: (i, 0, 0), memory_space=pltpu.SMEM),
                  pl.BlockSpec((TM, D_MODEL), lambda i, last: (i, 0))],
        out_specs=pl.BlockSpec(memory_space=pl.ANY),
        scratch_shapes=[pltpu.VMEM((TE, D_MODEL), F32), pltpu.VMEM((2, TM, D_MODEL), F32),
                        pltpu.SemaphoreType.DMA((2,))],
    )
    return pl.pallas_call(
        _dispatch_kernel,
        grid_spec=grid_spec,
        out_shape=jax.ShapeDtypeStruct((n_rows, D_MODEL), F32),
        compiler_params=_cparams(("arbitrary",)),
        name="moe_dispatch",
    )(last_tile_row, pos.reshape(n // TM, 1, 2 * TM), h)


def _experts_kernel(te_ref, nv_ref, xs_ref, wa_ref, wu_ref, wd_ref, ys_ref):
    del te_ref

    @pl.when(pl.program_id(0) < nv_ref[0])
    def _():
        x = xs_ref[...].astype(BF16)
        a = _dot(x, wa_ref[...])
        u = _dot(x, wu_ref[...])
        ys_ref[...] = _dot((a * jax.nn.sigmoid(a) * u).astype(BF16), wd_ref[...])


def _experts(xs, tile_expert, n_valid, wgu, wd):
    n_rows = xs.shape[0]
    row_blk = lambda i, te, nv: (jnp.minimum(i, nv[0] - 1), 0)
    resident = pl.Buffered(1)
    grid_spec = pltpu.PrefetchScalarGridSpec(
        num_scalar_prefetch=2,
        grid=(n_rows // TE,),
        in_specs=[pl.BlockSpec((TE, D_MODEL), row_blk),
                  pl.BlockSpec((None, D_MODEL, EXPERT_DIM), lambda i, te, nv: (te[i], 0, 0), pipeline_mode=resident),
                  pl.BlockSpec((None, D_MODEL, EXPERT_DIM), lambda i, te, nv: (te[i], 0, 1), pipeline_mode=resident),
                  pl.BlockSpec((None, EXPERT_DIM, D_MODEL), lambda i, te, nv: (te[i], 0, 0), pipeline_mode=resident)],
        out_specs=pl.BlockSpec((TE, D_MODEL), row_blk),
    )
    return pl.pallas_call(
        _experts_kernel,
        grid_spec=grid_spec,
        out_shape=jax.ShapeDtypeStruct((n_rows, D_MODEL), F32),
        compiler_params=_cparams(("arbitrary",)),
        name="moe_experts",
    )(tile_expert, n_valid, xs, wgu, wgu, wd)


def _combine_kernel(pos_ref, nxt_ref, x_ref, route_ref, mod_ref, gpost_ref, ys_ref, o_ref, buf_ref, sems):
    i = pl.program_id(0)
    n = x_ref.shape[0]

    def gather(idx_ref, half):
        def start(r, carry):
            for slot in range(2):
                pltpu.make_async_copy(ys_ref.at[pl.ds(idx_ref[0, 2 * r + slot], 1)],
                                      buf_ref.at[half, slot, pl.ds(r, 1)], sems.at[half]).start()
            return carry
        lax.fori_loop(0, n, start, 0, unroll=8)

    cur = i % 2

    @pl.when(i == 0)
    def _():
        gather(pos_ref, 0)

    @pl.when(i + 1 < pl.num_programs(0))
    def _():
        gather(nxt_ref, 1 - cur)

    for slot in range(2):
        pltpu.make_async_copy(ys_ref.at[pl.ds(0, n)], buf_ref.at[cur, slot], sems.at[cur]).wait()
    route = route_ref[...]
    y = route[:, R_G1:R_G1 + 1] * buf_ref[cur, 0] + route[:, R_G2:R_G2 + 1] * buf_ref[cur, 1]
    mod = mod_ref[...]
    o_ref[...] = x_ref[...] + mod[5:6] * _rms(y, gpost_ref[...])


def _combine(x_all, route, pos, ys, mods, gpost, n_lat_tiles, n_tiles):
    B = x_all.shape[0]
    n_steps = B * n_tiles
    tok = lambda w: pl.BlockSpec((None, TM, w), lambda i: (i // n_tiles, i % n_tiles, 0))
    idx = lambda ahead: pl.BlockSpec((None, 1, 2 * TM), lambda i: (jnp.minimum(i + ahead, n_steps - 1), 0, 0),
                                     memory_space=pltpu.SMEM)
    pos_tiles = pos.reshape(n_steps, 1, 2 * TM)
    return pl.pallas_call(
        _combine_kernel,
        grid=(n_steps,),
        in_specs=[idx(0), idx(1), tok(D_MODEL), tok(LANES),
                  pl.BlockSpec((None, None, N_MOD, D_MODEL),
                               lambda i: (i // n_tiles, jnp.where(i % n_tiles >= n_lat_tiles, 1, 0), 0, 0)),
                  _full_spec((1, D_MODEL)),
                  pl.BlockSpec(memory_space=pl.ANY)],
        out_specs=tok(D_MODEL),
        out_shape=jax.ShapeDtypeStruct((B, n_tiles * TM, D_MODEL), F32),
        scratch_shapes=[pltpu.VMEM((2, 2, TM, D_MODEL), F32), pltpu.SemaphoreType.DMA((2,))],
        compiler_params=_cparams(("arbitrary",)),
        name="moe_combine",
    )(pos_tiles, pos_tiles, x_all, route, mods, gpost, ys)


def _routed_ffn(x_all, h, route, counts, mods, gpost, wgu, wd, n_lat_tiles, n_tiles):
    B = x_all.shape[0]
    n_tok = B * n_tiles * TM
    rec = route.reshape(n_tok, LANES)
    expert = rec[:, R_E1:R_E2 + 1].astype(jnp.int32)
    rank = rec[:, R_RANK1:R_RANK2 + 1].astype(jnp.int32)
    tiles_per_expert = (counts[0, :N_EXPERTS].astype(jnp.int32) + TE - 1) // TE
    tile_end = jnp.cumsum(tiles_per_expert)
    n_valid = tile_end[-1:]
    pos = (tile_end - tiles_per_expert)[expert] * TE + rank
    n_tiles_max = (2 * n_tok) // TE + N_EXPERTS
    tile_id = jnp.minimum(jnp.arange(n_tiles_max, dtype=jnp.int32), n_valid - 1)
    tile_expert = jnp.minimum(jnp.searchsorted(tile_end, tile_id, side="right"), N_EXPERTS - 1).astype(jnp.int32)
    last_tile_row = jnp.where(tiles_per_expert > 0, (tile_end - 1) * TE, -1).astype(jnp.int32)
    xs = _dispatch(h.reshape(n_tok, D_MODEL), pos, last_tile_row, n_tiles_max * TE)
    ys = _experts(xs, tile_expert, n_valid.astype(jnp.int32), wgu, wd)
    return _combine(x_all, route, pos, ys, mods, gpost, n_lat_tiles, n_tiles)


def _rope_angles(n_lat, rot_dim):
    rows = n_lat // GRID_W
    row = jnp.repeat(jnp.arange(rows, dtype=F32), GRID_W)
    col = jnp.tile(jnp.arange(GRID_W, dtype=F32), rows)
    axis_dim = rot_dim // 2
    inv_freq = 1.0 / (ROPE_BASE ** (jnp.arange(0, axis_dim, 2, dtype=F32) / axis_dim))
    ar = row[:, None] * inv_freq
    ac = col[:, None] * inv_freq
    return jnp.concatenate([ar, ar, ac, ac], axis=-1)


def _row_stage_matrix(r):
    nb = DFT_ROW_COLS
    size = 2 * r * nb
    i = lax.broadcasted_iota(jnp.int32, (size, size), 0)
    j = lax.broadcasted_iota(jnp.int32, (size, size), 1)
    part_i, k1, col_i = i // (r * nb), (i % (r * nb)) // nb, i % nb
    part_j, n1, col_j = j // (r * nb), (j % (r * nb)) // nb, j % nb
    ang = ((k1 * n1) % r).astype(F32) * (2.0 * math.pi / r)
    val = jnp.where(part_i == part_j, jnp.cos(ang), -jnp.sin(ang))
    val = jnp.where(jnp.logical_and(part_i == 1, part_j == 1), -val, val)
    return jnp.where(col_i == col_j, val, 0.0)


def _constants(n_lat, n_ctx):
    lane = jnp.arange(LANES)
    ang = _rope_angles(n_lat, HEAD_DIM)
    cos_h = jnp.tile(jnp.cos(ang), (1, 2))
    sin_h = jnp.tile(jnp.sin(ang), (1, 2)) * jnp.where((lane % 32) < 16, -1.0, 1.0)
    cos_h = jnp.concatenate([cos_h, jnp.ones((n_ctx, LANES), F32)], axis=0)
    sin_h = jnp.concatenate([sin_h, jnp.zeros((n_ctx, LANES), F32)], axis=0)
    angm = _rope_angles(n_lat, MLA_ROPE)
    pad_l = jnp.zeros((n_lat, MLA_NOPE), F32)
    pad_r = jnp.zeros((n_lat, LANES - MLA_NOPE - MLA_ROPE), F32)
    cos_m = jnp.concatenate([pad_l + 1.0, jnp.cos(angm), pad_r + 1.0], axis=1)
    sin_m = jnp.concatenate([pad_l, jnp.sin(angm), pad_r], axis=1) * jnp.where((lane % 16) < 8, -1.0, 1.0)
    cos_m = jnp.concatenate([cos_m, jnp.ones((n_ctx, LANES), F32)], axis=0)
    sin_m = jnp.concatenate([sin_m, jnp.zeros((n_ctx, LANES), F32)], axis=0)
    idx = jnp.arange(512)
    seg = jnp.where((idx[:, None] // HEAD_DIM) == (idx[None, :] // HEAD_DIM), 1.0 / HEAD_DIM, 0.0).astype(BF16)

    def dft(n):
        i = jnp.arange(n, dtype=jnp.int32)
        a = ((i[:, None] * i[None, :]) % n).astype(F32) * (2.0 * math.pi / n)
        s = n ** -0.5
        return jnp.cos(a) * s, jnp.sin(a) * s

    cc, sc = dft(FOURIER_GROUP)
    cx, sx = dft(n_ctx)
    r = n_lat // GRID_W
    c64, s64 = (t * GRID_W ** 0.5 * n_lat ** -0.5 for t in dft(GRID_W))
    k1 = jnp.arange(r, dtype=jnp.int32)[:, None]
    n2 = jnp.arange(GRID_W, dtype=jnp.int32)[None, :]
    tw = ((k1 * n2) % n_lat).astype(F32) * (2.0 * math.pi / n_lat)
    wide = lambda t: jnp.broadcast_to(t[:, :, None], (r, GRID_W, LANES))
    return {
        "cos_h": cos_h, "sin_h": sin_h, "cos_m": cos_m, "sin_m": sin_m, "seg": seg,
        "cc": cc.astype(BF16), "sc": sc.astype(BF16),
        "m1x": _row_stage_matrix(r).astype(BF16),
        "tw_cos": wide(jnp.cos(tw)), "tw_sin": wide(jnp.sin(tw)),
        "m3": jnp.concatenate([c64, s64], axis=1).astype(BF16),
        "w_ctx": jnp.concatenate([cx, -sx], axis=1).astype(BF16),
    }


_GQA_HEAD_ORDER = (0, 4, 1, 5, 2, 6, 3, 7)


def _layer_weights(l, w_in, gqa_q_norm, gqa_k_norm, mla_q_norm, mla_kv_norm, mla_w_q_up, mla_w_kv_up,
                   w_branch, w_gate, b_gate, w_out):
    wi = w_in[l]
    gq = wi[:, 2048:2560].reshape(D_MODEL, GQA_Q_HEADS, HEAD_DIM)[:, jnp.array(_GQA_HEAD_ORDER)]
    zeros = lambda n: jnp.zeros((D_MODEL, n), F32)
    w_in_p = jnp.concatenate([wi[:, :2048], gq.reshape(D_MODEL, 512), wi[:, 2560:3200],
                              zeros(MLA_NOPE), wi[:, 3200:3232], zeros(LANES - MLA_NOPE - MLA_ROPE)], axis=1)
    qu = mla_w_q_up[l].reshape(MLA_Q_RANK, MLA_HEADS, MLA_NOPE + MLA_ROPE)
    qu = jnp.pad(qu, ((0, 0), (0, 0), (0, LANES - MLA_NOPE - MLA_ROPE))).reshape(MLA_Q_RANK, MLA_HEADS * LANES)
    kvu = mla_w_kv_up[l].reshape(MLA_KV_RANK, MLA_HEADS, MLA_NOPE + MLA_V)
    ku = jnp.pad(kvu[:, :, :MLA_NOPE], ((0, 0), (0, 0), (0, LANES - MLA_NOPE))).reshape(MLA_KV_RANK, -1)
    vu = kvu[:, :, MLA_NOPE:].reshape(MLA_KV_RANK, MLA_HEADS * MLA_V)
    wb = w_branch[l]
    wb_gqa = wb[2].reshape(GQA_Q_HEADS, HEAD_DIM, D_MODEL)[jnp.array(_GQA_HEAD_ORDER)].reshape(BRANCH_W, D_MODEL)
    wb = jnp.stack([wb[0], wb[1], wb_gqa, wb[3]])
    return {
        "w_in": w_in_p.astype(BF16),
        "gq_norm": jnp.tile(gqa_q_norm[l], GQA_Q_HEADS).reshape(1, 512),
        "gk_norm": jnp.tile(gqa_k_norm[l], GQA_KV_HEADS).reshape(1, LANES),
        "mq_norm": mla_q_norm[l].reshape(1, MLA_Q_RANK),
        "mkv_norm": mla_kv_norm[l].reshape(1, MLA_KV_RANK),
        "w_q_up": qu.astype(BF16), "w_k_up": ku.astype(BF16), "w_v_up": vu.astype(BF16),
        "w_branch": wb.astype(BF16), "w_gate": w_gate[l].astype(BF16), "b_gate": b_gate[l],
        "w_out": w_out[l].astype(BF16),
    }


def kernel(x, c, ctx, c_ctx, w_mod, b_mod, mix_pre_norm, mix_post_norm, ffn_pre_norm, ffn_post_norm,
           w_in, diff_lambda, diff_subnorm, gqa_q_norm, gqa_k_norm, mla_q_norm, mla_kv_norm,
           mla_w_q_up, mla_w_kv_up, w_branch, w_gate, b_gate, w_out,
           dense_w_gate_up, dense_w_down, moe_router, moe_w_gate_up, moe_w_down):
    B, n_lat, d = x.shape
    n_ctx = ctx.shape[1]
    depth = w_mod.shape[0]
    assert d == D_MODEL and n_lat % TM == 0 and n_ctx % TM == 0 and n_lat % n_ctx == 0
    assert n_lat % GRID_W == 0
    T = n_lat + n_ctx
    n_lat_tiles, n_all_tiles = n_lat // TM, T // TM
    consts = _constants(n_lat, n_ctx)
    x_all = (x, ctx)
    c_rows = jnp.pad(jnp.concatenate([c, c_ctx[None]], axis=0), ((0, (-(B + 1)) % 8), (0, 0)))
    row = lambda v: v.reshape(1, -1)

    for layer in range(depth):
        last = layer == depth - 1
        n_tiles = n_lat_tiles if last else n_all_tiles
        lam_init = 0.8 - 0.6 * math.exp(-0.3 * layer)
        wts = _layer_weights(layer, w_in, gqa_q_norm, gqa_k_norm, mla_q_norm, mla_kv_norm, mla_w_q_up,
                             mla_w_kv_up, w_branch, w_gate, b_gate, w_out)
        mod_rows = _modulation(c_rows, w_mod[layer], b_mod[layer])
        mods = jnp.stack([mod_rows[:B].reshape(B, N_MOD, D_MODEL),
                          jnp.broadcast_to(mod_rows[B].reshape(1, N_MOD, D_MODEL), (B, N_MOD, D_MODEL))], axis=1)

        (h, dq, dk, dv, gq, gk, gv, mq, mk, mv, pq_lat, pq_ctx) = _project(
            x_all, mods, row(mix_pre_norm[layer]), wts, consts, n_lat_tiles)

        diff_extra = (diff_lambda[layer], diff_subnorm[layer].reshape(1, LANES))
        ys = []
        for mode, q, k, v in (("diff", dq, dk, dv), ("gqa", gq, gk, gv), ("mla", mq, mk, mv)):
            kw = dict(lam_init=lam_init, extra=diff_extra)
            tq = TQ if n_lat % TQ == 0 else TM
            y = _attention(mode, q, k, v, None, tq=tq, nq=n_lat // tq, q_blk0=0, tk=T, k_blk=0,
                           maps_per_step=MAPS_PER_STEP, **kw)
            if not last:
                y = _attention(mode, q, k, v, y, tq=TM, nq=n_ctx // TM, q_blk0=n_lat_tiles,
                               tk=n_ctx, k_blk=n_lat // n_ctx, maps_per_step=N_MAPS, **kw)
            ys.append(y)
        y_diff, y_gqa, y_mla = ys

        a_rows = _dft_rows(pq_lat, consts["m1x"])
        y_f = _dft_cols(a_rows, consts["tw_cos"], consts["tw_sin"], consts["m3"]).reshape(B, n_lat, BRANCH_W)
        if not last:
            y_f = jnp.concatenate([y_f, _dft_ctx(pq_ctx, consts["w_ctx"])], axis=1)

        merge_args = (x_all, h, y_diff, y_f, y_gqa, y_mla, mods, wts, row(mix_post_norm[layer]),
                      n_lat_tiles, n_tiles)
        if layer % 2 == 0:
            x_all = _merge(*merge_args)
            x_all = _dense_ffn(x_all, mods, row(ffn_pre_norm[layer]), row(ffn_post_norm[layer]),
                               dense_w_gate_up[layer // 2].astype(BF16), dense_w_down[layer // 2].astype(BF16),
                               n_lat_tiles, n_tiles)
        else:
            wr = jnp.pad(moe_router[layer // 2], ((0, 0), (0, LANES - N_EXPERTS)))
            idx = jnp.arange(TM)
            tri = jnp.where(idx[:, None] > idx[None, :], 1.0, 0.0).astype(BF16)
            x_all, h_ffn, route, counts = _merge(*merge_args, router=(row(ffn_pre_norm[layer]), wr, tri))
            x_all = _routed_ffn(x_all, h_ffn, route, counts, mods, row(ffn_post_norm[layer]),
                                moe_w_gate_up[layer // 2].astype(BF16), moe_w_down[layer // 2].astype(BF16),
                                n_lat_tiles, n_tiles)
    return x_all[:, :n_lat]
```

```python
import functools
import math

import jax
import jax.numpy as jnp
from jax import lax
from jax.experimental import pallas as pl
from jax.experimental.pallas import tpu as pltpu

F32 = jnp.float32
BF16 = jnp.bfloat16

D_MODEL = 1024
GRID_W = 64
ROPE_BASE = 10000.0
EPS = 1e-6
N_MOD = 6
HEAD_DIM = 64
GQA_Q_HEADS = 8
GQA_KV_HEADS = 2
MLA_HEADS = 8
MLA_Q_RANK = 256
MLA_KV_RANK = 128
MLA_NOPE = 64
MLA_ROPE = 32
MLA_V = 64
BRANCH_W = 512
FOURIER_GROUP = 128
FFN_DIM = 2816
N_EXPERTS = 8
EXPERT_DIM = 3584
LOG2E = math.log2(math.e)
QK_SCALE = HEAD_DIM ** -0.5 * LOG2E
MLA_SCALE = (MLA_NOPE + MLA_ROPE) ** -0.5 * LOG2E
IN_COLS_PADDED = 3328
COL = dict(dq=0, dk=512, dv=1024, fourier=1536, gq=2048, gk=2560, gv=2688, mcq=2816, mckv=3072, kr=3200)

LANES = 128
VMEM_LIMIT = 56 * 1024 * 1024

TM = 256
TQ = 1024


def _cparams(sem):
    return pltpu.CompilerParams(dimension_semantics=sem, vmem_limit_bytes=VMEM_LIMIT)


def _full_spec(shape):
    n = len(shape)
    return pl.BlockSpec(shape, lambda *_: (0,) * n)


def _split_bf16(x):
    hi = x.astype(BF16)
    lo = (x - hi.astype(F32)).astype(BF16)
    return hi, lo


def _dot(a, b):
    return jnp.dot(a, b, preferred_element_type=F32)


def _dot3(a, w):
    ah, al = _split_bf16(a)
    wh, wl = _split_bf16(w)
    return _dot(ah, wh) + _dot(ah, wl) + _dot(al, wh)


def _rms(x, gain):
    ms = jnp.mean(x * x, axis=-1, keepdims=True)
    return x * lax.rsqrt(ms + EPS) * gain


def _mod_kernel(c_ref, w_ref, b_ref, o_ref):
    c = c_ref[...]
    a = c * jax.nn.sigmoid(c)
    o_ref[...] = _dot3(a, w_ref[...]) + b_ref[...]


def _modulation(c_rows, w_mod_l, b_mod_l):
    rows = c_rows.shape[0]
    n = w_mod_l.shape[1]
    tn = 512
    return pl.pallas_call(
        _mod_kernel,
        grid=(n // tn,),
        in_specs=[pl.BlockSpec((rows, D_MODEL), lambda j: (0, 0)),
                  pl.BlockSpec((D_MODEL, tn), lambda j: (0, j)),
                  pl.BlockSpec((1, tn), lambda j: (0, j))],
        out_specs=pl.BlockSpec((rows, tn), lambda j: (0, j)),
        out_shape=jax.ShapeDtypeStruct((rows, n), F32),
        compiler_params=_cparams(("arbitrary",)),
        name="modulation",
    )(c_rows, w_mod_l, b_mod_l.reshape(1, n))


def _rope(x, cos, sin_signed, half):
    lane = lax.broadcasted_iota(jnp.int32, x.shape, 1)
    first = (lane % (2 * half)) < half
    up = pltpu.roll(x, LANES - half, 1)
    down = pltpu.roll(x, half, 1)
    return x * cos + jnp.where(first, up, down) * sin_signed


def _stream_specs(xs, n_lat_tiles):
    if not isinstance(xs, tuple):
        return [pl.BlockSpec((None, TM, D_MODEL), lambda b, t: (b, t, 0))], [xs]
    return ([pl.BlockSpec((None, TM, D_MODEL), lambda b, t: (b, jnp.minimum(t, n_lat_tiles - 1), 0)),
             pl.BlockSpec((None, TM, D_MODEL), lambda b, t: (b, jnp.maximum(t - n_lat_tiles, 0), 0))], list(xs))


def _stream_tile(x_refs, n_lat_tiles):
    if len(x_refs) == 1:
        return x_refs[0][...]
    return jnp.where(pl.program_id(1) < n_lat_tiles, x_refs[0][...], x_refs[1][...])


def _stream_dims(xs):
    if isinstance(xs, tuple):
        return xs[0].shape[0], xs[0].shape[1] + xs[1].shape[1]
    return xs.shape[0], xs.shape[1]


def _proj_kernel(n_lat_tiles, n_x, *refs):
    x_refs, refs = refs[:n_x], refs[n_x:]
    (mod_ref, gpre_ref, win_ref, gqn_ref, gkn_ref, mqn_ref, mkvn_ref,
     wqup_ref, wkup_ref, wvup_ref, seg_ref, cc_ref, sc_ref,
     cosh_ref, sinh_ref, cosm_ref, sinm_ref,
     h_ref, dq_ref, dk_ref, dv_ref, gq_ref, gk_ref, gv_ref, mq_ref, mk_ref, mv_ref, pq_ref, pqc_ref) = refs
    x = _stream_tile(x_refs, n_lat_tiles)
    mod = mod_ref[...]
    h = _rms(x, gpre_ref[...]) * (1.0 + mod[1:2]) + mod[0:1]
    hb = h.astype(BF16)
    h_ref[...] = hb
    z = _dot(hb, win_ref[...])

    def zsec(name, off=0, width=LANES):
        return z[:, COL[name] + off:COL[name] + off + width]

    cosh, sinh = cosh_ref[...], sinh_ref[...]
    cosm, sinm = cosm_ref[...], sinm_ref[...]
    seg = seg_ref[...]

    def head_rms(v, gain, w):
        sq = v * v
        hi, lo = _split_bf16(sq)
        ms = _dot(hi, seg[:w, :w]) + _dot(lo, seg[:w, :w])
        return v * lax.rsqrt(ms + EPS) * gain

    for j in range(4):
        c0 = j * LANES
        dq_ref[:, c0:c0 + LANES] = (_rope(zsec("dq", c0), cosh, sinh, 16) * QK_SCALE).astype(BF16)
        dk_ref[:, c0:c0 + LANES] = _rope(zsec("dk", c0), cosh, sinh, 16).astype(BF16)
    dv_ref[...] = zsec("dv", 0, BRANCH_W).astype(BF16)
    fparts = []
    for g in range(4):
        fb = zsec("fourier", g * LANES).astype(BF16)
        fparts.append((_dot(fb, cc_ref[...]).astype(BF16), _dot(fb, sc_ref[...]).astype(BF16)))
    is_lat = pl.program_id(1) < n_lat_tiles

    def put(ref):
        for g, (pc, ps) in enumerate(fparts):
            ref[0, :, g * LANES:(g + 1) * LANES] = pc
            ref[1, :, g * LANES:(g + 1) * LANES] = ps

    pl.when(is_lat)(lambda: put(pq_ref))
    pl.when(jnp.logical_not(is_lat))(lambda: put(pqc_ref))
    gq = head_rms(zsec("gq", 0, 512), gqn_ref[...], 512)
    for j in range(4):
        c0 = j * LANES
        gq_ref[:, c0:c0 + LANES] = (_rope(gq[:, c0:c0 + LANES], cosh, sinh, 16) * QK_SCALE).astype(BF16)
    gk = head_rms(zsec("gk"), gkn_ref[...], LANES)
    gk_ref[...] = _rope(gk, cosh, sinh, 16).astype(BF16)
    gv_ref[...] = zsec("gv").astype(BF16)
    cq = _rms(zsec("mcq", 0, MLA_Q_RANK), mqn_ref[...]).astype(BF16)
    mq = _dot(cq, wqup_ref[...])
    ckv = _rms(zsec("mckv", 0, MLA_KV_RANK), mkvn_ref[...]).astype(BF16)
    mk = _dot(ckv, wkup_ref[...])
    mv_ref[...] = _dot(ckv, wvup_ref[...]).astype(BF16)
    kr = _rope(zsec("kr"), cosm, sinm, 8)
    for hh in range(MLA_HEADS):
        c0 = hh * LANES
        mq_ref[:, c0:c0 + LANES] = (_rope(mq[:, c0:c0 + LANES], cosm, sinm, 8) * MLA_SCALE).astype(BF16)
        mk_ref[:, c0:c0 + LANES] = (mk[:, c0:c0 + LANES] + kr).astype(BF16)


def _project(x_all, mods, gpre, wts, consts, n_lat_tiles):
    B, T = _stream_dims(x_all)
    nt = T // TM
    tok = lambda w: pl.BlockSpec((None, TM, w), lambda b, t: (b, t, 0))
    tab = pl.BlockSpec((TM, LANES), lambda b, t: (t, 0))
    x_specs, x_args = _stream_specs(x_all, n_lat_tiles)
    in_specs = x_specs + [
        pl.BlockSpec((None, None, N_MOD, D_MODEL), lambda b, t: (b, jnp.where(t >= n_lat_tiles, 1, 0), 0, 0)),
        _full_spec((1, D_MODEL)),
        _full_spec((D_MODEL, IN_COLS_PADDED)),
        _full_spec((1, 512)), _full_spec((1, LANES)), _full_spec((1, MLA_Q_RANK)), _full_spec((1, MLA_KV_RANK)),
        _full_spec((MLA_Q_RANK, MLA_HEADS * LANES)),
        _full_spec((MLA_KV_RANK, MLA_HEADS * LANES)),
        _full_spec((MLA_KV_RANK, MLA_HEADS * MLA_V)),
        _full_spec((512, 512)), _full_spec((LANES, LANES)), _full_spec((LANES, LANES)),
        tab, tab, tab, tab,
    ]
    widths = [D_MODEL, 512, 512, 512, 512, LANES, LANES, MLA_HEADS * LANES, MLA_HEADS * LANES, 512]
    out_specs = [tok(w) for w in widths]
    out_shape = [jax.ShapeDtypeStruct((B, T, w), BF16) for w in widths]
    out_specs.append(pl.BlockSpec((2, None, TM, 512), lambda b, t: (0, b, jnp.minimum(t, n_lat_tiles - 1), 0)))
    out_shape.append(jax.ShapeDtypeStruct((2, B, n_lat_tiles * TM, 512), BF16))
    out_specs.append(pl.BlockSpec((2, None, TM, 512), lambda b, t: (0, b, jnp.maximum(t - n_lat_tiles, 0), 0)))
    out_shape.append(jax.ShapeDtypeStruct((2, B, T - n_lat_tiles * TM, 512), BF16))
    return pl.pallas_call(
        functools.partial(_proj_kernel, n_lat_tiles, len(x_args)),
        grid=(B, nt),
        in_specs=in_specs,
        out_specs=out_specs,
        out_shape=out_shape,
        compiler_params=_cparams(("arbitrary", "arbitrary")),
        name="project",
    )(*x_args, mods, gpre, wts["w_in"], wts["gq_norm"], wts["gk_norm"], wts["mq_norm"], wts["mkv_norm"],
      wts["w_q_up"], wts["w_k_up"], wts["w_v_up"], consts["seg"], consts["cc"], consts["sc"],
      consts["cos_h"], consts["sin_h"], consts["cos_m"], consts["sin_m"])


N_MAPS = 8
MAPS_PER_STEP = 2


def _map_plan(mode, m):
    if mode == "diff":
        return m // 2, m % 2, m // 2, m // 2
    if mode == "gqa":
        return m // 2, m % 2, 0, 0
    return m, None, m, m // 2


def _attn_kernel(mode, lam_init, q_ref, k_ref, v_ref, *rest):
    o_ref, vx_ref = rest[-2:]
    if mode == "diff":
        lam_ref, sub_ref = rest[:2]
    nt = (((1,), (1,)), ((), ()))
    lane = lax.broadcasted_iota(jnp.int32, (1, LANES), 1)
    low = lane < HEAD_DIM

    @pl.when(pl.program_id(2) == 0)
    def _():
        for g in range(vx_ref.shape[0]):
            vx_ref[g, :, :LANES] = v_ref[:, g * LANES:(g + 1) * LANES]
            vx_ref[g, :, LANES:] = jnp.ones((vx_ref.shape[1], LANES), BF16)

    def scores(m):
        qblk, half, kblk, _ = _map_plan(mode, m)
        q = q_ref[:, qblk * LANES:(qblk + 1) * LANES]
        if half is not None:
            q = jnp.where(low if half == 0 else jnp.logical_not(low), q, jnp.zeros_like(q))
        return lax.dot_general(q, k_ref[:, kblk * LANES:(kblk + 1) * LANES], nt, preferred_element_type=F32)

    def numerator(s):
        return jnp.exp2(s - jnp.max(s, axis=-1, keepdims=True)).astype(BF16)

    def values(m, p):
        e = _dot(p, vx_ref[_map_plan(mode, m)[3]])
        return e[:, :LANES] / e[:, LANES:]

    n_maps = 2 * o_ref.shape[1] // LANES
    s, pl_, y = {}, {}, {}
    for t in range(n_maps + 2):
        if t < n_maps:
            s[t] = scores(t)
        if 0 <= t - 1 < n_maps:
            pl_[t - 1] = numerator(s.pop(t - 1))
        if 0 <= t - 2:
            y[t - 2] = values(t - 2, pl_.pop(t - 2))

    if mode == "diff":
        dl = lam_ref[...]
        lam = (jnp.exp(jnp.sum(dl[0:1] * dl[1:2], axis=-1, keepdims=True))
               - jnp.exp(jnp.sum(dl[2:3] * dl[3:4], axis=-1, keepdims=True)) + lam_init)
    for j in range(n_maps // 2):
        if mode == "diff":
            out = _rms(y[2 * j] - lam * y[2 * j + 1], sub_ref[...]) * (1.0 - lam_init)
        else:
            out = jnp.where(low, y[2 * j], y[2 * j + 1])
        o_ref[:, j * LANES:(j + 1) * LANES] = out.astype(BF16)


def _attention(mode, q, k, v, y_prev, *, tq, nq, q_blk0, tk, k_blk, maps_per_step, lam_init=0.0, extra=()):
    B, T, _ = q.shape
    n_steps = N_MAPS // maps_per_step
    wq, wk, wv = q.shape[2] // n_steps, max(k.shape[2] // n_steps, LANES), max(v.shape[2] // n_steps, LANES)
    kv_blk = (lambda g: 0) if mode == "gqa" else (lambda g: g)
    in_specs = [
        pl.BlockSpec((None, tq, wq), lambda b, g, i: (b, i + q_blk0, g)),
        pl.BlockSpec((None, tk, wk), lambda b, g, i: (b, k_blk, kv_blk(g))),
        pl.BlockSpec((None, tk, wv), lambda b, g, i: (b, k_blk, kv_blk(g))),
    ]
    args = [q, k, v]
    if mode == "diff":
        in_specs += [_full_spec((4, HEAD_DIM)), _full_spec((1, LANES))]
        args += list(extra)
    aliases = {}
    if y_prev is not None:
        in_specs.append(pl.BlockSpec(memory_space=pl.ANY))
        args.append(y_prev)
        aliases = {len(args) - 1: 0}
    wo = BRANCH_W // n_steps
    return pl.pallas_call(
        functools.partial(_attn_kernel, mode, lam_init),
        grid=(B, n_steps, nq),
        in_specs=in_specs,
        out_specs=pl.BlockSpec((None, tq, wo), lambda b, g, i: (b, i + q_blk0, g)),
        out_shape=jax.ShapeDtypeStruct((B, T, BRANCH_W), BF16),
        input_output_aliases=aliases,
        scratch_shapes=[pltpu.VMEM((wv // LANES, tk, 2 * LANES), BF16)],
        compiler_params=_cparams(("arbitrary", "arbitrary", "arbitrary")),
        name="attention_" + mode,
    )(*args)


DFT_COLS_PER_STEP = 8


DFT_ROW_COLS = 16


def _dft_rows_kernel(pq_ref, m1x_ref, a_ref):
    shape = pq_ref.shape
    z = pq_ref[...].reshape(shape[0] * shape[1] * shape[2], shape[3])
    a_ref[...] = _dot(m1x_ref[...], z).astype(BF16).reshape(shape)


def _dft_rows(pq_lat, m1x):
    _, B, n_lat, w = pq_lat.shape
    r = n_lat // GRID_W
    nb = DFT_ROW_COLS
    return pl.pallas_call(
        _dft_rows_kernel,
        grid=(B, GRID_W // nb),
        in_specs=[pl.BlockSpec((2, None, r, nb, w), lambda b, j: (0, b, 0, j, 0)),
                  _full_spec((2 * r * nb, 2 * r * nb))],
        out_specs=pl.BlockSpec((None, 2, r, nb, w), lambda b, j: (b, 0, 0, j, 0)),
        out_shape=jax.ShapeDtypeStruct((B, 2, r, GRID_W, w), BF16),
        compiler_params=_cparams(("arbitrary", "arbitrary")),
        name="dft_rows",
    )(pq_lat.reshape(2, B, r, GRID_W, w), m1x)


def _dft_cols_kernel(a_ref, ct_ref, st_ref, m3_ref, o_ref):
    kb = a_ref.shape[1]
    w = a_ref.shape[3]
    for j in range(kb):
        ar = a_ref[0, j].astype(F32)
        ai = a_ref[1, j].astype(F32)
        ct = jnp.concatenate([ct_ref[j]] * (w // LANES), axis=1)
        st = jnp.concatenate([st_ref[j]] * (w // LANES), axis=1)
        b = jnp.concatenate([ar * ct + ai * st, ai * ct - ar * st], axis=0).astype(BF16)
        o_ref[:, j, :] = _dot(m3_ref[...], b).astype(BF16)


def _dft_cols(a, ct, st, m3):
    B, _, r, _, w = a.shape
    kb = DFT_COLS_PER_STEP
    return pl.pallas_call(
        _dft_cols_kernel,
        grid=(B, r // kb),
        in_specs=[pl.BlockSpec((None, 2, kb, GRID_W, w), lambda b, k: (b, 0, k, 0, 0)),
                  pl.BlockSpec((kb, GRID_W, LANES), lambda b, k: (k, 0, 0)),
                  pl.BlockSpec((kb, GRID_W, LANES), lambda b, k: (k, 0, 0)),
                  _full_spec((GRID_W, 2 * GRID_W))],
        out_specs=pl.BlockSpec((None, GRID_W, kb, w), lambda b, k: (b, 0, k, 0)),
        out_shape=jax.ShapeDtypeStruct((B, GRID_W, r, w), BF16),
        compiler_params=_cparams(("arbitrary", "arbitrary")),
        name="dft_cols",
    )(a, ct, st, m3)


def _dft_ctx_kernel(pq_ref, w_ref, o_ref):
    zin = jnp.concatenate([pq_ref[0], pq_ref[1]], axis=0)
    o_ref[...] = _dot(w_ref[...], zin).astype(BF16)


def _dft_ctx(pq_ctx, w_ctx):
    _, B, n_ctx, w = pq_ctx.shape
    return pl.pallas_call(
        _dft_ctx_kernel,
        grid=(B,),
        in_specs=[pl.BlockSpec((2, None, n_ctx, w), lambda b: (0, b, 0, 0)), _full_spec((n_ctx, 2 * n_ctx))],
        out_specs=pl.BlockSpec((None, n_ctx, w), lambda b: (b, 0, 0)),
        out_shape=jax.ShapeDtypeStruct((B, n_ctx, w), BF16),
        compiler_params=_cparams(("arbitrary",)),
        name="dft_ctx",
    )(pq_ctx, w_ctx)


def _merge_kernel(n_lat_tiles, n_x, *refs):
    x_refs, refs = refs[:n_x], refs[n_x:]
    h_ref, y0_ref, y1_ref, y2_ref, y3_ref, mod_ref, wg_ref, bg_ref, wb_ref, wo_ref, gpost_ref = refs[:11]
    rest = refs[11:]
    h = h_ref[...]
    ys = (y0_ref, y1_ref, y2_ref, y3_ref)
    acc = None
    for i in range(4):
        gate = jax.nn.sigmoid(_dot(h, wg_ref[i]) + bg_ref[i:i + 1])
        term = gate * _dot(ys[i][...], wb_ref[i])
        acc = term if acc is None else acc + term
    m = _dot(acc.astype(BF16), wo_ref[...])
    mod = mod_ref[...]
    x = _stream_tile(x_refs, n_lat_tiles) + mod[2:3] * _rms(m, gpost_ref[...])
    o_ref = rest[3] if len(rest) > 1 else rest[0]
    o_ref[...] = x
    if len(rest) > 1:
        gffn_ref, wr_ref, tri_ref, _, hf_ref, route_ref, cnt_ref, carry_ref = rest
        _route_tile(x, mod, gffn_ref, wr_ref, tri_ref, hf_ref, route_ref, cnt_ref, carry_ref)


def _merge(x_all, h, y_diff, y_f, y_gqa, y_mla, mods, wts, gpost, n_lat_tiles, n_tiles, router=None):
    B, _ = _stream_dims(x_all)
    tm = TM
    tok = lambda w: pl.BlockSpec((None, tm, w), lambda b, t: (b, t, 0))
    x_specs, x_args = _stream_specs(x_all, n_lat_tiles)
    in_specs = x_specs + [
        tok(D_MODEL), tok(BRANCH_W), tok(BRANCH_W), tok(BRANCH_W), tok(BRANCH_W),
        pl.BlockSpec((None, None, N_MOD, D_MODEL), lambda b, t: (b, jnp.where(t >= n_lat_tiles, 1, 0), 0, 0)),
        _full_spec((4, D_MODEL, D_MODEL)), _full_spec((4, D_MODEL)), _full_spec((4, BRANCH_W, D_MODEL)),
        _full_spec((D_MODEL, D_MODEL)), _full_spec((1, D_MODEL)),
    ]
    args = x_args + [h, y_diff, y_f, y_gqa, y_mla, mods, wts["w_gate"], wts["b_gate"], wts["w_branch"],
                     wts["w_out"], gpost]
    out_specs = [tok(D_MODEL)]
    out_shape = [jax.ShapeDtypeStruct((B, n_tiles * tm, D_MODEL), F32)]
    scratch = []
    if router is not None:
        in_specs += [_full_spec((1, D_MODEL)), _full_spec((D_MODEL, LANES)), _full_spec((tm, tm))]
        args += list(router)
        out_specs += [tok(D_MODEL), tok(LANES), _full_spec((1, LANES))]
        out_shape += [jax.ShapeDtypeStruct((B, n_tiles * tm, D_MODEL), F32),
                      jax.ShapeDtypeStruct((B, n_tiles * tm, LANES), F32),
                      jax.ShapeDtypeStruct((1, LANES), F32)]
        scratch = [pltpu.VMEM((1, LANES), F32)]
    out = pl.pallas_call(
        functools.partial(_merge_kernel, n_lat_tiles, len(x_args)),
        grid=(B, n_tiles),
        in_specs=in_specs,
        out_specs=out_specs,
        out_shape=out_shape,
        scratch_shapes=scratch,
        compiler_params=_cparams(("arbitrary", "arbitrary")),
        name="merge",
    )(*args)
    return out[0] if router is None else out


def _dense_ffn_kernel(nf, tf, x_ref, mod_ref, gpre_ref, gpost_ref, wgu_ref, wd_ref, o_ref):
    x = x_ref[...]
    mod = mod_ref[...]
    hb = (_rms(x, gpre_ref[...]) * (1.0 + mod[4:5]) + mod[3:4]).astype(BF16)
    acc = None
    for f in range(nf):
        a = _dot(hb, wgu_ref[:, f * tf:(f + 1) * tf])
        u = _dot(hb, wgu_ref[:, FFN_DIM + f * tf:FFN_DIM + (f + 1) * tf])
        act = (a * jax.nn.sigmoid(a) * u).astype(BF16)
        part = _dot(act, wd_ref[f * tf:(f + 1) * tf, :])
        acc = part if acc is None else acc + part
    o_ref[...] = x + mod[5:6] * _rms(acc, gpost_ref[...])


def _dense_ffn(x_all, mods, gpre, gpost, wgu, wd, n_lat_tiles, n_tiles):
    B = x_all.shape[0]
    nf, tf = 1, FFN_DIM
    tok = pl.BlockSpec((None, TM, D_MODEL), lambda b, t: (b, t, 0))
    return pl.pallas_call(
        functools.partial(_dense_ffn_kernel, nf, tf),
        grid=(B, n_tiles),
        in_specs=[tok,
                  pl.BlockSpec((None, None, N_MOD, D_MODEL),
                               lambda b, t: (b, jnp.where(t >= n_lat_tiles, 1, 0), 0, 0)),
                  _full_spec((1, D_MODEL)), _full_spec((1, D_MODEL)),
                  _full_spec((D_MODEL, 2 * FFN_DIM)), _full_spec((FFN_DIM, D_MODEL))],
        out_specs=tok,
        out_shape=jax.ShapeDtypeStruct((B, n_tiles * TM, D_MODEL), F32),
        compiler_params=_cparams(("arbitrary", "arbitrary")),
        name="dense_ffn",
    )(x_all, mods, gpre, gpost, wgu, wd)


TE = 512
R_E1, R_E2, R_G1, R_G2, R_RANK1, R_RANK2 = range(6)


def _route_tile(x, mod, gpre_ref, wr_ref, tri_ref, h_ref, route_ref, cnt_ref, carry_ref):
    first = jnp.logical_and(pl.program_id(0) == 0, pl.program_id(1) == 0)

    @pl.when(first)
    def _():
        carry_ref[...] = jnp.zeros_like(carry_ref)

    h = _rms(x, gpre_ref[...]) * (1.0 + mod[4:5]) + mod[3:4]
    h_ref[...] = h
    logits = _dot3(h, wr_ref[...])
    lane = lax.broadcasted_iota(jnp.int32, logits.shape, 1)
    neg = jnp.float32(-jnp.inf)
    l1 = jnp.where(lane < N_EXPERTS, logits, neg)
    m1 = jnp.max(l1, axis=-1, keepdims=True)
    i1 = jnp.min(jnp.where(l1 == m1, lane, LANES), axis=-1, keepdims=True)
    l2 = jnp.where(lane == i1, neg, l1)
    m2 = jnp.max(l2, axis=-1, keepdims=True)
    i2 = jnp.min(jnp.where(l2 == m2, lane, LANES), axis=-1, keepdims=True)
    e2 = jnp.exp(m2 - m1)
    g1 = 1.0 / (1.0 + e2)
    g2 = e2 / (1.0 + e2)
    sel1, sel2 = lane == i1, lane == i2
    chosen = jnp.where(jnp.logical_or(sel1, sel2), 1.0, 0.0)
    before = _dot(tri_ref[...], chosen.astype(BF16)) + carry_ref[...]
    r1 = jnp.sum(jnp.where(sel1, before, 0.0), axis=-1, keepdims=True)
    r2 = jnp.sum(jnp.where(sel2, before, 0.0), axis=-1, keepdims=True)
    carry_ref[...] += jnp.sum(chosen, axis=0, keepdims=True)
    cnt_ref[...] = carry_ref[...]
    rec = jnp.zeros_like(logits)
    for col, val in ((R_E1, i1.astype(F32)), (R_E2, i2.astype(F32)), (R_G1, g1), (R_G2, g2),
                     (R_RANK1, r1), (R_RANK2, r2)):
        rec = jnp.where(lane == col, val, rec)
    route_ref[...] = rec


def _dispatch_kernel(last_ref, pos_ref, h_ref, xs_ref, zero_ref, stage_ref, sems):
    i = pl.program_id(0)
    n = h_ref.shape[0]

    @pl.when(i == 0)
    def _():
        zero_ref[...] = jnp.zeros_like(zero_ref)
        for e in range(last_ref.shape[0]):
            @pl.when(last_ref[e] >= 0)
            def _():
                fill = pltpu.make_async_copy(zero_ref, xs_ref.at[pl.ds(pl.multiple_of(last_ref[e], TE), TE)],
                                             sems.at[0])
                fill.start()
                fill.wait()

    cur = i % 2
    stage_ref[cur] = h_ref[...]

    def start(r, carry):
        for slot in range(2):
            pltpu.make_async_copy(stage_ref.at[cur, pl.ds(r, 1)], xs_ref.at[pl.ds(pos_ref[0, 2 * r + slot], 1)],
                                  sems.at[cur]).start()
        return carry

    lax.fori_loop(0, n, start, 0, unroll=8)

    def drain(half):
        for _ in range(2):
            pltpu.make_async_copy(stage_ref.at[half], xs_ref.at[pl.ds(0, n)], sems.at[half]).wait()

    pl.when(i > 0)(lambda: drain(1 - cur))
    pl.when(i == pl.num_programs(0) - 1)(lambda: drain(cur))


def _dispatch(h, pos, last_tile_row, n_rows):
    n = h.shape[0]
    grid_spec = pltpu.PrefetchScalarGridSpec(
        num_scalar_prefetch=1,
        grid=(n // TM,),
        in_specs=[pl.BlockSpec((None, 1, 2 * TM), lambda i, last: (i, 0, 0), memory_space=pltpu.SMEM),
                  pl.BlockSpec((TM, D_MODEL), lambda i, last: (i, 0))],
        out_specs=pl.BlockSpec(memory_space=pl.ANY),
        scratch_shapes=[pltpu.VMEM((TE, D_MODEL), F32), pltpu.VMEM((2, TM, D_MODEL), F32),
                        pltpu.SemaphoreType.DMA((2,))],
    )
    return pl.pallas_call(
        _dispatch_kernel,
        grid_spec=grid_spec,
        out_shape=jax.ShapeDtypeStruct((n_rows, D_MODEL), F32),
        compiler_params=_cparams(("arbitrary",)),
        name="moe_dispatch",
    )(last_tile_row, pos.reshape(n // TM, 1, 2 * TM), h)


def _experts_kernel(te_ref, nv_ref, xs_ref, wa_ref, wu_ref, wd_ref, ys_ref):
    del te_ref
    used = pl.program_id(0) < nv_ref[0]

    @pl.when(used)
    def _():
        x = xs_ref[...].astype(BF16)
        a = _dot(x, wa_ref[...])
        u = _dot(x, wu_ref[...])
        ys_ref[...] = _dot((a * jax.nn.sigmoid(a) * u).astype(BF16), wd_ref[...])

    @pl.when(jnp.logical_not(used))
    def _():
        ys_ref[...] = jnp.zeros_like(ys_ref)


def _experts(xs, tile_expert, n_valid, wgu, wd):
    n_rows = xs.shape[0]
    row_blk = lambda i, te, nv: (jnp.minimum(i, nv[0] - 1), 0)
    resident = pl.Buffered(1)
    grid_spec = pltpu.PrefetchScalarGridSpec(
        num_scalar_prefetch=2,
        grid=(n_rows // TE,),
        in_specs=[pl.BlockSpec((TE, D_MODEL), row_blk),
                  pl.BlockSpec((None, D_MODEL, EXPERT_DIM), lambda i, te, nv: (te[i], 0, 0), pipeline_mode=resident),
                  pl.BlockSpec((None, D_MODEL, EXPERT_DIM), lambda i, te, nv: (te[i], 0, 1), pipeline_mode=resident),
                  pl.BlockSpec((None, EXPERT_DIM, D_MODEL), lambda i, te, nv: (te[i], 0, 0), pipeline_mode=resident)],
        out_specs=pl.BlockSpec((TE, D_MODEL), lambda i, te, nv: (i, 0)),
    )
    return pl.pallas_call(
        _experts_kernel,
        grid_spec=grid_spec,
        out_shape=jax.ShapeDtypeStruct((n_rows, D_MODEL), F32),
        compiler_params=_cparams(("arbitrary",)),
        name="moe_experts",
    )(tile_expert, n_valid, xs, wgu, wgu, wd)


def _combine_kernel(pos_ref, nxt_ref, x_ref, route_ref, mod_ref, gpost_ref, ys_ref, o_ref, buf_ref, sems):
    i = pl.program_id(0)
    n = x_ref.shape[0]

    def gather(idx_ref, half):
        def start(r, carry):
            for slot in range(2):
                pltpu.make_async_copy(ys_ref.at[pl.ds(idx_ref[0, 2 * r + slot], 1)],
                                      buf_ref.at[half, slot, pl.ds(r, 1)], sems.at[half]).start()
            return carry
        lax.fori_loop(0, n, start, 0, unroll=8)

    cur = i % 2

    @pl.when(i == 0)
    def _():
        gather(pos_ref, 0)

    @pl.when(i + 1 < pl.num_programs(0))
    def _():
        gather(nxt_ref, 1 - cur)

    for slot in range(2):
        pltpu.make_async_copy(ys_ref.at[pl.ds(0, n)], buf_ref.at[cur, slot], sems.at[cur]).wait()
    route = route_ref[...]
    y = route[:, R_G1:R_G1 + 1] * buf_ref[cur, 0] + route[:, R_G2:R_G2 + 1] * buf_ref[cur, 1]
    mod = mod_ref[...]
    o_ref[...] = x_ref[...] + mod[5:6] * _rms(y, gpost_ref[...])


def _combine(x_all, route, pos, ys, mods, gpost, n_lat_tiles, n_tiles):
    B = x_all.shape[0]
    n_steps = B * n_tiles
    tok = lambda w: pl.BlockSpec((None, TM, w), lambda i: (i // n_tiles, i % n_tiles, 0))
    idx = lambda ahead: pl.BlockSpec((None, 1, 2 * TM), lambda i: (jnp.minimum(i + ahead, n_steps - 1), 0, 0),
                                     memory_space=pltpu.SMEM)
    pos_tiles = pos.reshape(n_steps, 1, 2 * TM)
    return pl.pallas_call(
        _combine_kernel,
        grid=(n_steps,),
        in_specs=[idx(0), idx(1), tok(D_MODEL), tok(LANES),
                  pl.BlockSpec((None, None, N_MOD, D_MODEL),
                               lambda i: (i // n_tiles, jnp.where(i % n_tiles >= n_lat_tiles, 1, 0), 0, 0)),
                  _full_spec((1, D_MODEL)),
                  pl.BlockSpec(memory_space=pl.ANY)],
        out_specs=tok(D_MODEL),
        out_shape=jax.ShapeDtypeStruct((B, n_tiles * TM, D_MODEL), F32),
        scratch_shapes=[pltpu.VMEM((2, 2, TM, D_MODEL), F32), pltpu.SemaphoreType.DMA((2,))],
        compiler_params=_cparams(("arbitrary",)),
        name="moe_combine",
    )(pos_tiles, pos_tiles, x_all, route, mods, gpost, ys)


def _routed_ffn(x_all, h, route, counts, mods, gpost, wgu, wd, n_lat_tiles, n_tiles):
    B = x_all.shape[0]
    n_tok = B * n_tiles * TM
    rec = route.reshape(n_tok, LANES)
    expert = rec[:, R_E1:R_E2 + 1].astype(jnp.int32)
    rank = rec[:, R_RANK1:R_RANK2 + 1].astype(jnp.int32)
    tiles_per_expert = (counts[0, :N_EXPERTS].astype(jnp.int32) + TE - 1) // TE
    tile_end = jnp.cumsum(tiles_per_expert)
    n_valid = tile_end[-1:]
    pos = (tile_end - tiles_per_expert)[expert] * TE + rank
    n_tiles_max = (2 * n_tok) // TE + N_EXPERTS
    tile_id = jnp.minimum(jnp.arange(n_tiles_max, dtype=jnp.int32), n_valid - 1)
    tile_expert = jnp.minimum(jnp.searchsorted(tile_end, tile_id, side="right"), N_EXPERTS - 1).astype(jnp.int32)
    spare = n_valid + jnp.arange(N_EXPERTS, dtype=jnp.int32)
    last_tile_row = jnp.concatenate([jnp.where(tiles_per_expert > 0, (tile_end - 1) * TE, -1),
                                     jnp.where(spare < n_tiles_max, spare * TE, -1)]).astype(jnp.int32)
    xs = _dispatch(h.reshape(n_tok, D_MODEL), pos, last_tile_row, n_tiles_max * TE)
    ys = _experts(xs, tile_expert, n_valid.astype(jnp.int32), wgu, wd)
    return _combine(x_all, route, pos, ys, mods, gpost, n_lat_tiles, n_tiles)


def _rope_angles(n_lat, rot_dim):
    rows = n_lat // GRID_W
    row = jnp.repeat(jnp.arange(rows, dtype=F32), GRID_W)
    col = jnp.tile(jnp.arange(GRID_W, dtype=F32), rows)
    axis_dim = rot_dim // 2
    inv_freq = 1.0 / (ROPE_BASE ** (jnp.arange(0, axis_dim, 2, dtype=F32) / axis_dim))
    ar = row[:, None] * inv_freq
    ac = col[:, None] * inv_freq
    return jnp.concatenate([ar, ar, ac, ac], axis=-1)


def _row_stage_matrix(r):
    nb = DFT_ROW_COLS
    size = 2 * r * nb
    i = lax.broadcasted_iota(jnp.int32, (size, size), 0)
    j = lax.broadcasted_iota(jnp.int32, (size, size), 1)
    part_i, k1, col_i = i // (r * nb), (i % (r * nb)) // nb, i % nb
    part_j, n1, col_j = j // (r * nb), (j % (r * nb)) // nb, j % nb
    ang = ((k1 * n1) % r).astype(F32) * (2.0 * math.pi / r)
    val = jnp.where(part_i == part_j, jnp.cos(ang), -jnp.sin(ang))
    val = jnp.where(jnp.logical_and(part_i == 1, part_j == 1), -val, val)
    return jnp.where(col_i == col_j, val, 0.0)


def _constants(n_lat, n_ctx):
    lane = jnp.arange(LANES)
    ang = _rope_angles(n_lat, HEAD_DIM)
    cos_h = jnp.tile(jnp.cos(ang), (1, 2))
    sin_h = jnp.tile(jnp.sin(ang), (1, 2)) * jnp.where((lane % 32) < 16, -1.0, 1.0)
    cos_h = jnp.concatenate([cos_h, jnp.ones((n_ctx, LANES), F32)], axis=0)
    sin_h = jnp.concatenate([sin_h, jnp.zeros((n_ctx, LANES), F32)], axis=0)
    angm = _rope_angles(n_lat, MLA_ROPE)
    pad_l = jnp.zeros((n_lat, MLA_NOPE), F32)
    pad_r = jnp.zeros((n_lat, LANES - MLA_NOPE - MLA_ROPE), F32)
    cos_m = jnp.concatenate([pad_l + 1.0, jnp.cos(angm), pad_r + 1.0], axis=1)
    sin_m = jnp.concatenate([pad_l, jnp.sin(angm), pad_r], axis=1) * jnp.where((lane % 16) < 8, -1.0, 1.0)
    cos_m = jnp.concatenate([cos_m, jnp.ones((n_ctx, LANES), F32)], axis=0)
    sin_m = jnp.concatenate([sin_m, jnp.zeros((n_ctx, LANES), F32)], axis=0)
    idx = jnp.arange(512)
    seg = jnp.where((idx[:, None] // HEAD_DIM) == (idx[None, :] // HEAD_DIM), 1.0 / HEAD_DIM, 0.0).astype(BF16)

    def dft(n):
        i = jnp.arange(n, dtype=jnp.int32)
        a = ((i[:, None] * i[None, :]) % n).astype(F32) * (2.0 * math.pi / n)
        s = n ** -0.5
        return jnp.cos(a) * s, jnp.sin(a) * s

    cc, sc = dft(FOURIER_GROUP)
    cx, sx = dft(n_ctx)
    r = n_lat // GRID_W
    c64, s64 = (t * GRID_W ** 0.5 * n_lat ** -0.5 for t in dft(GRID_W))
    k1 = jnp.arange(r, dtype=jnp.int32)[:, None]
    n2 = jnp.arange(GRID_W, dtype=jnp.int32)[None, :]
    tw = ((k1 * n2) % n_lat).astype(F32) * (2.0 * math.pi / n_lat)
    wide = lambda t: jnp.broadcast_to(t[:, :, None], (r, GRID_W, LANES))
    return {
        "cos_h": cos_h, "sin_h": sin_h, "cos_m": cos_m, "sin_m": sin_m, "seg": seg,
        "cc": cc.astype(BF16), "sc": sc.astype(BF16),
        "m1x": _row_stage_matrix(r).astype(BF16),
        "tw_cos": wide(jnp.cos(tw)), "tw_sin": wide(jnp.sin(tw)),
        "m3": jnp.concatenate([c64, s64], axis=1).astype(BF16),
        "w_ctx": jnp.concatenate([cx, -sx], axis=1).astype(BF16),
    }


_GQA_HEAD_ORDER = (0, 4, 1, 5, 2, 6, 3, 7)


def _layer_weights(l, w_in, gqa_q_norm, gqa_k_norm, mla_q_norm, mla_kv_norm, mla_w_q_up, mla_w_kv_up,
                   w_branch, w_gate, b_gate, w_out):
    wi = w_in[l]
    kr0 = COL["kr"]
    gq = wi[:, COL["gq"]:COL["gk"]].reshape(D_MODEL, GQA_Q_HEADS, HEAD_DIM)[:, jnp.array(_GQA_HEAD_ORDER)]
    zeros = lambda n: jnp.zeros((D_MODEL, n), F32)
    w_in_p = jnp.concatenate([wi[:, :COL["gq"]], gq.reshape(D_MODEL, 512), wi[:, COL["gk"]:kr0],
                              zeros(MLA_NOPE), wi[:, kr0:kr0 + MLA_ROPE], zeros(LANES - MLA_NOPE - MLA_ROPE)], axis=1)
    qu = mla_w_q_up[l].reshape(MLA_Q_RANK, MLA_HEADS, MLA_NOPE + MLA_ROPE)
    qu = jnp.pad(qu, ((0, 0), (0, 0), (0, LANES - MLA_NOPE - MLA_ROPE))).reshape(MLA_Q_RANK, MLA_HEADS * LANES)
    kvu = mla_w_kv_up[l].reshape(MLA_KV_RANK, MLA_HEADS, MLA_NOPE + MLA_V)
    ku = jnp.pad(kvu[:, :, :MLA_NOPE], ((0, 0), (0, 0), (0, LANES - MLA_NOPE))).reshape(MLA_KV_RANK, -1)
    vu = kvu[:, :, MLA_NOPE:].reshape(MLA_KV_RANK, MLA_HEADS * MLA_V)
    wb = w_branch[l]
    wb_gqa = wb[2].reshape(GQA_Q_HEADS, HEAD_DIM, D_MODEL)[jnp.array(_GQA_HEAD_ORDER)].reshape(BRANCH_W, D_MODEL)
    wb = jnp.stack([wb[0], wb[1], wb_gqa, wb[3]])
    return {
        "w_in": w_in_p.astype(BF16),
        "gq_norm": jnp.tile(gqa_q_norm[l], GQA_Q_HEADS).reshape(1, 512),
        "gk_norm": jnp.tile(gqa_k_norm[l], GQA_KV_HEADS).reshape(1, LANES),
        "mq_norm": mla_q_norm[l].reshape(1, MLA_Q_RANK),
        "mkv_norm": mla_kv_norm[l].reshape(1, MLA_KV_RANK),
        "w_q_up": qu.astype(BF16), "w_k_up": ku.astype(BF16), "w_v_up": vu.astype(BF16),
        "w_branch": wb.astype(BF16), "w_gate": w_gate[l].astype(BF16), "b_gate": b_gate[l],
        "w_out": w_out[l].astype(BF16),
    }


def kernel(x, c, ctx, c_ctx, w_mod, b_mod, mix_pre_norm, mix_post_norm, ffn_pre_norm, ffn_post_norm,
           w_in, diff_lambda, diff_subnorm, gqa_q_norm, gqa_k_norm, mla_q_norm, mla_kv_norm,
           mla_w_q_up, mla_w_kv_up, w_branch, w_gate, b_gate, w_out,
           dense_w_gate_up, dense_w_down, moe_router, moe_w_gate_up, moe_w_down):
    B, n_lat, d = x.shape
    n_ctx = ctx.shape[1]
    depth = w_mod.shape[0]
    assert d == D_MODEL and n_lat % TM == 0 and n_ctx % TM == 0 and n_lat % n_ctx == 0
    assert n_lat % GRID_W == 0
    T = n_lat + n_ctx
    n_lat_tiles, n_all_tiles = n_lat // TM, T // TM
    consts = _constants(n_lat, n_ctx)
    x_all = (x, ctx)
    c_rows = jnp.pad(jnp.concatenate([c, c_ctx[None]], axis=0), ((0, (-(B + 1)) % 8), (0, 0)))
    row = lambda v: v.reshape(1, -1)

    for layer in range(depth):
        last = layer == depth - 1
        n_tiles = n_lat_tiles if last else n_all_tiles
        lam_init = 0.8 - 0.6 * math.exp(-0.3 * layer)
        wts = _layer_weights(layer, w_in, gqa_q_norm, gqa_k_norm, mla_q_norm, mla_kv_norm, mla_w_q_up,
                             mla_w_kv_up, w_branch, w_gate, b_gate, w_out)
        mod_rows = _modulation(c_rows, w_mod[layer], b_mod[layer])
        mods = jnp.stack([mod_rows[:B].reshape(B, N_MOD, D_MODEL),
                          jnp.broadcast_to(mod_rows[B].reshape(1, N_MOD, D_MODEL), (B, N_MOD, D_MODEL))], axis=1)

        (h, dq, dk, dv, gq, gk, gv, mq, mk, mv, pq_lat, pq_ctx) = _project(
            x_all, mods, row(mix_pre_norm[layer]), wts, consts, n_lat_tiles)

        diff_extra = (diff_lambda[layer], diff_subnorm[layer].reshape(1, LANES))
        ys = []
        for mode, q, k, v in (("diff", dq, dk, dv), ("gqa", gq, gk, gv), ("mla", mq, mk, mv)):
            kw = dict(lam_init=lam_init, extra=diff_extra)
            tq = TQ if n_lat % TQ == 0 else TM
            y = _attention(mode, q, k, v, jnp.zeros((B, T, BRANCH_W), BF16), tq=tq, nq=n_lat // tq, q_blk0=0,
                           tk=T, k_blk=0, maps_per_step=MAPS_PER_STEP, **kw)
            if not last:
                y = _attention(mode, q, k, v, y, tq=TM, nq=n_ctx // TM, q_blk0=n_lat_tiles,
                               tk=n_ctx, k_blk=n_lat // n_ctx, maps_per_step=N_MAPS, **kw)
            ys.append(y)
        y_diff, y_gqa, y_mla = ys

        a_rows = _dft_rows(pq_lat, consts["m1x"])
        y_f = _dft_cols(a_rows, consts["tw_cos"], consts["tw_sin"], consts["m3"]).reshape(B, n_lat, BRANCH_W)
        if not last:
            y_f = jnp.concatenate([y_f, _dft_ctx(pq_ctx, consts["w_ctx"])], axis=1)

        merge_args = (x_all, h, y_diff, y_f, y_gqa, y_mla, mods, wts, row(mix_post_norm[layer]))
        if layer % 2 == 0:
            x_all = _merge(*merge_args, n_lat_tiles, n_tiles)
            x_all = _dense_ffn(x_all, mods, row(ffn_pre_norm[layer]), row(ffn_post_norm[layer]),
                               dense_w_gate_up[layer // 2].astype(BF16), dense_w_down[layer // 2].astype(BF16),
                               n_lat_tiles, n_tiles)
        else:
            wr = jnp.pad(moe_router[layer // 2], ((0, 0), (0, LANES - N_EXPERTS)))
            idx = jnp.arange(TM)
            tri = jnp.where(idx[:, None] > idx[None, :], 1.0, 0.0).astype(BF16)
            x_all, h_ffn, route, counts = _merge(*merge_args, n_lat_tiles, n_tiles,
                                                 router=(row(ffn_pre_norm[layer]), wr, tri))
            x_all = _routed_ffn(x_all, h_ffn, route, counts, mods, row(ffn_post_norm[layer]),
                                moe_w_gate_up[layer // 2].astype(BF16), moe_w_down[layer // 2].astype(BF16),
                                n_lat_tiles, n_tiles)
    return x_all[:, :n_lat]
```

```python
import functools
import math

import jax
import jax.numpy as jnp
from jax import lax
from jax.experimental import pallas as pl
from jax.experimental.pallas import tpu as pltpu

F32 = jnp.float32
BF16 = jnp.bfloat16

D_MODEL = 1024
GRID_W = 64
ROPE_BASE = 10000.0
EPS = 1e-6
N_MOD = 6
HEAD_DIM = 64
GQA_Q_HEADS = 8
GQA_KV_HEADS = 2
MLA_HEADS = 8
MLA_Q_RANK = 256
MLA_KV_RANK = 128
MLA_NOPE = 64
MLA_ROPE = 32
MLA_V = 64
BRANCH_W = 512
FOURIER_GROUP = 128
FFN_DIM = 2816
N_EXPERTS = 8
EXPERT_DIM = 3584
LOG2E = math.log2(math.e)
QK_SCALE = HEAD_DIM ** -0.5 * LOG2E
MLA_SCALE = (MLA_NOPE + MLA_ROPE) ** -0.5 * LOG2E
IN_COLS_PADDED = 3328
COL = dict(dq=0, dk=512, dv=1024, fourier=1536, gq=2048, gk=2560, gv=2688, mcq=2816, mckv=3072, kr=3200)

LANES = 128
VMEM_LIMIT = 56 * 1024 * 1024

TM = 256
TQ = 1024


def _cparams(sem):
    return pltpu.CompilerParams(dimension_semantics=sem, vmem_limit_bytes=VMEM_LIMIT)


def _full_spec(shape):
    n = len(shape)
    return pl.BlockSpec(shape, lambda *_: (0,) * n)


def _split_bf16(x):
    hi = x.astype(BF16)
    lo = (x - hi.astype(F32)).astype(BF16)
    return hi, lo


def _dot(a, b):
    return jnp.dot(a, b, preferred_element_type=F32)


def _dot3(a, w):
    ah, al = _split_bf16(a)
    wh, wl = _split_bf16(w)
    return _dot(ah, wh) + _dot(ah, wl) + _dot(al, wh)


def _rms(x, gain):
    ms = jnp.mean(x * x, axis=-1, keepdims=True)
    return x * lax.rsqrt(ms + EPS) * gain


def _mod_kernel(c_ref, w_ref, b_ref, o_ref):
    c = c_ref[...]
    a = c * jax.nn.sigmoid(c)
    o_ref[...] = _dot3(a, w_ref[...]) + b_ref[...]


def _modulation(c_rows, w_mod_l, b_mod_l):
    rows = c_rows.shape[0]
    n = w_mod_l.shape[1]
    tn = 512
    return pl.pallas_call(
        _mod_kernel,
        grid=(n // tn,),
        in_specs=[pl.BlockSpec((rows, D_MODEL), lambda j: (0, 0)),
                  pl.BlockSpec((D_MODEL, tn), lambda j: (0, j)),
                  pl.BlockSpec((1, tn), lambda j: (0, j))],
        out_specs=pl.BlockSpec((rows, tn), lambda j: (0, j)),
        out_shape=jax.ShapeDtypeStruct((rows, n), F32),
        compiler_params=_cparams(("arbitrary",)),
        name="modulation",
    )(c_rows, w_mod_l, b_mod_l.reshape(1, n))


def _rope(x, cos, sin_signed, half):
    lane = lax.broadcasted_iota(jnp.int32, x.shape, 1)
    first = (lane % (2 * half)) < half
    up = pltpu.roll(x, LANES - half, 1)
    down = pltpu.roll(x, half, 1)
    return x * cos + jnp.where(first, up, down) * sin_signed


def _stream_specs(xs, n_lat_tiles, width=D_MODEL):
    if not isinstance(xs, tuple):
        return [pl.BlockSpec((None, TM, width), lambda b, t: (b, t, 0))], [xs]
    return ([pl.BlockSpec((None, TM, width), lambda b, t: (b, jnp.minimum(t, n_lat_tiles - 1), 0)),
             pl.BlockSpec((None, TM, width), lambda b, t: (b, jnp.maximum(t - n_lat_tiles, 0), 0))], list(xs))


def _stream_tile(x_refs, n_lat_tiles):
    if len(x_refs) == 1:
        return x_refs[0][...]
    return jnp.where(pl.program_id(1) < n_lat_tiles, x_refs[0][...], x_refs[1][...])


def _stream_dims(xs):
    if isinstance(xs, tuple):
        return xs[0].shape[0], xs[0].shape[1] + xs[1].shape[1]
    return xs.shape[0], xs.shape[1]


def _proj_kernel(n_lat_tiles, n_x, *refs):
    x_refs, refs = refs[:n_x], refs[n_x:]
    (mod_ref, gpre_ref, win_ref, gqn_ref, gkn_ref, mqn_ref, mkvn_ref,
     wqup_ref, wkup_ref, wvup_ref, seg_ref, cc_ref, sc_ref,
     cosh_ref, sinh_ref, cosm_ref, sinm_ref,
     h_ref, dq_ref, dk_ref, dv_ref, gq_ref, gk_ref, gv_ref, mq_ref, mk_ref, mv_ref, pq_ref, pqc_ref) = refs
    x = _stream_tile(x_refs, n_lat_tiles)
    mod = mod_ref[...]
    h = _rms(x, gpre_ref[...]) * (1.0 + mod[1:2]) + mod[0:1]
    hb = h.astype(BF16)
    h_ref[...] = hb
    z = _dot(hb, win_ref[...])

    def zsec(name, off=0, width=LANES):
        return z[:, COL[name] + off:COL[name] + off + width]

    cosh, sinh = cosh_ref[...], sinh_ref[...]
    cosm, sinm = cosm_ref[...], sinm_ref[...]
    seg = seg_ref[...]

    def head_rms(v, gain, w):
        sq = v * v
        hi, lo = _split_bf16(sq)
        ms = _dot(hi, seg[:w, :w]) + _dot(lo, seg[:w, :w])
        return v * lax.rsqrt(ms + EPS) * gain

    for j in range(4):
        c0 = j * LANES
        dq_ref[:, c0:c0 + LANES] = (_rope(zsec("dq", c0), cosh, sinh, 16) * QK_SCALE).astype(BF16)
        dk_ref[:, c0:c0 + LANES] = _rope(zsec("dk", c0), cosh, sinh, 16).astype(BF16)
    dv_ref[...] = zsec("dv", 0, BRANCH_W).astype(BF16)
    fparts = []
    for g in range(4):
        fb = zsec("fourier", g * LANES).astype(BF16)
        fparts.append((_dot(fb, cc_ref[...]).astype(BF16), _dot(fb, sc_ref[...]).astype(BF16)))
    is_lat = pl.program_id(1) < n_lat_tiles

    def put(ref):
        for g, (pc, ps) in enumerate(fparts):
            ref[0, :, g * LANES:(g + 1) * LANES] = pc
            ref[1, :, g * LANES:(g + 1) * LANES] = ps

    pl.when(is_lat)(lambda: put(pq_ref))
    pl.when(jnp.logical_not(is_lat))(lambda: put(pqc_ref))
    gq = head_rms(zsec("gq", 0, 512), gqn_ref[...], 512)
    for j in range(4):
        c0 = j * LANES
        gq_ref[:, c0:c0 + LANES] = (_rope(gq[:, c0:c0 + LANES], cosh, sinh, 16) * QK_SCALE).astype(BF16)
    gk = head_rms(zsec("gk"), gkn_ref[...], LANES)
    gk_ref[...] = _rope(gk, cosh, sinh, 16).astype(BF16)
    gv_ref[...] = zsec("gv").astype(BF16)
    cq = _rms(zsec("mcq", 0, MLA_Q_RANK), mqn_ref[...]).astype(BF16)
    mq = _dot(cq, wqup_ref[...])
    ckv = _rms(zsec("mckv", 0, MLA_KV_RANK), mkvn_ref[...]).astype(BF16)
    mk = _dot(ckv, wkup_ref[...])
    mv_ref[...] = _dot(ckv, wvup_ref[...]).astype(BF16)
    kr = _rope(zsec("kr"), cosm, sinm, 8)
    for hh in range(MLA_HEADS):
        c0 = hh * LANES
        mq_ref[:, c0:c0 + LANES] = (_rope(mq[:, c0:c0 + LANES], cosm, sinm, 8) * MLA_SCALE).astype(BF16)
        mk_ref[:, c0:c0 + LANES] = (mk[:, c0:c0 + LANES] + kr).astype(BF16)


def _project(x_all, mods, gpre, wts, consts, n_lat_tiles):
    B, T = _stream_dims(x_all)
    nt = T // TM
    tok = lambda w: pl.BlockSpec((None, TM, w), lambda b, t: (b, t, 0))
    tab = pl.BlockSpec((TM, LANES), lambda b, t: (t, 0))
    x_specs, x_args = _stream_specs(x_all, n_lat_tiles)
    in_specs = x_specs + [
        pl.BlockSpec((None, None, N_MOD, D_MODEL), lambda b, t: (b, jnp.where(t >= n_lat_tiles, 1, 0), 0, 0)),
        _full_spec((1, D_MODEL)),
        _full_spec((D_MODEL, IN_COLS_PADDED)),
        _full_spec((1, 512)), _full_spec((1, LANES)), _full_spec((1, MLA_Q_RANK)), _full_spec((1, MLA_KV_RANK)),
        _full_spec((MLA_Q_RANK, MLA_HEADS * LANES)),
        _full_spec((MLA_KV_RANK, MLA_HEADS * LANES)),
        _full_spec((MLA_KV_RANK, MLA_HEADS * MLA_V)),
        _full_spec((512, 512)), _full_spec((LANES, LANES)), _full_spec((LANES, LANES)),
        tab, tab, tab, tab,
    ]
    widths = [D_MODEL, 512, 512, 512, 512, LANES, LANES, MLA_HEADS * LANES, MLA_HEADS * LANES, 512]
    out_specs = [tok(w) for w in widths]
    out_shape = [jax.ShapeDtypeStruct((B, T, w), BF16) for w in widths]
    out_specs.append(pl.BlockSpec((2, None, TM, 512), lambda b, t: (0, b, jnp.minimum(t, n_lat_tiles - 1), 0)))
    out_shape.append(jax.ShapeDtypeStruct((2, B, n_lat_tiles * TM, 512), BF16))
    out_specs.append(pl.BlockSpec((2, None, TM, 512), lambda b, t: (0, b, jnp.maximum(t - n_lat_tiles, 0), 0)))
    out_shape.append(jax.ShapeDtypeStruct((2, B, T - n_lat_tiles * TM, 512), BF16))
    return pl.pallas_call(
        functools.partial(_proj_kernel, n_lat_tiles, len(x_args)),
        grid=(B, nt),
        in_specs=in_specs,
        out_specs=out_specs,
        out_shape=out_shape,
        compiler_params=_cparams(("arbitrary", "arbitrary")),
        name="project",
    )(*x_args, mods, gpre, wts["w_in"], wts["gq_norm"], wts["gk_norm"], wts["mq_norm"], wts["mkv_norm"],
      wts["w_q_up"], wts["w_k_up"], wts["w_v_up"], consts["seg"], consts["cc"], consts["sc"],
      consts["cos_h"], consts["sin_h"], consts["cos_m"], consts["sin_m"])


N_MAPS = 8
MAPS_PER_STEP = 2


def _map_plan(mode, m):
    if mode == "diff":
        return m // 2, m % 2, m // 2, m // 2
    if mode == "gqa":
        return m // 2, m % 2, 0, 0
    return m, None, m, m // 2


def _attn_kernel(mode, lam_init, q_ref, k_ref, v_ref, *rest):
    o_ref, vx_ref = rest[-2:]
    if mode == "diff":
        lam_ref, sub_ref = rest[:2]
    nt = (((1,), (1,)), ((), ()))
    lane = lax.broadcasted_iota(jnp.int32, (1, LANES), 1)
    low = lane < HEAD_DIM

    @pl.when(pl.program_id(2) == 0)
    def _():
        for g in range(vx_ref.shape[0]):
            vx_ref[g, :, :LANES] = v_ref[:, g * LANES:(g + 1) * LANES]
            vx_ref[g, :, LANES:] = jnp.ones((vx_ref.shape[1], LANES), BF16)

    def scores(m):
        qblk, half, kblk, _ = _map_plan(mode, m)
        q = q_ref[:, qblk * LANES:(qblk + 1) * LANES]
        if half is not None:
            q = jnp.where(low if half == 0 else jnp.logical_not(low), q, jnp.zeros_like(q))
        return lax.dot_general(q, k_ref[:, kblk * LANES:(kblk + 1) * LANES], nt, preferred_element_type=F32)

    def numerator(s):
        return jnp.exp2(s - jnp.max(s, axis=-1, keepdims=True)).astype(BF16)

    def values(m, p):
        e = _dot(p, vx_ref[_map_plan(mode, m)[3]])
        return e[:, :LANES] / e[:, LANES:]

    n_maps = 2 * o_ref.shape[1] // LANES
    s, pl_, y = {}, {}, {}
    for t in range(n_maps + 2):
        if t < n_maps:
            s[t] = scores(t)
        if 0 <= t - 1 < n_maps:
            pl_[t - 1] = numerator(s.pop(t - 1))
        if 0 <= t - 2:
            y[t - 2] = values(t - 2, pl_.pop(t - 2))

    if mode == "diff":
        dl = lam_ref[...]
        lam = (jnp.exp(jnp.sum(dl[0:1] * dl[1:2], axis=-1, keepdims=True))
               - jnp.exp(jnp.sum(dl[2:3] * dl[3:4], axis=-1, keepdims=True)) + lam_init)
    for j in range(n_maps // 2):
        if mode == "diff":
            out = _rms(y[2 * j] - lam * y[2 * j + 1], sub_ref[...]) * (1.0 - lam_init)
        else:
            out = jnp.where(low, y[2 * j], y[2 * j + 1])
        o_ref[:, j * LANES:(j + 1) * LANES] = out.astype(BF16)


def _attention(mode, q, k, v, *, tq, nq, q_blk0, tk, k_blk, maps_per_step, lam_init=0.0, extra=()):
    B = q.shape[0]
    n_steps = N_MAPS // maps_per_step
    wq, wk, wv = q.shape[2] // n_steps, max(k.shape[2] // n_steps, LANES), max(v.shape[2] // n_steps, LANES)
    kv_blk = (lambda g: 0) if mode == "gqa" else (lambda g: g)
    in_specs = [
        pl.BlockSpec((None, tq, wq), lambda b, g, i: (b, i + q_blk0, g)),
        pl.BlockSpec((None, tk, wk), lambda b, g, i: (b, k_blk, kv_blk(g))),
        pl.BlockSpec((None, tk, wv), lambda b, g, i: (b, k_blk, kv_blk(g))),
    ]
    args = [q, k, v]
    if mode == "diff":
        in_specs += [_full_spec((4, HEAD_DIM)), _full_spec((1, LANES))]
        args += list(extra)
    wo = BRANCH_W // n_steps
    return pl.pallas_call(
        functools.partial(_attn_kernel, mode, lam_init),
        grid=(B, n_steps, nq),
        in_specs=in_specs,
        out_specs=pl.BlockSpec((None, tq, wo), lambda b, g, i: (b, i, g)),
        out_shape=jax.ShapeDtypeStruct((B, nq * tq, BRANCH_W), BF16),
        scratch_shapes=[pltpu.VMEM((wv // LANES, tk, 2 * LANES), BF16)],
        compiler_params=_cparams(("arbitrary", "arbitrary", "arbitrary")),
        name="attention_" + mode,
    )(*args)


DFT_COLS_PER_STEP = 8


DFT_ROW_COLS = 16


def _dft_rows_kernel(pq_ref, m1x_ref, a_ref):
    shape = pq_ref.shape
    z = pq_ref[...].reshape(shape[0] * shape[1] * shape[2], shape[3])
    a_ref[...] = _dot(m1x_ref[...], z).astype(BF16).reshape(shape)


def _dft_rows(pq_lat, m1x):
    _, B, n_lat, w = pq_lat.shape
    r = n_lat // GRID_W
    nb = DFT_ROW_COLS
    return pl.pallas_call(
        _dft_rows_kernel,
        grid=(B, GRID_W // nb),
        in_specs=[pl.BlockSpec((2, None, r, nb, w), lambda b, j: (0, b, 0, j, 0)),
                  _full_spec((2 * r * nb, 2 * r * nb))],
        out_specs=pl.BlockSpec((None, 2, r, nb, w), lambda b, j: (b, 0, 0, j, 0)),
        out_shape=jax.ShapeDtypeStruct((B, 2, r, GRID_W, w), BF16),
        compiler_params=_cparams(("arbitrary", "arbitrary")),
        name="dft_rows",
    )(pq_lat.reshape(2, B, r, GRID_W, w), m1x)


def _dft_cols_kernel(a_ref, ct_ref, st_ref, m3_ref, o_ref):
    kb = a_ref.shape[1]
    w = a_ref.shape[3]
    for j in range(kb):
        ar = a_ref[0, j].astype(F32)
        ai = a_ref[1, j].astype(F32)
        ct = jnp.concatenate([ct_ref[j]] * (w // LANES), axis=1)
        st = jnp.concatenate([st_ref[j]] * (w // LANES), axis=1)
        b = jnp.concatenate([ar * ct + ai * st, ai * ct - ar * st], axis=0).astype(BF16)
        o_ref[:, j, :] = _dot(m3_ref[...], b).astype(BF16)


def _dft_cols(a, ct, st, m3):
    B, _, r, _, w = a.shape
    kb = DFT_COLS_PER_STEP
    return pl.pallas_call(
        _dft_cols_kernel,
        grid=(B, r // kb),
        in_specs=[pl.BlockSpec((None, 2, kb, GRID_W, w), lambda b, k: (b, 0, k, 0, 0)),
                  pl.BlockSpec((kb, GRID_W, LANES), lambda b, k: (k, 0, 0)),
                  pl.BlockSpec((kb, GRID_W, LANES), lambda b, k: (k, 0, 0)),
                  _full_spec((GRID_W, 2 * GRID_W))],
        out_specs=pl.BlockSpec((None, GRID_W, kb, w), lambda b, k: (b, 0, k, 0)),
        out_shape=jax.ShapeDtypeStruct((B, GRID_W, r, w), BF16),
        compiler_params=_cparams(("arbitrary", "arbitrary")),
        name="dft_cols",
    )(a, ct, st, m3)


def _dft_ctx_kernel(pq_ref, w_ref, o_ref):
    zin = jnp.concatenate([pq_ref[0], pq_ref[1]], axis=0)
    o_ref[...] = _dot(w_ref[...], zin).astype(BF16)


def _dft_ctx(pq_ctx, w_ctx):
    _, B, n_ctx, w = pq_ctx.shape
    return pl.pallas_call(
        _dft_ctx_kernel,
        grid=(B,),
        in_specs=[pl.BlockSpec((2, None, n_ctx, w), lambda b: (0, b, 0, 0)), _full_spec((n_ctx, 2 * n_ctx))],
        out_specs=pl.BlockSpec((None, n_ctx, w), lambda b: (b, 0, 0)),
        out_shape=jax.ShapeDtypeStruct((B, n_ctx, w), BF16),
        compiler_params=_cparams(("arbitrary",)),
        name="dft_ctx",
    )(pq_ctx, w_ctx)


def _merge_kernel(n_lat_tiles, counts, *refs):
    streams = []
    for n in counts:
        streams.append(refs[:n])
        refs = refs[n:]
    x_refs, y_refs = streams[0], streams[1:]
    h_ref, mod_ref, wg_ref, bg_ref, wb_ref, wo_ref, gpost_ref = refs[:7]
    rest = refs[7:]
    h = h_ref[...]
    acc = None
    for i in range(4):
        gate = jax.nn.sigmoid(_dot(h, wg_ref[i]) + bg_ref[i:i + 1])
        term = gate * _dot(_stream_tile(y_refs[i], n_lat_tiles), wb_ref[i])
        acc = term if acc is None else acc + term
    m = _dot(acc.astype(BF16), wo_ref[...])
    mod = mod_ref[...]
    x = _stream_tile(x_refs, n_lat_tiles) + mod[2:3] * _rms(m, gpost_ref[...])
    o_ref = rest[3] if len(rest) > 1 else rest[0]
    o_ref[...] = x
    if len(rest) > 1:
        gffn_ref, wr_ref, tri_ref, _, hf_ref, route_ref, cnt_ref, carry_ref = rest
        _route_tile(x, mod, gffn_ref, wr_ref, tri_ref, hf_ref, route_ref, cnt_ref, carry_ref)


def _merge(x_all, h, y_diff, y_f, y_gqa, y_mla, mods, wts, gpost, n_lat_tiles, n_tiles, router=None):
    B, _ = _stream_dims(x_all)
    tm = TM
    tok = lambda w: pl.BlockSpec((None, tm, w), lambda b, t: (b, t, 0))
    in_specs, args, counts = [], [], []
    for stream, width in ((x_all, D_MODEL), (y_diff, BRANCH_W), (y_f, BRANCH_W), (y_gqa, BRANCH_W),
                          (y_mla, BRANCH_W)):
        specs, operands = _stream_specs(stream, n_lat_tiles, width)
        in_specs += specs
        args += operands
        counts.append(len(operands))
    in_specs += [
        tok(D_MODEL),
        pl.BlockSpec((None, None, N_MOD, D_MODEL), lambda b, t: (b, jnp.where(t >= n_lat_tiles, 1, 0), 0, 0)),
        _full_spec((4, D_MODEL, D_MODEL)), _full_spec((4, D_MODEL)), _full_spec((4, BRANCH_W, D_MODEL)),
        _full_spec((D_MODEL, D_MODEL)), _full_spec((1, D_MODEL)),
    ]
    args += [h, mods, wts["w_gate"], wts["b_gate"], wts["w_branch"], wts["w_out"], gpost]
    out_specs = [tok(D_MODEL)]
    out_shape = [jax.ShapeDtypeStruct((B, n_tiles * tm, D_MODEL), F32)]
    scratch = []
    if router is not None:
        in_specs += [_full_spec((1, D_MODEL)), _full_spec((D_MODEL, LANES)), _full_spec((tm, tm))]
        args += list(router)
        out_specs += [tok(D_MODEL), tok(LANES), _full_spec((1, LANES))]
        out_shape += [jax.ShapeDtypeStruct((B, n_tiles * tm, D_MODEL), F32),
                      jax.ShapeDtypeStruct((B, n_tiles * tm, LANES), F32),
                      jax.ShapeDtypeStruct((1, LANES), F32)]
        scratch = [pltpu.VMEM((1, LANES), F32)]
    out = pl.pallas_call(
        functools.partial(_merge_kernel, n_lat_tiles, tuple(counts)),
        grid=(B, n_tiles),
        in_specs=in_specs,
        out_specs=out_specs,
        out_shape=out_shape,
        scratch_shapes=scratch,
        compiler_params=_cparams(("arbitrary", "arbitrary")),
        name="merge",
    )(*args)
    return out[0] if router is None else out


def _dense_ffn_kernel(nf, tf, x_ref, mod_ref, gpre_ref, gpost_ref, wgu_ref, wd_ref, o_ref):
    x = x_ref[...]
    mod = mod_ref[...]
    hb = (_rms(x, gpre_ref[...]) * (1.0 + mod[4:5]) + mod[3:4]).astype(BF16)
    acc = None
    for f in range(nf):
        a = _dot(hb, wgu_ref[:, f * tf:(f + 1) * tf])
        u = _dot(hb, wgu_ref[:, FFN_DIM + f * tf:FFN_DIM + (f + 1) * tf])
        act = (a * jax.nn.sigmoid(a) * u).astype(BF16)
        part = _dot(act, wd_ref[f * tf:(f + 1) * tf, :])
        acc = part if acc is None else acc + part
    o_ref[...] = x + mod[5:6] * _rms(acc, gpost_ref[...])


def _dense_ffn(x_all, mods, gpre, gpost, wgu, wd, n_lat_tiles, n_tiles):
    B = x_all.shape[0]
    nf, tf = 1, FFN_DIM
    tok = pl.BlockSpec((None, TM, D_MODEL), lambda b, t: (b, t, 0))
    return pl.pallas_call(
        functools.partial(_dense_ffn_kernel, nf, tf),
        grid=(B, n_tiles),
        in_specs=[tok,
                  pl.BlockSpec((None, None, N_MOD, D_MODEL),
                               lambda b, t: (b, jnp.where(t >= n_lat_tiles, 1, 0), 0, 0)),
                  _full_spec((1, D_MODEL)), _full_spec((1, D_MODEL)),
                  _full_spec((D_MODEL, 2 * FFN_DIM)), _full_spec((FFN_DIM, D_MODEL))],
        out_specs=tok,
        out_shape=jax.ShapeDtypeStruct((B, n_tiles * TM, D_MODEL), F32),
        compiler_params=_cparams(("arbitrary", "arbitrary")),
        name="dense_ffn",
    )(x_all, mods, gpre, gpost, wgu, wd)


TE = 512
R_E1, R_E2, R_G1, R_G2, R_RANK1, R_RANK2 = range(6)


def _route_tile(x, mod, gpre_ref, wr_ref, tri_ref, h_ref, route_ref, cnt_ref, carry_ref):
    first = jnp.logical_and(pl.program_id(0) == 0, pl.program_id(1) == 0)

    @pl.when(first)
    def _():
        carry_ref[...] = jnp.zeros_like(carry_ref)

    h = _rms(x, gpre_ref[...]) * (1.0 + mod[4:5]) + mod[3:4]
    h_ref[...] = h
    logits = _dot3(h, wr_ref[...])
    lane = lax.broadcasted_iota(jnp.int32, logits.shape, 1)
    neg = jnp.float32(-jnp.inf)
    l1 = jnp.where(lane < N_EXPERTS, logits, neg)
    m1 = jnp.max(l1, axis=-1, keepdims=True)
    i1 = jnp.min(jnp.where(l1 == m1, lane, LANES), axis=-1, keepdims=True)
    l2 = jnp.where(lane == i1, neg, l1)
    m2 = jnp.max(l2, axis=-1, keepdims=True)
    i2 = jnp.min(jnp.where(l2 == m2, lane, LANES), axis=-1, keepdims=True)
    e2 = jnp.exp(m2 - m1)
    g1 = 1.0 / (1.0 + e2)
    g2 = e2 / (1.0 + e2)
    sel1, sel2 = lane == i1, lane == i2
    chosen = jnp.where(jnp.logical_or(sel1, sel2), 1.0, 0.0)
    before = _dot(tri_ref[...], chosen.astype(BF16)) + carry_ref[...]
    r1 = jnp.sum(jnp.where(sel1, before, 0.0), axis=-1, keepdims=True)
    r2 = jnp.sum(jnp.where(sel2, before, 0.0), axis=-1, keepdims=True)
    carry_ref[...] += jnp.sum(chosen, axis=0, keepdims=True)
    cnt_ref[...] = carry_ref[...]
    rec = jnp.zeros_like(logits)
    for col, val in ((R_E1, i1.astype(F32)), (R_E2, i2.astype(F32)), (R_G1, g1), (R_G2, g2),
                     (R_RANK1, r1), (R_RANK2, r2)):
        rec = jnp.where(lane == col, val, rec)
    route_ref[...] = rec


def _dispatch_kernel(last_ref, pos_ref, h_ref, xs_ref, zero_ref, stage_ref, sems):
    i = pl.program_id(0)
    n = h_ref.shape[0]

    @pl.when(i == 0)
    def _():
        zero_ref[...] = jnp.zeros_like(zero_ref)
        for e in range(last_ref.shape[0]):
            @pl.when(last_ref[e] >= 0)
            def _():
                fill = pltpu.make_async_copy(zero_ref, xs_ref.at[pl.ds(pl.multiple_of(last_ref[e], TE), TE)],
                                             sems.at[0])
                fill.start()
                fill.wait()

    cur = i % 2
    stage_ref[cur] = h_ref[...]

    def start(r, carry):
        for slot in range(2):
            pltpu.make_async_copy(stage_ref.at[cur, pl.ds(r, 1)], xs_ref.at[pl.ds(pos_ref[0, 2 * r + slot], 1)],
                                  sems.at[cur]).start(priority=slot)
        return carry

    lax.fori_loop(0, n, start, 0, unroll=8)

    def drain(half):
        for _ in range(2):
            pltpu.make_async_copy(stage_ref.at[half], xs_ref.at[pl.ds(0, n)], sems.at[half]).wait()

    pl.when(i > 0)(lambda: drain(1 - cur))
    pl.when(i == pl.num_programs(0) - 1)(lambda: drain(cur))


def _dispatch(h, pos, last_tile_row, n_rows):
    n = h.shape[0]
    grid_spec = pltpu.PrefetchScalarGridSpec(
        num_scalar_prefetch=1,
        grid=(n // TM,),
        in_specs=[pl.BlockSpec((None, 1, 2 * TM), lambda i, last: (i, 0, 0), memory_space=pltpu.SMEM),
                  pl.BlockSpec((TM, D_MODEL), lambda i, last: (i, 0))],
        out_specs=pl.BlockSpec(memory_space=pl.ANY),
        scratch_shapes=[pltpu.VMEM((TE, D_MODEL), F32), pltpu.VMEM((2, TM, D_MODEL), F32),
                        pltpu.SemaphoreType.DMA((2,))],
    )
    return pl.pallas_call(
        _dispatch_kernel,
        grid_spec=grid_spec,
        out_shape=jax.ShapeDtypeStruct((n_rows, D_MODEL), F32),
        compiler_params=_cparams(("arbitrary",)),
        name="moe_dispatch",
    )(last_tile_row, pos.reshape(n // TM, 1, 2 * TM), h)


def _experts_kernel(te_ref, nv_ref, xs_ref, wa_ref, wu_ref, wd_ref, ys_ref):
    del te_ref
    used = pl.program_id(0) < nv_ref[0]

    @pl.when(used)
    def _():
        x = xs_ref[...].astype(BF16)
        a = _dot(x, wa_ref[...])
        u = _dot(x, wu_ref[...])
        ys_ref[...] = _dot((a * jax.nn.sigmoid(a) * u).astype(BF16), wd_ref[...])

    @pl.when(jnp.logical_not(used))
    def _():
        ys_ref[...] = jnp.zeros_like(ys_ref)


def _experts(xs, tile_expert, n_valid, wgu, wd):
    n_rows = xs.shape[0]
    row_blk = lambda i, te, nv: (jnp.minimum(i, nv[0] - 1), 0)
    resident = pl.Buffered(1)
    grid_spec = pltpu.PrefetchScalarGridSpec(
        num_scalar_prefetch=2,
        grid=(n_rows // TE,),
        in_specs=[pl.BlockSpec((TE, D_MODEL), row_blk),
                  pl.BlockSpec((None, D_MODEL, EXPERT_DIM), lambda i, te, nv: (te[i], 0, 0), pipeline_mode=resident),
                  pl.BlockSpec((None, D_MODEL, EXPERT_DIM), lambda i, te, nv: (te[i], 0, 1), pipeline_mode=resident),
                  pl.BlockSpec((None, EXPERT_DIM, D_MODEL), lambda i, te, nv: (te[i], 0, 0), pipeline_mode=resident)],
        out_specs=pl.BlockSpec((TE, D_MODEL), lambda i, te, nv: (i, 0)),
    )
    return pl.pallas_call(
        _experts_kernel,
        grid_spec=grid_spec,
        out_shape=jax.ShapeDtypeStruct((n_rows, D_MODEL), F32),
        compiler_params=_cparams(("arbitrary",)),
        name="moe_experts",
    )(tile_expert, n_valid, xs, wgu, wgu, wd)


def _combine_kernel(pos_ref, nxt_ref, x_ref, route_ref, mod_ref, gpost_ref, ys_ref, o_ref, buf_ref, sems):
    i = pl.program_id(0)
    n = x_ref.shape[0]

    def gather(idx_ref, half):
        def start(r, carry):
            for slot in range(2):
                pltpu.make_async_copy(ys_ref.at[pl.ds(idx_ref[0, 2 * r + slot], 1)],
                                      buf_ref.at[half, slot, pl.ds(r, 1)], sems.at[half]).start(priority=slot)
            return carry
        lax.fori_loop(0, n, start, 0, unroll=8)

    cur = i % 2

    @pl.when(i == 0)
    def _():
        gather(pos_ref, 0)

    @pl.when(i + 1 < pl.num_programs(0))
    def _():
        gather(nxt_ref, 1 - cur)

    for slot in range(2):
        pltpu.make_async_copy(ys_ref.at[pl.ds(0, n)], buf_ref.at[cur, slot], sems.at[cur]).wait()
    route = route_ref[...]
    y = route[:, R_G1:R_G1 + 1] * buf_ref[cur, 0] + route[:, R_G2:R_G2 + 1] * buf_ref[cur, 1]
    mod = mod_ref[...]
    o_ref[...] = x_ref[...] + mod[5:6] * _rms(y, gpost_ref[...])


def _combine(x_all, route, pos, ys, mods, gpost, n_lat_tiles, n_tiles):
    B = x_all.shape[0]
    n_steps = B * n_tiles
    tok = lambda w: pl.BlockSpec((None, TM, w), lambda i: (i // n_tiles, i % n_tiles, 0))
    idx = lambda ahead: pl.BlockSpec((None, 1, 2 * TM), lambda i: (jnp.minimum(i + ahead, n_steps - 1), 0, 0),
                                     memory_space=pltpu.SMEM)
    pos_tiles = pos.reshape(n_steps, 1, 2 * TM)
    return pl.pallas_call(
        _combine_kernel,
        grid=(n_steps,),
        in_specs=[idx(0), idx(1), tok(D_MODEL), tok(LANES),
                  pl.BlockSpec((None, None, N_MOD, D_MODEL),
                               lambda i: (i // n_tiles, jnp.where(i % n_tiles >= n_lat_tiles, 1, 0), 0, 0)),
                  _full_spec((1, D_MODEL)),
                  pl.BlockSpec(memory_space=pl.ANY)],
        out_specs=tok(D_MODEL),
        out_shape=jax.ShapeDtypeStruct((B, n_tiles * TM, D_MODEL), F32),
        scratch_shapes=[pltpu.VMEM((2, 2, TM, D_MODEL), F32), pltpu.SemaphoreType.DMA((2,))],
        compiler_params=_cparams(("arbitrary",)),
        name="moe_combine",
    )(pos_tiles, pos_tiles, x_all, route, mods, gpost, ys)


def _routed_ffn(x_all, h, route, counts, mods, gpost, wgu, wd, n_lat_tiles, n_tiles):
    B = x_all.shape[0]
    n_tok = B * n_tiles * TM
    rec = route.reshape(n_tok, LANES)
    expert = rec[:, R_E1:R_E2 + 1].astype(jnp.int32)
    rank = rec[:, R_RANK1:R_RANK2 + 1].astype(jnp.int32)
    tiles_per_expert = (counts[0, :N_EXPERTS].astype(jnp.int32) + TE - 1) // TE
    tile_end = jnp.cumsum(tiles_per_expert)
    n_valid = tile_end[-1:]
    pos = (tile_end - tiles_per_expert)[expert] * TE + rank
    n_tiles_max = (2 * n_tok) // TE + N_EXPERTS
    tile_id = jnp.minimum(jnp.arange(n_tiles_max, dtype=jnp.int32), n_valid - 1)
    tile_expert = jnp.minimum(jnp.searchsorted(tile_end, tile_id, side="right"), N_EXPERTS - 1).astype(jnp.int32)
    spare = n_valid + jnp.arange(N_EXPERTS, dtype=jnp.int32)
    last_tile_row = jnp.concatenate([jnp.where(tiles_per_expert > 0, (tile_end - 1) * TE, -1),
                                     jnp.where(spare < n_tiles_max, spare * TE, -1)]).astype(jnp.int32)
    xs = _dispatch(h.reshape(n_tok, D_MODEL), pos, last_tile_row, n_tiles_max * TE)
    ys = _experts(xs, tile_expert, n_valid.astype(jnp.int32), wgu, wd)
    return _combine(x_all, route, pos, ys, mods, gpost, n_lat_tiles, n_tiles)


def _rope_angles(n_lat, rot_dim):
    rows = n_lat // GRID_W
    row = jnp.repeat(jnp.arange(rows, dtype=F32), GRID_W)
    col = jnp.tile(jnp.arange(GRID_W, dtype=F32), rows)
    axis_dim = rot_dim // 2
    inv_freq = 1.0 / (ROPE_BASE ** (jnp.arange(0, axis_dim, 2, dtype=F32) / axis_dim))
    ar = row[:, None] * inv_freq
    ac = col[:, None] * inv_freq
    return jnp.concatenate([ar, ar, ac, ac], axis=-1)


def _row_stage_matrix(r):
    nb = DFT_ROW_COLS
    size = 2 * r * nb
    i = lax.broadcasted_iota(jnp.int32, (size, size), 0)
    j = lax.broadcasted_iota(jnp.int32, (size, size), 1)
    part_i, k1, col_i = i // (r * nb), (i % (r * nb)) // nb, i % nb
    part_j, n1, col_j = j // (r * nb), (j % (r * nb)) // nb, j % nb
    ang = ((k1 * n1) % r).astype(F32) * (2.0 * math.pi / r)
    val = jnp.where(part_i == part_j, jnp.cos(ang), -jnp.sin(ang))
    val = jnp.where(jnp.logical_and(part_i == 1, part_j == 1), -val, val)
    return jnp.where(col_i == col_j, val, 0.0)


def _constants(n_lat, n_ctx):
    lane = jnp.arange(LANES)
    ang = _rope_angles(n_lat, HEAD_DIM)
    cos_h = jnp.tile(jnp.cos(ang), (1, 2))
    sin_h = jnp.tile(jnp.sin(ang), (1, 2)) * jnp.where((lane % 32) < 16, -1.0, 1.0)
    cos_h = jnp.concatenate([cos_h, jnp.ones((n_ctx, LANES), F32)], axis=0)
    sin_h = jnp.concatenate([sin_h, jnp.zeros((n_ctx, LANES), F32)], axis=0)
    angm = _rope_angles(n_lat, MLA_ROPE)
    pad_l = jnp.zeros((n_lat, MLA_NOPE), F32)
    pad_r = jnp.zeros((n_lat, LANES - MLA_NOPE - MLA_ROPE), F32)
    cos_m = jnp.concatenate([pad_l + 1.0, jnp.cos(angm), pad_r + 1.0], axis=1)
    sin_m = jnp.concatenate([pad_l, jnp.sin(angm), pad_r], axis=1) * jnp.where((lane % 16) < 8, -1.0, 1.0)
    cos_m = jnp.concatenate([cos_m, jnp.ones((n_ctx, LANES), F32)], axis=0)
    sin_m = jnp.concatenate([sin_m, jnp.zeros((n_ctx, LANES), F32)], axis=0)
    idx = jnp.arange(512)
    seg = jnp.where((idx[:, None] // HEAD_DIM) == (idx[None, :] // HEAD_DIM), 1.0 / HEAD_DIM, 0.0).astype(BF16)

    def dft(n):
        i = jnp.arange(n, dtype=jnp.int32)
        a = ((i[:, None] * i[None, :]) % n).astype(F32) * (2.0 * math.pi / n)
        s = n ** -0.5
        return jnp.cos(a) * s, jnp.sin(a) * s

    cc, sc = dft(FOURIER_GROUP)
    cx, sx = dft(n_ctx)
    r = n_lat // GRID_W
    c64, s64 = (t * GRID_W ** 0.5 * n_lat ** -0.5 for t in dft(GRID_W))
    k1 = jnp.arange(r, dtype=jnp.int32)[:, None]
    n2 = jnp.arange(GRID_W, dtype=jnp.int32)[None, :]
    tw = ((k1 * n2) % n_lat).astype(F32) * (2.0 * math.pi / n_lat)
    wide = lambda t: jnp.broadcast_to(t[:, :, None], (r, GRID_W, LANES))
    return {
        "cos_h": cos_h, "sin_h": sin_h, "cos_m": cos_m, "sin_m": sin_m, "seg": seg,
        "cc": cc.astype(BF16), "sc": sc.astype(BF16),
        "m1x": _row_stage_matrix(r).astype(BF16),
        "tw_cos": wide(jnp.cos(tw)), "tw_sin": wide(jnp.sin(tw)),
        "m3": jnp.concatenate([c64, s64], axis=1).astype(BF16),
        "w_ctx": jnp.concatenate([cx, -sx], axis=1).astype(BF16),
    }


_GQA_HEAD_ORDER = (0, 4, 1, 5, 2, 6, 3, 7)


def _layer_weights(l, w_in, gqa_q_norm, gqa_k_norm, mla_q_norm, mla_kv_norm, mla_w_q_up, mla_w_kv_up,
                   w_branch, w_gate, b_gate, w_out):
    wi = w_in[l]
    kr0 = COL["kr"]
    gq = wi[:, COL["gq"]:COL["gk"]].reshape(D_MODEL, GQA_Q_HEADS, HEAD_DIM)[:, jnp.array(_GQA_HEAD_ORDER)]
    zeros = lambda n: jnp.zeros((D_MODEL, n), F32)
    w_in_p = jnp.concatenate([wi[:, :COL["gq"]], gq.reshape(D_MODEL, 512), wi[:, COL["gk"]:kr0],
                              zeros(MLA_NOPE), wi[:, kr0:kr0 + MLA_ROPE], zeros(LANES - MLA_NOPE - MLA_ROPE)], axis=1)
    qu = mla_w_q_up[l].reshape(MLA_Q_RANK, MLA_HEADS, MLA_NOPE + MLA_ROPE)
    qu = jnp.pad(qu, ((0, 0), (0, 0), (0, LANES - MLA_NOPE - MLA_ROPE))).reshape(MLA_Q_RANK, MLA_HEADS * LANES)
    kvu = mla_w_kv_up[l].reshape(MLA_KV_RANK, MLA_HEADS, MLA_NOPE + MLA_V)
    ku = jnp.pad(kvu[:, :, :MLA_NOPE], ((0, 0), (0, 0), (0, LANES - MLA_NOPE))).reshape(MLA_KV_RANK, -1)
    vu = kvu[:, :, MLA_NOPE:].reshape(MLA_KV_RANK, MLA_HEADS * MLA_V)
    wb = w_branch[l]
    wb_gqa = wb[2].reshape(GQA_Q_HEADS, HEAD_DIM, D_MODEL)[jnp.array(_GQA_HEAD_ORDER)].reshape(BRANCH_W, D_MODEL)
    wb = jnp.stack([wb[0], wb[1], wb_gqa, wb[3]])
    return {
        "w_in": w_in_p.astype(BF16),
        "gq_norm": jnp.tile(gqa_q_norm[l], GQA_Q_HEADS).reshape(1, 512),
        "gk_norm": jnp.tile(gqa_k_norm[l], GQA_KV_HEADS).reshape(1, LANES),
        "mq_norm": mla_q_norm[l].reshape(1, MLA_Q_RANK),
        "mkv_norm": mla_kv_norm[l].reshape(1, MLA_KV_RANK),
        "w_q_up": qu.astype(BF16), "w_k_up": ku.astype(BF16), "w_v_up": vu.astype(BF16),
        "w_branch": wb.astype(BF16), "w_gate": w_gate[l].astype(BF16), "b_gate": b_gate[l],
        "w_out": w_out[l].astype(BF16),
    }


def kernel(x, c, ctx, c_ctx, w_mod, b_mod, mix_pre_norm, mix_post_norm, ffn_pre_norm, ffn_post_norm,
           w_in, diff_lambda, diff_subnorm, gqa_q_norm, gqa_k_norm, mla_q_norm, mla_kv_norm,
           mla_w_q_up, mla_w_kv_up, w_branch, w_gate, b_gate, w_out,
           dense_w_gate_up, dense_w_down, moe_router, moe_w_gate_up, moe_w_down):
    B, n_lat, d = x.shape
    n_ctx = ctx.shape[1]
    depth = w_mod.shape[0]
    assert d == D_MODEL and n_lat % TM == 0 and n_ctx % TM == 0 and n_lat % n_ctx == 0
    assert n_lat % (GRID_W * DFT_COLS_PER_STEP) == 0 and GRID_W % DFT_ROW_COLS == 0
    T = n_lat + n_ctx
    n_lat_tiles, n_all_tiles = n_lat // TM, T // TM
    consts = _constants(n_lat, n_ctx)
    x_all = (x, ctx)
    c_rows = jnp.pad(jnp.concatenate([c, c_ctx[None]], axis=0), ((0, (-(B + 1)) % 8), (0, 0)))
    row = lambda v: v.reshape(1, -1)

    for layer in range(depth):
        last = layer == depth - 1
        n_tiles = n_lat_tiles if last else n_all_tiles
        lam_init = 0.8 - 0.6 * math.exp(-0.3 * layer)
        wts = _layer_weights(layer, w_in, gqa_q_norm, gqa_k_norm, mla_q_norm, mla_kv_norm, mla_w_q_up,
                             mla_w_kv_up, w_branch, w_gate, b_gate, w_out)
        mod_rows = _modulation(c_rows, w_mod[layer], b_mod[layer])
        mods = jnp.stack([mod_rows[:B].reshape(B, N_MOD, D_MODEL),
                          jnp.broadcast_to(mod_rows[B].reshape(1, N_MOD, D_MODEL), (B, N_MOD, D_MODEL))], axis=1)

        (h, dq, dk, dv, gq, gk, gv, mq, mk, mv, pq_lat, pq_ctx) = _project(
            x_all, mods, row(mix_pre_norm[layer]), wts, consts, n_lat_tiles)

        diff_extra = (diff_lambda[layer], diff_subnorm[layer].reshape(1, LANES))
        ys = []
        for mode, q, k, v in (("diff", dq, dk, dv), ("gqa", gq, gk, gv), ("mla", mq, mk, mv)):
            kw = dict(lam_init=lam_init, extra=diff_extra)
            tq = TQ if n_lat % TQ == 0 else TM
            y = _attention(mode, q, k, v, tq=tq, nq=n_lat // tq, q_blk0=0, tk=T, k_blk=0,
                           maps_per_step=MAPS_PER_STEP, **kw)
            if not last:
                y = (y, _attention(mode, q, k, v, tq=TM, nq=n_ctx // TM, q_blk0=n_lat_tiles,
                                   tk=n_ctx, k_blk=n_lat // n_ctx, maps_per_step=N_MAPS, **kw))
            ys.append(y)
        y_diff, y_gqa, y_mla = ys

        a_rows = _dft_rows(pq_lat, consts["m1x"])
        y_f = _dft_cols(a_rows, consts["tw_cos"], consts["tw_sin"], consts["m3"]).reshape(B, n_lat, BRANCH_W)
        if not last:
            y_f = (y_f, _dft_ctx(pq_ctx, consts["w_ctx"]))

        merge_args = (x_all, h, y_diff, y_f, y_gqa, y_mla, mods, wts, row(mix_post_norm[layer]))
        if layer % 2 == 0:
            x_all = _merge(*merge_args, n_lat_tiles, n_tiles)
            x_all = _dense_ffn(x_all, mods, row(ffn_pre_norm[layer]), row(ffn_post_norm[layer]),
                               dense_w_gate_up[layer // 2].astype(BF16), dense_w_down[layer // 2].astype(BF16),
                               n_lat_tiles, n_tiles)
        else:
            wr = jnp.pad(moe_router[layer // 2], ((0, 0), (0, LANES - N_EXPERTS)))
            idx = jnp.arange(TM)
            tri = jnp.where(idx[:, None] > idx[None, :], 1.0, 0.0).astype(BF16)
            x_all, h_ffn, route, counts = _merge(*merge_args, n_lat_tiles, n_tiles,
                                                 router=(row(ffn_pre_norm[layer]), wr, tri))
            x_all = _routed_ffn(x_all, h_ffn, route, counts, mods, row(ffn_post_norm[layer]),
                                moe_w_gate_up[layer // 2].astype(BF16), moe_w_down[layer // 2].astype(BF16),
                                n_lat_tiles, n_tiles)
    return x_all[:, :n_lat]
```
